```python
import jax, jax.numpy as jnp
from jax import lax
import numpy as np

D_MODEL = 2048
BATCH = 8
SEQ = 2048
DEPTH = 1

MEM_LEN = 256
CONV_WIDTH = D_MODEL // 4
CONV_GROUPS = 8
CONV_K = 3
DIFF_WIDTH = D_MODEL // 2
DIFF_VDIM = 128
DIFF_HALF = 64
DIFF_HEADS = DIFF_WIDTH // DIFF_VDIM
MEM_WIDTH = D_MODEL // 4
MEM_HEADS = 4
MEM_HEAD_DIM = MEM_WIDTH // MEM_HEADS
MIX_WIDTH = CONV_WIDTH + DIFF_WIDTH + MEM_WIDTH
IN_SPLITS = [CONV_WIDTH, CONV_WIDTH, CONV_WIDTH,
             DIFF_WIDTH, DIFF_WIDTH, DIFF_WIDTH, MEM_WIDTH]
IN_WIDTH = sum(IN_SPLITS)
ROT_DIM = DIFF_HALF // 4
ROPE_THETA = 500000.0
FFN_HIDDEN = -(-(8 * D_MODEL) // (3 * 256)) * 256
Q_BLOCK = 128
EPS = 1e-6

kernel_name = "hybrid_conv_diffattn_memxattn_layer"


def rmsnorm(x, g):
    xf = x.astype(jnp.float32)
    r = lax.rsqrt(jnp.mean(xf * xf, axis=-1, keepdims=True) + EPS)
    return (xf * r * g.astype(jnp.float32)).astype(x.dtype)


def lambda_init(layer_idx):
    return 0.8 - 0.6 * float(np.exp(-0.3 * (layer_idx - 1)))


def rope_tables(positions):
    inv_freq = ROPE_THETA ** (-jnp.arange(0, ROT_DIM, 2, dtype=jnp.float32) / ROT_DIM)
    ang = positions.astype(jnp.float32)[..., None] * inv_freq
    return jnp.cos(ang), jnp.sin(ang)


def apply_partial_rope(t, cos, sin):
    c = cos[:, :, None, None, :]
    s = sin[:, :, None, None, :]
    tf = t.astype(jnp.float32)
    half = ROT_DIM // 2
    r1, r2, rest = tf[..., :half], tf[..., half:ROT_DIM], tf[..., ROT_DIM:]
    out = jnp.concatenate([r1 * c - r2 * s, r2 * c + r1 * s, rest], axis=-1)
    return out.astype(t.dtype)


def short_gated_conv(u, c_gate, b_gate, conv_w):
    z = c_gate * u
    w = conv_w[:, None, :].astype(z.dtype)
    conv = lax.conv_general_dilated(
        z, w, window_strides=(1,), padding=[(CONV_K - 1, 0)],
        dimension_numbers=("NWC", "WIO", "NWC"), feature_group_count=CONV_WIDTH)
    return b_gate * conv


def diff_attention(q, k, v, cos, sin, g_q, g_k, lq1, lk1, lq2, lk2, g_sub, lam_init):
    B, S, _ = q.shape
    q = q.reshape(B, S, DIFF_HEADS, 2, DIFF_HALF)
    k = k.reshape(B, S, DIFF_HEADS, 2, DIFF_HALF)
    vf = v.reshape(B, S, DIFF_HEADS, DIFF_VDIM).astype(jnp.float32)
    q = apply_partial_rope(rmsnorm(q, g_q), cos, sin)
    k = apply_partial_rope(rmsnorm(k, g_k), cos, sin)
    lam = (jnp.exp(jnp.sum(lq1.astype(jnp.float32) * lk1.astype(jnp.float32)))
           - jnp.exp(jnp.sum(lq2.astype(jnp.float32) * lk2.astype(jnp.float32)))
           + lam_init)
    scale = DIFF_HALF ** -0.5
    key_pos = jnp.arange(S)

    def one_block(i):
        start = i * Q_BLOCK
        qb = lax.dynamic_slice_in_dim(q, start, Q_BLOCK, axis=1)
        s = jnp.einsum('bqhmd,bkhmd->bhmqk', qb, k,
                       preferred_element_type=jnp.float32) * scale
        qpos = start + jnp.arange(Q_BLOCK)
        mask = qpos[:, None] >= key_pos[None, :]
        p = jax.nn.softmax(jnp.where(mask, s, -jnp.inf), axis=-1)
        a = p[:, :, 0] - lam * p[:, :, 1]
        return jnp.einsum('bhqk,bkhd->bqhd', a, vf)

    o = lax.map(one_block, jnp.arange(S // Q_BLOCK))
    o = jnp.transpose(o, (1, 0, 2, 3, 4)).reshape(B, S, DIFF_HEADS, DIFF_VDIM)
    o = rmsnorm(o, g_sub) * (1.0 - lam_init)
    return o.reshape(B, S, DIFF_WIDTH).astype(v.dtype)


def memory_cross_attention(q_m, kv_m, g_q, g_k):
    B, S, _ = q_m.shape
    q = rmsnorm(q_m.reshape(B, S, MEM_HEADS, MEM_HEAD_DIM), g_q)
    k, v = jnp.split(kv_m, 2, axis=-1)
    k = rmsnorm(k.reshape(B, MEM_LEN, MEM_HEADS, MEM_HEAD_DIM), g_k)
    v = v.reshape(B, MEM_LEN, MEM_HEADS, MEM_HEAD_DIM)
    s = jnp.einsum('bqhd,bmhd->bhqm', q, k,
                   preferred_element_type=jnp.float32) * (MEM_HEAD_DIM ** -0.5)
    p = jax.nn.softmax(s, axis=-1)
    o = jnp.einsum('bhqm,bmhd->bqhd', p, v.astype(jnp.float32))
    return o.reshape(B, S, MEM_WIDTH).astype(q_m.dtype)


def setup_inputs(seed: int = 0) -> dict:
    key = jax.random.key(seed)
    ks = jax.random.split(key, 26)
    f32 = jnp.float32
    L, D = DEPTH, D_MODEL

    def nrm(k, shape, scale):
        return jax.random.normal(k, shape, f32) * scale

    def gain(k, shape):
        return 1.0 + 0.05 * jax.random.normal(k, shape, f32)

    x = jax.random.normal(ks[0], (BATCH, SEQ, D), f32)
    mem = jax.random.normal(ks[1], (BATCH, MEM_LEN, D), f32)
    offsets = jax.random.randint(ks[2], (BATCH, 1), 0, 4096, dtype=jnp.int32)
    positions = offsets + jnp.arange(SEQ, dtype=jnp.int32)[None, :]
    return {
        "x": x,
        "mem": mem,
        "positions": positions,
        "g_mix": gain(ks[3], (L, D)),
        "g_mem": gain(ks[4], (L, D)),
        "w_in": nrm(ks[5], (L, D, IN_WIDTH), D ** -0.5),
        "conv_w": nrm(ks[6], (L, CONV_K, CONV_WIDTH), CONV_K ** -0.5),
        "g_conv_out": gain(ks[7], (L, CONV_WIDTH)),
        "g_dq": gain(ks[8], (L, DIFF_HALF)),
        "g_dk": gain(ks[9], (L, DIFF_HALF)),
        "lam_q1": nrm(ks[10], (L, DIFF_HALF), 0.1),
        "lam_k1": nrm(ks[11], (L, DIFF_HALF), 0.1),
        "lam_q2": nrm(ks[12], (L, DIFF_HALF), 0.1),
        "lam_k2": nrm(ks[13], (L, DIFF_HALF), 0.1),
        "g_sub": gain(ks[14], (L, DIFF_VDIM)),
        "w_mem_kv": nrm(ks[15], (L, D, 2 * MEM_WIDTH), D ** -0.5),
        "g_mq": gain(ks[16], (L, MEM_HEAD_DIM)),
        "g_mk": gain(ks[17], (L, MEM_HEAD_DIM)),
        "g_mem_out": gain(ks[18], (L, MEM_WIDTH)),
        "w_o": nrm(ks[19], (L, MIX_WIDTH, D), MIX_WIDTH ** -0.5),
        "g_ffn": gain(ks[20], (L, D)),
        "w_gate": nrm(ks[21], (L, D, FFN_HIDDEN), D ** -0.5),
        "w_up": nrm(ks[22], (L, D, FFN_HIDDEN), D ** -0.5),
        "w_down": nrm(ks[23], (L, FFN_HIDDEN, D), FFN_HIDDEN ** -0.5),
    }


def reference(x, mem, positions, g_mix, g_mem, w_in, conv_w, g_conv_out, g_dq, g_dk,
              lam_q1, lam_k1, lam_q2, lam_k2, g_sub, w_mem_kv, g_mq, g_mk, g_mem_out,
              w_o, g_ffn, w_gate, w_up, w_down):
    cos, sin = rope_tables(positions)
    split_idx = [int(i) for i in np.cumsum(IN_SPLITS)[:-1]]
    for l in range(DEPTH):
        h = rmsnorm(x, g_mix[l])
        proj = h @ w_in[l]
        u, c_gate, b_gate, q, k, v, q_m = jnp.split(proj, split_idx, axis=-1)

        y_conv = rmsnorm(short_gated_conv(u, c_gate, b_gate, conv_w[l]), g_conv_out[l])
        y_diff = diff_attention(q, k, v, cos, sin, g_dq[l], g_dk[l],
                                lam_q1[l], lam_k1[l], lam_q2[l], lam_k2[l],
                                g_sub[l], lambda_init(l + 1))
        kv_m = rmsnorm(mem, g_mem[l]) @ w_mem_kv[l]
        y_mem = rmsnorm(memory_cross_attention(q_m, kv_m, g_mq[l], g_mk[l]), g_mem_out[l])

        mixed = jnp.concatenate([y_conv, y_diff, y_mem], axis=-1)
        x = x + mixed @ w_o[l]

        hf = rmsnorm(x, g_ffn[l])
        x = x + (jax.nn.silu(hf @ w_gate[l]) * (hf @ w_up[l])) @ w_down[l]
    return x
```

```python
import functools

import jax
import jax.numpy as jnp
import numpy as np
from jax import lax
from jax.experimental import pallas as pl
from jax.experimental.pallas import tpu as pltpu

F32 = jnp.float32
BF16 = jnp.bfloat16

D_MODEL = 2048
SEQ = 2048
MEM_LEN = 256
CONV_WIDTH = 512
CONV_K = 3
DIFF_WIDTH = 1024
DIFF_VDIM = 128
DIFF_HALF = 64
DIFF_HEADS = 8
MEM_WIDTH = 512
MEM_HEADS = 4
MEM_HEAD_DIM = 128
IN_WIDTH = 3 * CONV_WIDTH + 3 * DIFF_WIDTH + MEM_WIDTH
ROT_DIM = 16
ROPE_THETA = 500000.0
FFN_HIDDEN = 5632
EPS = 1e-6
LAMBDA_INIT = 0.8 - 0.6 * float(np.exp(-0.3 * 0))

V7X_LANES = 128
V7X_SUBLANES = 8
V7X_VMEM_LIMIT_BYTES = 56 * 1024 * 1024

TM_PROJ = 256
TM_MEMKV = 256
TQ = 256
TK = 256
TQ_MEM = 512
TM_OUT = 512
TM_FFN = 512
TH_FFN = 512

_OFF_U = 0
_OFF_C = CONV_WIDTH
_OFF_B = 2 * CONV_WIDTH
_OFF_Q = 3 * CONV_WIDTH
_OFF_K = _OFF_Q + DIFF_WIDTH
_OFF_V = _OFF_K + DIFF_WIDTH
_OFF_QM = _OFF_V + DIFF_WIDTH


def _rms_scale(t, width):
    return lax.rsqrt(jnp.sum(t * t, axis=-1, keepdims=True) * (1.0 / width) + EPS)


def _in_proj_kernel(x_ref, g_ref, w_ref, convw_ref, gconv_ref,
                    yconv_ref, q_ref, k_ref, v_ref, qm_ref, z_scr, *, tiles_per_seq):
    i = pl.program_id(0)
    tm = x_ref.shape[0]
    halo = V7X_SUBLANES

    x = x_ref[...]
    h = (x * _rms_scale(x, D_MODEL) * g_ref[...]).astype(BF16)

    def proj(off, width):
        return jnp.dot(h, w_ref[:, off:off + width], preferred_element_type=F32)

    q_ref[...] = proj(_OFF_Q, DIFF_WIDTH)
    k_ref[...] = proj(_OFF_K, DIFF_WIDTH)
    v_ref[...] = proj(_OFF_V, DIFF_WIDTH).astype(BF16)
    qm_ref[...] = proj(_OFF_QM, MEM_WIDTH)

    @pl.when(i % tiles_per_seq == 0)
    def _():
        z_scr[0:halo, :] = jnp.zeros((halo, CONV_WIDTH), F32)

    z = proj(_OFF_C, CONV_WIDTH) * proj(_OFF_U, CONV_WIDTH)
    z_scr[halo:halo + tm, :] = z
    z1 = z_scr[halo - 1:halo - 1 + tm, :]
    z2 = z_scr[halo - 2:halo - 2 + tm, :]
    cw = convw_ref[...]
    conv = cw[0:1, :] * z2 + cw[1:2, :] * z1 + cw[2:3, :] * z
    y = proj(_OFF_B, CONV_WIDTH) * conv
    yconv_ref[...] = (y * _rms_scale(y, CONV_WIDTH) * gconv_ref[...]).astype(BF16)
    z_scr[0:halo, :] = z_scr[tm:tm + halo, :]


def _in_proj(x2, g_mix, w_in, conv_w, g_conv):
    t = x2.shape[0]
    tm = TM_PROJ
    grid = (t // tm,)
    row = lambda i: (i, 0)
    fixed = lambda i: (0, 0)
    return pl.pallas_call(
        functools.partial(_in_proj_kernel, tiles_per_seq=SEQ // tm),
        grid=grid,
        in_specs=[
            pl.BlockSpec((tm, D_MODEL), row),
            pl.BlockSpec((1, D_MODEL), fixed),
            pl.BlockSpec((D_MODEL, IN_WIDTH), fixed, pipeline_mode=pl.Buffered(1)),
            pl.BlockSpec((CONV_K, CONV_WIDTH), fixed),
            pl.BlockSpec((1, CONV_WIDTH), fixed),
        ],
        out_specs=[
            pl.BlockSpec((tm, CONV_WIDTH), row),
            pl.BlockSpec((tm, DIFF_WIDTH), row),
            pl.BlockSpec((tm, DIFF_WIDTH), row),
            pl.BlockSpec((tm, DIFF_WIDTH), row),
            pl.BlockSpec((tm, MEM_WIDTH), row),
        ],
        out_shape=[
            jax.ShapeDtypeStruct((t, CONV_WIDTH), BF16),
            jax.ShapeDtypeStruct((t, DIFF_WIDTH), F32),
            jax.ShapeDtypeStruct((t, DIFF_WIDTH), F32),
            jax.ShapeDtypeStruct((t, DIFF_WIDTH), BF16),
            jax.ShapeDtypeStruct((t, MEM_WIDTH), F32),
        ],
        scratch_shapes=[pltpu.VMEM((tm + 2 * V7X_SUBLANES, CONV_WIDTH), F32)],
        compiler_params=pltpu.CompilerParams(
            dimension_semantics=("arbitrary",),
            vmem_limit_bytes=V7X_VMEM_LIMIT_BYTES),
        name="in_proj",
    )(x2, g_mix, w_in, conv_w, g_conv)


def _mem_kv_kernel(mem_ref, g_ref, w_ref, gk_ref, k_ref, v_ref):
    x = mem_ref[...]
    h = (x * _rms_scale(x, D_MODEL) * g_ref[...]).astype(BF16)
    kv = jnp.dot(h, w_ref[...], preferred_element_type=F32)
    gk = gk_ref[...]
    for hd in range(MEM_HEADS):
        lo = hd * MEM_HEAD_DIM
        kh = kv[:, lo:lo + MEM_HEAD_DIM]
        k_ref[:, lo:lo + MEM_HEAD_DIM] = (kh * _rms_scale(kh, MEM_HEAD_DIM) * gk).astype(BF16)
    v_ref[...] = kv[:, MEM_WIDTH:].astype(BF16)


def _mem_kv(mem2, g_mem, w_kv, g_mk):
    t = mem2.shape[0]
    tm = TM_MEMKV
    row = lambda i: (i, 0)
    fixed = lambda i: (0, 0)
    return pl.pallas_call(
        _mem_kv_kernel,
        grid=(t // tm,),
        in_specs=[
            pl.BlockSpec((tm, D_MODEL), row),
            pl.BlockSpec((1, D_MODEL), fixed),
            pl.BlockSpec((D_MODEL, 2 * MEM_WIDTH), fixed),
            pl.BlockSpec((1, MEM_HEAD_DIM), fixed),
        ],
        out_specs=[pl.BlockSpec((tm, MEM_WIDTH), row), pl.BlockSpec((tm, MEM_WIDTH), row)],
        out_shape=[jax.ShapeDtypeStruct((t, MEM_WIDTH), BF16),
                   jax.ShapeDtypeStruct((t, MEM_WIDTH), BF16)],
        compiler_params=pltpu.CompilerParams(
            dimension_semantics=("parallel",),
            vmem_limit_bytes=V7X_VMEM_LIMIT_BYTES),
        name="mem_kv",
    )(mem2, g_mem, w_kv, g_mk)


def _qk_norm_rope(t, g2, cos_t, sin_a, sin_b):
    lane = lax.broadcasted_iota(jnp.int32, t.shape, 1)
    is_lo = lane < DIFF_HALF
    t2 = t * t
    s_lo = jnp.sum(jnp.where(is_lo, t2, 0.0), axis=-1, keepdims=True)
    s_hi = jnp.sum(jnp.where(is_lo, 0.0, t2), axis=-1, keepdims=True)
    r = jnp.where(is_lo,
                  lax.rsqrt(s_lo * (1.0 / DIFF_HALF) + EPS),
                  lax.rsqrt(s_hi * (1.0 / DIFF_HALF) + EPS))
    tn = t * r * g2
    half = ROT_DIM // 2
    return (tn * cos_t
            + pltpu.roll(tn, V7X_LANES - half, axis=1) * sin_a
            + pltpu.roll(tn, half, axis=1) * sin_b)


def _diff_attn_kernel(q_ref, k_ref, v_ref, cq_ref, saq_ref, sbq_ref, ck_ref, sak_ref, sbk_ref,
                      gq_ref, gk_ref, lq1_ref, lk1_ref, lq2_ref, lk2_ref, gsub_ref,
                      o_ref, kt_scr, m_scr, l_scr, acc_scr):
    i = pl.program_id(2)
    tq = q_ref.shape[1]
    tk = kt_scr.shape[2]
    nk = kt_scr.shape[0]

    @pl.when(i == 0)
    def _():
        gk = gk_ref[...]
        for jj in range(nk):
            rows = slice(jj * tk, (jj + 1) * tk)
            kn = _qk_norm_rope(k_ref[0, rows, :], gk, ck_ref[0, rows, :],
                               sak_ref[0, rows, :], sbk_ref[0, rows, :])
            kt_scr[jj] = kn.T.astype(BF16)

    qn = _qk_norm_rope(q_ref[0], gq_ref[...], cq_ref[0], saq_ref[0], sbq_ref[0])
    qn = qn * (DIFF_HALF ** -0.5)
    lane = lax.broadcasted_iota(jnp.int32, qn.shape, 1)
    is_lo = lane < DIFF_HALF
    qq = jnp.concatenate([jnp.where(is_lo, qn, 0.0), jnp.where(is_lo, 0.0, qn)],
                         axis=0).astype(BF16)

    m_scr[...] = jnp.full(m_scr.shape, -jnp.inf, F32)
    l_scr[...] = jnp.zeros(l_scr.shape, F32)
    acc_scr[...] = jnp.zeros(acc_scr.shape, F32)

    def block(j, masked):
        s = jnp.dot(qq, kt_scr[j], preferred_element_type=F32)
        if masked:
            r = lax.broadcasted_iota(jnp.int32, (tq, tk), 0)
            c = lax.broadcasted_iota(jnp.int32, (tq, tk), 1)
            keep = jnp.concatenate([c <= r, c <= r], axis=0)
            s = jnp.where(keep, s, -jnp.inf)
        m_prev = m_scr[...]
        m_new = jnp.maximum(m_prev, jnp.max(s, axis=-1, keepdims=True))
        alpha = jnp.exp(m_prev - m_new)
        p = jnp.exp(s - m_new)
        l_scr[...] = alpha * l_scr[...] + jnp.sum(p, axis=-1, keepdims=True)
        start = pl.multiple_of(j * tk, tk)
        vb = v_ref[0, pl.ds(start, tk), :]
        acc_scr[...] = alpha * acc_scr[...] + jnp.dot(p.astype(BF16), vb,
                                                      preferred_element_type=F32)
        m_scr[...] = m_new

    def body(j, carry):
        block(j, masked=False)
        return carry

    lax.fori_loop(0, i, body, 0)
    block(i, masked=True)

    lam = (jnp.exp(jnp.sum(lq1_ref[...] * lk1_ref[...], axis=-1, keepdims=True))
           - jnp.exp(jnp.sum(lq2_ref[...] * lk2_ref[...], axis=-1, keepdims=True))
           + LAMBDA_INIT)
    o_all = acc_scr[...] / l_scr[...]
    o = o_all[:tq] - lam * o_all[tq:]
    y = o * _rms_scale(o, DIFF_VDIM) * gsub_ref[...] * (1.0 - LAMBDA_INIT)
    o_ref[0] = y.astype(o_ref.dtype)


def _diff_attn(q3, k3, v3, cos_t, sin_a, sin_b, gq2, gk2, lq1, lk1, lq2, lk2, g_sub):
    b, s, _ = q3.shape
    tq, tk = TQ, TK
    assert tq == tk
    grid = (b, DIFF_HEADS, s // tq)
    q_tile = lambda bb, h, i: (bb, i, h)
    kv_full = lambda bb, h, i: (bb, 0, h)
    tab_tile = lambda bb, h, i: (bb, i, 0)
    tab_full = lambda bb, h, i: (bb, 0, 0)
    fixed = lambda bb, h, i: (0, 0)
    vec = lambda n: pl.BlockSpec((1, n), fixed)
    return pl.pallas_call(
        _diff_attn_kernel,
        grid=grid,
        in_specs=[
            pl.BlockSpec((1, tq, DIFF_VDIM), q_tile),
            pl.BlockSpec((1, s, DIFF_VDIM), kv_full),
            pl.BlockSpec((1, s, DIFF_VDIM), kv_full),
            pl.BlockSpec((1, tq, V7X_LANES), tab_tile),
            pl.BlockSpec((1, tq, V7X_LANES), tab_tile),
            pl.BlockSpec((1, tq, V7X_LANES), tab_tile),
            pl.BlockSpec((1, s, V7X_LANES), tab_full),
            pl.BlockSpec((1, s, V7X_LANES), tab_full),
            pl.BlockSpec((1, s, V7X_LANES), tab_full),
            vec(DIFF_VDIM), vec(DIFF_VDIM),
            vec(DIFF_HALF), vec(DIFF_HALF), vec(DIFF_HALF), vec(DIFF_HALF),
            vec(DIFF_VDIM),
        ],
        out_specs=pl.BlockSpec((1, tq, DIFF_VDIM), q_tile),
        out_shape=jax.ShapeDtypeStruct((b, s, DIFF_WIDTH), BF16),
        scratch_shapes=[
            pltpu.VMEM((s // tk, DIFF_VDIM, tk), BF16),
            pltpu.VMEM((2 * tq, 1), F32),
            pltpu.VMEM((2 * tq, 1), F32),
            pltpu.VMEM((2 * tq, DIFF_VDIM), F32),
        ],
        compiler_params=pltpu.CompilerParams(
            dimension_semantics=("parallel", "parallel", "arbitrary"),
            vmem_limit_bytes=V7X_VMEM_LIMIT_BYTES),
        name="diff_attn",
    )(q3, k3, v3, cos_t, sin_a, sin_b, cos_t, sin_a, sin_b,
      gq2, gk2, lq1, lk1, lq2, lk2, g_sub)


def _mem_attn_kernel(q_ref, k_ref, v_ref, gq_ref, gout_ref, o_ref):
    gq = gq_ref[...]
    outs = []
    for hd in range(MEM_HEADS):
        cols = slice(hd * MEM_HEAD_DIM, (hd + 1) * MEM_HEAD_DIM)
        qh = q_ref[0, :, cols]
        qh = (qh * _rms_scale(qh, MEM_HEAD_DIM) * gq * (MEM_HEAD_DIM ** -0.5)).astype(BF16)
        s = lax.dot_general(qh, k_ref[0, :, cols], (((1,), (1,)), ((), ())),
                            preferred_element_type=F32)
        e = jnp.exp(s - jnp.max(s, axis=-1, keepdims=True))
        denom = jnp.sum(e, axis=-1, keepdims=True)
        oh = jnp.dot(e.astype(BF16), v_ref[0, :, cols], preferred_element_type=F32)
        outs.append(oh / denom)
    o = jnp.concatenate(outs, axis=-1)
    o_ref[0] = (o * _rms_scale(o, MEM_WIDTH) * gout_ref[...]).astype(o_ref.dtype)


def _mem_attn(qm3, km3, vm3, g_mq, g_out):
    b, s, _ = qm3.shape
    tq = TQ_MEM
    tile = lambda bb, i: (bb, i, 0)
    full = lambda bb, i: (bb, 0, 0)
    fixed = lambda bb, i: (0, 0)
    return pl.pallas_call(
        _mem_attn_kernel,
        grid=(b, s // tq),
        in_specs=[
            pl.BlockSpec((1, tq, MEM_WIDTH), tile),
            pl.BlockSpec((1, MEM_LEN, MEM_WIDTH), full),
            pl.BlockSpec((1, MEM_LEN, MEM_WIDTH), full),
            pl.BlockSpec((1, MEM_HEAD_DIM), fixed),
            pl.BlockSpec((1, MEM_WIDTH), fixed),
        ],
        out_specs=pl.BlockSpec((1, tq, MEM_WIDTH), tile),
        out_shape=jax.ShapeDtypeStruct((b, s, MEM_WIDTH), BF16),
        compiler_params=pltpu.CompilerParams(
            dimension_semantics=("parallel", "parallel"),
            vmem_limit_bytes=V7X_VMEM_LIMIT_BYTES),
        name="mem_attn",
    )(qm3, km3, vm3, g_mq, g_out)


def _out_proj_kernel(x_ref, yc_ref, yd_ref, ym_ref, w_ref, o_ref):
    acc = jnp.dot(yc_ref[...], w_ref[0:CONV_WIDTH, :], preferred_element_type=F32)
    acc += jnp.dot(yd_ref[...], w_ref[CONV_WIDTH:CONV_WIDTH + DIFF_WIDTH, :],
                   preferred_element_type=F32)
    acc += jnp.dot(ym_ref[...], w_ref[CONV_WIDTH + DIFF_WIDTH:, :],
                   preferred_element_type=F32)
    o_ref[...] = x_ref[...] + acc


def _out_proj(x2, yc, yd, ym, w_o):
    t = x2.shape[0]
    tm = TM_OUT
    row = lambda i: (i, 0)
    fixed = lambda i: (0, 0)
    return pl.pallas_call(
        _out_proj_kernel,
        grid=(t // tm,),
        in_specs=[
            pl.BlockSpec((tm, D_MODEL), row),
            pl.BlockSpec((tm, CONV_WIDTH), row),
            pl.BlockSpec((tm, DIFF_WIDTH), row),
            pl.BlockSpec((tm, MEM_WIDTH), row),
            pl.BlockSpec((D_MODEL, D_MODEL), fixed, pipeline_mode=pl.Buffered(1)),
        ],
        out_specs=pl.BlockSpec((tm, D_MODEL), row),
        out_shape=jax.ShapeDtypeStruct((t, D_MODEL), F32),
        compiler_params=pltpu.CompilerParams(
            dimension_semantics=("parallel",),
            vmem_limit_bytes=V7X_VMEM_LIMIT_BYTES),
        name="out_proj",
    )(x2, yc, yd, ym, w_o)


def _ffn_kernel(x_ref, g_ref, wg_ref, wu_ref, wd_ref, o_ref, h_scr, acc_scr):
    j = pl.program_id(1)

    @pl.when(j == 0)
    def _():
        x = x_ref[...]
        h_scr[...] = (x * _rms_scale(x, D_MODEL) * g_ref[...]).astype(BF16)
        acc_scr[...] = jnp.zeros(acc_scr.shape, F32)

    h = h_scr[...]
    gate = jnp.dot(h, wg_ref[...], preferred_element_type=F32)
    up = jnp.dot(h, wu_ref[...], preferred_element_type=F32)
    act = (gate * (1.0 / (1.0 + jnp.exp(-gate))) * up).astype(BF16)
    acc_scr[...] += jnp.dot(act, wd_ref[...], preferred_element_type=F32)

    @pl.when(j == pl.num_programs(1) - 1)
    def _():
        o_ref[...] = x_ref[...] + acc_scr[...]


def _ffn(x2, g_ffn, w_gate, w_up, w_down):
    t = x2.shape[0]
    tm, th = TM_FFN, TH_FFN
    return pl.pallas_call(
        _ffn_kernel,
        grid=(t // tm, FFN_HIDDEN // th),
        in_specs=[
            pl.BlockSpec((tm, D_MODEL), lambda i, j: (i, 0)),
            pl.BlockSpec((1, D_MODEL), lambda i, j: (0, 0)),
            pl.BlockSpec((D_MODEL, th), lambda i, j: (0, j)),
            pl.BlockSpec((D_MODEL, th), lambda i, j: (0, j)),
            pl.BlockSpec((th, D_MODEL), lambda i, j: (j, 0)),
        ],
        out_specs=pl.BlockSpec((tm, D_MODEL), lambda i, j: (i, 0)),
        out_shape=jax.ShapeDtypeStruct((t, D_MODEL), F32),
        scratch_shapes=[pltpu.VMEM((tm, D_MODEL), BF16), pltpu.VMEM((tm, D_MODEL), F32)],
        compiler_params=pltpu.CompilerParams(
            dimension_semantics=("parallel", "arbitrary"),
            vmem_limit_bytes=V7X_VMEM_LIMIT_BYTES),
        name="ffn",
    )(x2, g_ffn, w_gate, w_up, w_down)


def _rope_tables(positions):
    half = ROT_DIM // 2
    inv_freq = ROPE_THETA ** (-jnp.arange(0, ROT_DIM, 2, dtype=F32) / ROT_DIM)
    ang = positions.astype(F32)[..., None] * inv_freq
    cos, sin = jnp.cos(ang), jnp.sin(ang)
    b, s = positions.shape
    one = jnp.ones((b, s, DIFF_HALF - ROT_DIM), F32)
    zero = jnp.zeros((b, s, DIFF_HALF - ROT_DIM), F32)
    zh = jnp.zeros((b, s, half), F32)
    cos_t = jnp.concatenate([cos, cos, one, cos, cos, one], axis=-1)
    sin_a = jnp.concatenate([-sin, zh, zero, -sin, zh, zero], axis=-1)
    sin_b = jnp.concatenate([zh, sin, zero, zh, sin, zero], axis=-1)
    return cos_t, sin_a, sin_b


def kernel(x, mem, positions, g_mix, g_mem, w_in, conv_w, g_conv_out, g_dq, g_dk,
           lam_q1, lam_k1, lam_q2, lam_k2, g_sub, w_mem_kv, g_mq, g_mk, g_mem_out,
           w_o, g_ffn, w_gate, w_up, w_down):
    b, s, d = x.shape
    assert (s, d) == (SEQ, D_MODEL) and mem.shape == (b, MEM_LEN, D_MODEL)
    assert g_mix.shape[0] == 1
    t = b * s
    x2 = x.reshape(t, d)
    mem2 = mem.reshape(b * MEM_LEN, d)
    cos_t, sin_a, sin_b = _rope_tables(positions)

    yc, q, k, v, qm = _in_proj(x2, g_mix, w_in[0].astype(BF16), conv_w[0], g_conv_out)
    km, vm = _mem_kv(mem2, g_mem, w_mem_kv[0].astype(BF16), g_mk)

    yd = _diff_attn(q.reshape(b, s, DIFF_WIDTH), k.reshape(b, s, DIFF_WIDTH),
                    v.reshape(b, s, DIFF_WIDTH), cos_t, sin_a, sin_b,
                    jnp.tile(g_dq, (1, 2)), jnp.tile(g_dk, (1, 2)),
                    lam_q1, lam_k1, lam_q2, lam_k2, g_sub)
    ym = _mem_attn(qm.reshape(b, s, MEM_WIDTH), km.reshape(b, MEM_LEN, MEM_WIDTH),
                   vm.reshape(b, MEM_LEN, MEM_WIDTH), g_mq, g_mem_out)

    x1 = _out_proj(x2, yc, yd.reshape(t, DIFF_WIDTH), ym.reshape(t, MEM_WIDTH),
                   w_o[0].astype(BF16))
    out = _ffn(x1, g_ffn, w_gate[0].astype(BF16), w_up[0].astype(BF16),
               w_down[0].astype(BF16))
    return out.reshape(b, s, d)
```

```python
import functools

import jax
import jax.numpy as jnp
import numpy as np
from jax import lax
from jax.experimental import pallas as pl
from jax.experimental.pallas import tpu as pltpu

F32 = jnp.float32
BF16 = jnp.bfloat16

D_MODEL = 2048
SEQ = 2048
MEM_LEN = 256
CONV_WIDTH = 512
CONV_K = 3
DIFF_WIDTH = 1024
DIFF_VDIM = 128
DIFF_HALF = 64
DIFF_HEADS = 8
MEM_WIDTH = 512
MEM_HEADS = 4
MEM_HEAD_DIM = 128
IN_WIDTH = 3 * CONV_WIDTH + 3 * DIFF_WIDTH + MEM_WIDTH
ROT_DIM = 16
ROPE_THETA = 500000.0
FFN_HIDDEN = 5632
EPS = 1e-6
LAMBDA_INIT = 0.8 - 0.6 * float(np.exp(-0.3 * 0))

V7X_LANES = 128
V7X_SUBLANES = 8
V7X_VMEM_LIMIT_BYTES = 56 * 1024 * 1024

TM_PROJ = 256
TM_MEMKV = 256
TQ = 256
TK = 256
TQ_MEM = 512
TM_OUT = 512
TM_FFN = 512
TH_FFN = 512

_OFF_U = 0
_OFF_C = CONV_WIDTH
_OFF_B = 2 * CONV_WIDTH
_OFF_Q = 3 * CONV_WIDTH
_OFF_K = _OFF_Q + DIFF_WIDTH
_OFF_V = _OFF_K + DIFF_WIDTH
_OFF_QM = _OFF_V + DIFF_WIDTH


def _rms_scale(t, width):
    return lax.rsqrt(jnp.sum(t * t, axis=-1, keepdims=True) * (1.0 / width) + EPS)


def _in_proj_kernel(x_ref, g_ref, w_ref, convw_ref, gconv_ref,
                    yconv_ref, q_ref, k_ref, v_ref, qm_ref, z_scr, *, tiles_per_seq):
    i = pl.program_id(0)
    tm = x_ref.shape[0]
    halo = V7X_SUBLANES

    x = x_ref[...]
    h = (x * _rms_scale(x, D_MODEL) * g_ref[...]).astype(BF16)

    def proj(off, width):
        return jnp.dot(h, w_ref[:, off:off + width], preferred_element_type=F32)

    q_ref[...] = proj(_OFF_Q, DIFF_WIDTH)
    k_ref[...] = proj(_OFF_K, DIFF_WIDTH)
    v_ref[...] = proj(_OFF_V, DIFF_WIDTH).astype(BF16)
    qm_ref[...] = proj(_OFF_QM, MEM_WIDTH)

    @pl.when(i % tiles_per_seq == 0)
    def _():
        z_scr[0:halo, :] = jnp.zeros((halo, CONV_WIDTH), F32)

    z = proj(_OFF_C, CONV_WIDTH) * proj(_OFF_U, CONV_WIDTH)
    z_scr[halo:halo + tm, :] = z
    z1 = z_scr[halo - 1:halo - 1 + tm, :]
    z2 = z_scr[halo - 2:halo - 2 + tm, :]
    cw = convw_ref[...]
    conv = cw[0:1, :] * z2 + cw[1:2, :] * z1 + cw[2:3, :] * z
    y = proj(_OFF_B, CONV_WIDTH) * conv
    yconv_ref[...] = (y * _rms_scale(y, CONV_WIDTH) * gconv_ref[...]).astype(BF16)
    z_scr[0:halo, :] = z_scr[tm:tm + halo, :]


def _in_proj(x2, g_mix, w_in, conv_w, g_conv):
    t = x2.shape[0]
    tm = TM_PROJ
    grid = (t // tm,)
    row = lambda i: (i, 0)
    fixed = lambda i: (0, 0)
    return pl.pallas_call(
        functools.partial(_in_proj_kernel, tiles_per_seq=SEQ // tm),
        grid=grid,
        in_specs=[
            pl.BlockSpec((tm, D_MODEL), row),
            pl.BlockSpec((1, D_MODEL), fixed),
            pl.BlockSpec((D_MODEL, IN_WIDTH), fixed, pipeline_mode=pl.Buffered(1)),
            pl.BlockSpec((CONV_K, CONV_WIDTH), fixed),
            pl.BlockSpec((1, CONV_WIDTH), fixed),
        ],
        out_specs=[
            pl.BlockSpec((tm, CONV_WIDTH), row),
            pl.BlockSpec((tm, DIFF_WIDTH), row),
            pl.BlockSpec((tm, DIFF_WIDTH), row),
            pl.BlockSpec((tm, DIFF_WIDTH), row),
            pl.BlockSpec((tm, MEM_WIDTH), row),
        ],
        out_shape=[
            jax.ShapeDtypeStruct((t, CONV_WIDTH), BF16),
            jax.ShapeDtypeStruct((t, DIFF_WIDTH), F32),
            jax.ShapeDtypeStruct((t, DIFF_WIDTH), F32),
            jax.ShapeDtypeStruct((t, DIFF_WIDTH), BF16),
            jax.ShapeDtypeStruct((t, MEM_WIDTH), F32),
        ],
        scratch_shapes=[pltpu.VMEM((tm + 2 * V7X_SUBLANES, CONV_WIDTH), F32)],
        compiler_params=pltpu.CompilerParams(
            dimension_semantics=("arbitrary",),
            vmem_limit_bytes=V7X_VMEM_LIMIT_BYTES),
        name="in_proj",
    )(x2, g_mix, w_in, conv_w, g_conv)


def _mem_kv_kernel(mem_ref, g_ref, w_ref, gk_ref, k_ref, v_ref):
    x = mem_ref[...]
    h = (x * _rms_scale(x, D_MODEL) * g_ref[...]).astype(BF16)
    kv = jnp.dot(h, w_ref[...], preferred_element_type=F32)
    gk = gk_ref[...]
    for hd in range(MEM_HEADS):
        lo = hd * MEM_HEAD_DIM
        kh = kv[:, lo:lo + MEM_HEAD_DIM]
        k_ref[:, lo:lo + MEM_HEAD_DIM] = (kh * _rms_scale(kh, MEM_HEAD_DIM) * gk).astype(BF16)
    v_ref[...] = kv[:, MEM_WIDTH:].astype(BF16)


def _mem_kv(mem2, g_mem, w_kv, g_mk):
    t = mem2.shape[0]
    tm = TM_MEMKV
    row = lambda i: (i, 0)
    fixed = lambda i: (0, 0)
    return pl.pallas_call(
        _mem_kv_kernel,
        grid=(t // tm,),
        in_specs=[
            pl.BlockSpec((tm, D_MODEL), row),
            pl.BlockSpec((1, D_MODEL), fixed),
            pl.BlockSpec((D_MODEL, 2 * MEM_WIDTH), fixed),
            pl.BlockSpec((1, MEM_HEAD_DIM), fixed),
        ],
        out_specs=[pl.BlockSpec((tm, MEM_WIDTH), row), pl.BlockSpec((tm, MEM_WIDTH), row)],
        out_shape=[jax.ShapeDtypeStruct((t, MEM_WIDTH), BF16),
                   jax.ShapeDtypeStruct((t, MEM_WIDTH), BF16)],
        compiler_params=pltpu.CompilerParams(
            dimension_semantics=("parallel",),
            vmem_limit_bytes=V7X_VMEM_LIMIT_BYTES),
        name="mem_kv",
    )(mem2, g_mem, w_kv, g_mk)


def _qk_norm_rope(t, g2, cos_t, sin_a, sin_b):
    lane = lax.broadcasted_iota(jnp.int32, t.shape, 1)
    is_lo = lane < DIFF_HALF
    t2 = t * t
    s_lo = jnp.sum(jnp.where(is_lo, t2, 0.0), axis=-1, keepdims=True)
    s_hi = jnp.sum(jnp.where(is_lo, 0.0, t2), axis=-1, keepdims=True)
    r = jnp.where(is_lo,
                  lax.rsqrt(s_lo * (1.0 / DIFF_HALF) + EPS),
                  lax.rsqrt(s_hi * (1.0 / DIFF_HALF) + EPS))
    tn = t * r * g2
    half = ROT_DIM // 2
    return (tn * cos_t
            + pltpu.roll(tn, V7X_LANES - half, axis=1) * sin_a
            + pltpu.roll(tn, half, axis=1) * sin_b)


def _diff_attn_kernel(q_ref, k_ref, v_ref, cq_ref, saq_ref, sbq_ref, ck_ref, sak_ref, sbk_ref,
                      gq_ref, gk_ref, lq1_ref, lk1_ref, lq2_ref, lk2_ref, gsub_ref,
                      o_ref, kn_scr, vt_scr, acc_scr):
    i = pl.program_id(2)
    tq = q_ref.shape[1]
    nk, tk, _ = kn_scr.shape

    @pl.when(i == 0)
    def _():
        gk = gk_ref[...]
        for jj in range(nk):
            rows = slice(jj * tk, (jj + 1) * tk)
            kn = _qk_norm_rope(k_ref[0, rows, :], gk, ck_ref[0, rows, :],
                               sak_ref[0, rows, :], sbk_ref[0, rows, :])
            kn_scr[jj] = kn.astype(BF16)
            vt_scr[jj] = v_ref[0, rows, :].astype(F32).T.astype(BF16)

    qn = _qk_norm_rope(q_ref[0], gq_ref[...], cq_ref[0], saq_ref[0], sbq_ref[0])
    qt = (qn * (DIFF_HALF ** -0.5)).T
    feat = lax.broadcasted_iota(jnp.int32, qt.shape, 0)
    is_lo = feat < DIFF_HALF
    qq = jnp.concatenate([jnp.where(is_lo, qt, 0.0), jnp.where(is_lo, 0.0, qt)],
                         axis=1).astype(BF16)

    acc_scr[...] = jnp.zeros(acc_scr.shape, F32)

    def block(j, m_prev, l_prev, masked):
        s = jnp.dot(kn_scr[j], qq, preferred_element_type=F32)
        if masked:
            key = lax.broadcasted_iota(jnp.int32, (tk, tq), 0)
            qry = lax.broadcasted_iota(jnp.int32, (tk, tq), 1)
            keep = jnp.concatenate([key <= qry, key <= qry], axis=1)
            s = jnp.where(keep, s, -jnp.inf)
        m_new = jnp.maximum(m_prev, jnp.max(s, axis=0, keepdims=True))
        alpha = jnp.exp(m_prev - m_new)
        p = jnp.exp(s - m_new)
        l_new = alpha * l_prev + jnp.sum(p, axis=0, keepdims=True)
        acc_scr[...] = alpha * acc_scr[...] + jnp.dot(vt_scr[j], p.astype(BF16),
                                                      preferred_element_type=F32)
        return m_new, l_new

    def body(j, carry):
        return block(j, *carry, masked=False)

    m0 = jnp.full((1, 2 * tq), -jnp.inf, F32)
    l0 = jnp.zeros((1, 2 * tq), F32)
    m1, l1 = lax.fori_loop(0, i, body, (m0, l0))
    _, l2 = block(i, m1, l1, masked=True)

    lam = (jnp.exp(jnp.sum(lq1_ref[...] * lk1_ref[...], axis=-1, keepdims=True))
           - jnp.exp(jnp.sum(lq2_ref[...] * lk2_ref[...], axis=-1, keepdims=True))
           + LAMBDA_INIT)
    o_all = acc_scr[...] / l2
    o = (o_all[:, :tq] - lam * o_all[:, tq:]).T
    y = o * _rms_scale(o, DIFF_VDIM) * gsub_ref[...] * (1.0 - LAMBDA_INIT)
    o_ref[0] = y.astype(o_ref.dtype)


def _diff_attn(q3, k3, v3, cos_t, sin_a, sin_b, gq2, gk2, lq1, lk1, lq2, lk2, g_sub):
    b, s, _ = q3.shape
    tq, tk = TQ, TK
    assert tq == tk
    grid = (b, DIFF_HEADS, s // tq)
    q_tile = lambda bb, h, i: (bb, i, h)
    kv_full = lambda bb, h, i: (bb, 0, h)
    tab_tile = lambda bb, h, i: (bb, i, 0)
    tab_full = lambda bb, h, i: (bb, 0, 0)
    fixed = lambda bb, h, i: (0, 0)
    vec = lambda n: pl.BlockSpec((1, n), fixed)
    return pl.pallas_call(
        _diff_attn_kernel,
        grid=grid,
        in_specs=[
            pl.BlockSpec((1, tq, DIFF_VDIM), q_tile),
            pl.BlockSpec((1, s, DIFF_VDIM), kv_full),
            pl.BlockSpec((1, s, DIFF_VDIM), kv_full),
            pl.BlockSpec((1, tq, V7X_LANES), tab_tile),
            pl.BlockSpec((1, tq, V7X_LANES), tab_tile),
            pl.BlockSpec((1, tq, V7X_LANES), tab_tile),
            pl.BlockSpec((1, s, V7X_LANES), tab_full),
            pl.BlockSpec((1, s, V7X_LANES), tab_full),
            pl.BlockSpec((1, s, V7X_LANES), tab_full),
            vec(DIFF_VDIM), vec(DIFF_VDIM),
            vec(DIFF_HALF), vec(DIFF_HALF), vec(DIFF_HALF), vec(DIFF_HALF),
            vec(DIFF_VDIM),
        ],
        out_specs=pl.BlockSpec((1, tq, DIFF_VDIM), q_tile),
        out_shape=jax.ShapeDtypeStruct((b, s, DIFF_WIDTH), BF16),
        scratch_shapes=[
            pltpu.VMEM((s // tk, tk, DIFF_VDIM), BF16),
            pltpu.VMEM((s // tk, DIFF_VDIM, tk), BF16),
            pltpu.VMEM((DIFF_VDIM, 2 * tq), F32),
        ],
        compiler_params=pltpu.CompilerParams(
            dimension_semantics=("parallel", "parallel", "arbitrary"),
            vmem_limit_bytes=V7X_VMEM_LIMIT_BYTES),
        name="diff_attn",
    )(q3, k3, v3, cos_t, sin_a, sin_b, cos_t, sin_a, sin_b,
      gq2, gk2, lq1, lk1, lq2, lk2, g_sub)


def _mem_attn_kernel(q_ref, k_ref, v_ref, gq_ref, gout_ref, o_ref):
    gq = gq_ref[...]
    outs = []
    for hd in range(MEM_HEADS):
        cols = slice(hd * MEM_HEAD_DIM, (hd + 1) * MEM_HEAD_DIM)
        qh = q_ref[0, :, cols]
        qh = (qh * _rms_scale(qh, MEM_HEAD_DIM) * gq * (MEM_HEAD_DIM ** -0.5)).astype(BF16)
        s = lax.dot_general(qh, k_ref[0, :, cols], (((1,), (1,)), ((), ())),
                            preferred_element_type=F32)
        e = jnp.exp(s - jnp.max(s, axis=-1, keepdims=True))
        denom = jnp.sum(e, axis=-1, keepdims=True)
        oh = jnp.dot(e.astype(BF16), v_ref[0, :, cols], preferred_element_type=F32)
        outs.append(oh / denom)
    o = jnp.concatenate(outs, axis=-1)
    o_ref[0] = (o * _rms_scale(o, MEM_WIDTH) * gout_ref[...]).astype(o_ref.dtype)


def _mem_attn(qm3, km3, vm3, g_mq, g_out):
    b, s, _ = qm3.shape
    tq = TQ_MEM
    tile = lambda bb, i: (bb, i, 0)
    full = lambda bb, i: (bb, 0, 0)
    fixed = lambda bb, i: (0, 0)
    return pl.pallas_call(
        _mem_attn_kernel,
        grid=(b, s // tq),
        in_specs=[
            pl.BlockSpec((1, tq, MEM_WIDTH), tile),
            pl.BlockSpec((1, MEM_LEN, MEM_WIDTH), full),
            pl.BlockSpec((1, MEM_LEN, MEM_WIDTH), full),
            pl.BlockSpec((1, MEM_HEAD_DIM), fixed),
            pl.BlockSpec((1, MEM_WIDTH), fixed),
        ],
        out_specs=pl.BlockSpec((1, tq, MEM_WIDTH), tile),
        out_shape=jax.ShapeDtypeStruct((b, s, MEM_WIDTH), BF16),
        compiler_params=pltpu.CompilerParams(
            dimension_semantics=("parallel", "parallel"),
            vmem_limit_bytes=V7X_VMEM_LIMIT_BYTES),
        name="mem_attn",
    )(qm3, km3, vm3, g_mq, g_out)


def _out_proj_kernel(x_ref, yc_ref, yd_ref, ym_ref, w_ref, o_ref):
    acc = jnp.dot(yc_ref[...], w_ref[0:CONV_WIDTH, :], preferred_element_type=F32)
    acc += jnp.dot(yd_ref[...], w_ref[CONV_WIDTH:CONV_WIDTH + DIFF_WIDTH, :],
                   preferred_element_type=F32)
    acc += jnp.dot(ym_ref[...], w_ref[CONV_WIDTH + DIFF_WIDTH:, :],
                   preferred_element_type=F32)
    o_ref[...] = x_ref[...] + acc


def _out_proj(x2, yc, yd, ym, w_o):
    t = x2.shape[0]
    tm = TM_OUT
    row = lambda i: (i, 0)
    fixed = lambda i: (0, 0)
    return pl.pallas_call(
        _out_proj_kernel,
        grid=(t // tm,),
        in_specs=[
            pl.BlockSpec((tm, D_MODEL), row),
            pl.BlockSpec((tm, CONV_WIDTH), row),
            pl.BlockSpec((tm, DIFF_WIDTH), row),
            pl.BlockSpec((tm, MEM_WIDTH), row),
            pl.BlockSpec((D_MODEL, D_MODEL), fixed, pipeline_mode=pl.Buffered(1)),
        ],
        out_specs=pl.BlockSpec((tm, D_MODEL), row),
        out_shape=jax.ShapeDtypeStruct((t, D_MODEL), F32),
        compiler_params=pltpu.CompilerParams(
            dimension_semantics=("parallel",),
            vmem_limit_bytes=V7X_VMEM_LIMIT_BYTES),
        name="out_proj",
    )(x2, yc, yd, ym, w_o)


def _ffn_kernel(x_ref, g_ref, wg_ref, wu_ref, wd_ref, o_ref, h_scr, acc_scr):
    j = pl.program_id(1)

    @pl.when(j == 0)
    def _():
        x = x_ref[...]
        h_scr[...] = (x * _rms_scale(x, D_MODEL) * g_ref[...]).astype(BF16)
        acc_scr[...] = jnp.zeros(acc_scr.shape, F32)

    h = h_scr[...]
    gate = jnp.dot(h, wg_ref[...], preferred_element_type=F32)
    up = jnp.dot(h, wu_ref[...], preferred_element_type=F32)
    act = (gate * (1.0 / (1.0 + jnp.exp(-gate))) * up).astype(BF16)
    acc_scr[...] += jnp.dot(act, wd_ref[...], preferred_element_type=F32)

    @pl.when(j == pl.num_programs(1) - 1)
    def _():
        o_ref[...] = x_ref[...] + acc_scr[...]


def _ffn(x2, g_ffn, w_gate, w_up, w_down):
    t = x2.shape[0]
    tm, th = TM_FFN, TH_FFN
    return pl.pallas_call(
        _ffn_kernel,
        grid=(t // tm, FFN_HIDDEN // th),
        in_specs=[
            pl.BlockSpec((tm, D_MODEL), lambda i, j: (i, 0)),
            pl.BlockSpec((1, D_MODEL), lambda i, j: (0, 0)),
            pl.BlockSpec((D_MODEL, th), lambda i, j: (0, j)),
            pl.BlockSpec((D_MODEL, th), lambda i, j: (0, j)),
            pl.BlockSpec((th, D_MODEL), lambda i, j: (j, 0)),
        ],
        out_specs=pl.BlockSpec((tm, D_MODEL), lambda i, j: (i, 0)),
        out_shape=jax.ShapeDtypeStruct((t, D_MODEL), F32),
        scratch_shapes=[pltpu.VMEM((tm, D_MODEL), BF16), pltpu.VMEM((tm, D_MODEL), F32)],
        compiler_params=pltpu.CompilerParams(
            dimension_semantics=("parallel", "arbitrary"),
            vmem_limit_bytes=V7X_VMEM_LIMIT_BYTES),
        name="ffn",
    )(x2, g_ffn, w_gate, w_up, w_down)


def _rope_tables(positions):
    half = ROT_DIM // 2
    inv_freq = ROPE_THETA ** (-jnp.arange(0, ROT_DIM, 2, dtype=F32) / ROT_DIM)
    ang = positions.astype(F32)[..., None] * inv_freq
    cos, sin = jnp.cos(ang), jnp.sin(ang)
    b, s = positions.shape
    one = jnp.ones((b, s, DIFF_HALF - ROT_DIM), F32)
    zero = jnp.zeros((b, s, DIFF_HALF - ROT_DIM), F32)
    zh = jnp.zeros((b, s, half), F32)
    cos_t = jnp.concatenate([cos, cos, one, cos, cos, one], axis=-1)
    sin_a = jnp.concatenate([-sin, zh, zero, -sin, zh, zero], axis=-1)
    sin_b = jnp.concatenate([zh, sin, zero, zh, sin, zero], axis=-1)
    return cos_t, sin_a, sin_b


def kernel(x, mem, positions, g_mix, g_mem, w_in, conv_w, g_conv_out, g_dq, g_dk,
           lam_q1, lam_k1, lam_q2, lam_k2, g_sub, w_mem_kv, g_mq, g_mk, g_mem_out,
           w_o, g_ffn, w_gate, w_up, w_down):
    b, s, d = x.shape
    assert (s, d) == (SEQ, D_MODEL) and mem.shape == (b, MEM_LEN, D_MODEL)
    assert g_mix.shape[0] == 1
    t = b * s
    x2 = x.reshape(t, d)
    mem2 = mem.reshape(b * MEM_LEN, d)
    cos_t, sin_a, sin_b = _rope_tables(positions)

    yc, q, k, v, qm = _in_proj(x2, g_mix, w_in[0].astype(BF16), conv_w[0], g_conv_out)
    km, vm = _mem_kv(mem2, g_mem, w_mem_kv[0].astype(BF16), g_mk)

    yd = _diff_attn(q.reshape(b, s, DIFF_WIDTH), k.reshape(b, s, DIFF_WIDTH),
                    v.reshape(b, s, DIFF_WIDTH), cos_t, sin_a, sin_b,
                    jnp.tile(g_dq, (1, 2)), jnp.tile(g_dk, (1, 2)),
                    lam_q1, lam_k1, lam_q2, lam_k2, g_sub)
    ym = _mem_attn(qm.reshape(b, s, MEM_WIDTH), km.reshape(b, MEM_LEN, MEM_WIDTH),
                   vm.reshape(b, MEM_LEN, MEM_WIDTH), g_mq, g_mem_out)

    x1 = _out_proj(x2, yc, yd.reshape(t, DIFF_WIDTH), ym.reshape(t, MEM_WIDTH),
                   w_o[0].astype(BF16))
    out = _ffn(x1, g_ffn, w_gate[0].astype(BF16), w_up[0].astype(BF16),
               w_down[0].astype(BF16))
    return out.reshape(b, s, d)
```

```python
import functools

import jax
import jax.numpy as jnp
import numpy as np
from jax import lax
from jax.experimental import pallas as pl
from jax.experimental.pallas import tpu as pltpu

F32 = jnp.float32
BF16 = jnp.bfloat16

D_MODEL = 2048
SEQ = 2048
MEM_LEN = 256
CONV_WIDTH = 512
CONV_K = 3
DIFF_WIDTH = 1024
DIFF_VDIM = 128
DIFF_HALF = 64
DIFF_HEADS = 8
MEM_WIDTH = 512
MEM_HEADS = 4
MEM_HEAD_DIM = 128
IN_WIDTH = 3 * CONV_WIDTH + 3 * DIFF_WIDTH + MEM_WIDTH
ROT_DIM = 16
ROPE_THETA = 500000.0
FFN_HIDDEN = 5632
EPS = 1e-6
LAMBDA_INIT = 0.8 - 0.6 * float(np.exp(-0.3 * 0))

V7X_LANES = 128
V7X_SUBLANES = 8
V7X_VMEM_LIMIT_BYTES = 56 * 1024 * 1024

TM_PROJ = 256
TM_MEMKV = 256
TQ_MEM = 512
TM_OUT = 512
TM_FFN = 512
TH_FFN = 512

_OFF_U = 0
_OFF_C = CONV_WIDTH
_OFF_B = 2 * CONV_WIDTH
_OFF_Q = 3 * CONV_WIDTH
_OFF_K = _OFF_Q + DIFF_WIDTH
_OFF_V = _OFF_K + DIFF_WIDTH
_OFF_QM = _OFF_V + DIFF_WIDTH


def _rms_scale(t, width):
    return lax.rsqrt(jnp.sum(t * t, axis=-1, keepdims=True) * (1.0 / width) + EPS)


def _qk_norm_rope(t, g2, cos_t, sin_a, sin_b):
    lane = lax.broadcasted_iota(jnp.int32, t.shape, 1)
    is_lo = lane < DIFF_HALF
    t2 = t * t
    s_lo = jnp.sum(jnp.where(is_lo, t2, 0.0), axis=-1, keepdims=True)
    s_hi = jnp.sum(jnp.where(is_lo, 0.0, t2), axis=-1, keepdims=True)
    r = jnp.where(is_lo,
                  lax.rsqrt(s_lo * (1.0 / DIFF_HALF) + EPS),
                  lax.rsqrt(s_hi * (1.0 / DIFF_HALF) + EPS))
    tn = t * r * g2
    half = ROT_DIM // 2
    return (tn * cos_t
            + pltpu.roll(tn, V7X_LANES - half, axis=1) * sin_a
            + pltpu.roll(tn, half, axis=1) * sin_b)


def _in_proj_kernel(x_ref, g_ref, w_ref, convw_ref, gconv_ref, cos_ref, sa_ref, sb_ref,
                    gq_ref, gk_ref,
                    yconv_ref, qt_ref, kn_ref, vt_ref, qm_ref, z_scr, *, tiles_per_seq):
    i = pl.program_id(0)
    tm = x_ref.shape[0]
    halo = V7X_SUBLANES

    x = x_ref[...]
    h = (x * _rms_scale(x, D_MODEL) * g_ref[...]).astype(BF16)

    def proj(off, width):
        return jnp.dot(h, w_ref[:, off:off + width], preferred_element_type=F32)

    qf = proj(_OFF_Q, DIFF_WIDTH)
    kf = proj(_OFF_K, DIFF_WIDTH)
    vf = proj(_OFF_V, DIFF_WIDTH)
    cos_t, sin_a, sin_b = cos_ref[...], sa_ref[...], sb_ref[...]
    gq, gk = gq_ref[...], gk_ref[...]
    for hd in range(DIFF_HEADS):
        cols = slice(hd * DIFF_VDIM, (hd + 1) * DIFF_VDIM)
        qn = _qk_norm_rope(qf[:, cols], gq, cos_t, sin_a, sin_b) * (DIFF_HALF ** -0.5)
        qt_ref[0, 0, cols, :] = qn.T.astype(BF16)
        kn_ref[:, cols] = _qk_norm_rope(kf[:, cols], gk, cos_t, sin_a, sin_b).astype(BF16)
        vt_ref[0, 0, cols, :] = vf[:, cols].T.astype(BF16)
    qm_ref[...] = proj(_OFF_QM, MEM_WIDTH)

    @pl.when(i % tiles_per_seq == 0)
    def _():
        z_scr[0:halo, :] = jnp.zeros((halo, CONV_WIDTH), F32)

    z = proj(_OFF_C, CONV_WIDTH) * proj(_OFF_U, CONV_WIDTH)
    z_scr[halo:halo + tm, :] = z
    z1 = z_scr[halo - 1:halo - 1 + tm, :]
    z2 = z_scr[halo - 2:halo - 2 + tm, :]
    cw = convw_ref[...]
    conv = cw[0:1, :] * z2 + cw[1:2, :] * z1 + cw[2:3, :] * z
    y = proj(_OFF_B, CONV_WIDTH) * conv
    yconv_ref[...] = (y * _rms_scale(y, CONV_WIDTH) * gconv_ref[...]).astype(BF16)
    z_scr[0:halo, :] = z_scr[tm:tm + halo, :]


def _in_proj(x2, g_mix, w_in, conv_w, g_conv, cos_t, sin_a, sin_b, gq2, gk2):
    t = x2.shape[0]
    tm = TM_PROJ
    tps = SEQ // tm
    nb = t // SEQ
    grid = (t // tm,)
    row = lambda i: (i, 0)
    fixed = lambda i: (0, 0)
    tile_t = lambda i: (i // tps, i % tps, 0, 0)
    return pl.pallas_call(
        functools.partial(_in_proj_kernel, tiles_per_seq=tps),
        grid=grid,
        in_specs=[
            pl.BlockSpec((tm, D_MODEL), row),
            pl.BlockSpec((1, D_MODEL), fixed),
            pl.BlockSpec((D_MODEL, IN_WIDTH), fixed, pipeline_mode=pl.Buffered(1)),
            pl.BlockSpec((CONV_K, CONV_WIDTH), fixed),
            pl.BlockSpec((1, CONV_WIDTH), fixed),
            pl.BlockSpec((tm, V7X_LANES), row),
            pl.BlockSpec((tm, V7X_LANES), row),
            pl.BlockSpec((tm, V7X_LANES), row),
            pl.BlockSpec((1, DIFF_VDIM), fixed),
            pl.BlockSpec((1, DIFF_VDIM), fixed),
        ],
        out_specs=[
            pl.BlockSpec((tm, CONV_WIDTH), row),
            pl.BlockSpec((1, 1, DIFF_WIDTH, tm), tile_t),
            pl.BlockSpec((tm, DIFF_WIDTH), row),
            pl.BlockSpec((1, 1, DIFF_WIDTH, tm), tile_t),
            pl.BlockSpec((tm, MEM_WIDTH), row),
        ],
        out_shape=[
            jax.ShapeDtypeStruct((t, CONV_WIDTH), BF16),
            jax.ShapeDtypeStruct((nb, tps, DIFF_WIDTH, tm), BF16),
            jax.ShapeDtypeStruct((t, DIFF_WIDTH), BF16),
            jax.ShapeDtypeStruct((nb, tps, DIFF_WIDTH, tm), BF16),
            jax.ShapeDtypeStruct((t, MEM_WIDTH), F32),
        ],
        scratch_shapes=[pltpu.VMEM((tm + V7X_SUBLANES, CONV_WIDTH), F32)],
        compiler_params=pltpu.CompilerParams(
            dimension_semantics=("arbitrary",),
            vmem_limit_bytes=V7X_VMEM_LIMIT_BYTES),
        name="in_proj",
    )(x2, g_mix, w_in, conv_w, g_conv, cos_t, sin_a, sin_b, gq2, gk2)


def _mem_kv_kernel(mem_ref, g_ref, w_ref, gk_ref, k_ref, v_ref):
    x = mem_ref[...]
    h = (x * _rms_scale(x, D_MODEL) * g_ref[...]).astype(BF16)
    kv = jnp.dot(h, w_ref[...], preferred_element_type=F32)
    gk = gk_ref[...]
    for hd in range(MEM_HEADS):
        lo = hd * MEM_HEAD_DIM
        kh = kv[:, lo:lo + MEM_HEAD_DIM]
        k_ref[:, lo:lo + MEM_HEAD_DIM] = (kh * _rms_scale(kh, MEM_HEAD_DIM) * gk).astype(BF16)
    v_ref[...] = kv[:, MEM_WIDTH:].astype(BF16)


def _mem_kv(mem2, g_mem, w_kv, g_mk):
    t = mem2.shape[0]
    tm = TM_MEMKV
    row = lambda i: (i, 0)
    fixed = lambda i: (0, 0)
    return pl.pallas_call(
        _mem_kv_kernel,
        grid=(t // tm,),
        in_specs=[
            pl.BlockSpec((tm, D_MODEL), row),
            pl.BlockSpec((1, D_MODEL), fixed),
            pl.BlockSpec((D_MODEL, 2 * MEM_WIDTH), fixed),
            pl.BlockSpec((1, MEM_HEAD_DIM), fixed),
        ],
        out_specs=[pl.BlockSpec((tm, MEM_WIDTH), row), pl.BlockSpec((tm, MEM_WIDTH), row)],
        out_shape=[jax.ShapeDtypeStruct((t, MEM_WIDTH), BF16),
                   jax.ShapeDtypeStruct((t, MEM_WIDTH), BF16)],
        compiler_params=pltpu.CompilerParams(
            dimension_semantics=("parallel",),
            vmem_limit_bytes=V7X_VMEM_LIMIT_BYTES),
        name="mem_kv",
    )(mem2, g_mem, w_kv, g_mk)


def _diff_attn_kernel(qt_ref, kn_ref, vt_ref, lq1_ref, lk1_ref, lq2_ref, lk2_ref, gsub_ref,
                      o_ref):
    nq, tq = qt_ref.shape[1], qt_ref.shape[3]
    tk = vt_ref.shape[3]

    lam = (jnp.exp(jnp.sum(lq1_ref[...] * lk1_ref[...], axis=-1, keepdims=True))
           - jnp.exp(jnp.sum(lq2_ref[...] * lk2_ref[...], axis=-1, keepdims=True))
           + LAMBDA_INIT)
    key = lax.broadcasted_iota(jnp.int32, (tk, tq), 0)
    qry = lax.broadcasted_iota(jnp.int32, (tk, tq), 1)
    causal = jnp.concatenate([key <= qry, key <= qry], axis=1)

    def scores(c, j):
        qt = qt_ref[0, c]
        is_lo = lax.broadcasted_iota(jnp.int32, qt.shape, 0) < DIFF_HALF
        zero = jnp.zeros_like(qt)
        qq = jnp.concatenate([jnp.where(is_lo, qt, zero), jnp.where(is_lo, zero, qt)], axis=1)
        return jnp.dot(kn_ref[0, j * tk:(j + 1) * tk, :], qq, preferred_element_type=F32)

    steps = [(c, j) for c in range(nq) for j in range(c + 1)]
    s_next = scores(*steps[0])
    m = l = acc = None
    for n, (c, j) in enumerate(steps):
        s = s_next
        if n + 1 < len(steps):
            s_next = scores(*steps[n + 1])
        if j == c:
            s = jnp.where(causal, s, -jnp.inf)
        m_blk = jnp.max(s, axis=0, keepdims=True)
        m_new = m_blk if j == 0 else jnp.maximum(m, m_blk)
        p = jnp.exp(s - m_new)
        l_blk = jnp.sum(p, axis=0, keepdims=True)
        pv = jnp.dot(vt_ref[0, j], p.astype(BF16), preferred_element_type=F32)
        if j == 0:
            l, acc = l_blk, pv
        else:
            alpha = jnp.exp(m - m_new)
            l, acc = alpha * l + l_blk, alpha * acc + pv
        m = m_new
        if j == c:
            o_all = acc / l
            o = (o_all[:, :tq] - lam * o_all[:, tq:]).T
            y = o * _rms_scale(o, DIFF_VDIM) * gsub_ref[...] * (1.0 - LAMBDA_INIT)
            o_ref[0, c * tq:(c + 1) * tq, :] = y.astype(o_ref.dtype)


def _diff_attn(qt4, kn3, vt4, lq1, lk1, lq2, lk2, g_sub):
    b, nq, _, tq = qt4.shape
    s = kn3.shape[1]
    nk, tk = vt4.shape[1], vt4.shape[3]
    assert tq == tk and nq == nk
    fixed = lambda bb, h: (0, 0)
    vec = lambda n: pl.BlockSpec((1, n), fixed)
    return pl.pallas_call(
        _diff_attn_kernel,
        grid=(b, DIFF_HEADS),
        in_specs=[
            pl.BlockSpec((1, nq, DIFF_VDIM, tq), lambda bb, h: (bb, 0, h, 0)),
            pl.BlockSpec((1, s, DIFF_VDIM), lambda bb, h: (bb, 0, h)),
            pl.BlockSpec((1, nk, DIFF_VDIM, tk), lambda bb, h: (bb, 0, h, 0)),
            vec(DIFF_HALF), vec(DIFF_HALF), vec(DIFF_HALF), vec(DIFF_HALF),
            vec(DIFF_VDIM),
        ],
        out_specs=pl.BlockSpec((1, s, DIFF_VDIM), lambda bb, h: (bb, 0, h)),
        out_shape=jax.ShapeDtypeStruct((b, s, DIFF_WIDTH), BF16),
        compiler_params=pltpu.CompilerParams(
            dimension_semantics=("parallel", "parallel"),
            vmem_limit_bytes=V7X_VMEM_LIMIT_BYTES),
        name="diff_attn",
    )(qt4, kn3, vt4, lq1, lk1, lq2, lk2, g_sub)


def _mem_attn_kernel(q_ref, k_ref, v_ref, gq_ref, gout_ref, o_ref):
    gq = gq_ref[...]
    outs = []
    for hd in range(MEM_HEADS):
        cols = slice(hd * MEM_HEAD_DIM, (hd + 1) * MEM_HEAD_DIM)
        qh = q_ref[0, :, cols]
        qh = (qh * _rms_scale(qh, MEM_HEAD_DIM) * gq * (MEM_HEAD_DIM ** -0.5)).astype(BF16)
        s = lax.dot_general(qh, k_ref[0, :, cols], (((1,), (1,)), ((), ())),
                            preferred_element_type=F32)
        e = jnp.exp(s - jnp.max(s, axis=-1, keepdims=True))
        denom = jnp.sum(e, axis=-1, keepdims=True)
        oh = jnp.dot(e.astype(BF16), v_ref[0, :, cols], preferred_element_type=F32)
        outs.append(oh / denom)
    o = jnp.concatenate(outs, axis=-1)
    o_ref[0] = (o * _rms_scale(o, MEM_WIDTH) * gout_ref[...]).astype(o_ref.dtype)


def _mem_attn(qm3, km3, vm3, g_mq, g_out):
    b, s, _ = qm3.shape
    tq = TQ_MEM
    tile = lambda bb, i: (bb, i, 0)
    full = lambda bb, i: (bb, 0, 0)
    fixed = lambda bb, i: (0, 0)
    return pl.pallas_call(
        _mem_attn_kernel,
        grid=(b, s // tq),
        in_specs=[
            pl.BlockSpec((1, tq, MEM_WIDTH), tile),
            pl.BlockSpec((1, MEM_LEN, MEM_WIDTH), full),
            pl.BlockSpec((1, MEM_LEN, MEM_WIDTH), full),
            pl.BlockSpec((1, MEM_HEAD_DIM), fixed),
            pl.BlockSpec((1, MEM_WIDTH), fixed),
        ],
        out_specs=pl.BlockSpec((1, tq, MEM_WIDTH), tile),
        out_shape=jax.ShapeDtypeStruct((b, s, MEM_WIDTH), BF16),
        compiler_params=pltpu.CompilerParams(
            dimension_semantics=("parallel", "parallel"),
            vmem_limit_bytes=V7X_VMEM_LIMIT_BYTES),
        name="mem_attn",
    )(qm3, km3, vm3, g_mq, g_out)


def _out_proj_kernel(x_ref, yc_ref, yd_ref, ym_ref, w_ref, o_ref):
    acc = jnp.dot(yc_ref[...], w_ref[0:CONV_WIDTH, :], preferred_element_type=F32)
    acc += jnp.dot(yd_ref[...], w_ref[CONV_WIDTH:CONV_WIDTH + DIFF_WIDTH, :],
                   preferred_element_type=F32)
    acc += jnp.dot(ym_ref[...], w_ref[CONV_WIDTH + DIFF_WIDTH:, :],
                   preferred_element_type=F32)
    o_ref[...] = x_ref[...] + acc


def _out_proj(x2, yc, yd, ym, w_o):
    t = x2.shape[0]
    tm = TM_OUT
    row = lambda i: (i, 0)
    fixed = lambda i: (0, 0)
    return pl.pallas_call(
        _out_proj_kernel,
        grid=(t // tm,),
        in_specs=[
            pl.BlockSpec((tm, D_MODEL), row),
            pl.BlockSpec((tm, CONV_WIDTH), row),
            pl.BlockSpec((tm, DIFF_WIDTH), row),
            pl.BlockSpec((tm, MEM_WIDTH), row),
            pl.BlockSpec((D_MODEL, D_MODEL), fixed, pipeline_mode=pl.Buffered(1)),
        ],
        out_specs=pl.BlockSpec((tm, D_MODEL), row),
        out_shape=jax.ShapeDtypeStruct((t, D_MODEL), F32),
        compiler_params=pltpu.CompilerParams(
            dimension_semantics=("parallel",),
            vmem_limit_bytes=V7X_VMEM_LIMIT_BYTES),
        name="out_proj",
    )(x2, yc, yd, ym, w_o)


def _ffn_kernel(x_ref, g_ref, wg_ref, wu_ref, wd_ref, o_ref, h_scr, acc_scr):
    j = pl.program_id(1)

    @pl.when(j == 0)
    def _():
        x = x_ref[...]
        h_scr[...] = (x * _rms_scale(x, D_MODEL) * g_ref[...]).astype(BF16)
        acc_scr[...] = jnp.zeros(acc_scr.shape, F32)

    h = h_scr[...]
    gate = jnp.dot(h, wg_ref[...], preferred_element_type=F32)
    up = jnp.dot(h, wu_ref[...], preferred_element_type=F32)
    act = (gate * (1.0 / (1.0 + jnp.exp(-gate))) * up).astype(BF16)
    acc_scr[...] += jnp.dot(act, wd_ref[...], preferred_element_type=F32)

    @pl.when(j == pl.num_programs(1) - 1)
    def _():
        o_ref[...] = x_ref[...] + acc_scr[...]


def _ffn(x2, g_ffn, w_gate, w_up, w_down):
    t = x2.shape[0]
    tm, th = TM_FFN, TH_FFN
    return pl.pallas_call(
        _ffn_kernel,
        grid=(t // tm, FFN_HIDDEN // th),
        in_specs=[
            pl.BlockSpec((tm, D_MODEL), lambda i, j: (i, 0)),
            pl.BlockSpec((1, D_MODEL), lambda i, j: (0, 0)),
            pl.BlockSpec((D_MODEL, th), lambda i, j: (0, j)),
            pl.BlockSpec((D_MODEL, th), lambda i, j: (0, j)),
            pl.BlockSpec((th, D_MODEL), lambda i, j: (j, 0)),
        ],
        out_specs=pl.BlockSpec((tm, D_MODEL), lambda i, j: (i, 0)),
        out_shape=jax.ShapeDtypeStruct((t, D_MODEL), F32),
        scratch_shapes=[pltpu.VMEM((tm, D_MODEL), BF16), pltpu.VMEM((tm, D_MODEL), F32)],
        compiler_params=pltpu.CompilerParams(
            dimension_semantics=("parallel", "arbitrary"),
            vmem_limit_bytes=V7X_VMEM_LIMIT_BYTES),
        name="ffn",
    )(x2, g_ffn, w_gate, w_up, w_down)


def _rope_tables(positions):
    half = ROT_DIM // 2
    inv_freq = ROPE_THETA ** (-jnp.arange(0, ROT_DIM, 2, dtype=F32) / ROT_DIM)
    ang = positions.astype(F32)[..., None] * inv_freq
    cos, sin = jnp.cos(ang), jnp.sin(ang)
    b, s = positions.shape
    one = jnp.ones((b, s, DIFF_HALF - ROT_DIM), F32)
    zero = jnp.zeros((b, s, DIFF_HALF - ROT_DIM), F32)
    zh = jnp.zeros((b, s, half), F32)
    cos_t = jnp.concatenate([cos, cos, one, cos, cos, one], axis=-1)
    sin_a = jnp.concatenate([-sin, zh, zero, -sin, zh, zero], axis=-1)
    sin_b = jnp.concatenate([zh, sin, zero, zh, sin, zero], axis=-1)
    return cos_t, sin_a, sin_b


def kernel(x, mem, positions, g_mix, g_mem, w_in, conv_w, g_conv_out, g_dq, g_dk,
           lam_q1, lam_k1, lam_q2, lam_k2, g_sub, w_mem_kv, g_mq, g_mk, g_mem_out,
           w_o, g_ffn, w_gate, w_up, w_down):
    b, s, d = x.shape
    assert (s, d) == (SEQ, D_MODEL) and mem.shape == (b, MEM_LEN, D_MODEL)
    assert g_mix.shape[0] == 1
    t = b * s
    x2 = x.reshape(t, d)
    mem2 = mem.reshape(b * MEM_LEN, d)
    cos_t, sin_a, sin_b = _rope_tables(positions)

    tab = lambda a: a.reshape(t, V7X_LANES)
    yc, qt, kn, vt, qm = _in_proj(x2, g_mix, w_in[0].astype(BF16), conv_w[0], g_conv_out,
                                  tab(cos_t), tab(sin_a), tab(sin_b),
                                  jnp.tile(g_dq, (1, 2)), jnp.tile(g_dk, (1, 2)))
    km, vm = _mem_kv(mem2, g_mem, w_mem_kv[0].astype(BF16), g_mk)

    yd = _diff_attn(qt, kn.reshape(b, s, DIFF_WIDTH), vt,
                    lam_q1, lam_k1, lam_q2, lam_k2, g_sub)
    ym = _mem_attn(qm.reshape(b, s, MEM_WIDTH), km.reshape(b, MEM_LEN, MEM_WIDTH),
                   vm.reshape(b, MEM_LEN, MEM_WIDTH), g_mq, g_mem_out)

    x1 = _out_proj(x2, yc, yd.reshape(t, DIFF_WIDTH), ym.reshape(t, MEM_WIDTH),
                   w_o[0].astype(BF16))
    out = _ffn(x1, g_ffn, w_gate[0].astype(BF16), w_up[0].astype(BF16),
               w_down[0].astype(BF16))
    return out.reshape(b, s, d)
```

```python
import functools

import jax
import jax.numpy as jnp
import numpy as np
from jax import lax
from jax.experimental import pallas as pl
from jax.experimental.pallas import tpu as pltpu

F32 = jnp.float32
BF16 = jnp.bfloat16

D_MODEL = 2048
SEQ = 2048
MEM_LEN = 256
CONV_WIDTH = 512
CONV_K = 3
DIFF_WIDTH = 1024
DIFF_VDIM = 128
DIFF_HALF = 64
DIFF_HEADS = 8
MEM_WIDTH = 512
MEM_HEADS = 4
MEM_HEAD_DIM = 128
IN_WIDTH = 3 * CONV_WIDTH + 3 * DIFF_WIDTH + MEM_WIDTH
ROT_DIM = 16
ROPE_THETA = 500000.0
FFN_HIDDEN = 5632
EPS = 1e-6
LAMBDA_INIT = 0.8 - 0.6 * float(np.exp(-0.3 * 0))
LOG2_E = float(np.log2(np.e))

V7X_LANES = 128
V7X_SUBLANES = 8
V7X_BF16_SUBLANES = 16
V7X_VMEM_LIMIT_BYTES = 56 * 1024 * 1024

TM_PROJ = 256
TM_MEMKV = 256
TQ_MEM = 512
TM_OUT = 512
TM_FFN = 512
TH_FFN = 512

_OFF_U = 0
_OFF_C = CONV_WIDTH
_OFF_B = 2 * CONV_WIDTH
_OFF_Q = 3 * CONV_WIDTH
_OFF_K = _OFF_Q + DIFF_WIDTH
_OFF_V = _OFF_K + DIFF_WIDTH
_OFF_QM = _OFF_V + DIFF_WIDTH


def _rms_scale(t, width):
    return lax.rsqrt(jnp.sum(t * t, axis=-1, keepdims=True) * (1.0 / width) + EPS)


def _qk_norm_rope(t, g2, cos_t, sin_a, sin_b):
    lane = lax.broadcasted_iota(jnp.int32, t.shape, 1)
    is_lo = lane < DIFF_HALF
    t2 = t * t
    s_lo = jnp.sum(jnp.where(is_lo, t2, 0.0), axis=-1, keepdims=True)
    s_hi = jnp.sum(jnp.where(is_lo, 0.0, t2), axis=-1, keepdims=True)
    r = jnp.where(is_lo,
                  lax.rsqrt(s_lo * (1.0 / DIFF_HALF) + EPS),
                  lax.rsqrt(s_hi * (1.0 / DIFF_HALF) + EPS))
    tn = t * r * g2
    half = ROT_DIM // 2
    return (tn * cos_t
            + pltpu.roll(tn, V7X_LANES - half, axis=1) * sin_a
            + pltpu.roll(tn, half, axis=1) * sin_b)


def _in_proj_kernel(x_ref, g_ref, w_ref, convw_ref, gconv_ref, cos_ref, sin_ref,
                    gq_ref, gk_ref,
                    yconv_ref, qt_ref, kn_ref, vt_ref, qm_ref, z_scr, *, tiles_per_seq):
    i = pl.program_id(0)
    tm = x_ref.shape[0]
    halo = V7X_SUBLANES

    x = x_ref[...]
    h = (x * _rms_scale(x, D_MODEL) * g_ref[...]).astype(BF16)

    def proj(off, width):
        return jnp.dot(h, w_ref[:, off:off + width], preferred_element_type=F32)

    qf = proj(_OFF_Q, DIFF_WIDTH)
    kf = proj(_OFF_K, DIFF_WIDTH)
    vf = proj(_OFF_V, DIFF_WIDTH)
    cos_t, sin_t = cos_ref[...], sin_ref[...]
    pos_in_half = lax.broadcasted_iota(jnp.int32, sin_t.shape, 1) % DIFF_HALF
    sin_a = jnp.where(pos_in_half < ROT_DIM // 2, -sin_t, 0.0)
    sin_b = jnp.where((pos_in_half >= ROT_DIM // 2) & (pos_in_half < ROT_DIM), sin_t, 0.0)
    gq, gk = gq_ref[...], gk_ref[...]
    for hd in range(DIFF_HEADS):
        cols = slice(hd * DIFF_VDIM, (hd + 1) * DIFF_VDIM)
        qn = _qk_norm_rope(qf[:, cols], gq, cos_t, sin_a, sin_b) * (DIFF_HALF ** -0.5 * LOG2_E)
        qt_ref[0, 0, cols, :] = qn.T.astype(BF16)
        kn_ref[:, cols] = _qk_norm_rope(kf[:, cols], gk, cos_t, sin_a, sin_b).astype(BF16)
        vt_ref[0, 0, cols, :] = vf[:, cols].T.astype(BF16)
    qm_ref[...] = proj(_OFF_QM, MEM_WIDTH)

    @pl.when(i % tiles_per_seq == 0)
    def _():
        z_scr[0:halo, :] = jnp.zeros((halo, CONV_WIDTH), F32)

    z = proj(_OFF_C, CONV_WIDTH) * proj(_OFF_U, CONV_WIDTH)
    z_scr[halo:halo + tm, :] = z
    z1 = z_scr[halo - 1:halo - 1 + tm, :]
    z2 = z_scr[halo - 2:halo - 2 + tm, :]
    cw = convw_ref[...]
    conv = cw[0:1, :] * z2 + cw[1:2, :] * z1 + cw[2:3, :] * z
    y = proj(_OFF_B, CONV_WIDTH) * conv
    yconv_ref[...] = (y * _rms_scale(y, CONV_WIDTH) * gconv_ref[...]).astype(BF16)
    z_scr[0:halo, :] = z_scr[tm:tm + halo, :]


def _in_proj(x2, g_mix, w_in, conv_w, g_conv, cos_t, sin_t, gq2, gk2):
    t = x2.shape[0]
    tm = TM_PROJ
    tps = SEQ // tm
    nb = t // SEQ
    grid = (t // tm,)
    row = lambda i: (i, 0)
    fixed = lambda i: (0, 0)
    tile_t = lambda i: (i // tps, i % tps, 0, 0)
    return pl.pallas_call(
        functools.partial(_in_proj_kernel, tiles_per_seq=tps),
        grid=grid,
        in_specs=[
            pl.BlockSpec((tm, D_MODEL), row),
            pl.BlockSpec((1, D_MODEL), fixed),
            pl.BlockSpec((D_MODEL, IN_WIDTH), fixed, pipeline_mode=pl.Buffered(1)),
            pl.BlockSpec((CONV_K, CONV_WIDTH), fixed),
            pl.BlockSpec((1, CONV_WIDTH), fixed),
            pl.BlockSpec((tm, V7X_LANES), row),
            pl.BlockSpec((tm, V7X_LANES), row),
            pl.BlockSpec((1, DIFF_VDIM), fixed),
            pl.BlockSpec((1, DIFF_VDIM), fixed),
        ],
        out_specs=[
            pl.BlockSpec((tm, CONV_WIDTH), row),
            pl.BlockSpec((1, 1, DIFF_WIDTH, tm), tile_t),
            pl.BlockSpec((tm, DIFF_WIDTH), row),
            pl.BlockSpec((1, 1, DIFF_WIDTH, tm), tile_t),
            pl.BlockSpec((tm, MEM_WIDTH), row),
        ],
        out_shape=[
            jax.ShapeDtypeStruct((t, CONV_WIDTH), BF16),
            jax.ShapeDtypeStruct((nb, tps, DIFF_WIDTH, tm), BF16),
            jax.ShapeDtypeStruct((t, DIFF_WIDTH), BF16),
            jax.ShapeDtypeStruct((nb, tps, DIFF_WIDTH, tm), BF16),
            jax.ShapeDtypeStruct((t, MEM_WIDTH), F32),
        ],
        scratch_shapes=[pltpu.VMEM((tm + V7X_SUBLANES, CONV_WIDTH), F32)],
        compiler_params=pltpu.CompilerParams(
            dimension_semantics=("arbitrary",),
            vmem_limit_bytes=V7X_VMEM_LIMIT_BYTES),
        name="in_proj",
    )(x2, g_mix, w_in, conv_w, g_conv, cos_t, sin_t, gq2, gk2)


def _mem_kv_kernel(mem_ref, g_ref, w_ref, gk_ref, k_ref, v_ref):
    x = mem_ref[...]
    h = (x * _rms_scale(x, D_MODEL) * g_ref[...]).astype(BF16)
    kv = jnp.dot(h, w_ref[...], preferred_element_type=F32)
    gk = gk_ref[...]
    for hd in range(MEM_HEADS):
        lo = hd * MEM_HEAD_DIM
        kh = kv[:, lo:lo + MEM_HEAD_DIM]
        k_ref[:, lo:lo + MEM_HEAD_DIM] = (kh * _rms_scale(kh, MEM_HEAD_DIM) * gk).astype(BF16)
    v_ref[...] = kv[:, MEM_WIDTH:].astype(BF16)


def _mem_kv(mem2, g_mem, w_kv, g_mk):
    t = mem2.shape[0]
    tm = TM_MEMKV
    row = lambda i: (i, 0)
    fixed = lambda i: (0, 0)
    return pl.pallas_call(
        _mem_kv_kernel,
        grid=(t // tm,),
        in_specs=[
            pl.BlockSpec((tm, D_MODEL), row),
            pl.BlockSpec((1, D_MODEL), fixed),
            pl.BlockSpec((D_MODEL, 2 * MEM_WIDTH), fixed),
            pl.BlockSpec((1, MEM_HEAD_DIM), fixed),
        ],
        out_specs=[pl.BlockSpec((tm, MEM_WIDTH), row), pl.BlockSpec((tm, MEM_WIDTH), row)],
        out_shape=[jax.ShapeDtypeStruct((t, MEM_WIDTH), BF16),
                   jax.ShapeDtypeStruct((t, MEM_WIDTH), BF16)],
        compiler_params=pltpu.CompilerParams(
            dimension_semantics=("parallel",),
            vmem_limit_bytes=V7X_VMEM_LIMIT_BYTES),
        name="mem_kv",
    )(mem2, g_mem, w_kv, g_mk)


def _diff_attn_kernel(qt_ref, kn_ref, vt_ref, lq1_ref, lk1_ref, lq2_ref, lk2_ref, gsub_ref,
                      o_ref):
    nq, tq = qt_ref.shape[1], qt_ref.shape[3]
    tk = vt_ref.shape[3]

    lam = (jnp.exp(jnp.sum(lq1_ref[...] * lk1_ref[...], axis=-1, keepdims=True))
           - jnp.exp(jnp.sum(lq2_ref[...] * lk2_ref[...], axis=-1, keepdims=True))
           + LAMBDA_INIT)
    key = lax.broadcasted_iota(jnp.int32, (tk, tq), 0)
    qry = lax.broadcasted_iota(jnp.int32, (tk, tq), 1)
    causal = jnp.concatenate([key <= qry, key <= qry], axis=1)
    sum_rows = (lax.broadcasted_iota(jnp.int32, (V7X_BF16_SUBLANES, tk), 0) == 0).astype(BF16)

    def scores(c, j):
        qt = qt_ref[0, c]
        is_lo = lax.broadcasted_iota(jnp.int32, qt.shape, 0) < DIFF_HALF
        zero = jnp.zeros_like(qt)
        qq = jnp.concatenate([jnp.where(is_lo, qt, zero), jnp.where(is_lo, zero, qt)], axis=1)
        return jnp.dot(kn_ref[0, j * tk:(j + 1) * tk, :], qq, preferred_element_type=F32)

    steps = [(c, j) for c in range(nq) for j in range(c + 1)]
    s_next = scores(*steps[0])
    m = acc = None
    for n, (c, j) in enumerate(steps):
        s = s_next
        if n + 1 < len(steps):
            s_next = scores(*steps[n + 1])
        if j == c:
            s = jnp.where(causal, s, -jnp.inf)
        m_blk = jnp.max(s, axis=0, keepdims=True)
        m_new = m_blk if j == 0 else jnp.maximum(m, m_blk)
        p = jnp.exp2(s - m_new)
        vt_ext = jnp.concatenate([vt_ref[0, j], sum_rows], axis=0)
        pv = jnp.dot(vt_ext, p.astype(BF16), preferred_element_type=F32)
        acc = pv if j == 0 else jnp.exp2(m - m_new) * acc + pv
        m = m_new
        if j == c:
            o_all = acc[:DIFF_VDIM] / acc[DIFF_VDIM:DIFF_VDIM + 1]
            o = (o_all[:, :tq] - lam * o_all[:, tq:]).T
            y = o * _rms_scale(o, DIFF_VDIM) * gsub_ref[...] * (1.0 - LAMBDA_INIT)
            o_ref[0, c * tq:(c + 1) * tq, :] = y.astype(o_ref.dtype)


def _diff_attn(qt4, kn3, vt4, lq1, lk1, lq2, lk2, g_sub):
    b, nq, _, tq = qt4.shape
    s = kn3.shape[1]
    nk, tk = vt4.shape[1], vt4.shape[3]
    assert tq == tk and nq == nk
    fixed = lambda bb, h: (0, 0)
    vec = lambda n: pl.BlockSpec((1, n), fixed)
    return pl.pallas_call(
        _diff_attn_kernel,
        grid=(b, DIFF_HEADS),
        in_specs=[
            pl.BlockSpec((1, nq, DIFF_VDIM, tq), lambda bb, h: (bb, 0, h, 0)),
            pl.BlockSpec((1, s, DIFF_VDIM), lambda bb, h: (bb, 0, h)),
            pl.BlockSpec((1, nk, DIFF_VDIM, tk), lambda bb, h: (bb, 0, h, 0)),
            vec(DIFF_HALF), vec(DIFF_HALF), vec(DIFF_HALF), vec(DIFF_HALF),
            vec(DIFF_VDIM),
        ],
        out_specs=pl.BlockSpec((1, s, DIFF_VDIM), lambda bb, h: (bb, 0, h)),
        out_shape=jax.ShapeDtypeStruct((b, s, DIFF_WIDTH), BF16),
        compiler_params=pltpu.CompilerParams(
            dimension_semantics=("parallel", "parallel"),
            vmem_limit_bytes=V7X_VMEM_LIMIT_BYTES),
        name="diff_attn",
    )(qt4, kn3, vt4, lq1, lk1, lq2, lk2, g_sub)


def _mem_attn_kernel(q_ref, k_ref, v_ref, gq_ref, gout_ref, o_ref):
    gq = gq_ref[...]
    outs = []
    for hd in range(MEM_HEADS):
        cols = slice(hd * MEM_HEAD_DIM, (hd + 1) * MEM_HEAD_DIM)
        qh = q_ref[0, :, cols]
        qh = (qh * _rms_scale(qh, MEM_HEAD_DIM) * gq * (MEM_HEAD_DIM ** -0.5)).astype(BF16)
        s = lax.dot_general(qh, k_ref[0, :, cols], (((1,), (1,)), ((), ())),
                            preferred_element_type=F32)
        e = jnp.exp(s - jnp.max(s, axis=-1, keepdims=True))
        denom = jnp.sum(e, axis=-1, keepdims=True)
        oh = jnp.dot(e.astype(BF16), v_ref[0, :, cols], preferred_element_type=F32)
        outs.append(oh / denom)
    o = jnp.concatenate(outs, axis=-1)
    o_ref[0] = (o * _rms_scale(o, MEM_WIDTH) * gout_ref[...]).astype(o_ref.dtype)


def _mem_attn(qm3, km3, vm3, g_mq, g_out):
    b, s, _ = qm3.shape
    tq = TQ_MEM
    tile = lambda bb, i: (bb, i, 0)
    full = lambda bb, i: (bb, 0, 0)
    fixed = lambda bb, i: (0, 0)
    return pl.pallas_call(
        _mem_attn_kernel,
        grid=(b, s // tq),
        in_specs=[
            pl.BlockSpec((1, tq, MEM_WIDTH), tile),
            pl.BlockSpec((1, MEM_LEN, MEM_WIDTH), full),
            pl.BlockSpec((1, MEM_LEN, MEM_WIDTH), full),
            pl.BlockSpec((1, MEM_HEAD_DIM), fixed),
            pl.BlockSpec((1, MEM_WIDTH), fixed),
        ],
        out_specs=pl.BlockSpec((1, tq, MEM_WIDTH), tile),
        out_shape=jax.ShapeDtypeStruct((b, s, MEM_WIDTH), BF16),
        compiler_params=pltpu.CompilerParams(
            dimension_semantics=("parallel", "parallel"),
            vmem_limit_bytes=V7X_VMEM_LIMIT_BYTES),
        name="mem_attn",
    )(qm3, km3, vm3, g_mq, g_out)


def _out_proj_kernel(x_ref, yc_ref, yd_ref, ym_ref, w_ref, o_ref):
    acc = jnp.dot(yc_ref[...], w_ref[0:CONV_WIDTH, :], preferred_element_type=F32)
    acc += jnp.dot(yd_ref[...], w_ref[CONV_WIDTH:CONV_WIDTH + DIFF_WIDTH, :],
                   preferred_element_type=F32)
    acc += jnp.dot(ym_ref[...], w_ref[CONV_WIDTH + DIFF_WIDTH:, :],
                   preferred_element_type=F32)
    o_ref[...] = x_ref[...] + acc


def _out_proj(x2, yc, yd, ym, w_o):
    t = x2.shape[0]
    tm = TM_OUT
    row = lambda i: (i, 0)
    fixed = lambda i: (0, 0)
    return pl.pallas_call(
        _out_proj_kernel,
        grid=(t // tm,),
        in_specs=[
            pl.BlockSpec((tm, D_MODEL), row),
            pl.BlockSpec((tm, CONV_WIDTH), row),
            pl.BlockSpec((tm, DIFF_WIDTH), row),
            pl.BlockSpec((tm, MEM_WIDTH), row),
            pl.BlockSpec((D_MODEL, D_MODEL), fixed, pipeline_mode=pl.Buffered(1)),
        ],
        out_specs=pl.BlockSpec((tm, D_MODEL), row),
        out_shape=jax.ShapeDtypeStruct((t, D_MODEL), F32),
        compiler_params=pltpu.CompilerParams(
            dimension_semantics=("parallel",),
            vmem_limit_bytes=V7X_VMEM_LIMIT_BYTES),
        name="out_proj",
    )(x2, yc, yd, ym, w_o)


def _ffn_kernel(x_ref, g_ref, wg_ref, wu_ref, wd_ref, o_ref, h_scr, acc_scr):
    j = pl.program_id(1)

    @pl.when(j == 0)
    def _():
        x = x_ref[...]
        h_scr[...] = (x * _rms_scale(x, D_MODEL) * g_ref[...]).astype(BF16)
        acc_scr[...] = jnp.zeros(acc_scr.shape, F32)

    h = h_scr[...]
    gate = jnp.dot(h, wg_ref[...], preferred_element_type=F32)
    up = jnp.dot(h, wu_ref[...], preferred_element_type=F32)
    act = (gate * (1.0 / (1.0 + jnp.exp(-gate))) * up).astype(BF16)
    acc_scr[...] += jnp.dot(act, wd_ref[...], preferred_element_type=F32)

    @pl.when(j == pl.num_programs(1) - 1)
    def _():
        o_ref[...] = x_ref[...] + acc_scr[...]


def _ffn(x2, g_ffn, w_gate, w_up, w_down):
    t = x2.shape[0]
    tm, th = TM_FFN, TH_FFN
    return pl.pallas_call(
        _ffn_kernel,
        grid=(t // tm, FFN_HIDDEN // th),
        in_specs=[
            pl.BlockSpec((tm, D_MODEL), lambda i, j: (i, 0)),
            pl.BlockSpec((1, D_MODEL), lambda i, j: (0, 0)),
            pl.BlockSpec((D_MODEL, th), lambda i, j: (0, j)),
            pl.BlockSpec((D_MODEL, th), lambda i, j: (0, j)),
            pl.BlockSpec((th, D_MODEL), lambda i, j: (j, 0)),
        ],
        out_specs=pl.BlockSpec((tm, D_MODEL), lambda i, j: (i, 0)),
        out_shape=jax.ShapeDtypeStruct((t, D_MODEL), F32),
        scratch_shapes=[pltpu.VMEM((tm, D_MODEL), BF16), pltpu.VMEM((tm, D_MODEL), F32)],
        compiler_params=pltpu.CompilerParams(
            dimension_semantics=("parallel", "arbitrary"),
            vmem_limit_bytes=V7X_VMEM_LIMIT_BYTES),
        name="ffn",
    )(x2, g_ffn, w_gate, w_up, w_down)


def _rope_tables(positions):
    inv_freq = ROPE_THETA ** (-jnp.arange(0, ROT_DIM, 2, dtype=F32) / ROT_DIM)
    pad = jnp.zeros((DIFF_HALF - ROT_DIM,), F32)
    lane_freq = jnp.concatenate([inv_freq, inv_freq, pad, inv_freq, inv_freq, pad])
    ang = positions.astype(F32).reshape(-1, 1) * lane_freq[None, :]
    return jnp.cos(ang), jnp.sin(ang)


def kernel(x, mem, positions, g_mix, g_mem, w_in, conv_w, g_conv_out, g_dq, g_dk,
           lam_q1, lam_k1, lam_q2, lam_k2, g_sub, w_mem_kv, g_mq, g_mk, g_mem_out,
           w_o, g_ffn, w_gate, w_up, w_down):
    b, s, d = x.shape
    assert (s, d) == (SEQ, D_MODEL) and mem.shape == (b, MEM_LEN, D_MODEL)
    assert g_mix.shape[0] == 1
    t = b * s
    x2 = x.reshape(t, d)
    mem2 = mem.reshape(b * MEM_LEN, d)
    cos_t, sin_t = _rope_tables(positions)

    yc, qt, kn, vt, qm = _in_proj(x2, g_mix, w_in[0].astype(BF16), conv_w[0], g_conv_out,
                                  cos_t, sin_t,
                                  jnp.tile(g_dq, (1, 2)), jnp.tile(g_dk, (1, 2)))
    km, vm = _mem_kv(mem2, g_mem, w_mem_kv[0].astype(BF16), g_mk)

    yd = _diff_attn(qt, kn.reshape(b, s, DIFF_WIDTH), vt,
                    lam_q1, lam_k1, lam_q2, lam_k2, g_sub)
    ym = _mem_attn(qm.reshape(b, s, MEM_WIDTH), km.reshape(b, MEM_LEN, MEM_WIDTH),
                   vm.reshape(b, MEM_LEN, MEM_WIDTH), g_mq, g_mem_out)

    x1 = _out_proj(x2, yc, yd.reshape(t, DIFF_WIDTH), ym.reshape(t, MEM_WIDTH),
                   w_o[0].astype(BF16))
    out = _ffn(x1, g_ffn, w_gate[0].astype(BF16), w_up[0].astype(BF16),
               w_down[0].astype(BF16))
    return out.reshape(b, s, d)
```

```python
import functools

import jax
import jax.numpy as jnp
import numpy as np
from jax import lax
from jax.experimental import pallas as pl
from jax.experimental.pallas import tpu as pltpu

F32 = jnp.float32
BF16 = jnp.bfloat16

D_MODEL = 2048
SEQ = 2048
MEM_LEN = 256
CONV_WIDTH = 512
CONV_K = 3
DIFF_WIDTH = 1024
DIFF_VDIM = 128
DIFF_HALF = 64
DIFF_HEADS = 8
MEM_WIDTH = 512
MEM_HEADS = 4
MEM_HEAD_DIM = 128
IN_WIDTH = 3 * CONV_WIDTH + 3 * DIFF_WIDTH + MEM_WIDTH
ROT_DIM = 16
ROPE_THETA = 500000.0
FFN_HIDDEN = 5632
EPS = 1e-6
LAMBDA_INIT = 0.8 - 0.6 * float(np.exp(-0.3 * 0))
LOG2_E = float(np.log2(np.e))

V7X_LANES = 128
V7X_SUBLANES = 8
V7X_BF16_SUBLANES = 16
V7X_VMEM_LIMIT_BYTES = 56 * 1024 * 1024

TM_PROJ = 256
TM_MEMKV = 256
TM_OUT = 512
TM_FFN = 512
TH_FFN = 512

_OFF_U = 0
_OFF_C = CONV_WIDTH
_OFF_B = 2 * CONV_WIDTH
_OFF_Q = 3 * CONV_WIDTH
_OFF_K = _OFF_Q + DIFF_WIDTH
_OFF_V = _OFF_K + DIFF_WIDTH
_OFF_QM = _OFF_V + DIFF_WIDTH


def _rms_scale(t, width):
    return lax.rsqrt(jnp.sum(t * t, axis=-1, keepdims=True) * (1.0 / width) + EPS)


def _qk_norm_rope(t, g2, cos_t, sin_a, sin_b):
    lane = lax.broadcasted_iota(jnp.int32, t.shape, 1)
    is_lo = lane < DIFF_HALF
    t2 = t * t
    s_lo = jnp.sum(jnp.where(is_lo, t2, 0.0), axis=-1, keepdims=True)
    s_hi = jnp.sum(jnp.where(is_lo, 0.0, t2), axis=-1, keepdims=True)
    r = jnp.where(is_lo,
                  lax.rsqrt(s_lo * (1.0 / DIFF_HALF) + EPS),
                  lax.rsqrt(s_hi * (1.0 / DIFF_HALF) + EPS))
    tn = t * r * g2
    half = ROT_DIM // 2
    return (tn * cos_t
            + pltpu.roll(tn, V7X_LANES - half, axis=1) * sin_a
            + pltpu.roll(tn, half, axis=1) * sin_b)


def _in_proj_kernel(x_ref, g_ref, w_ref, convw_ref, gconv_ref, cos_ref, sin_ref,
                    gq_ref, gk_ref,
                    yconv_ref, qt_ref, kn_ref, vt_ref, qm_ref, z_scr, *, tiles_per_seq):
    i = pl.program_id(0)
    tm = x_ref.shape[0]
    halo = V7X_SUBLANES

    @pl.when(i % tiles_per_seq == 0)
    def _():
        z_scr[0:halo, :] = jnp.zeros((halo, CONV_WIDTH), F32)

    x = x_ref[...]
    h = (x * _rms_scale(x, D_MODEL) * g_ref[...]).astype(BF16)

    def proj(off, width):
        return jnp.dot(h, w_ref[:, off:off + width], preferred_element_type=F32)

    z = proj(_OFF_C, CONV_WIDTH) * proj(_OFF_U, CONV_WIDTH)
    z_scr[halo:halo + tm, :] = z
    z1 = z_scr[halo - 1:halo - 1 + tm, :]
    z2 = z_scr[halo - 2:halo - 2 + tm, :]
    cw = convw_ref[...]
    conv = cw[0:1, :] * z2 + cw[1:2, :] * z1 + cw[2:3, :] * z
    y = proj(_OFF_B, CONV_WIDTH) * conv
    yconv_ref[...] = (y * _rms_scale(y, CONV_WIDTH) * gconv_ref[...]).astype(BF16)
    z_scr[0:halo, :] = z_scr[tm:tm + halo, :]

    qf = proj(_OFF_Q, DIFF_WIDTH)
    kf = proj(_OFF_K, DIFF_WIDTH)
    vf = proj(_OFF_V, DIFF_WIDTH)
    cos_t, sin_t = cos_ref[...], sin_ref[...]
    pos_in_half = lax.broadcasted_iota(jnp.int32, sin_t.shape, 1) % DIFF_HALF
    sin_a = jnp.where(pos_in_half < ROT_DIM // 2, -sin_t, 0.0)
    sin_b = jnp.where((pos_in_half >= ROT_DIM // 2) & (pos_in_half < ROT_DIM), sin_t, 0.0)
    gq, gk = gq_ref[...], gk_ref[...]
    for hd in range(DIFF_HEADS):
        cols = slice(hd * DIFF_VDIM, (hd + 1) * DIFF_VDIM)
        qn = _qk_norm_rope(qf[:, cols], gq, cos_t, sin_a, sin_b) * (DIFF_HALF ** -0.5 * LOG2_E)
        qt_ref[0, 0, cols, :] = qn.T.astype(BF16)
        kn_ref[:, cols] = _qk_norm_rope(kf[:, cols], gk, cos_t, sin_a, sin_b).astype(BF16)
        vt_ref[0, 0, cols, :] = vf[:, cols].T.astype(BF16)
    qm_ref[...] = proj(_OFF_QM, MEM_WIDTH)


def _in_proj(x2, g_mix, w_in, conv_w, g_conv, cos_t, sin_t, gq2, gk2):
    t = x2.shape[0]
    tm = TM_PROJ
    tps = SEQ // tm
    nb = t // SEQ
    grid = (t // tm,)
    row = lambda i: (i, 0)
    fixed = lambda i: (0, 0)
    tile_t = lambda i: (i // tps, i % tps, 0, 0)
    return pl.pallas_call(
        functools.partial(_in_proj_kernel, tiles_per_seq=tps),
        grid=grid,
        in_specs=[
            pl.BlockSpec((tm, D_MODEL), row),
            pl.BlockSpec((1, D_MODEL), fixed),
            pl.BlockSpec((D_MODEL, IN_WIDTH), fixed, pipeline_mode=pl.Buffered(1)),
            pl.BlockSpec((CONV_K, CONV_WIDTH), fixed),
            pl.BlockSpec((1, CONV_WIDTH), fixed),
            pl.BlockSpec((tm, V7X_LANES), row),
            pl.BlockSpec((tm, V7X_LANES), row),
            pl.BlockSpec((1, DIFF_VDIM), fixed),
            pl.BlockSpec((1, DIFF_VDIM), fixed),
        ],
        out_specs=[
            pl.BlockSpec((tm, CONV_WIDTH), row),
            pl.BlockSpec((1, 1, DIFF_WIDTH, tm), tile_t),
            pl.BlockSpec((tm, DIFF_WIDTH), row),
            pl.BlockSpec((1, 1, DIFF_WIDTH, tm), tile_t),
            pl.BlockSpec((tm, MEM_WIDTH), row),
        ],
        out_shape=[
            jax.ShapeDtypeStruct((t, CONV_WIDTH), BF16),
            jax.ShapeDtypeStruct((nb, tps, DIFF_WIDTH, tm), BF16),
            jax.ShapeDtypeStruct((t, DIFF_WIDTH), BF16),
            jax.ShapeDtypeStruct((nb, tps, DIFF_WIDTH, tm), BF16),
            jax.ShapeDtypeStruct((t, MEM_WIDTH), F32),
        ],
        scratch_shapes=[pltpu.VMEM((tm + V7X_SUBLANES, CONV_WIDTH), F32)],
        compiler_params=pltpu.CompilerParams(
            dimension_semantics=("arbitrary",),
            vmem_limit_bytes=V7X_VMEM_LIMIT_BYTES),
        name="in_proj",
    )(x2, g_mix, w_in, conv_w, g_conv, cos_t, sin_t, gq2, gk2)


def _mem_kv_kernel(mem_ref, g_ref, w_ref, gk_ref, k_ref, v_ref):
    x = mem_ref[...]
    h = (x * _rms_scale(x, D_MODEL) * g_ref[...]).astype(BF16)
    kv = jnp.dot(h, w_ref[...], preferred_element_type=F32)
    gk = gk_ref[...]
    for hd in range(MEM_HEADS):
        lo = hd * MEM_HEAD_DIM
        kh = kv[:, lo:lo + MEM_HEAD_DIM]
        k_ref[:, lo:lo + MEM_HEAD_DIM] = (kh * _rms_scale(kh, MEM_HEAD_DIM) * gk).astype(BF16)
    v_ref[...] = kv[:, MEM_WIDTH:].astype(BF16)


def _mem_kv(mem2, g_mem, w_kv, g_mk):
    t = mem2.shape[0]
    tm = TM_MEMKV
    row = lambda i: (i, 0)
    fixed = lambda i: (0, 0)
    return pl.pallas_call(
        _mem_kv_kernel,
        grid=(t // tm,),
        in_specs=[
            pl.BlockSpec((tm, D_MODEL), row),
            pl.BlockSpec((1, D_MODEL), fixed),
            pl.BlockSpec((D_MODEL, 2 * MEM_WIDTH), fixed),
            pl.BlockSpec((1, MEM_HEAD_DIM), fixed),
        ],
        out_specs=[pl.BlockSpec((tm, MEM_WIDTH), row), pl.BlockSpec((tm, MEM_WIDTH), row)],
        out_shape=[jax.ShapeDtypeStruct((t, MEM_WIDTH), BF16),
                   jax.ShapeDtypeStruct((t, MEM_WIDTH), BF16)],
        compiler_params=pltpu.CompilerParams(
            dimension_semantics=("parallel",),
            vmem_limit_bytes=V7X_VMEM_LIMIT_BYTES),
        name="mem_kv",
    )(mem2, g_mem, w_kv, g_mk)


def _diff_attn_kernel(qt_ref, kn_ref, vt_ref, lq1_ref, lk1_ref, lq2_ref, lk2_ref, gsub_ref,
                      o_ref):
    nq, tq = qt_ref.shape[1], qt_ref.shape[3]
    tk = vt_ref.shape[3]

    lam = (jnp.exp(jnp.sum(lq1_ref[...] * lk1_ref[...], axis=-1, keepdims=True))
           - jnp.exp(jnp.sum(lq2_ref[...] * lk2_ref[...], axis=-1, keepdims=True))
           + LAMBDA_INIT)
    key = lax.broadcasted_iota(jnp.int32, (tk, tq), 0)
    qry = lax.broadcasted_iota(jnp.int32, (tk, tq), 1)
    causal = jnp.concatenate([key <= qry, key <= qry], axis=1)
    sum_rows = (lax.broadcasted_iota(jnp.int32, (V7X_BF16_SUBLANES, tk), 0) == 0).astype(BF16)

    def scores(c, j):
        qt = qt_ref[0, c]
        is_lo = lax.broadcasted_iota(jnp.int32, qt.shape, 0) < DIFF_HALF
        zero = jnp.zeros_like(qt)
        qq = jnp.concatenate([jnp.where(is_lo, qt, zero), jnp.where(is_lo, zero, qt)], axis=1)
        return jnp.dot(kn_ref[0, j * tk:(j + 1) * tk, :], qq, preferred_element_type=F32)

    steps = [(c, j) for c in range(nq) for j in range(c + 1)]
    s_next = scores(*steps[0])
    m = acc = None
    for n, (c, j) in enumerate(steps):
        s = s_next
        if n + 1 < len(steps):
            s_next = scores(*steps[n + 1])
        if j == c:
            s = jnp.where(causal, s, -jnp.inf)
        m_blk = jnp.max(s, axis=0, keepdims=True)
        m_new = m_blk if j == 0 else jnp.maximum(m, m_blk)
        p = jnp.exp2(s - m_new)
        vt_ext = jnp.concatenate([vt_ref[0, j], sum_rows], axis=0)
        pv = jnp.dot(vt_ext, p.astype(BF16), preferred_element_type=F32)
        acc = pv if j == 0 else jnp.exp2(m - m_new) * acc + pv
        m = m_new
        if j == c:
            o_all = acc[:DIFF_VDIM] / acc[DIFF_VDIM:DIFF_VDIM + 1]
            o = (o_all[:, :tq] - lam * o_all[:, tq:]).T
            y = o * _rms_scale(o, DIFF_VDIM) * gsub_ref[...] * (1.0 - LAMBDA_INIT)
            o_ref[0, c * tq:(c + 1) * tq, :] = y.astype(o_ref.dtype)


def _diff_attn(qt4, kn3, vt4, lq1, lk1, lq2, lk2, g_sub):
    b, nq, _, tq = qt4.shape
    s = kn3.shape[1]
    nk, tk = vt4.shape[1], vt4.shape[3]
    assert tq == tk and nq == nk
    fixed = lambda bb, h: (0, 0)
    vec = lambda n: pl.BlockSpec((1, n), fixed)
    return pl.pallas_call(
        _diff_attn_kernel,
        grid=(b, DIFF_HEADS),
        in_specs=[
            pl.BlockSpec((1, nq, DIFF_VDIM, tq), lambda bb, h: (bb, 0, h, 0)),
            pl.BlockSpec((1, s, DIFF_VDIM), lambda bb, h: (bb, 0, h)),
            pl.BlockSpec((1, nk, DIFF_VDIM, tk), lambda bb, h: (bb, 0, h, 0)),
            vec(DIFF_HALF), vec(DIFF_HALF), vec(DIFF_HALF), vec(DIFF_HALF),
            vec(DIFF_VDIM),
        ],
        out_specs=pl.BlockSpec((1, s, DIFF_VDIM), lambda bb, h: (bb, 0, h)),
        out_shape=jax.ShapeDtypeStruct((b, s, DIFF_WIDTH), BF16),
        compiler_params=pltpu.CompilerParams(
            dimension_semantics=("parallel", "parallel"),
            vmem_limit_bytes=V7X_VMEM_LIMIT_BYTES),
        name="diff_attn",
    )(qt4, kn3, vt4, lq1, lk1, lq2, lk2, g_sub)


def _out_proj_kernel(x_ref, yc_ref, yd_ref, qm_ref, km_ref, vm_ref, gq_ref, gmo_ref, gffn_ref,
                     w_ref, x1_ref, hf_ref):
    gq = gq_ref[...]
    probs, denoms = [], []
    for hd in range(MEM_HEADS):
        cols = slice(hd * MEM_HEAD_DIM, (hd + 1) * MEM_HEAD_DIM)
        qh = qm_ref[:, cols]
        qh = (qh * _rms_scale(qh, MEM_HEAD_DIM) * gq * (MEM_HEAD_DIM ** -0.5)).astype(BF16)
        s = lax.dot_general(qh, km_ref[0, :, cols], (((1,), (1,)), ((), ())),
                            preferred_element_type=F32)
        e = jnp.exp(s - jnp.max(s, axis=-1, keepdims=True))
        denoms.append(jnp.sum(e, axis=-1, keepdims=True))
        probs.append(e.astype(BF16))

    acc = jnp.dot(yc_ref[...], w_ref[0:CONV_WIDTH, :], preferred_element_type=F32)

    outs = []
    for hd in range(MEM_HEADS):
        cols = slice(hd * MEM_HEAD_DIM, (hd + 1) * MEM_HEAD_DIM)
        oh = jnp.dot(probs[hd], vm_ref[0, :, cols], preferred_element_type=F32)
        outs.append(oh / denoms[hd])
    o = jnp.concatenate(outs, axis=-1)
    ym = (o * _rms_scale(o, MEM_WIDTH) * gmo_ref[...]).astype(BF16)

    acc += jnp.dot(yd_ref[...], w_ref[CONV_WIDTH:CONV_WIDTH + DIFF_WIDTH, :],
                   preferred_element_type=F32)
    acc += jnp.dot(ym, w_ref[CONV_WIDTH + DIFF_WIDTH:, :], preferred_element_type=F32)
    x1 = x_ref[...] + acc
    x1_ref[...] = x1
    hf_ref[...] = (x1 * _rms_scale(x1, D_MODEL) * gffn_ref[...]).astype(BF16)


def _out_proj(x2, yc, yd, qm, km3, vm3, g_mq, g_mem_out, g_ffn, w_o):
    t = x2.shape[0]
    tm = TM_OUT
    tps = SEQ // tm
    row = lambda i: (i, 0)
    fixed = lambda i: (0, 0)
    mem_blk = lambda i: (i // tps, 0, 0)
    return pl.pallas_call(
        _out_proj_kernel,
        grid=(t // tm,),
        in_specs=[
            pl.BlockSpec((tm, D_MODEL), row),
            pl.BlockSpec((tm, CONV_WIDTH), row),
            pl.BlockSpec((tm, DIFF_WIDTH), row),
            pl.BlockSpec((tm, MEM_WIDTH), row),
            pl.BlockSpec((1, MEM_LEN, MEM_WIDTH), mem_blk),
            pl.BlockSpec((1, MEM_LEN, MEM_WIDTH), mem_blk),
            pl.BlockSpec((1, MEM_HEAD_DIM), fixed),
            pl.BlockSpec((1, MEM_WIDTH), fixed),
            pl.BlockSpec((1, D_MODEL), fixed),
            pl.BlockSpec((D_MODEL, D_MODEL), fixed, pipeline_mode=pl.Buffered(1)),
        ],
        out_specs=[pl.BlockSpec((tm, D_MODEL), row), pl.BlockSpec((tm, D_MODEL), row)],
        out_shape=[jax.ShapeDtypeStruct((t, D_MODEL), F32),
                   jax.ShapeDtypeStruct((t, D_MODEL), BF16)],
        compiler_params=pltpu.CompilerParams(
            dimension_semantics=("parallel",),
            vmem_limit_bytes=V7X_VMEM_LIMIT_BYTES),
        name="out_proj",
    )(x2, yc, yd, qm, km3, vm3, g_mq, g_mem_out, g_ffn, w_o)


def _ffn_kernel(hf_ref, x1_ref, wg_ref, wu_ref, wd_ref, o_ref):
    @pl.when(pl.program_id(1) == 0)
    def _():
        o_ref[...] = x1_ref[...]

    h = hf_ref[...]
    gate = jnp.dot(h, wg_ref[...], preferred_element_type=F32)
    up = jnp.dot(h, wu_ref[...], preferred_element_type=F32)
    act = (gate * (1.0 / (1.0 + jnp.exp(-gate))) * up).astype(BF16)
    o_ref[...] += jnp.dot(act, wd_ref[...], preferred_element_type=F32)


def _ffn(hf, x1, w_gate, w_up, w_down):
    t = x1.shape[0]
    tm, th = TM_FFN, TH_FFN
    return pl.pallas_call(
        _ffn_kernel,
        grid=(t // tm, FFN_HIDDEN // th),
        in_specs=[
            pl.BlockSpec((tm, D_MODEL), lambda i, j: (i, 0)),
            pl.BlockSpec((tm, D_MODEL), lambda i, j: (i, 0)),
            pl.BlockSpec((D_MODEL, th), lambda i, j: (0, j)),
            pl.BlockSpec((D_MODEL, th), lambda i, j: (0, j)),
            pl.BlockSpec((th, D_MODEL), lambda i, j: (j, 0)),
        ],
        out_specs=pl.BlockSpec((tm, D_MODEL), lambda i, j: (i, 0)),
        out_shape=jax.ShapeDtypeStruct((t, D_MODEL), F32),
        compiler_params=pltpu.CompilerParams(
            dimension_semantics=("parallel", "arbitrary"),
            vmem_limit_bytes=V7X_VMEM_LIMIT_BYTES),
        name="ffn",
    )(hf, x1, w_gate, w_up, w_down)


def _rope_tables(positions):
    inv_freq = ROPE_THETA ** (-jnp.arange(0, ROT_DIM, 2, dtype=F32) / ROT_DIM)
    pad = jnp.zeros((DIFF_HALF - ROT_DIM,), F32)
    lane_freq = jnp.concatenate([inv_freq, inv_freq, pad, inv_freq, inv_freq, pad])
    ang = positions.astype(F32).reshape(-1, 1) * lane_freq[None, :]
    return jnp.cos(ang), jnp.sin(ang)


def kernel(x, mem, positions, g_mix, g_mem, w_in, conv_w, g_conv_out, g_dq, g_dk,
           lam_q1, lam_k1, lam_q2, lam_k2, g_sub, w_mem_kv, g_mq, g_mk, g_mem_out,
           w_o, g_ffn, w_gate, w_up, w_down):
    b, s, d = x.shape
    assert (s, d) == (SEQ, D_MODEL) and mem.shape == (b, MEM_LEN, D_MODEL)
    assert g_mix.shape[0] == 1
    t = b * s
    x2 = x.reshape(t, d)
    mem2 = mem.reshape(b * MEM_LEN, d)
    cos_t, sin_t = _rope_tables(positions)

    yc, qt, kn, vt, qm = _in_proj(x2, g_mix, w_in[0].astype(BF16), conv_w[0], g_conv_out,
                                  cos_t, sin_t,
                                  jnp.tile(g_dq, (1, 2)), jnp.tile(g_dk, (1, 2)))
    km, vm = _mem_kv(mem2, g_mem, w_mem_kv[0].astype(BF16), g_mk)

    yd = _diff_attn(qt, kn.reshape(b, s, DIFF_WIDTH), vt,
                    lam_q1, lam_k1, lam_q2, lam_k2, g_sub)
    x1, hf = _out_proj(x2, yc, yd.reshape(t, DIFF_WIDTH), qm,
                       km.reshape(b, MEM_LEN, MEM_WIDTH), vm.reshape(b, MEM_LEN, MEM_WIDTH),
                       g_mq, g_mem_out, g_ffn, w_o[0].astype(BF16))
    out = _ffn(hf, x1, w_gate[0].astype(BF16), w_up[0].astype(BF16), w_down[0].astype(BF16))
    return out.reshape(b, s, d)
```

```python
import functools

import jax
import jax.numpy as jnp
import numpy as np
from jax import lax
from jax.experimental import pallas as pl
from jax.experimental.pallas import tpu as pltpu

F32 = jnp.float32
BF16 = jnp.bfloat16

D_MODEL = 2048
SEQ = 2048
MEM_LEN = 256
CONV_WIDTH = 512
CONV_K = 3
DIFF_WIDTH = 1024
DIFF_VDIM = 128
DIFF_HALF = 64
DIFF_HEADS = 8
MEM_WIDTH = 512
MEM_HEADS = 4
MEM_HEAD_DIM = 128
IN_WIDTH = 3 * CONV_WIDTH + 3 * DIFF_WIDTH + MEM_WIDTH
ROT_DIM = 16
ROPE_THETA = 500000.0
FFN_HIDDEN = 5632
EPS = 1e-6
LAMBDA_INIT = 0.8 - 0.6 * float(np.exp(-0.3 * 0))
LOG2_E = float(np.log2(np.e))

V7X_LANES = 128
V7X_SUBLANES = 8
V7X_BF16_SUBLANES = 16
V7X_VMEM_LIMIT_BYTES = 56 * 1024 * 1024

TM_PROJ = 256
TM_MEMKV = 256
TM_OUT = 512
TM_FFN = 1024
TH_FFN = 256

_OFF_U = 0
_OFF_C = CONV_WIDTH
_OFF_B = 2 * CONV_WIDTH
_OFF_Q = 3 * CONV_WIDTH
_OFF_K = _OFF_Q + DIFF_WIDTH
_OFF_V = _OFF_K + DIFF_WIDTH
_OFF_QM = _OFF_V + DIFF_WIDTH


def _rms_scale(t, width):
    return lax.rsqrt(jnp.sum(t * t, axis=-1, keepdims=True) * (1.0 / width) + EPS)


def _qk_norm_rope(t, g2, cos_t, sin_a, sin_b):
    lane = lax.broadcasted_iota(jnp.int32, t.shape, 1)
    is_lo = lane < DIFF_HALF
    t2 = t * t
    s_lo = jnp.sum(jnp.where(is_lo, t2, 0.0), axis=-1, keepdims=True)
    s_hi = jnp.sum(jnp.where(is_lo, 0.0, t2), axis=-1, keepdims=True)
    r = jnp.where(is_lo,
                  lax.rsqrt(s_lo * (1.0 / DIFF_HALF) + EPS),
                  lax.rsqrt(s_hi * (1.0 / DIFF_HALF) + EPS))
    tn = t * r * g2
    half = ROT_DIM // 2
    return (tn * cos_t
            + pltpu.roll(tn, V7X_LANES - half, axis=1) * sin_a
            + pltpu.roll(tn, half, axis=1) * sin_b)


def _in_proj_kernel(x_ref, g_ref, w_ref, convw_ref, gconv_ref, cos_ref, sin_ref,
                    gq_ref, gk_ref,
                    yconv_ref, qt_ref, kn_ref, vt_ref, qm_ref, z_scr, *, tiles_per_seq):
    i = pl.program_id(0)
    tm = x_ref.shape[0]
    halo = V7X_SUBLANES

    @pl.when(i % tiles_per_seq == 0)
    def _():
        z_scr[0:halo, :] = jnp.zeros((halo, CONV_WIDTH), F32)

    x = x_ref[...]
    h = (x * _rms_scale(x, D_MODEL) * g_ref[...]).astype(BF16)

    def proj(off, width):
        return jnp.dot(h, w_ref[:, off:off + width], preferred_element_type=F32)

    z = proj(_OFF_C, CONV_WIDTH) * proj(_OFF_U, CONV_WIDTH)
    z_scr[halo:halo + tm, :] = z
    z1 = z_scr[halo - 1:halo - 1 + tm, :]
    z2 = z_scr[halo - 2:halo - 2 + tm, :]
    cw = convw_ref[...]
    conv = cw[0:1, :] * z2 + cw[1:2, :] * z1 + cw[2:3, :] * z
    y = proj(_OFF_B, CONV_WIDTH) * conv
    yconv_ref[...] = (y * _rms_scale(y, CONV_WIDTH) * gconv_ref[...]).astype(BF16)
    z_scr[0:halo, :] = z_scr[tm:tm + halo, :]

    qf = proj(_OFF_Q, DIFF_WIDTH)
    kf = proj(_OFF_K, DIFF_WIDTH)
    vf = proj(_OFF_V, DIFF_WIDTH)
    cos_t, sin_t = cos_ref[...], sin_ref[...]
    pos_in_half = lax.broadcasted_iota(jnp.int32, sin_t.shape, 1) % DIFF_HALF
    sin_a = jnp.where(pos_in_half < ROT_DIM // 2, -sin_t, 0.0)
    sin_b = jnp.where((pos_in_half >= ROT_DIM // 2) & (pos_in_half < ROT_DIM), sin_t, 0.0)
    gq, gk = gq_ref[...], gk_ref[...]
    for hd in range(DIFF_HEADS):
        cols = slice(hd * DIFF_VDIM, (hd + 1) * DIFF_VDIM)
        qn = _qk_norm_rope(qf[:, cols], gq, cos_t, sin_a, sin_b) * (DIFF_HALF ** -0.5 * LOG2_E)
        qt_ref[0, 0, cols, :] = qn.T.astype(BF16)
        kn_ref[:, cols] = _qk_norm_rope(kf[:, cols], gk, cos_t, sin_a, sin_b).astype(BF16)
        vt_ref[0, 0, cols, :] = vf[:, cols].T.astype(BF16)
    qm_ref[...] = proj(_OFF_QM, MEM_WIDTH)


def _in_proj(x2, g_mix, w_in, conv_w, g_conv, cos_t, sin_t, gq2, gk2):
    t = x2.shape[0]
    tm = TM_PROJ
    tps = SEQ // tm
    nb = t // SEQ
    grid = (t // tm,)
    row = lambda i: (i, 0)
    fixed = lambda i: (0, 0)
    tile_t = lambda i: (i // tps, i % tps, 0, 0)
    return pl.pallas_call(
        functools.partial(_in_proj_kernel, tiles_per_seq=tps),
        grid=grid,
        in_specs=[
            pl.BlockSpec((tm, D_MODEL), row),
            pl.BlockSpec((1, D_MODEL), fixed),
            pl.BlockSpec((D_MODEL, IN_WIDTH), fixed, pipeline_mode=pl.Buffered(1)),
            pl.BlockSpec((CONV_K, CONV_WIDTH), fixed),
            pl.BlockSpec((1, CONV_WIDTH), fixed),
            pl.BlockSpec((tm, V7X_LANES), row),
            pl.BlockSpec((tm, V7X_LANES), row),
            pl.BlockSpec((1, DIFF_VDIM), fixed),
            pl.BlockSpec((1, DIFF_VDIM), fixed),
        ],
        out_specs=[
            pl.BlockSpec((tm, CONV_WIDTH), row),
            pl.BlockSpec((1, 1, DIFF_WIDTH, tm), tile_t),
            pl.BlockSpec((tm, DIFF_WIDTH), row),
            pl.BlockSpec((1, 1, DIFF_WIDTH, tm), tile_t),
            pl.BlockSpec((tm, MEM_WIDTH), row),
        ],
        out_shape=[
            jax.ShapeDtypeStruct((t, CONV_WIDTH), BF16),
            jax.ShapeDtypeStruct((nb, tps, DIFF_WIDTH, tm), BF16),
            jax.ShapeDtypeStruct((t, DIFF_WIDTH), BF16),
            jax.ShapeDtypeStruct((nb, tps, DIFF_WIDTH, tm), BF16),
            jax.ShapeDtypeStruct((t, MEM_WIDTH), F32),
        ],
        scratch_shapes=[pltpu.VMEM((tm + V7X_SUBLANES, CONV_WIDTH), F32)],
        compiler_params=pltpu.CompilerParams(
            dimension_semantics=("arbitrary",),
            vmem_limit_bytes=V7X_VMEM_LIMIT_BYTES),
        name="in_proj",
    )(x2, g_mix, w_in, conv_w, g_conv, cos_t, sin_t, gq2, gk2)


def _mem_kv_kernel(mem_ref, g_ref, w_ref, gk_ref, k_ref, v_ref):
    x = mem_ref[...]
    h = (x * _rms_scale(x, D_MODEL) * g_ref[...]).astype(BF16)
    kv = jnp.dot(h, w_ref[...], preferred_element_type=F32)
    gk = gk_ref[...]
    for hd in range(MEM_HEADS):
        lo = hd * MEM_HEAD_DIM
        kh = kv[:, lo:lo + MEM_HEAD_DIM]
        k_ref[:, lo:lo + MEM_HEAD_DIM] = (kh * _rms_scale(kh, MEM_HEAD_DIM) * gk).astype(BF16)
    v_ref[...] = kv[:, MEM_WIDTH:].astype(BF16)


def _mem_kv(mem2, g_mem, w_kv, g_mk):
    t = mem2.shape[0]
    tm = TM_MEMKV
    row = lambda i: (i, 0)
    fixed = lambda i: (0, 0)
    return pl.pallas_call(
        _mem_kv_kernel,
        grid=(t // tm,),
        in_specs=[
            pl.BlockSpec((tm, D_MODEL), row),
            pl.BlockSpec((1, D_MODEL), fixed),
            pl.BlockSpec((D_MODEL, 2 * MEM_WIDTH), fixed),
            pl.BlockSpec((1, MEM_HEAD_DIM), fixed),
        ],
        out_specs=[pl.BlockSpec((tm, MEM_WIDTH), row), pl.BlockSpec((tm, MEM_WIDTH), row)],
        out_shape=[jax.ShapeDtypeStruct((t, MEM_WIDTH), BF16),
                   jax.ShapeDtypeStruct((t, MEM_WIDTH), BF16)],
        compiler_params=pltpu.CompilerParams(
            dimension_semantics=("parallel",),
            vmem_limit_bytes=V7X_VMEM_LIMIT_BYTES),
        name="mem_kv",
    )(mem2, g_mem, w_kv, g_mk)


def _diff_attn_kernel(qt_ref, kn_ref, vt_ref, lq1_ref, lk1_ref, lq2_ref, lk2_ref, gsub_ref,
                      o_ref):
    nq, tq = qt_ref.shape[1], qt_ref.shape[3]
    tk = vt_ref.shape[3]

    lam = (jnp.exp(jnp.sum(lq1_ref[...] * lk1_ref[...], axis=-1, keepdims=True))
           - jnp.exp(jnp.sum(lq2_ref[...] * lk2_ref[...], axis=-1, keepdims=True))
           + LAMBDA_INIT)
    key = lax.broadcasted_iota(jnp.int32, (tk, tq), 0)
    qry = lax.broadcasted_iota(jnp.int32, (tk, tq), 1)
    causal = jnp.concatenate([key <= qry, key <= qry], axis=1)
    sum_rows = (lax.broadcasted_iota(jnp.int32, (V7X_BF16_SUBLANES, tk), 0) == 0).astype(BF16)

    def scores(c, j):
        qt = qt_ref[0, c]
        is_lo = lax.broadcasted_iota(jnp.int32, qt.shape, 0) < DIFF_HALF
        zero = jnp.zeros_like(qt)
        qq = jnp.concatenate([jnp.where(is_lo, qt, zero), jnp.where(is_lo, zero, qt)], axis=1)
        return jnp.dot(kn_ref[0, j * tk:(j + 1) * tk, :], qq, preferred_element_type=F32)

    steps = [(c, j) for c in range(nq) for j in range(c + 1)]
    s_next = scores(*steps[0])
    m = acc = None
    for n, (c, j) in enumerate(steps):
        s = s_next
        if n + 1 < len(steps):
            s_next = scores(*steps[n + 1])
        if j == c:
            s = jnp.where(causal, s, -jnp.inf)
        m_blk = jnp.max(s, axis=0, keepdims=True)
        m_new = m_blk if j == 0 else jnp.maximum(m, m_blk)
        p = jnp.exp2(s - m_new)
        vt_ext = jnp.concatenate([vt_ref[0, j], sum_rows], axis=0)
        pv = jnp.dot(vt_ext, p.astype(BF16), preferred_element_type=F32)
        acc = pv if j == 0 else jnp.exp2(m - m_new) * acc + pv
        m = m_new
        if j == c:
            o_all = acc[:DIFF_VDIM] / acc[DIFF_VDIM:DIFF_VDIM + 1]
            o = (o_all[:, :tq] - lam * o_all[:, tq:]).T
            y = o * _rms_scale(o, DIFF_VDIM) * gsub_ref[...] * (1.0 - LAMBDA_INIT)
            o_ref[0, c * tq:(c + 1) * tq, :] = y.astype(o_ref.dtype)


def _diff_attn(qt4, kn3, vt4, lq1, lk1, lq2, lk2, g_sub):
    b, nq, _, tq = qt4.shape
    s = kn3.shape[1]
    nk, tk = vt4.shape[1], vt4.shape[3]
    assert tq == tk and nq == nk
    fixed = lambda bb, h: (0, 0)
    vec = lambda n: pl.BlockSpec((1, n), fixed)
    return pl.pallas_call(
        _diff_attn_kernel,
        grid=(b, DIFF_HEADS),
        in_specs=[
            pl.BlockSpec((1, nq, DIFF_VDIM, tq), lambda bb, h: (bb, 0, h, 0)),
            pl.BlockSpec((1, s, DIFF_VDIM), lambda bb, h: (bb, 0, h)),
            pl.BlockSpec((1, nk, DIFF_VDIM, tk), lambda bb, h: (bb, 0, h, 0)),
            vec(DIFF_HALF), vec(DIFF_HALF), vec(DIFF_HALF), vec(DIFF_HALF),
            vec(DIFF_VDIM),
        ],
        out_specs=pl.BlockSpec((1, s, DIFF_VDIM), lambda bb, h: (bb, 0, h)),
        out_shape=jax.ShapeDtypeStruct((b, s, DIFF_WIDTH), BF16),
        compiler_params=pltpu.CompilerParams(
            dimension_semantics=("parallel", "parallel"),
            vmem_limit_bytes=V7X_VMEM_LIMIT_BYTES),
        name="diff_attn",
    )(qt4, kn3, vt4, lq1, lk1, lq2, lk2, g_sub)


def _out_proj_kernel(x_ref, yc_ref, yd_ref, qm_ref, km_ref, vm_ref, gq_ref, gmo_ref, gffn_ref,
                     w_ref, x1_ref, hf_ref):
    gq = gq_ref[...]
    probs, denoms = [], []
    for hd in range(MEM_HEADS):
        cols = slice(hd * MEM_HEAD_DIM, (hd + 1) * MEM_HEAD_DIM)
        qh = qm_ref[:, cols]
        qh = (qh * _rms_scale(qh, MEM_HEAD_DIM) * gq * (MEM_HEAD_DIM ** -0.5)).astype(BF16)
        s = lax.dot_general(qh, km_ref[0, :, cols], (((1,), (1,)), ((), ())),
                            preferred_element_type=F32)
        e = jnp.exp(s - jnp.max(s, axis=-1, keepdims=True))
        denoms.append(jnp.sum(e, axis=-1, keepdims=True))
        probs.append(e.astype(BF16))

    acc = jnp.dot(yc_ref[...], w_ref[0:CONV_WIDTH, :], preferred_element_type=F32)

    outs = []
    for hd in range(MEM_HEADS):
        cols = slice(hd * MEM_HEAD_DIM, (hd + 1) * MEM_HEAD_DIM)
        oh = jnp.dot(probs[hd], vm_ref[0, :, cols], preferred_element_type=F32)
        outs.append(oh / denoms[hd])
    o = jnp.concatenate(outs, axis=-1)
    ym = (o * _rms_scale(o, MEM_WIDTH) * gmo_ref[...]).astype(BF16)

    acc += jnp.dot(yd_ref[...], w_ref[CONV_WIDTH:CONV_WIDTH + DIFF_WIDTH, :],
                   preferred_element_type=F32)
    acc += jnp.dot(ym, w_ref[CONV_WIDTH + DIFF_WIDTH:, :], preferred_element_type=F32)
    x1 = x_ref[...] + acc
    x1_ref[...] = x1
    hf_ref[...] = (x1 * _rms_scale(x1, D_MODEL) * gffn_ref[...]).astype(BF16)


def _out_proj(x2, yc, yd, qm, km3, vm3, g_mq, g_mem_out, g_ffn, w_o):
    t = x2.shape[0]
    tm = TM_OUT
    tps = SEQ // tm
    row = lambda i: (i, 0)
    fixed = lambda i: (0, 0)
    mem_blk = lambda i: (i // tps, 0, 0)
    return pl.pallas_call(
        _out_proj_kernel,
        grid=(t // tm,),
        in_specs=[
            pl.BlockSpec((tm, D_MODEL), row),
            pl.BlockSpec((tm, CONV_WIDTH), row),
            pl.BlockSpec((tm, DIFF_WIDTH), row),
            pl.BlockSpec((tm, MEM_WIDTH), row),
            pl.BlockSpec((1, MEM_LEN, MEM_WIDTH), mem_blk),
            pl.BlockSpec((1, MEM_LEN, MEM_WIDTH), mem_blk),
            pl.BlockSpec((1, MEM_HEAD_DIM), fixed),
            pl.BlockSpec((1, MEM_WIDTH), fixed),
            pl.BlockSpec((1, D_MODEL), fixed),
            pl.BlockSpec((D_MODEL, D_MODEL), fixed, pipeline_mode=pl.Buffered(1)),
        ],
        out_specs=[pl.BlockSpec((tm, D_MODEL), row), pl.BlockSpec((tm, D_MODEL), row)],
        out_shape=[jax.ShapeDtypeStruct((t, D_MODEL), F32),
                   jax.ShapeDtypeStruct((t, D_MODEL), BF16)],
        compiler_params=pltpu.CompilerParams(
            dimension_semantics=("parallel",),
            vmem_limit_bytes=V7X_VMEM_LIMIT_BYTES),
        name="out_proj",
    )(x2, yc, yd, qm, km3, vm3, g_mq, g_mem_out, g_ffn, w_o)


def _ffn_kernel(hf_ref, x1_ref, wgu_ref, wd_ref, o_ref):
    @pl.when(pl.program_id(1) == 0)
    def _():
        o_ref[...] = x1_ref[...]

    th = wd_ref.shape[0]
    gu = jnp.dot(hf_ref[...], wgu_ref[0], preferred_element_type=F32)
    gate, up = gu[:, :th], gu[:, th:]
    act = (gate * (1.0 / (1.0 + jnp.exp(-gate))) * up).astype(BF16)
    o_ref[...] += jnp.dot(act, wd_ref[...], preferred_element_type=F32)


def _ffn(hf, x1, w_gate_up, w_down):
    t = x1.shape[0]
    tm, th = TM_FFN, TH_FFN
    return pl.pallas_call(
        _ffn_kernel,
        grid=(t // tm, FFN_HIDDEN // th),
        in_specs=[
            pl.BlockSpec((tm, D_MODEL), lambda i, j: (i, 0)),
            pl.BlockSpec((tm, D_MODEL), lambda i, j: (i, 0)),
            pl.BlockSpec((1, D_MODEL, 2 * th), lambda i, j: (j, 0, 0)),
            pl.BlockSpec((th, D_MODEL), lambda i, j: (j, 0)),
        ],
        out_specs=pl.BlockSpec((tm, D_MODEL), lambda i, j: (i, 0)),
        out_shape=jax.ShapeDtypeStruct((t, D_MODEL), F32),
        compiler_params=pltpu.CompilerParams(
            dimension_semantics=("parallel", "arbitrary"),
            vmem_limit_bytes=V7X_VMEM_LIMIT_BYTES),
        name="ffn",
    )(hf, x1, w_gate_up, w_down)


def _rope_tables(positions):
    inv_freq = ROPE_THETA ** (-jnp.arange(0, ROT_DIM, 2, dtype=F32) / ROT_DIM)
    pad = jnp.zeros((DIFF_HALF - ROT_DIM,), F32)
    lane_freq = jnp.concatenate([inv_freq, inv_freq, pad, inv_freq, inv_freq, pad])
    ang = positions.astype(F32).reshape(-1, 1) * lane_freq[None, :]
    return jnp.cos(ang), jnp.sin(ang)


def kernel(x, mem, positions, g_mix, g_mem, w_in, conv_w, g_conv_out, g_dq, g_dk,
           lam_q1, lam_k1, lam_q2, lam_k2, g_sub, w_mem_kv, g_mq, g_mk, g_mem_out,
           w_o, g_ffn, w_gate, w_up, w_down):
    b, s, d = x.shape
    assert (s, d) == (SEQ, D_MODEL) and mem.shape == (b, MEM_LEN, D_MODEL)
    assert g_mix.shape[0] == 1
    t = b * s
    x2 = x.reshape(t, d)
    mem2 = mem.reshape(b * MEM_LEN, d)
    cos_t, sin_t = _rope_tables(positions)

    yc, qt, kn, vt, qm = _in_proj(x2, g_mix, w_in[0].astype(BF16), conv_w[0], g_conv_out,
                                  cos_t, sin_t,
                                  jnp.tile(g_dq, (1, 2)), jnp.tile(g_dk, (1, 2)))
    km, vm = _mem_kv(mem2, g_mem, w_mem_kv[0].astype(BF16), g_mk)

    yd = _diff_attn(qt, kn.reshape(b, s, DIFF_WIDTH), vt,
                    lam_q1, lam_k1, lam_q2, lam_k2, g_sub)
    x1, hf = _out_proj(x2, yc, yd.reshape(t, DIFF_WIDTH), qm,
                       km.reshape(b, MEM_LEN, MEM_WIDTH), vm.reshape(b, MEM_LEN, MEM_WIDTH),
                       g_mq, g_mem_out, g_ffn, w_o[0].astype(BF16))
    nh = FFN_HIDDEN // TH_FFN
    chunked = lambda w: w[0].astype(BF16).reshape(D_MODEL, nh, TH_FFN).transpose(1, 0, 2)
    w_gate_up = jnp.concatenate([chunked(w_gate), chunked(w_up)], axis=-1)
    out = _ffn(hf, x1, w_gate_up, w_down[0].astype(BF16))
    return out.reshape(b, s, d)
```

```python
import functools

import jax
import jax.numpy as jnp
import numpy as np
from jax import lax
from jax.experimental import pallas as pl
from jax.experimental.pallas import tpu as pltpu

F32 = jnp.float32
BF16 = jnp.bfloat16

D_MODEL = 2048
SEQ = 2048
MEM_LEN = 256
CONV_WIDTH = 512
CONV_K = 3
DIFF_WIDTH = 1024
DIFF_VDIM = 128
DIFF_HALF = 64
DIFF_HEADS = 8
MEM_WIDTH = 512
MEM_HEADS = 4
MEM_HEAD_DIM = 128
IN_WIDTH = 3 * CONV_WIDTH + 3 * DIFF_WIDTH + MEM_WIDTH
ROT_DIM = 16
ROPE_THETA = 500000.0
FFN_HIDDEN = 5632
EPS = 1e-6
LAMBDA_INIT = 0.8 - 0.6 * float(np.exp(-0.3 * 0))
LOG2_E = float(np.log2(np.e))
MAX_SHIFT_FREE_LOG2_SCORE = 60.0

V7X_LANES = 128
V7X_SUBLANES = 8
V7X_BF16_SUBLANES = 16
V7X_VMEM_LIMIT_BYTES = 56 * 1024 * 1024

TM_PROJ = 256
TM_MEMKV = 256
TM_OUT = 512
TM_FFN = 512
TH_FFN = 512

_OFF_U = 0
_OFF_C = CONV_WIDTH
_OFF_B = 2 * CONV_WIDTH
_OFF_Q = 3 * CONV_WIDTH
_OFF_K = _OFF_Q + DIFF_WIDTH
_OFF_V = _OFF_K + DIFF_WIDTH
_OFF_QM = _OFF_V + DIFF_WIDTH


def _rms_scale(t, width):
    return lax.rsqrt(jnp.sum(t * t, axis=-1, keepdims=True) * (1.0 / width) + EPS)


def _qk_norm_rope(t, g2, cos_t, sin_a, sin_b):
    lane = lax.broadcasted_iota(jnp.int32, t.shape, 1)
    is_lo = lane < DIFF_HALF
    t2 = t * t
    s_lo = jnp.sum(jnp.where(is_lo, t2, 0.0), axis=-1, keepdims=True)
    s_hi = jnp.sum(jnp.where(is_lo, 0.0, t2), axis=-1, keepdims=True)
    r = jnp.where(is_lo,
                  lax.rsqrt(s_lo * (1.0 / DIFF_HALF) + EPS),
                  lax.rsqrt(s_hi * (1.0 / DIFF_HALF) + EPS))
    tn = t * r * g2
    half = ROT_DIM // 2
    return (tn * cos_t
            + pltpu.roll(tn, V7X_LANES - half, axis=1) * sin_a
            + pltpu.roll(tn, half, axis=1) * sin_b)


def _in_proj_kernel(x_ref, g_ref, w_ref, convw_ref, gconv_ref, cos_ref, sin_ref,
                    gq_ref, gk_ref,
                    yconv_ref, qt_ref, kn_ref, vt_ref, qm_ref, z_scr, *, tiles_per_seq):
    i = pl.program_id(0)
    tm = x_ref.shape[0]
    halo = V7X_SUBLANES

    @pl.when(i % tiles_per_seq == 0)
    def _():
        z_scr[0:halo, :] = jnp.zeros((halo, CONV_WIDTH), F32)

    x = x_ref[...]
    h = (x * _rms_scale(x, D_MODEL) * g_ref[...]).astype(BF16)

    def proj(off, width):
        return jnp.dot(h, w_ref[:, off:off + width], preferred_element_type=F32)

    z = proj(_OFF_C, CONV_WIDTH) * proj(_OFF_U, CONV_WIDTH)
    z_scr[halo:halo + tm, :] = z
    z1 = z_scr[halo - 1:halo - 1 + tm, :]
    z2 = z_scr[halo - 2:halo - 2 + tm, :]
    cw = convw_ref[...]
    conv = cw[0:1, :] * z2 + cw[1:2, :] * z1 + cw[2:3, :] * z
    y = proj(_OFF_B, CONV_WIDTH) * conv
    yconv_ref[...] = (y * _rms_scale(y, CONV_WIDTH) * gconv_ref[...]).astype(BF16)
    z_scr[0:halo, :] = z_scr[tm:tm + halo, :]

    qf = proj(_OFF_Q, DIFF_WIDTH)
    kf = proj(_OFF_K, DIFF_WIDTH)
    vf = proj(_OFF_V, DIFF_WIDTH)
    cos_t, sin_t = cos_ref[...], sin_ref[...]
    pos_in_half = lax.broadcasted_iota(jnp.int32, sin_t.shape, 1) % DIFF_HALF
    sin_a = jnp.where(pos_in_half < ROT_DIM // 2, -sin_t, 0.0)
    sin_b = jnp.where((pos_in_half >= ROT_DIM // 2) & (pos_in_half < ROT_DIM), sin_t, 0.0)
    gq, gk = gq_ref[...], gk_ref[...]
    for hd in range(DIFF_HEADS):
        cols = slice(hd * DIFF_VDIM, (hd + 1) * DIFF_VDIM)
        qn = _qk_norm_rope(qf[:, cols], gq, cos_t, sin_a, sin_b) * (DIFF_HALF ** -0.5 * LOG2_E)
        qt_ref[0, 0, cols, :] = qn.T.astype(BF16)
        kn_ref[:, cols] = _qk_norm_rope(kf[:, cols], gk, cos_t, sin_a, sin_b).astype(BF16)
        vt_ref[0, 0, cols, :] = vf[:, cols].T.astype(BF16)
    qm_ref[...] = proj(_OFF_QM, MEM_WIDTH)


def _in_proj(x2, g_mix, w_in, conv_w, g_conv, cos_t, sin_t, gq2, gk2):
    t = x2.shape[0]
    tm = TM_PROJ
    tps = SEQ // tm
    nb = t // SEQ
    grid = (t // tm,)
    row = lambda i: (i, 0)
    fixed = lambda i: (0, 0)
    tile_t = lambda i: (i // tps, i % tps, 0, 0)
    return pl.pallas_call(
        functools.partial(_in_proj_kernel, tiles_per_seq=tps),
        grid=grid,
        in_specs=[
            pl.BlockSpec((tm, D_MODEL), row),
            pl.BlockSpec((1, D_MODEL), fixed),
            pl.BlockSpec((D_MODEL, IN_WIDTH), fixed, pipeline_mode=pl.Buffered(1)),
            pl.BlockSpec((CONV_K, CONV_WIDTH), fixed),
            pl.BlockSpec((1, CONV_WIDTH), fixed),
            pl.BlockSpec((tm, V7X_LANES), row),
            pl.BlockSpec((tm, V7X_LANES), row),
            pl.BlockSpec((1, DIFF_VDIM), fixed),
            pl.BlockSpec((1, DIFF_VDIM), fixed),
        ],
        out_specs=[
            pl.BlockSpec((tm, CONV_WIDTH), row),
            pl.BlockSpec((1, 1, DIFF_WIDTH, tm), tile_t),
            pl.BlockSpec((tm, DIFF_WIDTH), row),
            pl.BlockSpec((1, 1, DIFF_WIDTH, tm), tile_t),
            pl.BlockSpec((tm, MEM_WIDTH), row),
        ],
        out_shape=[
            jax.ShapeDtypeStruct((t, CONV_WIDTH), BF16),
            jax.ShapeDtypeStruct((nb, tps, DIFF_WIDTH, tm), BF16),
            jax.ShapeDtypeStruct((t, DIFF_WIDTH), BF16),
            jax.ShapeDtypeStruct((nb, tps, DIFF_WIDTH, tm), BF16),
            jax.ShapeDtypeStruct((t, MEM_WIDTH), F32),
        ],
        scratch_shapes=[pltpu.VMEM((tm + V7X_SUBLANES, CONV_WIDTH), F32)],
        compiler_params=pltpu.CompilerParams(
            dimension_semantics=("arbitrary",),
            vmem_limit_bytes=V7X_VMEM_LIMIT_BYTES),
        name="in_proj",
    )(x2, g_mix, w_in, conv_w, g_conv, cos_t, sin_t, gq2, gk2)


def _mem_kv_kernel(mem_ref, g_ref, w_ref, gk_ref, k_ref, v_ref):
    x = mem_ref[...]
    h = (x * _rms_scale(x, D_MODEL) * g_ref[...]).astype(BF16)
    kv = jnp.dot(h, w_ref[...], preferred_element_type=F32)
    gk = gk_ref[...]
    for hd in range(MEM_HEADS):
        lo = hd * MEM_HEAD_DIM
        kh = kv[:, lo:lo + MEM_HEAD_DIM]
        k_ref[:, lo:lo + MEM_HEAD_DIM] = (kh * _rms_scale(kh, MEM_HEAD_DIM) * gk).astype(BF16)
    v_ref[...] = kv[:, MEM_WIDTH:].astype(BF16)


def _mem_kv(mem2, g_mem, w_kv, g_mk):
    t = mem2.shape[0]
    tm = TM_MEMKV
    row = lambda i: (i, 0)
    fixed = lambda i: (0, 0)
    return pl.pallas_call(
        _mem_kv_kernel,
        grid=(t // tm,),
        in_specs=[
            pl.BlockSpec((tm, D_MODEL), row),
            pl.BlockSpec((1, D_MODEL), fixed),
            pl.BlockSpec((D_MODEL, 2 * MEM_WIDTH), fixed),
            pl.BlockSpec((1, MEM_HEAD_DIM), fixed),
        ],
        out_specs=[pl.BlockSpec((tm, MEM_WIDTH), row), pl.BlockSpec((tm, MEM_WIDTH), row)],
        out_shape=[jax.ShapeDtypeStruct((t, MEM_WIDTH), BF16),
                   jax.ShapeDtypeStruct((t, MEM_WIDTH), BF16)],
        compiler_params=pltpu.CompilerParams(
            dimension_semantics=("parallel",),
            vmem_limit_bytes=V7X_VMEM_LIMIT_BYTES),
        name="mem_kv",
    )(mem2, g_mem, w_kv, g_mk)


def _diff_attn_kernel(bounded_ref, qt_ref, kn_ref, vt_ref, lq1_ref, lk1_ref, lq2_ref, lk2_ref,
                      gsub_ref, o_ref):
    nq, tq = qt_ref.shape[1], qt_ref.shape[3]
    tk = vt_ref.shape[3]

    lam = (jnp.exp(jnp.sum(lq1_ref[...] * lk1_ref[...], axis=-1, keepdims=True))
           - jnp.exp(jnp.sum(lq2_ref[...] * lk2_ref[...], axis=-1, keepdims=True))
           + LAMBDA_INIT)
    key = lax.broadcasted_iota(jnp.int32, (tk, tq), 0)
    qry = lax.broadcasted_iota(jnp.int32, (tk, tq), 1)
    causal = jnp.concatenate([key <= qry, key <= qry], axis=1)
    sum_rows = (lax.broadcasted_iota(jnp.int32, (V7X_BF16_SUBLANES, tk), 0) == 0).astype(BF16)

    def scores(c, j):
        qt = qt_ref[0, c]
        is_lo = lax.broadcasted_iota(jnp.int32, qt.shape, 0) < DIFF_HALF
        zero = jnp.zeros_like(qt)
        qq = jnp.concatenate([jnp.where(is_lo, qt, zero), jnp.where(is_lo, zero, qt)], axis=1)
        return jnp.dot(kn_ref[0, j * tk:(j + 1) * tk, :], qq, preferred_element_type=F32)

    steps = [(c, j) for c in range(nq) for j in range(c + 1)]

    def attend(shift_free):
        s_next = scores(*steps[0])
        m = acc = None
        for n, (c, j) in enumerate(steps):
            s = s_next
            if n + 1 < len(steps):
                s_next = scores(*steps[n + 1])
            if j == c:
                s = jnp.where(causal, s, -jnp.inf)
            vt_ext = jnp.concatenate([vt_ref[0, j], sum_rows], axis=0)
            if shift_free:
                pv = jnp.dot(vt_ext, jnp.exp2(s).astype(BF16), preferred_element_type=F32)
                acc = pv if j == 0 else acc + pv
            else:
                m_blk = jnp.max(s, axis=0, keepdims=True)
                m_new = m_blk if j == 0 else jnp.maximum(m, m_blk)
                p = jnp.exp2(s - m_new)
                pv = jnp.dot(vt_ext, p.astype(BF16), preferred_element_type=F32)
                acc = pv if j == 0 else jnp.exp2(m - m_new) * acc + pv
                m = m_new
            if j == c:
                o_all = acc[:DIFF_VDIM] / acc[DIFF_VDIM:DIFF_VDIM + 1]
                o = (o_all[:, :tq] - lam * o_all[:, tq:]).T
                y = o * _rms_scale(o, DIFF_VDIM) * gsub_ref[...] * (1.0 - LAMBDA_INIT)
                o_ref[0, c * tq:(c + 1) * tq, :] = y.astype(o_ref.dtype)

    pl.when(bounded_ref[0] != 0)(functools.partial(attend, True))
    pl.when(bounded_ref[0] == 0)(functools.partial(attend, False))


def _diff_attn(bounded, qt4, kn3, vt4, lq1, lk1, lq2, lk2, g_sub):
    b, nq, _, tq = qt4.shape
    s = kn3.shape[1]
    nk, tk = vt4.shape[1], vt4.shape[3]
    assert tq == tk and nq == nk
    fixed = lambda bb, h: (0, 0)
    vec = lambda n: pl.BlockSpec((1, n), fixed)
    return pl.pallas_call(
        _diff_attn_kernel,
        grid=(b, DIFF_HEADS),
        in_specs=[
            pl.BlockSpec(memory_space=pltpu.SMEM),
            pl.BlockSpec((1, nq, DIFF_VDIM, tq), lambda bb, h: (bb, 0, h, 0)),
            pl.BlockSpec((1, s, DIFF_VDIM), lambda bb, h: (bb, 0, h)),
            pl.BlockSpec((1, nk, DIFF_VDIM, tk), lambda bb, h: (bb, 0, h, 0)),
            vec(DIFF_HALF), vec(DIFF_HALF), vec(DIFF_HALF), vec(DIFF_HALF),
            vec(DIFF_VDIM),
        ],
        out_specs=pl.BlockSpec((1, s, DIFF_VDIM), lambda bb, h: (bb, 0, h)),
        out_shape=jax.ShapeDtypeStruct((b, s, DIFF_WIDTH), BF16),
        compiler_params=pltpu.CompilerParams(
            dimension_semantics=("parallel", "parallel"),
            vmem_limit_bytes=V7X_VMEM_LIMIT_BYTES),
        name="diff_attn",
    )(bounded, qt4, kn3, vt4, lq1, lk1, lq2, lk2, g_sub)


def _out_proj_kernel(x_ref, yc_ref, yd_ref, qm_ref, km_ref, vm_ref, gq_ref, gmo_ref, gffn_ref,
                     w_ref, x1_ref, hf_ref):
    gq = gq_ref[...]
    probs, denoms = [], []
    for hd in range(MEM_HEADS):
        cols = slice(hd * MEM_HEAD_DIM, (hd + 1) * MEM_HEAD_DIM)
        qh = qm_ref[:, cols]
        qh = (qh * _rms_scale(qh, MEM_HEAD_DIM) * gq * (MEM_HEAD_DIM ** -0.5)).astype(BF16)
        s = lax.dot_general(qh, km_ref[0, :, cols], (((1,), (1,)), ((), ())),
                            preferred_element_type=F32)
        e = jnp.exp(s - jnp.max(s, axis=-1, keepdims=True))
        denoms.append(jnp.sum(e, axis=-1, keepdims=True))
        probs.append(e.astype(BF16))

    acc = jnp.dot(yc_ref[...], w_ref[0:CONV_WIDTH, :], preferred_element_type=F32)

    outs = []
    for hd in range(MEM_HEADS):
        cols = slice(hd * MEM_HEAD_DIM, (hd + 1) * MEM_HEAD_DIM)
        oh = jnp.dot(probs[hd], vm_ref[0, :, cols], preferred_element_type=F32)
        outs.append(oh / denoms[hd])
    o = jnp.concatenate(outs, axis=-1)
    ym = (o * _rms_scale(o, MEM_WIDTH) * gmo_ref[...]).astype(BF16)

    acc += jnp.dot(yd_ref[...], w_ref[CONV_WIDTH:CONV_WIDTH + DIFF_WIDTH, :],
                   preferred_element_type=F32)
    acc += jnp.dot(ym, w_ref[CONV_WIDTH + DIFF_WIDTH:, :], preferred_element_type=F32)
    x1 = x_ref[...] + acc
    x1_ref[...] = x1
    hf_ref[...] = (x1 * _rms_scale(x1, D_MODEL) * gffn_ref[...]).astype(BF16)


def _out_proj(x2, yc, yd, qm, km3, vm3, g_mq, g_mem_out, g_ffn, w_o):
    t = x2.shape[0]
    tm = TM_OUT
    tps = SEQ // tm
    row = lambda i: (i, 0)
    fixed = lambda i: (0, 0)
    mem_blk = lambda i: (i // tps, 0, 0)
    return pl.pallas_call(
        _out_proj_kernel,
        grid=(t // tm,),
        in_specs=[
            pl.BlockSpec((tm, D_MODEL), row),
            pl.BlockSpec((tm, CONV_WIDTH), row),
            pl.BlockSpec((tm, DIFF_WIDTH), row),
            pl.BlockSpec((tm, MEM_WIDTH), row),
            pl.BlockSpec((1, MEM_LEN, MEM_WIDTH), mem_blk),
            pl.BlockSpec((1, MEM_LEN, MEM_WIDTH), mem_blk),
            pl.BlockSpec((1, MEM_HEAD_DIM), fixed),
            pl.BlockSpec((1, MEM_WIDTH), fixed),
            pl.BlockSpec((1, D_MODEL), fixed),
            pl.BlockSpec((D_MODEL, D_MODEL), fixed, pipeline_mode=pl.Buffered(1)),
        ],
        out_specs=[pl.BlockSpec((tm, D_MODEL), row), pl.BlockSpec((tm, D_MODEL), row)],
        out_shape=[jax.ShapeDtypeStruct((t, D_MODEL), F32),
                   jax.ShapeDtypeStruct((t, D_MODEL), BF16)],
        compiler_params=pltpu.CompilerParams(
            dimension_semantics=("parallel",),
            vmem_limit_bytes=V7X_VMEM_LIMIT_BYTES),
        name="out_proj",
    )(x2, yc, yd, qm, km3, vm3, g_mq, g_mem_out, g_ffn, w_o)


def _ffn_kernel(hf_ref, x1_ref, wg_ref, wu_ref, wd_ref, o_ref):
    @pl.when(pl.program_id(1) == 0)
    def _():
        o_ref[...] = x1_ref[...]

    h = hf_ref[...]
    gate = jnp.dot(h, wg_ref[...], preferred_element_type=F32)
    up = jnp.dot(h, wu_ref[...], preferred_element_type=F32)
    act = (gate * (1.0 / (1.0 + jnp.exp(-gate))) * up).astype(BF16)
    o_ref[...] += jnp.dot(act, wd_ref[...], preferred_element_type=F32)


def _ffn(hf, x1, w_gate, w_up, w_down):
    t = x1.shape[0]
    tm, th = TM_FFN, TH_FFN
    return pl.pallas_call(
        _ffn_kernel,
        grid=(t // tm, FFN_HIDDEN // th),
        in_specs=[
            pl.BlockSpec((tm, D_MODEL), lambda i, j: (i, 0)),
            pl.BlockSpec((tm, D_MODEL), lambda i, j: (i, 0)),
            pl.BlockSpec((D_MODEL, th), lambda i, j: (0, j)),
            pl.BlockSpec((D_MODEL, th), lambda i, j: (0, j)),
            pl.BlockSpec((th, D_MODEL), lambda i, j: (j, 0)),
        ],
        out_specs=pl.BlockSpec((tm, D_MODEL), lambda i, j: (i, 0)),
        out_shape=jax.ShapeDtypeStruct((t, D_MODEL), F32),
        compiler_params=pltpu.CompilerParams(
            dimension_semantics=("parallel", "arbitrary"),
            vmem_limit_bytes=V7X_VMEM_LIMIT_BYTES),
        name="ffn",
    )(hf, x1, w_gate, w_up, w_down)


def _rope_tables(positions):
    inv_freq = ROPE_THETA ** (-jnp.arange(0, ROT_DIM, 2, dtype=F32) / ROT_DIM)
    pad = jnp.zeros((DIFF_HALF - ROT_DIM,), F32)
    lane_freq = jnp.concatenate([inv_freq, inv_freq, pad, inv_freq, inv_freq, pad])
    ang = positions.astype(F32).reshape(-1, 1) * lane_freq[None, :]
    return jnp.cos(ang), jnp.sin(ang)


def kernel(x, mem, positions, g_mix, g_mem, w_in, conv_w, g_conv_out, g_dq, g_dk,
           lam_q1, lam_k1, lam_q2, lam_k2, g_sub, w_mem_kv, g_mq, g_mk, g_mem_out,
           w_o, g_ffn, w_gate, w_up, w_down):
    b, s, d = x.shape
    assert (s, d) == (SEQ, D_MODEL) and mem.shape == (b, MEM_LEN, D_MODEL)
    assert g_mix.shape[0] == 1
    t = b * s
    x2 = x.reshape(t, d)
    mem2 = mem.reshape(b * MEM_LEN, d)
    cos_t, sin_t = _rope_tables(positions)

    yc, qt, kn, vt, qm = _in_proj(x2, g_mix, w_in[0].astype(BF16), conv_w[0], g_conv_out,
                                  cos_t, sin_t,
                                  jnp.tile(g_dq, (1, 2)), jnp.tile(g_dk, (1, 2)))
    km, vm = _mem_kv(mem2, g_mem, w_mem_kv[0].astype(BF16), g_mk)

    score_bound = (DIFF_HALF ** 0.5 * LOG2_E) * jnp.max(jnp.abs(g_dq)) * jnp.max(jnp.abs(g_dk))
    bounded = (score_bound <= MAX_SHIFT_FREE_LOG2_SCORE).astype(jnp.int32).reshape(1)
    yd = _diff_attn(bounded, qt, kn.reshape(b, s, DIFF_WIDTH), vt,
                    lam_q1, lam_k1, lam_q2, lam_k2, g_sub)
    x1, hf = _out_proj(x2, yc, yd.reshape(t, DIFF_WIDTH), qm,
                       km.reshape(b, MEM_LEN, MEM_WIDTH), vm.reshape(b, MEM_LEN, MEM_WIDTH),
                       g_mq, g_mem_out, g_ffn, w_o[0].astype(BF16))
    out = _ffn(hf, x1, w_gate[0].astype(BF16), w_up[0].astype(BF16), w_down[0].astype(BF16))
    return out.reshape(b, s, d)
```

```python
import functools

import jax
import jax.numpy as jnp
import numpy as np
from jax import lax
from jax.experimental import pallas as pl
from jax.experimental.pallas import tpu as pltpu

F32 = jnp.float32
BF16 = jnp.bfloat16

D_MODEL = 2048
SEQ = 2048
MEM_LEN = 256
CONV_WIDTH = 512
CONV_K = 3
DIFF_WIDTH = 1024
DIFF_VDIM = 128
DIFF_HALF = 64
DIFF_HEADS = 8
MEM_WIDTH = 512
MEM_HEADS = 4
MEM_HEAD_DIM = 128
IN_WIDTH = 3 * CONV_WIDTH + 3 * DIFF_WIDTH + MEM_WIDTH
ROT_DIM = 16
ROPE_THETA = 500000.0
FFN_HIDDEN = 5632
EPS = 1e-6
LAMBDA_INIT = 0.8 - 0.6 * float(np.exp(-0.3 * 0))
LOG2_E = float(np.log2(np.e))
MAX_SHIFT_FREE_LOG2_SCORE = 60.0

V7X_LANES = 128
V7X_SUBLANES = 8
V7X_BF16_SUBLANES = 16
V7X_VMEM_LIMIT_BYTES = 56 * 1024 * 1024

TM_PROJ = 256
TM_MEMKV = 256
TM_OUT = 512
TM_FFN = 512
WEIGHT_CHUNK_ROWS = 128
TH_FFN = 512

_OFF_U = 0
_OFF_C = CONV_WIDTH
_OFF_B = 2 * CONV_WIDTH
_OFF_Q = 3 * CONV_WIDTH
_OFF_K = _OFF_Q + DIFF_WIDTH
_OFF_V = _OFF_K + DIFF_WIDTH
_OFF_QM = _OFF_V + DIFF_WIDTH


def _rms_scale(t, width):
    return lax.rsqrt(jnp.sum(t * t, axis=-1, keepdims=True) * (1.0 / width) + EPS)


def _load_weight_as_bf16(w_hbm, w_scr, stage, sem):
    rows = stage.shape[1]
    n_chunks = w_hbm.shape[0] // rows

    def chunk_copy(c):
        return pltpu.make_async_copy(w_hbm.at[pl.ds(c * rows, rows), :], stage.at[c % 2],
                                     sem.at[c % 2])

    chunk_copy(0).start()
    for c in range(n_chunks):
        if c + 1 < n_chunks:
            chunk_copy(c + 1).start()
        chunk_copy(c).wait()
        w_scr[c * rows:(c + 1) * rows, :] = stage[c % 2].astype(BF16)


def _resident_weight_scratch(k, n):
    return [pltpu.VMEM((k, n), BF16), pltpu.VMEM((2, WEIGHT_CHUNK_ROWS, n), F32),
            pltpu.SemaphoreType.DMA((2,))]


def _qk_norm_rope(t, g2, cos_t, sin_a, sin_b):
    lane = lax.broadcasted_iota(jnp.int32, t.shape, 1)
    is_lo = lane < DIFF_HALF
    t2 = t * t
    s_lo = jnp.sum(jnp.where(is_lo, t2, 0.0), axis=-1, keepdims=True)
    s_hi = jnp.sum(jnp.where(is_lo, 0.0, t2), axis=-1, keepdims=True)
    r = jnp.where(is_lo,
                  lax.rsqrt(s_lo * (1.0 / DIFF_HALF) + EPS),
                  lax.rsqrt(s_hi * (1.0 / DIFF_HALF) + EPS))
    tn = t * r * g2
    half = ROT_DIM // 2
    return (tn * cos_t
            + pltpu.roll(tn, V7X_LANES - half, axis=1) * sin_a
            + pltpu.roll(tn, half, axis=1) * sin_b)


def _in_proj_kernel(x_ref, g_ref, w_hbm, convw_ref, gconv_ref, cos_ref, sin_ref,
                    gq_ref, gk_ref,
                    yconv_ref, qt_ref, kn_ref, vt_ref, qm_ref,
                    z_scr, w_ref, w_stage, w_sem, *, tiles_per_seq):
    i = pl.program_id(0)
    tm = x_ref.shape[0]
    halo = V7X_SUBLANES

    @pl.when(i == 0)
    def _():
        _load_weight_as_bf16(w_hbm, w_ref, w_stage, w_sem)

    @pl.when(i % tiles_per_seq == 0)
    def _():
        z_scr[0:halo, :] = jnp.zeros((halo, CONV_WIDTH), F32)

    x = x_ref[...]
    h = (x * _rms_scale(x, D_MODEL) * g_ref[...]).astype(BF16)

    def proj(off, width):
        return jnp.dot(h, w_ref[:, off:off + width], preferred_element_type=F32)

    z = proj(_OFF_C, CONV_WIDTH) * proj(_OFF_U, CONV_WIDTH)
    z_scr[halo:halo + tm, :] = z
    z1 = z_scr[halo - 1:halo - 1 + tm, :]
    z2 = z_scr[halo - 2:halo - 2 + tm, :]
    cw = convw_ref[...]
    conv = cw[0:1, :] * z2 + cw[1:2, :] * z1 + cw[2:3, :] * z
    y = proj(_OFF_B, CONV_WIDTH) * conv
    yconv_ref[...] = (y * _rms_scale(y, CONV_WIDTH) * gconv_ref[...]).astype(BF16)
    z_scr[0:halo, :] = z_scr[tm:tm + halo, :]

    qf = proj(_OFF_Q, DIFF_WIDTH)
    kf = proj(_OFF_K, DIFF_WIDTH)
    vf = proj(_OFF_V, DIFF_WIDTH)
    cos_t, sin_t = cos_ref[...], sin_ref[...]
    pos_in_half = lax.broadcasted_iota(jnp.int32, sin_t.shape, 1) % DIFF_HALF
    sin_a = jnp.where(pos_in_half < ROT_DIM // 2, -sin_t, 0.0)
    sin_b = jnp.where((pos_in_half >= ROT_DIM // 2) & (pos_in_half < ROT_DIM), sin_t, 0.0)
    gq, gk = gq_ref[...], gk_ref[...]
    for hd in range(DIFF_HEADS):
        cols = slice(hd * DIFF_VDIM, (hd + 1) * DIFF_VDIM)
        qn = _qk_norm_rope(qf[:, cols], gq, cos_t, sin_a, sin_b) * (DIFF_HALF ** -0.5 * LOG2_E)
        qt_ref[0, 0, cols, :] = qn.T.astype(BF16)
        kn_ref[:, cols] = _qk_norm_rope(kf[:, cols], gk, cos_t, sin_a, sin_b).astype(BF16)
        vt_ref[0, 0, cols, :] = vf[:, cols].T.astype(BF16)
    qm_ref[...] = proj(_OFF_QM, MEM_WIDTH)


def _in_proj(x2, g_mix, w_in, conv_w, g_conv, cos_t, sin_t, gq2, gk2):
    t = x2.shape[0]
    tm = TM_PROJ
    tps = SEQ // tm
    nb = t // SEQ
    grid = (t // tm,)
    row = lambda i: (i, 0)
    fixed = lambda i: (0, 0)
    tile_t = lambda i: (i // tps, i % tps, 0, 0)
    return pl.pallas_call(
        functools.partial(_in_proj_kernel, tiles_per_seq=tps),
        grid=grid,
        in_specs=[
            pl.BlockSpec((tm, D_MODEL), row),
            pl.BlockSpec((1, D_MODEL), fixed),
            pl.BlockSpec(memory_space=pl.ANY),
            pl.BlockSpec((CONV_K, CONV_WIDTH), fixed),
            pl.BlockSpec((1, CONV_WIDTH), fixed),
            pl.BlockSpec((tm, V7X_LANES), row),
            pl.BlockSpec((tm, V7X_LANES), row),
            pl.BlockSpec((1, DIFF_VDIM), fixed),
            pl.BlockSpec((1, DIFF_VDIM), fixed),
        ],
        out_specs=[
            pl.BlockSpec((tm, CONV_WIDTH), row),
            pl.BlockSpec((1, 1, DIFF_WIDTH, tm), tile_t),
            pl.BlockSpec((tm, DIFF_WIDTH), row),
            pl.BlockSpec((1, 1, DIFF_WIDTH, tm), tile_t),
            pl.BlockSpec((tm, MEM_WIDTH), row),
        ],
        out_shape=[
            jax.ShapeDtypeStruct((t, CONV_WIDTH), BF16),
            jax.ShapeDtypeStruct((nb, tps, DIFF_WIDTH, tm), BF16),
            jax.ShapeDtypeStruct((t, DIFF_WIDTH), BF16),
            jax.ShapeDtypeStruct((nb, tps, DIFF_WIDTH, tm), BF16),
            jax.ShapeDtypeStruct((t, MEM_WIDTH), F32),
        ],
        scratch_shapes=[pltpu.VMEM((tm + V7X_SUBLANES, CONV_WIDTH), F32)]
        + _resident_weight_scratch(D_MODEL, IN_WIDTH),
        compiler_params=pltpu.CompilerParams(
            dimension_semantics=("arbitrary",),
            vmem_limit_bytes=V7X_VMEM_LIMIT_BYTES),
        name="in_proj",
    )(x2, g_mix, w_in, conv_w, g_conv, cos_t, sin_t, gq2, gk2)


def _mem_kv_kernel(mem_ref, g_ref, w_hbm, gk_ref, k_ref, v_ref, w_ref, w_stage, w_sem):
    @pl.when(pl.program_id(0) == 0)
    def _():
        _load_weight_as_bf16(w_hbm, w_ref, w_stage, w_sem)

    x = mem_ref[...]
    h = (x * _rms_scale(x, D_MODEL) * g_ref[...]).astype(BF16)
    kv = jnp.dot(h, w_ref[...], preferred_element_type=F32)
    gk = gk_ref[...]
    for hd in range(MEM_HEADS):
        lo = hd * MEM_HEAD_DIM
        kh = kv[:, lo:lo + MEM_HEAD_DIM]
        k_ref[:, lo:lo + MEM_HEAD_DIM] = (kh * _rms_scale(kh, MEM_HEAD_DIM) * gk).astype(BF16)
    v_ref[...] = kv[:, MEM_WIDTH:].astype(BF16)


def _mem_kv(mem2, g_mem, w_kv, g_mk):
    t = mem2.shape[0]
    tm = TM_MEMKV
    row = lambda i: (i, 0)
    fixed = lambda i: (0, 0)
    return pl.pallas_call(
        _mem_kv_kernel,
        grid=(t // tm,),
        in_specs=[
            pl.BlockSpec((tm, D_MODEL), row),
            pl.BlockSpec((1, D_MODEL), fixed),
            pl.BlockSpec(memory_space=pl.ANY),
            pl.BlockSpec((1, MEM_HEAD_DIM), fixed),
        ],
        out_specs=[pl.BlockSpec((tm, MEM_WIDTH), row), pl.BlockSpec((tm, MEM_WIDTH), row)],
        out_shape=[jax.ShapeDtypeStruct((t, MEM_WIDTH), BF16),
                   jax.ShapeDtypeStruct((t, MEM_WIDTH), BF16)],
        scratch_shapes=_resident_weight_scratch(D_MODEL, 2 * MEM_WIDTH),
        compiler_params=pltpu.CompilerParams(
            dimension_semantics=("arbitrary",),
            vmem_limit_bytes=V7X_VMEM_LIMIT_BYTES),
        name="mem_kv",
    )(mem2, g_mem, w_kv, g_mk)


def _diff_attn_kernel(bounded_ref, qt_ref, kn_ref, vt_ref, lq1_ref, lk1_ref, lq2_ref, lk2_ref,
                      gsub_ref, o_ref):
    nq, tq = qt_ref.shape[1], qt_ref.shape[3]
    tk = vt_ref.shape[3]

    lam = (jnp.exp(jnp.sum(lq1_ref[...] * lk1_ref[...], axis=-1, keepdims=True))
           - jnp.exp(jnp.sum(lq2_ref[...] * lk2_ref[...], axis=-1, keepdims=True))
           + LAMBDA_INIT)
    key = lax.broadcasted_iota(jnp.int32, (tk, tq), 0)
    qry = lax.broadcasted_iota(jnp.int32, (tk, tq), 1)
    causal = jnp.concatenate([key <= qry, key <= qry], axis=1)
    sum_rows = (lax.broadcasted_iota(jnp.int32, (V7X_BF16_SUBLANES, tk), 0) == 0).astype(BF16)

    def scores(c, j):
        qt = qt_ref[0, c]
        is_lo = lax.broadcasted_iota(jnp.int32, qt.shape, 0) < DIFF_HALF
        zero = jnp.zeros_like(qt)
        qq = jnp.concatenate([jnp.where(is_lo, qt, zero), jnp.where(is_lo, zero, qt)], axis=1)
        return jnp.dot(kn_ref[0, j * tk:(j + 1) * tk, :], qq, preferred_element_type=F32)

    steps = [(c, j) for c in range(nq) for j in range(c + 1)]

    def attend(shift_free):
        s_next = scores(*steps[0])
        m = acc = None
        for n, (c, j) in enumerate(steps):
            s = s_next
            if n + 1 < len(steps):
                s_next = scores(*steps[n + 1])
            if j == c:
                s = jnp.where(causal, s, -jnp.inf)
            vt_ext = jnp.concatenate([vt_ref[0, j], sum_rows], axis=0)
            if shift_free:
                pv = jnp.dot(vt_ext, jnp.exp2(s).astype(BF16), preferred_element_type=F32)
                acc = pv if j == 0 else acc + pv
            else:
                m_blk = jnp.max(s, axis=0, keepdims=True)
                m_new = m_blk if j == 0 else jnp.maximum(m, m_blk)
                p = jnp.exp2(s - m_new)
                pv = jnp.dot(vt_ext, p.astype(BF16), preferred_element_type=F32)
                acc = pv if j == 0 else jnp.exp2(m - m_new) * acc + pv
                m = m_new
            if j == c:
                o_all = acc[:DIFF_VDIM] / acc[DIFF_VDIM:DIFF_VDIM + 1]
                o = (o_all[:, :tq] - lam * o_all[:, tq:]).T
                y = o * _rms_scale(o, DIFF_VDIM) * gsub_ref[...] * (1.0 - LAMBDA_INIT)
                o_ref[0, c * tq:(c + 1) * tq, :] = y.astype(o_ref.dtype)

    pl.when(bounded_ref[0] != 0)(functools.partial(attend, True))
    pl.when(bounded_ref[0] == 0)(functools.partial(attend, False))


def _diff_attn(bounded, qt4, kn3, vt4, lq1, lk1, lq2, lk2, g_sub):
    b, nq, _, tq = qt4.shape
    s = kn3.shape[1]
    nk, tk = vt4.shape[1], vt4.shape[3]
    assert tq == tk and nq == nk
    fixed = lambda bb, h: (0, 0)
    vec = lambda n: pl.BlockSpec((1, n), fixed)
    return pl.pallas_call(
        _diff_attn_kernel,
        grid=(b, DIFF_HEADS),
        in_specs=[
            pl.BlockSpec(memory_space=pltpu.SMEM),
            pl.BlockSpec((1, nq, DIFF_VDIM, tq), lambda bb, h: (bb, 0, h, 0)),
            pl.BlockSpec((1, s, DIFF_VDIM), lambda bb, h: (bb, 0, h)),
            pl.BlockSpec((1, nk, DIFF_VDIM, tk), lambda bb, h: (bb, 0, h, 0)),
            vec(DIFF_HALF), vec(DIFF_HALF), vec(DIFF_HALF), vec(DIFF_HALF),
            vec(DIFF_VDIM),
        ],
        out_specs=pl.BlockSpec((1, s, DIFF_VDIM), lambda bb, h: (bb, 0, h)),
        out_shape=jax.ShapeDtypeStruct((b, s, DIFF_WIDTH), BF16),
        compiler_params=pltpu.CompilerParams(
            dimension_semantics=("parallel", "parallel"),
            vmem_limit_bytes=V7X_VMEM_LIMIT_BYTES),
        name="diff_attn",
    )(bounded, qt4, kn3, vt4, lq1, lk1, lq2, lk2, g_sub)


def _out_proj_kernel(x_ref, yc_ref, yd_ref, qm_ref, km_ref, vm_ref, gq_ref, gmo_ref, gffn_ref,
                     w_hbm, x1_ref, hf_ref, w_ref, w_stage, w_sem):
    @pl.when(pl.program_id(0) == 0)
    def _():
        _load_weight_as_bf16(w_hbm, w_ref, w_stage, w_sem)

    gq = gq_ref[...]
    probs, denoms = [], []
    for hd in range(MEM_HEADS):
        cols = slice(hd * MEM_HEAD_DIM, (hd + 1) * MEM_HEAD_DIM)
        qh = qm_ref[:, cols]
        qh = (qh * _rms_scale(qh, MEM_HEAD_DIM) * gq * (MEM_HEAD_DIM ** -0.5)).astype(BF16)
        s = lax.dot_general(qh, km_ref[0, :, cols], (((1,), (1,)), ((), ())),
                            preferred_element_type=F32)
        e = jnp.exp(s - jnp.max(s, axis=-1, keepdims=True))
        denoms.append(jnp.sum(e, axis=-1, keepdims=True))
        probs.append(e.astype(BF16))

    acc = jnp.dot(yc_ref[...], w_ref[0:CONV_WIDTH, :], preferred_element_type=F32)

    outs = []
    for hd in range(MEM_HEADS):
        cols = slice(hd * MEM_HEAD_DIM, (hd + 1) * MEM_HEAD_DIM)
        oh = jnp.dot(probs[hd], vm_ref[0, :, cols], preferred_element_type=F32)
        outs.append(oh / denoms[hd])
    o = jnp.concatenate(outs, axis=-1)
    ym = (o * _rms_scale(o, MEM_WIDTH) * gmo_ref[...]).astype(BF16)

    acc += jnp.dot(yd_ref[...], w_ref[CONV_WIDTH:CONV_WIDTH + DIFF_WIDTH, :],
                   preferred_element_type=F32)
    acc += jnp.dot(ym, w_ref[CONV_WIDTH + DIFF_WIDTH:, :], preferred_element_type=F32)
    x1 = x_ref[...] + acc
    x1_ref[...] = x1
    hf_ref[...] = (x1 * _rms_scale(x1, D_MODEL) * gffn_ref[...]).astype(BF16)


def _out_proj(x2, yc, yd, qm, km3, vm3, g_mq, g_mem_out, g_ffn, w_o):
    t = x2.shape[0]
    tm = TM_OUT
    tps = SEQ // tm
    row = lambda i: (i, 0)
    fixed = lambda i: (0, 0)
    mem_blk = lambda i: (i // tps, 0, 0)
    return pl.pallas_call(
        _out_proj_kernel,
        grid=(t // tm,),
        in_specs=[
            pl.BlockSpec((tm, D_MODEL), row),
            pl.BlockSpec((tm, CONV_WIDTH), row),
            pl.BlockSpec((tm, DIFF_WIDTH), row),
            pl.BlockSpec((tm, MEM_WIDTH), row),
            pl.BlockSpec((1, MEM_LEN, MEM_WIDTH), mem_blk),
            pl.BlockSpec((1, MEM_LEN, MEM_WIDTH), mem_blk),
            pl.BlockSpec((1, MEM_HEAD_DIM), fixed),
            pl.BlockSpec((1, MEM_WIDTH), fixed),
            pl.BlockSpec((1, D_MODEL), fixed),
            pl.BlockSpec(memory_space=pl.ANY),
        ],
        out_specs=[pl.BlockSpec((tm, D_MODEL), row), pl.BlockSpec((tm, D_MODEL), row)],
        out_shape=[jax.ShapeDtypeStruct((t, D_MODEL), F32),
                   jax.ShapeDtypeStruct((t, D_MODEL), BF16)],
        scratch_shapes=_resident_weight_scratch(D_MODEL, D_MODEL),
        compiler_params=pltpu.CompilerParams(
            dimension_semantics=("arbitrary",),
            vmem_limit_bytes=V7X_VMEM_LIMIT_BYTES),
        name="out_proj",
    )(x2, yc, yd, qm, km3, vm3, g_mq, g_mem_out, g_ffn, w_o)


def _ffn_kernel(hf_ref, x1_ref, wg_ref, wu_ref, wd_ref, o_ref):
    @pl.when(pl.program_id(1) == 0)
    def _():
        o_ref[...] = x1_ref[...]

    h = hf_ref[...]
    gate = jnp.dot(h, wg_ref[...], preferred_element_type=F32)
    up = jnp.dot(h, wu_ref[...], preferred_element_type=F32)
    act = (gate * (1.0 / (1.0 + jnp.exp(-gate))) * up).astype(BF16)
    o_ref[...] += jnp.dot(act, wd_ref[...], preferred_element_type=F32)


def _ffn(hf, x1, w_gate, w_up, w_down):
    t = x1.shape[0]
    tm, th = TM_FFN, TH_FFN
    return pl.pallas_call(
        _ffn_kernel,
        grid=(t // tm, FFN_HIDDEN // th),
        in_specs=[
            pl.BlockSpec((tm, D_MODEL), lambda i, j: (i, 0)),
            pl.BlockSpec((tm, D_MODEL), lambda i, j: (i, 0)),
            pl.BlockSpec((D_MODEL, th), lambda i, j: (0, j)),
            pl.BlockSpec((D_MODEL, th), lambda i, j: (0, j)),
            pl.BlockSpec((th, D_MODEL), lambda i, j: (j, 0)),
        ],
        out_specs=pl.BlockSpec((tm, D_MODEL), lambda i, j: (i, 0)),
        out_shape=jax.ShapeDtypeStruct((t, D_MODEL), F32),
        compiler_params=pltpu.CompilerParams(
            dimension_semantics=("parallel", "arbitrary"),
            vmem_limit_bytes=V7X_VMEM_LIMIT_BYTES),
        name="ffn",
    )(hf, x1, w_gate, w_up, w_down)


def _rope_tables(positions):
    inv_freq = ROPE_THETA ** (-jnp.arange(0, ROT_DIM, 2, dtype=F32) / ROT_DIM)
    pad = jnp.zeros((DIFF_HALF - ROT_DIM,), F32)
    lane_freq = jnp.concatenate([inv_freq, inv_freq, pad, inv_freq, inv_freq, pad])
    ang = positions.astype(F32).reshape(-1, 1) * lane_freq[None, :]
    return jnp.cos(ang), jnp.sin(ang)


def kernel(x, mem, positions, g_mix, g_mem, w_in, conv_w, g_conv_out, g_dq, g_dk,
           lam_q1, lam_k1, lam_q2, lam_k2, g_sub, w_mem_kv, g_mq, g_mk, g_mem_out,
           w_o, g_ffn, w_gate, w_up, w_down):
    b, s, d = x.shape
    assert (s, d) == (SEQ, D_MODEL) and mem.shape == (b, MEM_LEN, D_MODEL)
    assert g_mix.shape[0] == 1
    t = b * s
    x2 = x.reshape(t, d)
    mem2 = mem.reshape(b * MEM_LEN, d)
    cos_t, sin_t = _rope_tables(positions)

    yc, qt, kn, vt, qm = _in_proj(x2, g_mix, w_in[0], conv_w[0], g_conv_out,
                                  cos_t, sin_t,
                                  jnp.tile(g_dq, (1, 2)), jnp.tile(g_dk, (1, 2)))
    km, vm = _mem_kv(mem2, g_mem, w_mem_kv[0], g_mk)

    score_bound = (DIFF_HALF ** 0.5 * LOG2_E) * jnp.max(jnp.abs(g_dq)) * jnp.max(jnp.abs(g_dk))
    bounded = (score_bound <= MAX_SHIFT_FREE_LOG2_SCORE).astype(jnp.int32).reshape(1)
    yd = _diff_attn(bounded, qt, kn.reshape(b, s, DIFF_WIDTH), vt,
                    lam_q1, lam_k1, lam_q2, lam_k2, g_sub)
    x1, hf = _out_proj(x2, yc, yd.reshape(t, DIFF_WIDTH), qm,
                       km.reshape(b, MEM_LEN, MEM_WIDTH), vm.reshape(b, MEM_LEN, MEM_WIDTH),
                       g_mq, g_mem_out, g_ffn, w_o[0])
    out = _ffn(hf, x1, w_gate[0].astype(BF16), w_up[0].astype(BF16), w_down[0].astype(BF16))
    return out.reshape(b, s, d)
```

```python
import functools

import jax
import jax.numpy as jnp
import numpy as np
from jax import lax
from jax.experimental import pallas as pl
from jax.experimental.pallas import tpu as pltpu

F32 = jnp.float32
BF16 = jnp.bfloat16

D_MODEL = 2048
SEQ = 2048
MEM_LEN = 256
CONV_WIDTH = 512
CONV_K = 3
DIFF_WIDTH = 1024
DIFF_VDIM = 128
DIFF_HALF = 64
DIFF_HEADS = 8
MEM_WIDTH = 512
MEM_HEADS = 4
MEM_HEAD_DIM = 128
IN_WIDTH = 3 * CONV_WIDTH + 3 * DIFF_WIDTH + MEM_WIDTH
ROT_DIM = 16
ROPE_THETA = 500000.0
FFN_HIDDEN = 5632
EPS = 1e-6
LAMBDA_INIT = 0.8 - 0.6 * float(np.exp(-0.3 * 0))
LOG2_E = float(np.log2(np.e))
MAX_SHIFT_FREE_LOG2_SCORE = 60.0

V7X_LANES = 128
V7X_SUBLANES = 8
V7X_BF16_SUBLANES = 16
V7X_VMEM_LIMIT_BYTES = 56 * 1024 * 1024

TM_PROJ = 256
TM_MEMKV = 256
TM_OUT = 512
TM_FFN = 1024
TH_FFN = 256
WEIGHT_CHUNK_ROWS = 128

_OFF_U = 0
_OFF_C = CONV_WIDTH
_OFF_B = 2 * CONV_WIDTH
_OFF_Q = 3 * CONV_WIDTH
_OFF_K = _OFF_Q + DIFF_WIDTH
_OFF_V = _OFF_K + DIFF_WIDTH
_OFF_QM = _OFF_V + DIFF_WIDTH


def _rms_scale(t, width):
    return lax.rsqrt(jnp.sum(t * t, axis=-1, keepdims=True) * (1.0 / width) + EPS)


def _load_weight_as_bf16(w_hbm, w_scr, stage, sem):
    rows = stage.shape[1]
    n_chunks = w_hbm.shape[0] // rows

    def chunk_copy(c):
        return pltpu.make_async_copy(w_hbm.at[pl.ds(c * rows, rows), :], stage.at[c % 2],
                                     sem.at[c % 2])

    chunk_copy(0).start()
    for c in range(n_chunks):
        if c + 1 < n_chunks:
            chunk_copy(c + 1).start()
        chunk_copy(c).wait()
        w_scr[c * rows:(c + 1) * rows, :] = stage[c % 2].astype(BF16)


def _resident_weight_scratch(k, n):
    return [pltpu.VMEM((k, n), BF16), pltpu.VMEM((2, WEIGHT_CHUNK_ROWS, n), F32),
            pltpu.SemaphoreType.DMA((2,))]


def _qk_norm_rope(t, g2, cos_t, sin_a, sin_b):
    lane = lax.broadcasted_iota(jnp.int32, t.shape, 1)
    is_lo = lane < DIFF_HALF
    t2 = t * t
    s_lo = jnp.sum(jnp.where(is_lo, t2, 0.0), axis=-1, keepdims=True)
    s_hi = jnp.sum(jnp.where(is_lo, 0.0, t2), axis=-1, keepdims=True)
    r = jnp.where(is_lo,
                  lax.rsqrt(s_lo * (1.0 / DIFF_HALF) + EPS),
                  lax.rsqrt(s_hi * (1.0 / DIFF_HALF) + EPS))
    tn = t * r * g2
    half = ROT_DIM // 2
    return (tn * cos_t
            + pltpu.roll(tn, V7X_LANES - half, axis=1) * sin_a
            + pltpu.roll(tn, half, axis=1) * sin_b)


def _in_proj_kernel(x_ref, g_ref, w_hbm, convw_ref, gconv_ref, cos_ref, sin_ref,
                    gq_ref, gk_ref,
                    yconv_ref, qt_ref, kn_ref, vt_ref, qm_ref,
                    z_scr, w_ref, w_stage, w_sem, *, tiles_per_seq):
    i = pl.program_id(0)
    tm = x_ref.shape[0]
    halo = V7X_SUBLANES

    @pl.when(i == 0)
    def _():
        _load_weight_as_bf16(w_hbm, w_ref, w_stage, w_sem)

    @pl.when(i % tiles_per_seq == 0)
    def _():
        z_scr[0:halo, :] = jnp.zeros((halo, CONV_WIDTH), F32)

    x = x_ref[...]
    h = (x * _rms_scale(x, D_MODEL) * g_ref[...]).astype(BF16)

    def proj(off, width):
        return jnp.dot(h, w_ref[:, off:off + width], preferred_element_type=F32)

    z = proj(_OFF_C, CONV_WIDTH) * proj(_OFF_U, CONV_WIDTH)
    z_scr[halo:halo + tm, :] = z
    z1 = z_scr[halo - 1:halo - 1 + tm, :]
    z2 = z_scr[halo - 2:halo - 2 + tm, :]
    cw = convw_ref[...]
    conv = cw[0:1, :] * z2 + cw[1:2, :] * z1 + cw[2:3, :] * z
    y = proj(_OFF_B, CONV_WIDTH) * conv
    yconv_ref[...] = (y * _rms_scale(y, CONV_WIDTH) * gconv_ref[...]).astype(BF16)
    z_scr[0:halo, :] = z_scr[tm:tm + halo, :]

    qf = proj(_OFF_Q, DIFF_WIDTH)
    kf = proj(_OFF_K, DIFF_WIDTH)
    vf = proj(_OFF_V, DIFF_WIDTH)
    cos_t, sin_t = cos_ref[...], sin_ref[...]
    pos_in_half = lax.broadcasted_iota(jnp.int32, sin_t.shape, 1) % DIFF_HALF
    sin_a = jnp.where(pos_in_half < ROT_DIM // 2, -sin_t, 0.0)
    sin_b = jnp.where((pos_in_half >= ROT_DIM // 2) & (pos_in_half < ROT_DIM), sin_t, 0.0)
    gq, gk = gq_ref[...], gk_ref[...]
    for hd in range(DIFF_HEADS):
        cols = slice(hd * DIFF_VDIM, (hd + 1) * DIFF_VDIM)
        qn = _qk_norm_rope(qf[:, cols], gq, cos_t, sin_a, sin_b) * (DIFF_HALF ** -0.5 * LOG2_E)
        qt_ref[0, 0, cols, :] = qn.T.astype(BF16)
        kn_ref[:, cols] = _qk_norm_rope(kf[:, cols], gk, cos_t, sin_a, sin_b).astype(BF16)
        vt_ref[0, 0, cols, :] = vf[:, cols].T.astype(BF16)
    qm_ref[...] = proj(_OFF_QM, MEM_WIDTH)


def _in_proj(x2, g_mix, w_in, conv_w, g_conv, cos_t, sin_t, gq2, gk2):
    t = x2.shape[0]
    tm = TM_PROJ
    tps = SEQ // tm
    nb = t // SEQ
    grid = (t // tm,)
    row = lambda i: (i, 0)
    fixed = lambda i: (0, 0)
    tile_t = lambda i: (i // tps, i % tps, 0, 0)
    return pl.pallas_call(
        functools.partial(_in_proj_kernel, tiles_per_seq=tps),
        grid=grid,
        in_specs=[
            pl.BlockSpec((tm, D_MODEL), row),
            pl.BlockSpec((1, D_MODEL), fixed),
            pl.BlockSpec(memory_space=pl.ANY),
            pl.BlockSpec((CONV_K, CONV_WIDTH), fixed),
            pl.BlockSpec((1, CONV_WIDTH), fixed),
            pl.BlockSpec((tm, V7X_LANES), row),
            pl.BlockSpec((tm, V7X_LANES), row),
            pl.BlockSpec((1, DIFF_VDIM), fixed),
            pl.BlockSpec((1, DIFF_VDIM), fixed),
        ],
        out_specs=[
            pl.BlockSpec((tm, CONV_WIDTH), row),
            pl.BlockSpec((1, 1, DIFF_WIDTH, tm), tile_t),
            pl.BlockSpec((tm, DIFF_WIDTH), row),
            pl.BlockSpec((1, 1, DIFF_WIDTH, tm), tile_t),
            pl.BlockSpec((tm, MEM_WIDTH), row),
        ],
        out_shape=[
            jax.ShapeDtypeStruct((t, CONV_WIDTH), BF16),
            jax.ShapeDtypeStruct((nb, tps, DIFF_WIDTH, tm), BF16),
            jax.ShapeDtypeStruct((t, DIFF_WIDTH), BF16),
            jax.ShapeDtypeStruct((nb, tps, DIFF_WIDTH, tm), BF16),
            jax.ShapeDtypeStruct((t, MEM_WIDTH), F32),
        ],
        scratch_shapes=[pltpu.VMEM((tm + V7X_SUBLANES, CONV_WIDTH), F32)]
        + _resident_weight_scratch(D_MODEL, IN_WIDTH),
        compiler_params=pltpu.CompilerParams(
            dimension_semantics=("arbitrary",),
            vmem_limit_bytes=V7X_VMEM_LIMIT_BYTES),
        name="in_proj",
    )(x2, g_mix, w_in, conv_w, g_conv, cos_t, sin_t, gq2, gk2)


def _mem_kv_kernel(mem_ref, g_ref, w_hbm, gk_ref, k_ref, v_ref, w_ref, w_stage, w_sem):
    @pl.when(pl.program_id(0) == 0)
    def _():
        _load_weight_as_bf16(w_hbm, w_ref, w_stage, w_sem)

    x = mem_ref[...]
    h = (x * _rms_scale(x, D_MODEL) * g_ref[...]).astype(BF16)
    kv = jnp.dot(h, w_ref[...], preferred_element_type=F32)
    gk = gk_ref[...]
    for hd in range(MEM_HEADS):
        lo = hd * MEM_HEAD_DIM
        kh = kv[:, lo:lo + MEM_HEAD_DIM]
        k_ref[:, lo:lo + MEM_HEAD_DIM] = (kh * _rms_scale(kh, MEM_HEAD_DIM) * gk).astype(BF16)
    v_ref[...] = kv[:, MEM_WIDTH:].astype(BF16)


def _mem_kv(mem2, g_mem, w_kv, g_mk):
    t = mem2.shape[0]
    tm = TM_MEMKV
    row = lambda i: (i, 0)
    fixed = lambda i: (0, 0)
    return pl.pallas_call(
        _mem_kv_kernel,
        grid=(t // tm,),
        in_specs=[
            pl.BlockSpec((tm, D_MODEL), row),
            pl.BlockSpec((1, D_MODEL), fixed),
            pl.BlockSpec(memory_space=pl.ANY),
            pl.BlockSpec((1, MEM_HEAD_DIM), fixed),
        ],
        out_specs=[pl.BlockSpec((tm, MEM_WIDTH), row), pl.BlockSpec((tm, MEM_WIDTH), row)],
        out_shape=[jax.ShapeDtypeStruct((t, MEM_WIDTH), BF16),
                   jax.ShapeDtypeStruct((t, MEM_WIDTH), BF16)],
        scratch_shapes=_resident_weight_scratch(D_MODEL, 2 * MEM_WIDTH),
        compiler_params=pltpu.CompilerParams(
            dimension_semantics=("arbitrary",),
            vmem_limit_bytes=V7X_VMEM_LIMIT_BYTES),
        name="mem_kv",
    )(mem2, g_mem, w_kv, g_mk)


def _diff_attn_kernel(bounded_ref, qt_ref, kn_ref, vt_ref, lq1_ref, lk1_ref, lq2_ref, lk2_ref,
                      gsub_ref, o_ref):
    nq, tq = qt_ref.shape[1], qt_ref.shape[3]
    tk = vt_ref.shape[3]

    lam = (jnp.exp(jnp.sum(lq1_ref[...] * lk1_ref[...], axis=-1, keepdims=True))
           - jnp.exp(jnp.sum(lq2_ref[...] * lk2_ref[...], axis=-1, keepdims=True))
           + LAMBDA_INIT)
    key = lax.broadcasted_iota(jnp.int32, (tk, tq), 0)
    qry = lax.broadcasted_iota(jnp.int32, (tk, tq), 1)
    causal = jnp.concatenate([key <= qry, key <= qry], axis=1)
    sum_rows = (lax.broadcasted_iota(jnp.int32, (V7X_BF16_SUBLANES, tk), 0) == 0).astype(BF16)

    def scores(c, j):
        qt = qt_ref[0, c]
        is_lo = lax.broadcasted_iota(jnp.int32, qt.shape, 0) < DIFF_HALF
        zero = jnp.zeros_like(qt)
        qq = jnp.concatenate([jnp.where(is_lo, qt, zero), jnp.where(is_lo, zero, qt)], axis=1)
        return jnp.dot(kn_ref[0, j * tk:(j + 1) * tk, :], qq, preferred_element_type=F32)

    steps = [(c, j) for c in range(nq) for j in range(c + 1)]

    def attend(shift_free):
        s_next = scores(*steps[0])
        m = acc = None
        for n, (c, j) in enumerate(steps):
            s = s_next
            if n + 1 < len(steps):
                s_next = scores(*steps[n + 1])
            if j == c:
                s = jnp.where(causal, s, -jnp.inf)
            vt_ext = jnp.concatenate([vt_ref[0, j], sum_rows], axis=0)
            if shift_free:
                pv = jnp.dot(vt_ext, jnp.exp2(s).astype(BF16), preferred_element_type=F32)
                acc = pv if j == 0 else acc + pv
            else:
                m_blk = jnp.max(s, axis=0, keepdims=True)
                m_new = m_blk if j == 0 else jnp.maximum(m, m_blk)
                p = jnp.exp2(s - m_new)
                pv = jnp.dot(vt_ext, p.astype(BF16), preferred_element_type=F32)
                acc = pv if j == 0 else jnp.exp2(m - m_new) * acc + pv
                m = m_new
            if j == c:
                o_all = acc[:DIFF_VDIM] / acc[DIFF_VDIM:DIFF_VDIM + 1]
                o = (o_all[:, :tq] - lam * o_all[:, tq:]).T
                y = o * _rms_scale(o, DIFF_VDIM) * gsub_ref[...] * (1.0 - LAMBDA_INIT)
                o_ref[0, c * tq:(c + 1) * tq, :] = y.astype(o_ref.dtype)

    pl.when(bounded_ref[0] != 0)(functools.partial(attend, True))
    pl.when(bounded_ref[0] == 0)(functools.partial(attend, False))


def _diff_attn(bounded, qt4, kn3, vt4, lq1, lk1, lq2, lk2, g_sub):
    b, nq, _, tq = qt4.shape
    s = kn3.shape[1]
    nk, tk = vt4.shape[1], vt4.shape[3]
    assert tq == tk and nq == nk
    fixed = lambda bb, h: (0, 0)
    vec = lambda n: pl.BlockSpec((1, n), fixed)
    return pl.pallas_call(
        _diff_attn_kernel,
        grid=(b, DIFF_HEADS),
        in_specs=[
            pl.BlockSpec(memory_space=pltpu.SMEM),
            pl.BlockSpec((1, nq, DIFF_VDIM, tq), lambda bb, h: (bb, 0, h, 0)),
            pl.BlockSpec((1, s, DIFF_VDIM), lambda bb, h: (bb, 0, h)),
            pl.BlockSpec((1, nk, DIFF_VDIM, tk), lambda bb, h: (bb, 0, h, 0)),
            vec(DIFF_HALF), vec(DIFF_HALF), vec(DIFF_HALF), vec(DIFF_HALF),
            vec(DIFF_VDIM),
        ],
        out_specs=pl.BlockSpec((1, s, DIFF_VDIM), lambda bb, h: (bb, 0, h)),
        out_shape=jax.ShapeDtypeStruct((b, s, DIFF_WIDTH), BF16),
        compiler_params=pltpu.CompilerParams(
            dimension_semantics=("parallel", "parallel"),
            vmem_limit_bytes=V7X_VMEM_LIMIT_BYTES),
        name="diff_attn",
    )(bounded, qt4, kn3, vt4, lq1, lk1, lq2, lk2, g_sub)


def _out_proj_kernel(x_ref, yc_ref, yd_ref, qm_ref, km_ref, vm_ref, gq_ref, gmo_ref, gffn_ref,
                     w_hbm, x1_ref, hf_ref, w_ref, w_stage, w_sem):
    @pl.when(pl.program_id(0) == 0)
    def _():
        _load_weight_as_bf16(w_hbm, w_ref, w_stage, w_sem)

    gq = gq_ref[...]
    probs, denoms = [], []
    for hd in range(MEM_HEADS):
        cols = slice(hd * MEM_HEAD_DIM, (hd + 1) * MEM_HEAD_DIM)
        qh = qm_ref[:, cols]
        qh = (qh * _rms_scale(qh, MEM_HEAD_DIM) * gq * (MEM_HEAD_DIM ** -0.5)).astype(BF16)
        s = lax.dot_general(qh, km_ref[0, :, cols], (((1,), (1,)), ((), ())),
                            preferred_element_type=F32)
        e = jnp.exp(s - jnp.max(s, axis=-1, keepdims=True))
        denoms.append(jnp.sum(e, axis=-1, keepdims=True))
        probs.append(e.astype(BF16))

    acc = jnp.dot(yc_ref[...], w_ref[0:CONV_WIDTH, :], preferred_element_type=F32)

    outs = []
    for hd in range(MEM_HEADS):
        cols = slice(hd * MEM_HEAD_DIM, (hd + 1) * MEM_HEAD_DIM)
        oh = jnp.dot(probs[hd], vm_ref[0, :, cols], preferred_element_type=F32)
        outs.append(oh / denoms[hd])
    o = jnp.concatenate(outs, axis=-1)
    ym = (o * _rms_scale(o, MEM_WIDTH) * gmo_ref[...]).astype(BF16)

    acc += jnp.dot(yd_ref[...], w_ref[CONV_WIDTH:CONV_WIDTH + DIFF_WIDTH, :],
                   preferred_element_type=F32)
    acc += jnp.dot(ym, w_ref[CONV_WIDTH + DIFF_WIDTH:, :], preferred_element_type=F32)
    x1 = x_ref[...] + acc
    x1_ref[...] = x1
    hf_ref[...] = (x1 * _rms_scale(x1, D_MODEL) * gffn_ref[...]).astype(BF16)


def _out_proj(x2, yc, yd, qm, km3, vm3, g_mq, g_mem_out, g_ffn, w_o):
    t = x2.shape[0]
    tm = TM_OUT
    tps = SEQ // tm
    row = lambda i: (i, 0)
    fixed = lambda i: (0, 0)
    mem_blk = lambda i: (i // tps, 0, 0)
    return pl.pallas_call(
        _out_proj_kernel,
        grid=(t // tm,),
        in_specs=[
            pl.BlockSpec((tm, D_MODEL), row),
            pl.BlockSpec((tm, CONV_WIDTH), row),
            pl.BlockSpec((tm, DIFF_WIDTH), row),
            pl.BlockSpec((tm, MEM_WIDTH), row),
            pl.BlockSpec((1, MEM_LEN, MEM_WIDTH), mem_blk),
            pl.BlockSpec((1, MEM_LEN, MEM_WIDTH), mem_blk),
            pl.BlockSpec((1, MEM_HEAD_DIM), fixed),
            pl.BlockSpec((1, MEM_WIDTH), fixed),
            pl.BlockSpec((1, D_MODEL), fixed),
            pl.BlockSpec(memory_space=pl.ANY),
        ],
        out_specs=[pl.BlockSpec((tm, D_MODEL), row), pl.BlockSpec((tm, D_MODEL), row)],
        out_shape=[jax.ShapeDtypeStruct((t, D_MODEL), F32),
                   jax.ShapeDtypeStruct((t, D_MODEL), BF16)],
        scratch_shapes=_resident_weight_scratch(D_MODEL, D_MODEL),
        compiler_params=pltpu.CompilerParams(
            dimension_semantics=("arbitrary",),
            vmem_limit_bytes=V7X_VMEM_LIMIT_BYTES),
        name="out_proj",
    )(x2, yc, yd, qm, km3, vm3, g_mq, g_mem_out, g_ffn, w_o)


def _ffn_kernel(hf_ref, x1_hbm, wg_ref, wu_ref, wd_ref, o_ref, x1_sem):
    i = pl.program_id(0)
    tm = o_ref.shape[0]

    def chunk(first):
        if first:
            rows = pl.ds(pl.multiple_of(i * tm, tm), tm)
            residual_copy = pltpu.make_async_copy(x1_hbm.at[rows, :], o_ref, x1_sem)
            residual_copy.start()
        h = hf_ref[...]
        gate = jnp.dot(h, wg_ref[...].astype(BF16), preferred_element_type=F32)
        up = jnp.dot(h, wu_ref[...].astype(BF16), preferred_element_type=F32)
        act = (gate * (1.0 / (1.0 + jnp.exp(-gate))) * up).astype(BF16)
        if first:
            residual_copy.wait()
        o_ref[...] += jnp.dot(act, wd_ref[...].astype(BF16), preferred_element_type=F32)

    pl.when(pl.program_id(1) == 0)(functools.partial(chunk, True))
    pl.when(pl.program_id(1) != 0)(functools.partial(chunk, False))


def _ffn(hf, x1, w_gate, w_up, w_down):
    t = x1.shape[0]
    tm, th = TM_FFN, TH_FFN
    return pl.pallas_call(
        _ffn_kernel,
        grid=(t // tm, FFN_HIDDEN // th),
        in_specs=[
            pl.BlockSpec((tm, D_MODEL), lambda i, j: (i, 0)),
            pl.BlockSpec(memory_space=pl.ANY),
            pl.BlockSpec((D_MODEL, th), lambda i, j: (0, j)),
            pl.BlockSpec((D_MODEL, th), lambda i, j: (0, j)),
            pl.BlockSpec((th, D_MODEL), lambda i, j: (j, 0)),
        ],
        out_specs=pl.BlockSpec((tm, D_MODEL), lambda i, j: (i, 0)),
        out_shape=jax.ShapeDtypeStruct((t, D_MODEL), F32),
        scratch_shapes=[pltpu.SemaphoreType.DMA(())],
        compiler_params=pltpu.CompilerParams(
            dimension_semantics=("parallel", "arbitrary"),
            vmem_limit_bytes=V7X_VMEM_LIMIT_BYTES),
        name="ffn",
    )(hf, x1, w_gate, w_up, w_down)


def _rope_tables(positions):
    inv_freq = ROPE_THETA ** (-jnp.arange(0, ROT_DIM, 2, dtype=F32) / ROT_DIM)
    pad = jnp.zeros((DIFF_HALF - ROT_DIM,), F32)
    lane_freq = jnp.concatenate([inv_freq, inv_freq, pad, inv_freq, inv_freq, pad])
    ang = positions.astype(F32).reshape(-1, 1) * lane_freq[None, :]
    return jnp.cos(ang), jnp.sin(ang)


def kernel(x, mem, positions, g_mix, g_mem, w_in, conv_w, g_conv_out, g_dq, g_dk,
           lam_q1, lam_k1, lam_q2, lam_k2, g_sub, w_mem_kv, g_mq, g_mk, g_mem_out,
           w_o, g_ffn, w_gate, w_up, w_down):
    b, s, d = x.shape
    assert (s, d) == (SEQ, D_MODEL) and mem.shape == (b, MEM_LEN, D_MODEL)
    assert g_mix.shape[0] == 1
    t = b * s
    x2 = x.reshape(t, d)
    mem2 = mem.reshape(b * MEM_LEN, d)
    cos_t, sin_t = _rope_tables(positions)

    yc, qt, kn, vt, qm = _in_proj(x2, g_mix, w_in[0], conv_w[0], g_conv_out,
                                  cos_t, sin_t,
                                  jnp.tile(g_dq, (1, 2)), jnp.tile(g_dk, (1, 2)))
    km, vm = _mem_kv(mem2, g_mem, w_mem_kv[0], g_mk)

    score_bound = (DIFF_HALF ** 0.5 * LOG2_E) * jnp.max(jnp.abs(g_dq)) * jnp.max(jnp.abs(g_dk))
    bounded = (score_bound <= MAX_SHIFT_FREE_LOG2_SCORE).astype(jnp.int32).reshape(1)
    yd = _diff_attn(bounded, qt, kn.reshape(b, s, DIFF_WIDTH), vt,
                    lam_q1, lam_k1, lam_q2, lam_k2, g_sub)
    x1, hf = _out_proj(x2, yc, yd.reshape(t, DIFF_WIDTH), qm,
                       km.reshape(b, MEM_LEN, MEM_WIDTH), vm.reshape(b, MEM_LEN, MEM_WIDTH),
                       g_mq, g_mem_out, g_ffn, w_o[0])
    out = _ffn(hf, x1, w_gate[0], w_up[0], w_down[0])
    return out.reshape(b, s, d)
```

```python
import functools

import jax
import jax.numpy as jnp
import numpy as np
from jax import lax
from jax.experimental import pallas as pl
from jax.experimental.pallas import tpu as pltpu

F32 = jnp.float32
BF16 = jnp.bfloat16

D_MODEL = 2048
SEQ = 2048
MEM_LEN = 256
CONV_WIDTH = 512
CONV_K = 3
DIFF_WIDTH = 1024
DIFF_VDIM = 128
DIFF_HALF = 64
DIFF_HEADS = 8
MEM_WIDTH = 512
MEM_HEADS = 4
MEM_HEAD_DIM = 128
IN_WIDTH = 3 * CONV_WIDTH + 3 * DIFF_WIDTH + MEM_WIDTH
ROT_DIM = 16
ROPE_THETA = 500000.0
FFN_HIDDEN = 5632
EPS = 1e-6
LAMBDA_INIT = 0.8 - 0.6 * float(np.exp(-0.3 * 0))
LOG2_E = float(np.log2(np.e))
MAX_SHIFT_FREE_LOG2_SCORE = 60.0

V7X_LANES = 128
V7X_SUBLANES = 8
V7X_BF16_SUBLANES = 16
V7X_VMEM_LIMIT_BYTES = 56 * 1024 * 1024

TM_PROJ = 256
TM_MEMKV = 256
TM_OUT = 512
TM_FFN = 1024
TH_FFN = 256
WEIGHT_CHUNK_ROWS = 128
SCORE_LOOKAHEAD = 2

_OFF_U = 0
_OFF_C = CONV_WIDTH
_OFF_B = 2 * CONV_WIDTH
_OFF_Q = 3 * CONV_WIDTH
_OFF_K = _OFF_Q + DIFF_WIDTH
_OFF_V = _OFF_K + DIFF_WIDTH
_OFF_QM = _OFF_V + DIFF_WIDTH


def _rms_scale(t, width):
    return lax.rsqrt(jnp.sum(t * t, axis=-1, keepdims=True) * (1.0 / width) + EPS)


def _load_weight_as_bf16(w_hbm, w_scr, stage, sem):
    rows = stage.shape[1]
    n_chunks = w_hbm.shape[0] // rows

    def chunk_copy(c):
        return pltpu.make_async_copy(w_hbm.at[pl.ds(c * rows, rows), :], stage.at[c % 2],
                                     sem.at[c % 2])

    chunk_copy(0).start()
    for c in range(n_chunks):
        if c + 1 < n_chunks:
            chunk_copy(c + 1).start()
        chunk_copy(c).wait()
        w_scr[c * rows:(c + 1) * rows, :] = stage[c % 2].astype(BF16)


def _resident_weight_scratch(k, n):
    return [pltpu.VMEM((k, n), BF16), pltpu.VMEM((2, WEIGHT_CHUNK_ROWS, n), F32),
            pltpu.SemaphoreType.DMA((2,))]


def _qk_norm_rope(t, g2, cos_t, sin_a, sin_b):
    lane = lax.broadcasted_iota(jnp.int32, t.shape, 1)
    is_lo = lane < DIFF_HALF
    t2 = t * t
    s_lo = jnp.sum(jnp.where(is_lo, t2, 0.0), axis=-1, keepdims=True)
    s_hi = jnp.sum(jnp.where(is_lo, 0.0, t2), axis=-1, keepdims=True)
    r = jnp.where(is_lo,
                  lax.rsqrt(s_lo * (1.0 / DIFF_HALF) + EPS),
                  lax.rsqrt(s_hi * (1.0 / DIFF_HALF) + EPS))
    tn = t * r * g2
    half = ROT_DIM // 2
    return (tn * cos_t
            + pltpu.roll(tn, V7X_LANES - half, axis=1) * sin_a
            + pltpu.roll(tn, half, axis=1) * sin_b)


def _in_proj_kernel(x_ref, g_ref, w_hbm, convw_ref, gconv_ref, cos_ref, sin_ref,
                    gq_ref, gk_ref,
                    yconv_ref, qt_ref, kn_ref, vt_ref, qm_ref,
                    z_scr, w_ref, w_stage, w_sem, *, tiles_per_seq):
    i = pl.program_id(0)
    tm = x_ref.shape[0]
    halo = V7X_SUBLANES

    @pl.when(i == 0)
    def _():
        _load_weight_as_bf16(w_hbm, w_ref, w_stage, w_sem)

    @pl.when(i % tiles_per_seq == 0)
    def _():
        z_scr[0:halo, :] = jnp.zeros((halo, CONV_WIDTH), F32)

    x = x_ref[...]
    h = (x * _rms_scale(x, D_MODEL) * g_ref[...]).astype(BF16)

    def proj(off, width):
        return jnp.dot(h, w_ref[:, off:off + width], preferred_element_type=F32)

    z = proj(_OFF_C, CONV_WIDTH) * proj(_OFF_U, CONV_WIDTH)
    z_scr[halo:halo + tm, :] = z
    z1 = z_scr[halo - 1:halo - 1 + tm, :]
    z2 = z_scr[halo - 2:halo - 2 + tm, :]
    cw = convw_ref[...]
    conv = cw[0:1, :] * z2 + cw[1:2, :] * z1 + cw[2:3, :] * z
    y = proj(_OFF_B, CONV_WIDTH) * conv
    yconv_ref[...] = (y * _rms_scale(y, CONV_WIDTH) * gconv_ref[...]).astype(BF16)
    z_scr[0:halo, :] = z_scr[tm:tm + halo, :]

    qf = proj(_OFF_Q, DIFF_WIDTH)
    kf = proj(_OFF_K, DIFF_WIDTH)
    vf = proj(_OFF_V, DIFF_WIDTH)
    cos_t, sin_t = cos_ref[...], sin_ref[...]
    pos_in_half = lax.broadcasted_iota(jnp.int32, sin_t.shape, 1) % DIFF_HALF
    sin_a = jnp.where(pos_in_half < ROT_DIM // 2, -sin_t, 0.0)
    sin_b = jnp.where((pos_in_half >= ROT_DIM // 2) & (pos_in_half < ROT_DIM), sin_t, 0.0)
    gq, gk = gq_ref[...], gk_ref[...]
    for hd in range(DIFF_HEADS):
        cols = slice(hd * DIFF_VDIM, (hd + 1) * DIFF_VDIM)
        qn = _qk_norm_rope(qf[:, cols], gq, cos_t, sin_a, sin_b) * (DIFF_HALF ** -0.5 * LOG2_E)
        qt_ref[0, 0, cols, :] = qn.T.astype(BF16)
        kn_ref[:, cols] = _qk_norm_rope(kf[:, cols], gk, cos_t, sin_a, sin_b).astype(BF16)
        vt_ref[0, 0, cols, :] = vf[:, cols].T.astype(BF16)
    qm_ref[...] = proj(_OFF_QM, MEM_WIDTH)


def _in_proj(x2, g_mix, w_in, conv_w, g_conv, cos_t, sin_t, gq2, gk2):
    t = x2.shape[0]
    tm = TM_PROJ
    tps = SEQ // tm
    nb = t // SEQ
    grid = (t // tm,)
    row = lambda i: (i, 0)
    fixed = lambda i: (0, 0)
    tile_t = lambda i: (i // tps, i % tps, 0, 0)
    return pl.pallas_call(
        functools.partial(_in_proj_kernel, tiles_per_seq=tps),
        grid=grid,
        in_specs=[
            pl.BlockSpec((tm, D_MODEL), row),
            pl.BlockSpec((1, D_MODEL), fixed),
            pl.BlockSpec(memory_space=pl.ANY),
            pl.BlockSpec((CONV_K, CONV_WIDTH), fixed),
            pl.BlockSpec((1, CONV_WIDTH), fixed),
            pl.BlockSpec((tm, V7X_LANES), row),
            pl.BlockSpec((tm, V7X_LANES), row),
            pl.BlockSpec((1, DIFF_VDIM), fixed),
            pl.BlockSpec((1, DIFF_VDIM), fixed),
        ],
        out_specs=[
            pl.BlockSpec((tm, CONV_WIDTH), row),
            pl.BlockSpec((1, 1, DIFF_WIDTH, tm), tile_t),
            pl.BlockSpec((tm, DIFF_WIDTH), row),
            pl.BlockSpec((1, 1, DIFF_WIDTH, tm), tile_t),
            pl.BlockSpec((tm, MEM_WIDTH), row),
        ],
        out_shape=[
            jax.ShapeDtypeStruct((t, CONV_WIDTH), BF16),
            jax.ShapeDtypeStruct((nb, tps, DIFF_WIDTH, tm), BF16),
            jax.ShapeDtypeStruct((t, DIFF_WIDTH), BF16),
            jax.ShapeDtypeStruct((nb, tps, DIFF_WIDTH, tm), BF16),
            jax.ShapeDtypeStruct((t, MEM_WIDTH), F32),
        ],
        scratch_shapes=[pltpu.VMEM((tm + V7X_SUBLANES, CONV_WIDTH), F32)]
        + _resident_weight_scratch(D_MODEL, IN_WIDTH),
        compiler_params=pltpu.CompilerParams(
            dimension_semantics=("arbitrary",),
            vmem_limit_bytes=V7X_VMEM_LIMIT_BYTES),
        name="in_proj",
    )(x2, g_mix, w_in, conv_w, g_conv, cos_t, sin_t, gq2, gk2)


def _mem_kv_kernel(mem_ref, g_ref, w_hbm, gk_ref, k_ref, v_ref, w_ref, w_stage, w_sem):
    @pl.when(pl.program_id(0) == 0)
    def _():
        _load_weight_as_bf16(w_hbm, w_ref, w_stage, w_sem)

    x = mem_ref[...]
    h = (x * _rms_scale(x, D_MODEL) * g_ref[...]).astype(BF16)
    kv = jnp.dot(h, w_ref[...], preferred_element_type=F32)
    gk = gk_ref[...]
    for hd in range(MEM_HEADS):
        lo = hd * MEM_HEAD_DIM
        kh = kv[:, lo:lo + MEM_HEAD_DIM]
        k_ref[:, lo:lo + MEM_HEAD_DIM] = (kh * _rms_scale(kh, MEM_HEAD_DIM) * gk).astype(BF16)
    v_ref[...] = kv[:, MEM_WIDTH:].astype(BF16)


def _mem_kv(mem2, g_mem, w_kv, g_mk):
    t = mem2.shape[0]
    tm = TM_MEMKV
    row = lambda i: (i, 0)
    fixed = lambda i: (0, 0)
    return pl.pallas_call(
        _mem_kv_kernel,
        grid=(t // tm,),
        in_specs=[
            pl.BlockSpec((tm, D_MODEL), row),
            pl.BlockSpec((1, D_MODEL), fixed),
            pl.BlockSpec(memory_space=pl.ANY),
            pl.BlockSpec((1, MEM_HEAD_DIM), fixed),
        ],
        out_specs=[pl.BlockSpec((tm, MEM_WIDTH), row), pl.BlockSpec((tm, MEM_WIDTH), row)],
        out_shape=[jax.ShapeDtypeStruct((t, MEM_WIDTH), BF16),
                   jax.ShapeDtypeStruct((t, MEM_WIDTH), BF16)],
        scratch_shapes=_resident_weight_scratch(D_MODEL, 2 * MEM_WIDTH),
        compiler_params=pltpu.CompilerParams(
            dimension_semantics=("arbitrary",),
            vmem_limit_bytes=V7X_VMEM_LIMIT_BYTES),
        name="mem_kv",
    )(mem2, g_mem, w_kv, g_mk)


def _diff_attn_kernel(bounded_ref, qt_ref, kn_ref, vt_ref, lq1_ref, lk1_ref, lq2_ref, lk2_ref,
                      gsub_ref, o_ref):
    nq, tq = qt_ref.shape[1], qt_ref.shape[3]
    tk = vt_ref.shape[3]

    lam = (jnp.exp(jnp.sum(lq1_ref[...] * lk1_ref[...], axis=-1, keepdims=True))
           - jnp.exp(jnp.sum(lq2_ref[...] * lk2_ref[...], axis=-1, keepdims=True))
           + LAMBDA_INIT)
    key = lax.broadcasted_iota(jnp.int32, (tk, tq), 0)
    qry = lax.broadcasted_iota(jnp.int32, (tk, tq), 1)
    causal = jnp.concatenate([key <= qry, key <= qry], axis=1)
    sum_rows = (lax.broadcasted_iota(jnp.int32, (V7X_BF16_SUBLANES, tk), 0) == 0).astype(BF16)

    def scores(c, j):
        qt = qt_ref[0, c]
        is_lo = lax.broadcasted_iota(jnp.int32, qt.shape, 0) < DIFF_HALF
        zero = jnp.zeros_like(qt)
        qq = jnp.concatenate([jnp.where(is_lo, qt, zero), jnp.where(is_lo, zero, qt)], axis=1)
        return jnp.dot(kn_ref[0, j * tk:(j + 1) * tk, :], qq, preferred_element_type=F32)

    steps = [(c, j) for c in range(nq) for j in range(c + 1)]

    def attend(shift_free):
        pending = [scores(*st) for st in steps[:SCORE_LOOKAHEAD]]
        m = acc = None
        for n, (c, j) in enumerate(steps):
            s = pending.pop(0)
            if n + SCORE_LOOKAHEAD < len(steps):
                pending.append(scores(*steps[n + SCORE_LOOKAHEAD]))
            if j == c:
                s = jnp.where(causal, s, -jnp.inf)
            vt_ext = jnp.concatenate([vt_ref[0, j], sum_rows], axis=0)
            if shift_free:
                pv = jnp.dot(vt_ext, jnp.exp2(s).astype(BF16), preferred_element_type=F32)
                acc = pv if j == 0 else acc + pv
            else:
                m_blk = jnp.max(s, axis=0, keepdims=True)
                m_new = m_blk if j == 0 else jnp.maximum(m, m_blk)
                p = jnp.exp2(s - m_new)
                pv = jnp.dot(vt_ext, p.astype(BF16), preferred_element_type=F32)
                acc = pv if j == 0 else jnp.exp2(m - m_new) * acc + pv
                m = m_new
            if j == c:
                o_all = acc[:DIFF_VDIM] / acc[DIFF_VDIM:DIFF_VDIM + 1]
                o = (o_all[:, :tq] - lam * o_all[:, tq:]).T
                y = o * _rms_scale(o, DIFF_VDIM) * gsub_ref[...] * (1.0 - LAMBDA_INIT)
                o_ref[0, c * tq:(c + 1) * tq, :] = y.astype(o_ref.dtype)

    pl.when(bounded_ref[0] != 0)(functools.partial(attend, True))
    pl.when(bounded_ref[0] == 0)(functools.partial(attend, False))


def _diff_attn(bounded, qt4, kn3, vt4, lq1, lk1, lq2, lk2, g_sub):
    b, nq, _, tq = qt4.shape
    s = kn3.shape[1]
    nk, tk = vt4.shape[1], vt4.shape[3]
    assert tq == tk and nq == nk
    fixed = lambda bb, h: (0, 0)
    vec = lambda n: pl.BlockSpec((1, n), fixed)
    return pl.pallas_call(
        _diff_attn_kernel,
        grid=(b, DIFF_HEADS),
        in_specs=[
            pl.BlockSpec(memory_space=pltpu.SMEM),
            pl.BlockSpec((1, nq, DIFF_VDIM, tq), lambda bb, h: (bb, 0, h, 0)),
            pl.BlockSpec((1, s, DIFF_VDIM), lambda bb, h: (bb, 0, h)),
            pl.BlockSpec((1, nk, DIFF_VDIM, tk), lambda bb, h: (bb, 0, h, 0)),
            vec(DIFF_HALF), vec(DIFF_HALF), vec(DIFF_HALF), vec(DIFF_HALF),
            vec(DIFF_VDIM),
        ],
        out_specs=pl.BlockSpec((1, s, DIFF_VDIM), lambda bb, h: (bb, 0, h)),
        out_shape=jax.ShapeDtypeStruct((b, s, DIFF_WIDTH), BF16),
        compiler_params=pltpu.CompilerParams(
            dimension_semantics=("parallel", "parallel"),
            vmem_limit_bytes=V7X_VMEM_LIMIT_BYTES),
        name="diff_attn",
    )(bounded, qt4, kn3, vt4, lq1, lk1, lq2, lk2, g_sub)


def _out_proj_kernel(x_ref, yc_ref, yd_ref, qm_ref, km_ref, vm_ref, gq_ref, gmo_ref, gffn_ref,
                     w_hbm, x1_ref, hf_ref, w_ref, w_stage, w_sem):
    @pl.when(pl.program_id(0) == 0)
    def _():
        _load_weight_as_bf16(w_hbm, w_ref, w_stage, w_sem)

    gq = gq_ref[...]
    probs, denoms = [], []
    for hd in range(MEM_HEADS):
        cols = slice(hd * MEM_HEAD_DIM, (hd + 1) * MEM_HEAD_DIM)
        qh = qm_ref[:, cols]
        qh = (qh * _rms_scale(qh, MEM_HEAD_DIM) * gq * (MEM_HEAD_DIM ** -0.5)).astype(BF16)
        s = lax.dot_general(qh, km_ref[0, :, cols], (((1,), (1,)), ((), ())),
                            preferred_element_type=F32)
        e = jnp.exp(s - jnp.max(s, axis=-1, keepdims=True))
        denoms.append(jnp.sum(e, axis=-1, keepdims=True))
        probs.append(e.astype(BF16))

    acc = jnp.dot(yc_ref[...], w_ref[0:CONV_WIDTH, :], preferred_element_type=F32)

    outs = []
    for hd in range(MEM_HEADS):
        cols = slice(hd * MEM_HEAD_DIM, (hd + 1) * MEM_HEAD_DIM)
        oh = jnp.dot(probs[hd], vm_ref[0, :, cols], preferred_element_type=F32)
        outs.append(oh / denoms[hd])
    o = jnp.concatenate(outs, axis=-1)
    ym = (o * _rms_scale(o, MEM_WIDTH) * gmo_ref[...]).astype(BF16)

    acc += jnp.dot(yd_ref[...], w_ref[CONV_WIDTH:CONV_WIDTH + DIFF_WIDTH, :],
                   preferred_element_type=F32)
    acc += jnp.dot(ym, w_ref[CONV_WIDTH + DIFF_WIDTH:, :], preferred_element_type=F32)
    x1 = x_ref[...] + acc
    x1_ref[...] = x1
    hf_ref[...] = (x1 * _rms_scale(x1, D_MODEL) * gffn_ref[...]).astype(BF16)


def _out_proj(x2, yc, yd, qm, km3, vm3, g_mq, g_mem_out, g_ffn, w_o):
    t = x2.shape[0]
    tm = TM_OUT
    tps = SEQ // tm
    row = lambda i: (i, 0)
    fixed = lambda i: (0, 0)
    mem_blk = lambda i: (i // tps, 0, 0)
    return pl.pallas_call(
        _out_proj_kernel,
        grid=(t // tm,),
        in_specs=[
            pl.BlockSpec((tm, D_MODEL), row),
            pl.BlockSpec((tm, CONV_WIDTH), row),
            pl.BlockSpec((tm, DIFF_WIDTH), row),
            pl.BlockSpec((tm, MEM_WIDTH), row),
            pl.BlockSpec((1, MEM_LEN, MEM_WIDTH), mem_blk),
            pl.BlockSpec((1, MEM_LEN, MEM_WIDTH), mem_blk),
            pl.BlockSpec((1, MEM_HEAD_DIM), fixed),
            pl.BlockSpec((1, MEM_WIDTH), fixed),
            pl.BlockSpec((1, D_MODEL), fixed),
            pl.BlockSpec(memory_space=pl.ANY),
        ],
        out_specs=[pl.BlockSpec((tm, D_MODEL), row), pl.BlockSpec((tm, D_MODEL), row)],
        out_shape=[jax.ShapeDtypeStruct((t, D_MODEL), F32),
                   jax.ShapeDtypeStruct((t, D_MODEL), BF16)],
        scratch_shapes=_resident_weight_scratch(D_MODEL, D_MODEL),
        compiler_params=pltpu.CompilerParams(
            dimension_semantics=("arbitrary",),
            vmem_limit_bytes=V7X_VMEM_LIMIT_BYTES),
        name="out_proj",
    )(x2, yc, yd, qm, km3, vm3, g_mq, g_mem_out, g_ffn, w_o)


def _ffn_kernel(hf_ref, x1_hbm, wg_ref, wu_ref, wd_ref, o_ref, x1_sem):
    i = pl.program_id(0)
    tm = o_ref.shape[0]

    def chunk(first):
        if first:
            rows = pl.ds(pl.multiple_of(i * tm, tm), tm)
            residual_copy = pltpu.make_async_copy(x1_hbm.at[rows, :], o_ref, x1_sem)
            residual_copy.start()
        h = hf_ref[...]
        gate = jnp.dot(h, wg_ref[...].astype(BF16), preferred_element_type=F32)
        up = jnp.dot(h, wu_ref[...].astype(BF16), preferred_element_type=F32)
        act = (gate * (1.0 / (1.0 + jnp.exp(-gate))) * up).astype(BF16)
        if first:
            residual_copy.wait()
        o_ref[...] += jnp.dot(act, wd_ref[...].astype(BF16), preferred_element_type=F32)

    pl.when(pl.program_id(1) == 0)(functools.partial(chunk, True))
    pl.when(pl.program_id(1) != 0)(functools.partial(chunk, False))


def _ffn(hf, x1, w_gate, w_up, w_down):
    t = x1.shape[0]
    tm, th = TM_FFN, TH_FFN
    return pl.pallas_call(
        _ffn_kernel,
        grid=(t // tm, FFN_HIDDEN // th),
        in_specs=[
            pl.BlockSpec((tm, D_MODEL), lambda i, j: (i, 0)),
            pl.BlockSpec(memory_space=pl.ANY),
            pl.BlockSpec((D_MODEL, th), lambda i, j: (0, j)),
            pl.BlockSpec((D_MODEL, th), lambda i, j: (0, j)),
            pl.BlockSpec((th, D_MODEL), lambda i, j: (j, 0)),
        ],
        out_specs=pl.BlockSpec((tm, D_MODEL), lambda i, j: (i, 0)),
        out_shape=jax.ShapeDtypeStruct((t, D_MODEL), F32),
        scratch_shapes=[pltpu.SemaphoreType.DMA(())],
        compiler_params=pltpu.CompilerParams(
            dimension_semantics=("parallel", "arbitrary"),
            vmem_limit_bytes=V7X_VMEM_LIMIT_BYTES),
        name="ffn",
    )(hf, x1, w_gate, w_up, w_down)


def _rope_tables(positions):
    inv_freq = ROPE_THETA ** (-jnp.arange(0, ROT_DIM, 2, dtype=F32) / ROT_DIM)
    pad = jnp.zeros((DIFF_HALF - ROT_DIM,), F32)
    lane_freq = jnp.concatenate([inv_freq, inv_freq, pad, inv_freq, inv_freq, pad])
    ang = positions.astype(F32).reshape(-1, 1) * lane_freq[None, :]
    return jnp.cos(ang), jnp.sin(ang)


def kernel(x, mem, positions, g_mix, g_mem, w_in, conv_w, g_conv_out, g_dq, g_dk,
           lam_q1, lam_k1, lam_q2, lam_k2, g_sub, w_mem_kv, g_mq, g_mk, g_mem_out,
           w_o, g_ffn, w_gate, w_up, w_down):
    b, s, d = x.shape
    assert (s, d) == (SEQ, D_MODEL) and mem.shape == (b, MEM_LEN, D_MODEL)
    assert g_mix.shape[0] == 1
    t = b * s
    x2 = x.reshape(t, d)
    mem2 = mem.reshape(b * MEM_LEN, d)
    cos_t, sin_t = _rope_tables(positions)

    yc, qt, kn, vt, qm = _in_proj(x2, g_mix, w_in[0], conv_w[0], g_conv_out,
                                  cos_t, sin_t,
                                  jnp.tile(g_dq, (1, 2)), jnp.tile(g_dk, (1, 2)))
    km, vm = _mem_kv(mem2, g_mem, w_mem_kv[0], g_mk)

    score_bound = (DIFF_HALF ** 0.5 * LOG2_E) * jnp.max(jnp.abs(g_dq)) * jnp.max(jnp.abs(g_dk))
    bounded = (score_bound <= MAX_SHIFT_FREE_LOG2_SCORE).astype(jnp.int32).reshape(1)
    yd = _diff_attn(bounded, qt, kn.reshape(b, s, DIFF_WIDTH), vt,
                    lam_q1, lam_k1, lam_q2, lam_k2, g_sub)
    x1, hf = _out_proj(x2, yc, yd.reshape(t, DIFF_WIDTH), qm,
                       km.reshape(b, MEM_LEN, MEM_WIDTH), vm.reshape(b, MEM_LEN, MEM_WIDTH),
                       g_mq, g_mem_out, g_ffn, w_o[0])
    out = _ffn(hf, x1, w_gate[0], w_up[0], w_down[0])
    return out.reshape(b, s, d)
```

```python
import functools

import jax
import jax.numpy as jnp
import numpy as np
from jax import lax
from jax.experimental import pallas as pl
from jax.experimental.pallas import tpu as pltpu

F32 = jnp.float32
BF16 = jnp.bfloat16

D_MODEL = 2048
SEQ = 2048
MEM_LEN = 256
CONV_WIDTH = 512
CONV_K = 3
DIFF_WIDTH = 1024
DIFF_VDIM = 128
DIFF_HALF = 64
DIFF_HEADS = 8
MEM_WIDTH = 512
MEM_HEADS = 4
MEM_HEAD_DIM = 128
IN_WIDTH = 3 * CONV_WIDTH + 3 * DIFF_WIDTH + MEM_WIDTH
ROT_DIM = 16
ROPE_THETA = 500000.0
FFN_HIDDEN = 5632
EPS = 1e-6
LAMBDA_INIT = 0.8 - 0.6 * float(np.exp(-0.3 * 0))
LOG2_E = float(np.log2(np.e))
MAX_SHIFT_FREE_LOG2_SCORE = 60.0

V7X_LANES = 128
V7X_SUBLANES = 8
V7X_BF16_SUBLANES = 16
V7X_VMEM_LIMIT_BYTES = 56 * 1024 * 1024

TM_PROJ = 256
TM_MEMKV = 256
TM_OUT = 512
TM_FFN = 1024
TH_FFN = 256
WEIGHT_CHUNK_ROWS = 128
SCORE_LOOKAHEAD = 2
FFN_CAST_COLS = 1024

_OFF_U = 0
_OFF_C = CONV_WIDTH
_OFF_B = 2 * CONV_WIDTH
_OFF_Q = 3 * CONV_WIDTH
_OFF_K = _OFF_Q + DIFF_WIDTH
_OFF_V = _OFF_K + DIFF_WIDTH
_OFF_QM = _OFF_V + DIFF_WIDTH


def _rms_scale(t, width):
    return lax.rsqrt(jnp.sum(t * t, axis=-1, keepdims=True) * (1.0 / width) + EPS)


def _load_weight_as_bf16(w_hbm, w_scr, stage, sem):
    rows = stage.shape[1]
    n_chunks = w_hbm.shape[0] // rows

    def chunk_copy(c):
        return pltpu.make_async_copy(w_hbm.at[pl.ds(c * rows, rows), :], stage.at[c % 2],
                                     sem.at[c % 2])

    chunk_copy(0).start()
    for c in range(n_chunks):
        if c + 1 < n_chunks:
            chunk_copy(c + 1).start()
        chunk_copy(c).wait()
        w_scr[c * rows:(c + 1) * rows, :] = stage[c % 2].astype(BF16)


def _resident_weight_scratch(k, n):
    return [pltpu.VMEM((k, n), BF16), pltpu.VMEM((2, WEIGHT_CHUNK_ROWS, n), F32),
            pltpu.SemaphoreType.DMA((2,))]


def _qk_norm_rope(t, g2, cos_t, sin_a, sin_b):
    lane = lax.broadcasted_iota(jnp.int32, t.shape, 1)
    is_lo = lane < DIFF_HALF
    t2 = t * t
    s_lo = jnp.sum(jnp.where(is_lo, t2, 0.0), axis=-1, keepdims=True)
    s_hi = jnp.sum(jnp.where(is_lo, 0.0, t2), axis=-1, keepdims=True)
    r = jnp.where(is_lo,
                  lax.rsqrt(s_lo * (1.0 / DIFF_HALF) + EPS),
                  lax.rsqrt(s_hi * (1.0 / DIFF_HALF) + EPS))
    tn = t * r * g2
    half = ROT_DIM // 2
    return (tn * cos_t
            + pltpu.roll(tn, V7X_LANES - half, axis=1) * sin_a
            + pltpu.roll(tn, half, axis=1) * sin_b)


def _in_proj_kernel(x_ref, g_ref, w_hbm, convw_ref, gconv_ref, cos_ref, sin_ref,
                    gq_ref, gk_ref,
                    yconv_ref, qt_ref, kn_ref, vt_ref, qm_ref,
                    z_scr, w_ref, w_stage, w_sem, *, tiles_per_seq):
    i = pl.program_id(0)
    tm = x_ref.shape[0]
    halo = V7X_SUBLANES

    @pl.when(i == 0)
    def _():
        _load_weight_as_bf16(w_hbm, w_ref, w_stage, w_sem)

    @pl.when(i % tiles_per_seq == 0)
    def _():
        z_scr[0:halo, :] = jnp.zeros((halo, CONV_WIDTH), F32)

    x = x_ref[...]
    h = (x * _rms_scale(x, D_MODEL) * g_ref[...]).astype(BF16)

    def proj(off, width):
        return jnp.dot(h, w_ref[:, off:off + width], preferred_element_type=F32)

    z = proj(_OFF_C, CONV_WIDTH) * proj(_OFF_U, CONV_WIDTH)
    z_scr[halo:halo + tm, :] = z
    z1 = z_scr[halo - 1:halo - 1 + tm, :]
    z2 = z_scr[halo - 2:halo - 2 + tm, :]
    cw = convw_ref[...]
    conv = cw[0:1, :] * z2 + cw[1:2, :] * z1 + cw[2:3, :] * z
    y = proj(_OFF_B, CONV_WIDTH) * conv
    yconv_ref[...] = (y * _rms_scale(y, CONV_WIDTH) * gconv_ref[...]).astype(BF16)
    z_scr[0:halo, :] = z_scr[tm:tm + halo, :]

    qf = proj(_OFF_Q, DIFF_WIDTH)
    kf = proj(_OFF_K, DIFF_WIDTH)
    vf = proj(_OFF_V, DIFF_WIDTH)
    cos_t, sin_t = cos_ref[...], sin_ref[...]
    pos_in_half = lax.broadcasted_iota(jnp.int32, sin_t.shape, 1) % DIFF_HALF
    sin_a = jnp.where(pos_in_half < ROT_DIM // 2, -sin_t, 0.0)
    sin_b = jnp.where((pos_in_half >= ROT_DIM // 2) & (pos_in_half < ROT_DIM), sin_t, 0.0)
    gq, gk = gq_ref[...], gk_ref[...]
    for hd in range(DIFF_HEADS):
        cols = slice(hd * DIFF_VDIM, (hd + 1) * DIFF_VDIM)
        qn = _qk_norm_rope(qf[:, cols], gq, cos_t, sin_a, sin_b) * (DIFF_HALF ** -0.5 * LOG2_E)
        qt_ref[0, 0, cols, :] = qn.T.astype(BF16)
        kn_ref[:, cols] = _qk_norm_rope(kf[:, cols], gk, cos_t, sin_a, sin_b).astype(BF16)
        vt_ref[0, 0, cols, :] = vf[:, cols].T.astype(BF16)
    qm_ref[...] = proj(_OFF_QM, MEM_WIDTH)


def _in_proj(x2, g_mix, w_in, conv_w, g_conv, cos_t, sin_t, gq2, gk2):
    t = x2.shape[0]
    tm = TM_PROJ
    tps = SEQ // tm
    nb = t // SEQ
    grid = (t // tm,)
    row = lambda i: (i, 0)
    fixed = lambda i: (0, 0)
    tile_t = lambda i: (i // tps, i % tps, 0, 0)
    return pl.pallas_call(
        functools.partial(_in_proj_kernel, tiles_per_seq=tps),
        grid=grid,
        in_specs=[
            pl.BlockSpec((tm, D_MODEL), row),
            pl.BlockSpec((1, D_MODEL), fixed),
            pl.BlockSpec(memory_space=pl.ANY),
            pl.BlockSpec((CONV_K, CONV_WIDTH), fixed),
            pl.BlockSpec((1, CONV_WIDTH), fixed),
            pl.BlockSpec((tm, V7X_LANES), row),
            pl.BlockSpec((tm, V7X_LANES), row),
            pl.BlockSpec((1, DIFF_VDIM), fixed),
            pl.BlockSpec((1, DIFF_VDIM), fixed),
        ],
        out_specs=[
            pl.BlockSpec((tm, CONV_WIDTH), row),
            pl.BlockSpec((1, 1, DIFF_WIDTH, tm), tile_t),
            pl.BlockSpec((tm, DIFF_WIDTH), row),
            pl.BlockSpec((1, 1, DIFF_WIDTH, tm), tile_t),
            pl.BlockSpec((tm, MEM_WIDTH), row),
        ],
        out_shape=[
            jax.ShapeDtypeStruct((t, CONV_WIDTH), BF16),
            jax.ShapeDtypeStruct((nb, tps, DIFF_WIDTH, tm), BF16),
            jax.ShapeDtypeStruct((t, DIFF_WIDTH), BF16),
            jax.ShapeDtypeStruct((nb, tps, DIFF_WIDTH, tm), BF16),
            jax.ShapeDtypeStruct((t, MEM_WIDTH), F32),
        ],
        scratch_shapes=[pltpu.VMEM((tm + V7X_SUBLANES, CONV_WIDTH), F32)]
        + _resident_weight_scratch(D_MODEL, IN_WIDTH),
        compiler_params=pltpu.CompilerParams(
            dimension_semantics=("arbitrary",),
            vmem_limit_bytes=V7X_VMEM_LIMIT_BYTES),
        name="in_proj",
    )(x2, g_mix, w_in, conv_w, g_conv, cos_t, sin_t, gq2, gk2)


def _mem_kv_kernel(mem_ref, g_ref, w_hbm, gk_ref, k_ref, v_ref, w_ref, w_stage, w_sem):
    @pl.when(pl.program_id(0) == 0)
    def _():
        _load_weight_as_bf16(w_hbm, w_ref, w_stage, w_sem)

    x = mem_ref[...]
    h = (x * _rms_scale(x, D_MODEL) * g_ref[...]).astype(BF16)
    kv = jnp.dot(h, w_ref[...], preferred_element_type=F32)
    gk = gk_ref[...]
    for hd in range(MEM_HEADS):
        lo = hd * MEM_HEAD_DIM
        kh = kv[:, lo:lo + MEM_HEAD_DIM]
        k_ref[:, lo:lo + MEM_HEAD_DIM] = (kh * _rms_scale(kh, MEM_HEAD_DIM) * gk).astype(BF16)
    v_ref[...] = kv[:, MEM_WIDTH:].astype(BF16)


def _mem_kv(mem2, g_mem, w_kv, g_mk):
    t = mem2.shape[0]
    tm = TM_MEMKV
    row = lambda i: (i, 0)
    fixed = lambda i: (0, 0)
    return pl.pallas_call(
        _mem_kv_kernel,
        grid=(t // tm,),
        in_specs=[
            pl.BlockSpec((tm, D_MODEL), row),
            pl.BlockSpec((1, D_MODEL), fixed),
            pl.BlockSpec(memory_space=pl.ANY),
            pl.BlockSpec((1, MEM_HEAD_DIM), fixed),
        ],
        out_specs=[pl.BlockSpec((tm, MEM_WIDTH), row), pl.BlockSpec((tm, MEM_WIDTH), row)],
        out_shape=[jax.ShapeDtypeStruct((t, MEM_WIDTH), BF16),
                   jax.ShapeDtypeStruct((t, MEM_WIDTH), BF16)],
        scratch_shapes=_resident_weight_scratch(D_MODEL, 2 * MEM_WIDTH),
        compiler_params=pltpu.CompilerParams(
            dimension_semantics=("arbitrary",),
            vmem_limit_bytes=V7X_VMEM_LIMIT_BYTES),
        name="mem_kv",
    )(mem2, g_mem, w_kv, g_mk)


def _diff_attn_kernel(bounded_ref, qt_ref, kn_ref, vt_ref, lq1_ref, lk1_ref, lq2_ref, lk2_ref,
                      gsub_ref, wg_ref, wu_ref, wd_ref, o_ref, wg_out, wu_out, wd_out):
    nq, tq = qt_ref.shape[1], qt_ref.shape[3]
    tk = vt_ref.shape[3]

    lam = (jnp.exp(jnp.sum(lq1_ref[...] * lk1_ref[...], axis=-1, keepdims=True))
           - jnp.exp(jnp.sum(lq2_ref[...] * lk2_ref[...], axis=-1, keepdims=True))
           + LAMBDA_INIT)
    key = lax.broadcasted_iota(jnp.int32, (tk, tq), 0)
    qry = lax.broadcasted_iota(jnp.int32, (tk, tq), 1)
    causal = jnp.concatenate([key <= qry, key <= qry], axis=1)
    sum_rows = (lax.broadcasted_iota(jnp.int32, (V7X_BF16_SUBLANES, tk), 0) == 0).astype(BF16)

    def scores(c, j):
        qt = qt_ref[0, c]
        is_lo = lax.broadcasted_iota(jnp.int32, qt.shape, 0) < DIFF_HALF
        zero = jnp.zeros_like(qt)
        qq = jnp.concatenate([jnp.where(is_lo, qt, zero), jnp.where(is_lo, zero, qt)], axis=1)
        return jnp.dot(kn_ref[0, j * tk:(j + 1) * tk, :], qq, preferred_element_type=F32)

    steps = [(c, j) for c in range(nq) for j in range(c + 1)]

    def attend(shift_free):
        for w_f32, w_bf16 in ((wg_ref, wg_out), (wu_ref, wu_out), (wd_ref, wd_out)):
            w_bf16[...] = w_f32[...].astype(BF16)
        pending = [scores(*st) for st in steps[:SCORE_LOOKAHEAD]]
        m = acc = None
        for n, (c, j) in enumerate(steps):
            s = pending.pop(0)
            if n + SCORE_LOOKAHEAD < len(steps):
                pending.append(scores(*steps[n + SCORE_LOOKAHEAD]))
            if j == c:
                s = jnp.where(causal, s, -jnp.inf)
            vt_ext = jnp.concatenate([vt_ref[0, j], sum_rows], axis=0)
            if shift_free:
                pv = jnp.dot(vt_ext, jnp.exp2(s).astype(BF16), preferred_element_type=F32)
                acc = pv if j == 0 else acc + pv
            else:
                m_blk = jnp.max(s, axis=0, keepdims=True)
                m_new = m_blk if j == 0 else jnp.maximum(m, m_blk)
                p = jnp.exp2(s - m_new)
                pv = jnp.dot(vt_ext, p.astype(BF16), preferred_element_type=F32)
                acc = pv if j == 0 else jnp.exp2(m - m_new) * acc + pv
                m = m_new
            if j == c:
                o_all = acc[:DIFF_VDIM] / acc[DIFF_VDIM:DIFF_VDIM + 1]
                o = (o_all[:, :tq] - lam * o_all[:, tq:]).T
                y = o * _rms_scale(o, DIFF_VDIM) * gsub_ref[...] * (1.0 - LAMBDA_INIT)
                o_ref[0, c * tq:(c + 1) * tq, :] = y.astype(o_ref.dtype)

    pl.when(bounded_ref[0] != 0)(functools.partial(attend, True))
    pl.when(bounded_ref[0] == 0)(functools.partial(attend, False))


def _diff_attn(bounded, qt4, kn3, vt4, lq1, lk1, lq2, lk2, g_sub, ffn_weights):
    b, nq, _, tq = qt4.shape
    s = kn3.shape[1]
    nk, tk = vt4.shape[1], vt4.shape[3]
    assert tq == tk and nq == nk
    fixed = lambda bb, h: (0, 0)
    vec = lambda n: pl.BlockSpec((1, n), fixed)
    w_rows = ffn_weights[0].shape[0]
    cast_rows = w_rows // (b * DIFF_HEADS)
    assert cast_rows * b * DIFF_HEADS == w_rows and cast_rows % V7X_BF16_SUBLANES == 0
    w_blk = pl.BlockSpec((cast_rows, FFN_CAST_COLS), lambda bb, h: (bb * DIFF_HEADS + h, 0))
    return pl.pallas_call(
        _diff_attn_kernel,
        grid=(b, DIFF_HEADS),
        in_specs=[
            pl.BlockSpec(memory_space=pltpu.SMEM),
            pl.BlockSpec((1, nq, DIFF_VDIM, tq), lambda bb, h: (bb, 0, h, 0)),
            pl.BlockSpec((1, s, DIFF_VDIM), lambda bb, h: (bb, 0, h)),
            pl.BlockSpec((1, nk, DIFF_VDIM, tk), lambda bb, h: (bb, 0, h, 0)),
            vec(DIFF_HALF), vec(DIFF_HALF), vec(DIFF_HALF), vec(DIFF_HALF),
            vec(DIFF_VDIM),
            w_blk, w_blk, w_blk,
        ],
        out_specs=[pl.BlockSpec((1, s, DIFF_VDIM), lambda bb, h: (bb, 0, h)),
                   w_blk, w_blk, w_blk],
        out_shape=[jax.ShapeDtypeStruct((b, s, DIFF_WIDTH), BF16)]
        + [jax.ShapeDtypeStruct(w.shape, BF16) for w in ffn_weights],
        compiler_params=pltpu.CompilerParams(
            dimension_semantics=("parallel", "parallel"),
            vmem_limit_bytes=V7X_VMEM_LIMIT_BYTES),
        name="diff_attn",
    )(bounded, qt4, kn3, vt4, lq1, lk1, lq2, lk2, g_sub, *ffn_weights)


def _out_proj_kernel(x_ref, yc_ref, yd_ref, qm_ref, km_ref, vm_ref, gq_ref, gmo_ref, gffn_ref,
                     w_hbm, x1_ref, hf_ref, w_ref, w_stage, w_sem):
    @pl.when(pl.program_id(0) == 0)
    def _():
        _load_weight_as_bf16(w_hbm, w_ref, w_stage, w_sem)

    gq = gq_ref[...]
    probs, denoms = [], []
    for hd in range(MEM_HEADS):
        cols = slice(hd * MEM_HEAD_DIM, (hd + 1) * MEM_HEAD_DIM)
        qh = qm_ref[:, cols]
        qh = (qh * _rms_scale(qh, MEM_HEAD_DIM) * gq * (MEM_HEAD_DIM ** -0.5)).astype(BF16)
        s = lax.dot_general(qh, km_ref[0, :, cols], (((1,), (1,)), ((), ())),
                            preferred_element_type=F32)
        e = jnp.exp(s - jnp.max(s, axis=-1, keepdims=True))
        denoms.append(jnp.sum(e, axis=-1, keepdims=True))
        probs.append(e.astype(BF16))

    acc = jnp.dot(yc_ref[...], w_ref[0:CONV_WIDTH, :], preferred_element_type=F32)

    outs = []
    for hd in range(MEM_HEADS):
        cols = slice(hd * MEM_HEAD_DIM, (hd + 1) * MEM_HEAD_DIM)
        oh = jnp.dot(probs[hd], vm_ref[0, :, cols], preferred_element_type=F32)
        outs.append(oh / denoms[hd])
    o = jnp.concatenate(outs, axis=-1)
    ym = (o * _rms_scale(o, MEM_WIDTH) * gmo_ref[...]).astype(BF16)

    acc += jnp.dot(yd_ref[...], w_ref[CONV_WIDTH:CONV_WIDTH + DIFF_WIDTH, :],
                   preferred_element_type=F32)
    acc += jnp.dot(ym, w_ref[CONV_WIDTH + DIFF_WIDTH:, :], preferred_element_type=F32)
    x1 = x_ref[...] + acc
    x1_ref[...] = x1
    hf_ref[...] = (x1 * _rms_scale(x1, D_MODEL) * gffn_ref[...]).astype(BF16)


def _out_proj(x2, yc, yd, qm, km3, vm3, g_mq, g_mem_out, g_ffn, w_o):
    t = x2.shape[0]
    tm = TM_OUT
    tps = SEQ // tm
    row = lambda i: (i, 0)
    fixed = lambda i: (0, 0)
    mem_blk = lambda i: (i // tps, 0, 0)
    return pl.pallas_call(
        _out_proj_kernel,
        grid=(t // tm,),
        in_specs=[
            pl.BlockSpec((tm, D_MODEL), row),
            pl.BlockSpec((tm, CONV_WIDTH), row),
            pl.BlockSpec((tm, DIFF_WIDTH), row),
            pl.BlockSpec((tm, MEM_WIDTH), row),
            pl.BlockSpec((1, MEM_LEN, MEM_WIDTH), mem_blk),
            pl.BlockSpec((1, MEM_LEN, MEM_WIDTH), mem_blk),
            pl.BlockSpec((1, MEM_HEAD_DIM), fixed),
            pl.BlockSpec((1, MEM_WIDTH), fixed),
            pl.BlockSpec((1, D_MODEL), fixed),
            pl.BlockSpec(memory_space=pl.ANY),
        ],
        out_specs=[pl.BlockSpec((tm, D_MODEL), row), pl.BlockSpec((tm, D_MODEL), row)],
        out_shape=[jax.ShapeDtypeStruct((t, D_MODEL), F32),
                   jax.ShapeDtypeStruct((t, D_MODEL), BF16)],
        scratch_shapes=_resident_weight_scratch(D_MODEL, D_MODEL),
        compiler_params=pltpu.CompilerParams(
            dimension_semantics=("arbitrary",),
            vmem_limit_bytes=V7X_VMEM_LIMIT_BYTES),
        name="out_proj",
    )(x2, yc, yd, qm, km3, vm3, g_mq, g_mem_out, g_ffn, w_o)


def _ffn_kernel(hf_ref, x1_hbm, wg_ref, wu_ref, wd_ref, o_ref, x1_sem):
    i = pl.program_id(0)
    tm = o_ref.shape[0]

    def chunk(first):
        if first:
            rows = pl.ds(pl.multiple_of(i * tm, tm), tm)
            residual_copy = pltpu.make_async_copy(x1_hbm.at[rows, :], o_ref, x1_sem)
            residual_copy.start()
        h = hf_ref[...]
        gate = jnp.dot(h, wg_ref[...], preferred_element_type=F32)
        up = jnp.dot(h, wu_ref[...], preferred_element_type=F32)
        act = (gate * (1.0 / (1.0 + jnp.exp(-gate))) * up).astype(BF16)
        if first:
            residual_copy.wait()
        o_ref[...] += jnp.dot(act, wd_ref[...], preferred_element_type=F32)

    pl.when(pl.program_id(1) == 0)(functools.partial(chunk, True))
    pl.when(pl.program_id(1) != 0)(functools.partial(chunk, False))


def _ffn(hf, x1, w_gate, w_up, w_down):
    t = x1.shape[0]
    tm, th = TM_FFN, TH_FFN
    return pl.pallas_call(
        _ffn_kernel,
        grid=(t // tm, FFN_HIDDEN // th),
        in_specs=[
            pl.BlockSpec((tm, D_MODEL), lambda i, j: (i, 0)),
            pl.BlockSpec(memory_space=pl.ANY),
            pl.BlockSpec((D_MODEL, th), lambda i, j: (0, j)),
            pl.BlockSpec((D_MODEL, th), lambda i, j: (0, j)),
            pl.BlockSpec((th, D_MODEL), lambda i, j: (j, 0)),
        ],
        out_specs=pl.BlockSpec((tm, D_MODEL), lambda i, j: (i, 0)),
        out_shape=jax.ShapeDtypeStruct((t, D_MODEL), F32),
        scratch_shapes=[pltpu.SemaphoreType.DMA(())],
        compiler_params=pltpu.CompilerParams(
            dimension_semantics=("parallel", "arbitrary"),
            vmem_limit_bytes=V7X_VMEM_LIMIT_BYTES),
        name="ffn",
    )(hf, x1, w_gate, w_up, w_down)


def _rope_tables(positions):
    inv_freq = ROPE_THETA ** (-jnp.arange(0, ROT_DIM, 2, dtype=F32) / ROT_DIM)
    pad = jnp.zeros((DIFF_HALF - ROT_DIM,), F32)
    lane_freq = jnp.concatenate([inv_freq, inv_freq, pad, inv_freq, inv_freq, pad])
    ang = positions.astype(F32).reshape(-1, 1) * lane_freq[None, :]
    return jnp.cos(ang), jnp.sin(ang)


def kernel(x, mem, positions, g_mix, g_mem, w_in, conv_w, g_conv_out, g_dq, g_dk,
           lam_q1, lam_k1, lam_q2, lam_k2, g_sub, w_mem_kv, g_mq, g_mk, g_mem_out,
           w_o, g_ffn, w_gate, w_up, w_down):
    b, s, d = x.shape
    assert (s, d) == (SEQ, D_MODEL) and mem.shape == (b, MEM_LEN, D_MODEL)
    assert g_mix.shape[0] == 1
    t = b * s
    x2 = x.reshape(t, d)
    mem2 = mem.reshape(b * MEM_LEN, d)
    cos_t, sin_t = _rope_tables(positions)

    yc, qt, kn, vt, qm = _in_proj(x2, g_mix, w_in[0], conv_w[0], g_conv_out,
                                  cos_t, sin_t,
                                  jnp.tile(g_dq, (1, 2)), jnp.tile(g_dk, (1, 2)))
    km, vm = _mem_kv(mem2, g_mem, w_mem_kv[0], g_mk)

    score_bound = (DIFF_HALF ** 0.5 * LOG2_E) * jnp.max(jnp.abs(g_dq)) * jnp.max(jnp.abs(g_dk))
    bounded = (score_bound <= MAX_SHIFT_FREE_LOG2_SCORE).astype(jnp.int32).reshape(1)
    ffn_f32 = [w.reshape(-1, FFN_CAST_COLS) for w in (w_gate, w_up, w_down)]
    yd, wg16, wu16, wd16 = _diff_attn(bounded, qt, kn.reshape(b, s, DIFF_WIDTH), vt,
                                      lam_q1, lam_k1, lam_q2, lam_k2, g_sub, ffn_f32)
    x1, hf = _out_proj(x2, yc, yd.reshape(t, DIFF_WIDTH), qm,
                       km.reshape(b, MEM_LEN, MEM_WIDTH), vm.reshape(b, MEM_LEN, MEM_WIDTH),
                       g_mq, g_mem_out, g_ffn, w_o[0])
    out = _ffn(hf, x1, wg16.reshape(D_MODEL, FFN_HIDDEN), wu16.reshape(D_MODEL, FFN_HIDDEN),
               wd16.reshape(FFN_HIDDEN, D_MODEL))
    return out.reshape(b, s, d)
```

```python
import functools

import jax
import jax.numpy as jnp
import numpy as np
from jax import lax
from jax.experimental import pallas as pl
from jax.experimental.pallas import tpu as pltpu

F32 = jnp.float32
BF16 = jnp.bfloat16

D_MODEL = 2048
SEQ = 2048
MEM_LEN = 256
CONV_WIDTH = 512
CONV_K = 3
DIFF_WIDTH = 1024
DIFF_VDIM = 128
DIFF_HALF = 64
DIFF_HEADS = 8
MEM_WIDTH = 512
MEM_HEADS = 4
MEM_HEAD_DIM = 128
IN_WIDTH = 3 * CONV_WIDTH + 3 * DIFF_WIDTH + MEM_WIDTH
ROT_DIM = 16
ROPE_THETA = 500000.0
FFN_HIDDEN = 5632
EPS = 1e-6
LAMBDA_INIT = 0.8 - 0.6 * float(np.exp(-0.3 * 0))
LOG2_E = float(np.log2(np.e))
MAX_SHIFT_FREE_LOG2_SCORE = 60.0

V7X_LANES = 128
V7X_SUBLANES = 8
V7X_BF16_SUBLANES = 16
V7X_VMEM_LIMIT_BYTES = 56 * 1024 * 1024

TM_PROJ = 256
TM_MEMKV = 256
TM_OUT = 512
TM_FFN = 1024
TH_FFN = 256
WEIGHT_CHUNK_ROWS = 128
SCORE_LOOKAHEAD = 2

_OFF_U = 0
_OFF_C = CONV_WIDTH
_OFF_B = 2 * CONV_WIDTH
_OFF_Q = 3 * CONV_WIDTH
_OFF_K = _OFF_Q + DIFF_WIDTH
_OFF_V = _OFF_K + DIFF_WIDTH
_OFF_QM = _OFF_V + DIFF_WIDTH


def _rms_scale(t, width):
    return lax.rsqrt(jnp.sum(t * t, axis=-1, keepdims=True) * (1.0 / width) + EPS)


def _load_weight_as_bf16(w_hbm, w_scr, stage, sem):
    rows = stage.shape[1]
    n_chunks = w_hbm.shape[0] // rows

    def chunk_copy(c):
        return pltpu.make_async_copy(w_hbm.at[pl.ds(c * rows, rows), :], stage.at[c % 2],
                                     sem.at[c % 2])

    chunk_copy(0).start()
    for c in range(n_chunks):
        if c + 1 < n_chunks:
            chunk_copy(c + 1).start()
        chunk_copy(c).wait()
        w_scr[c * rows:(c + 1) * rows, :] = stage[c % 2].astype(BF16)


def _resident_weight_scratch(k, n):
    return [pltpu.VMEM((k, n), BF16), pltpu.VMEM((2, WEIGHT_CHUNK_ROWS, n), F32),
            pltpu.SemaphoreType.DMA((2,))]


def _qk_norm_rope(t, g2, cos_t, sin_a, sin_b):
    lane = lax.broadcasted_iota(jnp.int32, t.shape, 1)
    is_lo = lane < DIFF_HALF
    t2 = t * t
    s_lo = jnp.sum(jnp.where(is_lo, t2, 0.0), axis=-1, keepdims=True)
    s_hi = jnp.sum(jnp.where(is_lo, 0.0, t2), axis=-1, keepdims=True)
    r = jnp.where(is_lo,
                  lax.rsqrt(s_lo * (1.0 / DIFF_HALF) + EPS),
                  lax.rsqrt(s_hi * (1.0 / DIFF_HALF) + EPS))
    tn = t * r * g2
    half = ROT_DIM // 2
    return (tn * cos_t
            + pltpu.roll(tn, V7X_LANES - half, axis=1) * sin_a
            + pltpu.roll(tn, half, axis=1) * sin_b)


def _in_proj_kernel(x_ref, g_ref, w_hbm, convw_ref, gconv_ref, cos_ref, sin_ref,
                    gq_ref, gk_ref,
                    yconv_ref, qt_ref, kn_ref, vt_ref, qm_ref,
                    z_scr, w_ref, w_stage, w_sem, *, tiles_per_seq):
    i = pl.program_id(0)
    tm = x_ref.shape[0]
    halo = V7X_SUBLANES

    @pl.when(i == 0)
    def _():
        _load_weight_as_bf16(w_hbm, w_ref, w_stage, w_sem)

    @pl.when(i % tiles_per_seq == 0)
    def _():
        z_scr[0:halo, :] = jnp.zeros((halo, CONV_WIDTH), F32)

    x = x_ref[...]
    h = (x * _rms_scale(x, D_MODEL) * g_ref[...]).astype(BF16)

    def proj(off, width):
        return jnp.dot(h, w_ref[:, off:off + width], preferred_element_type=F32)

    z = proj(_OFF_C, CONV_WIDTH) * proj(_OFF_U, CONV_WIDTH)
    z_scr[halo:halo + tm, :] = z
    z1 = z_scr[halo - 1:halo - 1 + tm, :]
    z2 = z_scr[halo - 2:halo - 2 + tm, :]
    cw = convw_ref[...]
    conv = cw[0:1, :] * z2 + cw[1:2, :] * z1 + cw[2:3, :] * z
    y = proj(_OFF_B, CONV_WIDTH) * conv
    yconv_ref[...] = (y * _rms_scale(y, CONV_WIDTH) * gconv_ref[...]).astype(BF16)
    z_scr[0:halo, :] = z_scr[tm:tm + halo, :]

    qf = proj(_OFF_Q, DIFF_WIDTH)
    kf = proj(_OFF_K, DIFF_WIDTH)
    vf = proj(_OFF_V, DIFF_WIDTH)
    cos_t, sin_t = cos_ref[...], sin_ref[...]
    pos_in_half = lax.broadcasted_iota(jnp.int32, sin_t.shape, 1) % DIFF_HALF
    sin_a = jnp.where(pos_in_half < ROT_DIM // 2, -sin_t, 0.0)
    sin_b = jnp.where((pos_in_half >= ROT_DIM // 2) & (pos_in_half < ROT_DIM), sin_t, 0.0)
    gq, gk = gq_ref[...], gk_ref[...]
    for hd in range(DIFF_HEADS):
        cols = slice(hd * DIFF_VDIM, (hd + 1) * DIFF_VDIM)
        qn = _qk_norm_rope(qf[:, cols], gq, cos_t, sin_a, sin_b) * (DIFF_HALF ** -0.5 * LOG2_E)
        qt_ref[0, 0, cols, :] = qn.T.astype(BF16)
        kn_ref[:, cols] = _qk_norm_rope(kf[:, cols], gk, cos_t, sin_a, sin_b).astype(BF16)
        vt_ref[0, 0, cols, :] = vf[:, cols].T.astype(BF16)
    qm_ref[...] = proj(_OFF_QM, MEM_WIDTH)


def _in_proj(x2, g_mix, w_in, conv_w, g_conv, cos_t, sin_t, gq2, gk2):
    t = x2.shape[0]
    tm = TM_PROJ
    tps = SEQ // tm
    nb = t // SEQ
    grid = (t // tm,)
    row = lambda i: (i, 0)
    fixed = lambda i: (0, 0)
    tile_t = lambda i: (i // tps, i % tps, 0, 0)
    return pl.pallas_call(
        functools.partial(_in_proj_kernel, tiles_per_seq=tps),
        grid=grid,
        in_specs=[
            pl.BlockSpec((tm, D_MODEL), row),
            pl.BlockSpec((1, D_MODEL), fixed),
            pl.BlockSpec(memory_space=pl.ANY),
            pl.BlockSpec((CONV_K, CONV_WIDTH), fixed),
            pl.BlockSpec((1, CONV_WIDTH), fixed),
            pl.BlockSpec((tm, V7X_LANES), row),
            pl.BlockSpec((tm, V7X_LANES), row),
            pl.BlockSpec((1, DIFF_VDIM), fixed),
            pl.BlockSpec((1, DIFF_VDIM), fixed),
        ],
        out_specs=[
            pl.BlockSpec((tm, CONV_WIDTH), row),
            pl.BlockSpec((1, 1, DIFF_WIDTH, tm), tile_t),
            pl.BlockSpec((tm, DIFF_WIDTH), row),
            pl.BlockSpec((1, 1, DIFF_WIDTH, tm), tile_t),
            pl.BlockSpec((tm, MEM_WIDTH), row),
        ],
        out_shape=[
            jax.ShapeDtypeStruct((t, CONV_WIDTH), BF16),
            jax.ShapeDtypeStruct((nb, tps, DIFF_WIDTH, tm), BF16),
            jax.ShapeDtypeStruct((t, DIFF_WIDTH), BF16),
            jax.ShapeDtypeStruct((nb, tps, DIFF_WIDTH, tm), BF16),
            jax.ShapeDtypeStruct((t, MEM_WIDTH), F32),
        ],
        scratch_shapes=[pltpu.VMEM((tm + V7X_SUBLANES, CONV_WIDTH), F32)]
        + _resident_weight_scratch(D_MODEL, IN_WIDTH),
        compiler_params=pltpu.CompilerParams(
            dimension_semantics=("arbitrary",),
            vmem_limit_bytes=V7X_VMEM_LIMIT_BYTES),
        name="in_proj",
    )(x2, g_mix, w_in, conv_w, g_conv, cos_t, sin_t, gq2, gk2)


def _mem_kv_kernel(mem_ref, g_ref, w_hbm, gk_ref, k_ref, v_ref, w_ref, w_stage, w_sem):
    @pl.when(pl.program_id(0) == 0)
    def _():
        _load_weight_as_bf16(w_hbm, w_ref, w_stage, w_sem)

    x = mem_ref[...]
    h = (x * _rms_scale(x, D_MODEL) * g_ref[...]).astype(BF16)
    kv = jnp.dot(h, w_ref[...], preferred_element_type=F32)
    gk = gk_ref[...]
    for hd in range(MEM_HEADS):
        lo = hd * MEM_HEAD_DIM
        kh = kv[:, lo:lo + MEM_HEAD_DIM]
        k_ref[:, lo:lo + MEM_HEAD_DIM] = (kh * _rms_scale(kh, MEM_HEAD_DIM) * gk).astype(BF16)
    v_ref[...] = kv[:, MEM_WIDTH:].astype(BF16)


def _mem_kv(mem2, g_mem, w_kv, g_mk):
    t = mem2.shape[0]
    tm = TM_MEMKV
    row = lambda i: (i, 0)
    fixed = lambda i: (0, 0)
    return pl.pallas_call(
        _mem_kv_kernel,
        grid=(t // tm,),
        in_specs=[
            pl.BlockSpec((tm, D_MODEL), row),
            pl.BlockSpec((1, D_MODEL), fixed),
            pl.BlockSpec(memory_space=pl.ANY),
            pl.BlockSpec((1, MEM_HEAD_DIM), fixed),
        ],
        out_specs=[pl.BlockSpec((tm, MEM_WIDTH), row), pl.BlockSpec((tm, MEM_WIDTH), row)],
        out_shape=[jax.ShapeDtypeStruct((t, MEM_WIDTH), BF16),
                   jax.ShapeDtypeStruct((t, MEM_WIDTH), BF16)],
        scratch_shapes=_resident_weight_scratch(D_MODEL, 2 * MEM_WIDTH),
        compiler_params=pltpu.CompilerParams(
            dimension_semantics=("arbitrary",),
            vmem_limit_bytes=V7X_VMEM_LIMIT_BYTES),
        name="mem_kv",
    )(mem2, g_mem, w_kv, g_mk)


def _diff_attn_kernel(bounded_ref, qt_ref, kn_ref, vt_ref, lq1_ref, lk1_ref, lq2_ref, lk2_ref,
                      gsub_ref, wg_ref, wu_ref, wd_ref, o_ref, wgu_out, wd_out):
    nq, tq = qt_ref.shape[1], qt_ref.shape[3]
    tk = vt_ref.shape[3]

    lam = (jnp.exp(jnp.sum(lq1_ref[...] * lk1_ref[...], axis=-1, keepdims=True))
           - jnp.exp(jnp.sum(lq2_ref[...] * lk2_ref[...], axis=-1, keepdims=True))
           + LAMBDA_INIT)
    key = lax.broadcasted_iota(jnp.int32, (tk, tq), 0)
    qry = lax.broadcasted_iota(jnp.int32, (tk, tq), 1)
    causal = jnp.concatenate([key <= qry, key <= qry], axis=1)
    sum_rows = (lax.broadcasted_iota(jnp.int32, (V7X_BF16_SUBLANES, tk), 0) == 0).astype(BF16)

    def scores(c, j):
        qt = qt_ref[0, c]
        is_lo = lax.broadcasted_iota(jnp.int32, qt.shape, 0) < DIFF_HALF
        zero = jnp.zeros_like(qt)
        qq = jnp.concatenate([jnp.where(is_lo, qt, zero), jnp.where(is_lo, zero, qt)], axis=1)
        return jnp.dot(kn_ref[0, j * tk:(j + 1) * tk, :], qq, preferred_element_type=F32)

    steps = [(c, j) for c in range(nq) for j in range(c + 1)]

    def attend(shift_free):
        th = wgu_out.shape[2] // 2
        for ch in range(wgu_out.shape[0]):
            cols = slice(ch * th, (ch + 1) * th)
            wgu_out[ch, :, :th] = wg_ref[:, cols].astype(BF16)
            wgu_out[ch, :, th:] = wu_ref[:, cols].astype(BF16)
        wd_out[...] = wd_ref[...].astype(BF16)
        pending = [scores(*st) for st in steps[:SCORE_LOOKAHEAD]]
        m = acc = None
        for n, (c, j) in enumerate(steps):
            s = pending.pop(0)
            if n + SCORE_LOOKAHEAD < len(steps):
                pending.append(scores(*steps[n + SCORE_LOOKAHEAD]))
            if j == c:
                s = jnp.where(causal, s, -jnp.inf)
            vt_ext = jnp.concatenate([vt_ref[0, j], sum_rows], axis=0)
            if shift_free:
                pv = jnp.dot(vt_ext, jnp.exp2(s).astype(BF16), preferred_element_type=F32)
                acc = pv if j == 0 else acc + pv
            else:
                m_blk = jnp.max(s, axis=0, keepdims=True)
                m_new = m_blk if j == 0 else jnp.maximum(m, m_blk)
                p = jnp.exp2(s - m_new)
                pv = jnp.dot(vt_ext, p.astype(BF16), preferred_element_type=F32)
                acc = pv if j == 0 else jnp.exp2(m - m_new) * acc + pv
                m = m_new
            if j == c:
                o_all = acc[:DIFF_VDIM] / acc[DIFF_VDIM:DIFF_VDIM + 1]
                o = (o_all[:, :tq] - lam * o_all[:, tq:]).T
                y = o * _rms_scale(o, DIFF_VDIM) * gsub_ref[...] * (1.0 - LAMBDA_INIT)
                o_ref[0, c * tq:(c + 1) * tq, :] = y.astype(o_ref.dtype)

    pl.when(bounded_ref[0] != 0)(functools.partial(attend, True))
    pl.when(bounded_ref[0] == 0)(functools.partial(attend, False))


def _diff_attn(bounded, qt4, kn3, vt4, lq1, lk1, lq2, lk2, g_sub, w_gate, w_up, w_down):
    b, nq, _, tq = qt4.shape
    s = kn3.shape[1]
    nk, tk = vt4.shape[1], vt4.shape[3]
    assert tq == tk and nq == nk
    fixed = lambda bb, h: (0, 0)
    vec = lambda n: pl.BlockSpec((1, n), fixed)
    steps = b * DIFF_HEADS
    nh = FFN_HIDDEN // TH_FFN
    gu_rows = D_MODEL // steps
    down_rows, down_cols = FFN_HIDDEN // (steps // 2), D_MODEL // 2
    assert gu_rows % V7X_BF16_SUBLANES == 0 and down_rows % V7X_BF16_SUBLANES == 0
    step = lambda bb, h: bb * DIFF_HEADS + h
    gu_in = pl.BlockSpec((gu_rows, FFN_HIDDEN), lambda bb, h: (step(bb, h), 0))
    gu_out = pl.BlockSpec((nh, gu_rows, 2 * TH_FFN), lambda bb, h: (0, step(bb, h), 0))
    down_blk = pl.BlockSpec((down_rows, down_cols),
                            lambda bb, h: (step(bb, h) // 2, step(bb, h) % 2))
    return pl.pallas_call(
        _diff_attn_kernel,
        grid=(b, DIFF_HEADS),
        in_specs=[
            pl.BlockSpec(memory_space=pltpu.SMEM),
            pl.BlockSpec((1, nq, DIFF_VDIM, tq), lambda bb, h: (bb, 0, h, 0)),
            pl.BlockSpec((1, s, DIFF_VDIM), lambda bb, h: (bb, 0, h)),
            pl.BlockSpec((1, nk, DIFF_VDIM, tk), lambda bb, h: (bb, 0, h, 0)),
            vec(DIFF_HALF), vec(DIFF_HALF), vec(DIFF_HALF), vec(DIFF_HALF),
            vec(DIFF_VDIM),
            gu_in, gu_in, down_blk,
        ],
        out_specs=[pl.BlockSpec((1, s, DIFF_VDIM), lambda bb, h: (bb, 0, h)), gu_out, down_blk],
        out_shape=[jax.ShapeDtypeStruct((b, s, DIFF_WIDTH), BF16),
                   jax.ShapeDtypeStruct((nh, D_MODEL, 2 * TH_FFN), BF16),
                   jax.ShapeDtypeStruct((FFN_HIDDEN, D_MODEL), BF16)],
        compiler_params=pltpu.CompilerParams(
            dimension_semantics=("parallel", "parallel"),
            vmem_limit_bytes=V7X_VMEM_LIMIT_BYTES),
        name="diff_attn",
    )(bounded, qt4, kn3, vt4, lq1, lk1, lq2, lk2, g_sub, w_gate, w_up, w_down)


def _out_proj_kernel(x_ref, yc_ref, yd_ref, qm_ref, km_ref, vm_ref, gq_ref, gmo_ref, gffn_ref,
                     w_hbm, x1_ref, hf_ref, w_ref, w_stage, w_sem):
    @pl.when(pl.program_id(0) == 0)
    def _():
        _load_weight_as_bf16(w_hbm, w_ref, w_stage, w_sem)

    gq = gq_ref[...]
    probs, denoms = [], []
    for hd in range(MEM_HEADS):
        cols = slice(hd * MEM_HEAD_DIM, (hd + 1) * MEM_HEAD_DIM)
        qh = qm_ref[:, cols]
        qh = (qh * _rms_scale(qh, MEM_HEAD_DIM) * gq * (MEM_HEAD_DIM ** -0.5)).astype(BF16)
        s = lax.dot_general(qh, km_ref[0, :, cols], (((1,), (1,)), ((), ())),
                            preferred_element_type=F32)
        e = jnp.exp(s - jnp.max(s, axis=-1, keepdims=True))
        denoms.append(jnp.sum(e, axis=-1, keepdims=True))
        probs.append(e.astype(BF16))

    acc = jnp.dot(yc_ref[...], w_ref[0:CONV_WIDTH, :], preferred_element_type=F32)

    outs = []
    for hd in range(MEM_HEADS):
        cols = slice(hd * MEM_HEAD_DIM, (hd + 1) * MEM_HEAD_DIM)
        oh = jnp.dot(probs[hd], vm_ref[0, :, cols], preferred_element_type=F32)
        outs.append(oh / denoms[hd])
    o = jnp.concatenate(outs, axis=-1)
    ym = (o * _rms_scale(o, MEM_WIDTH) * gmo_ref[...]).astype(BF16)

    acc += jnp.dot(yd_ref[...], w_ref[CONV_WIDTH:CONV_WIDTH + DIFF_WIDTH, :],
                   preferred_element_type=F32)
    acc += jnp.dot(ym, w_ref[CONV_WIDTH + DIFF_WIDTH:, :], preferred_element_type=F32)
    x1 = x_ref[...] + acc
    x1_ref[...] = x1
    hf_ref[...] = (x1 * _rms_scale(x1, D_MODEL) * gffn_ref[...]).astype(BF16)


def _out_proj(x2, yc, yd, qm, km3, vm3, g_mq, g_mem_out, g_ffn, w_o):
    t = x2.shape[0]
    tm = TM_OUT
    tps = SEQ // tm
    row = lambda i: (i, 0)
    fixed = lambda i: (0, 0)
    mem_blk = lambda i: (i // tps, 0, 0)
    return pl.pallas_call(
        _out_proj_kernel,
        grid=(t // tm,),
        in_specs=[
            pl.BlockSpec((tm, D_MODEL), row),
            pl.BlockSpec((tm, CONV_WIDTH), row),
            pl.BlockSpec((tm, DIFF_WIDTH), row),
            pl.BlockSpec((tm, MEM_WIDTH), row),
            pl.BlockSpec((1, MEM_LEN, MEM_WIDTH), mem_blk),
            pl.BlockSpec((1, MEM_LEN, MEM_WIDTH), mem_blk),
            pl.BlockSpec((1, MEM_HEAD_DIM), fixed),
            pl.BlockSpec((1, MEM_WIDTH), fixed),
            pl.BlockSpec((1, D_MODEL), fixed),
            pl.BlockSpec(memory_space=pl.ANY),
        ],
        out_specs=[pl.BlockSpec((tm, D_MODEL), row), pl.BlockSpec((tm, D_MODEL), row)],
        out_shape=[jax.ShapeDtypeStruct((t, D_MODEL), F32),
                   jax.ShapeDtypeStruct((t, D_MODEL), BF16)],
        scratch_shapes=_resident_weight_scratch(D_MODEL, D_MODEL),
        compiler_params=pltpu.CompilerParams(
            dimension_semantics=("arbitrary",),
            vmem_limit_bytes=V7X_VMEM_LIMIT_BYTES),
        name="out_proj",
    )(x2, yc, yd, qm, km3, vm3, g_mq, g_mem_out, g_ffn, w_o)


def _ffn_kernel(hf_ref, x1_hbm, wgu_ref, wd_ref, o_ref, x1_sem):
    i = pl.program_id(0)
    tm = o_ref.shape[0]

    def chunk(first):
        if first:
            rows = pl.ds(pl.multiple_of(i * tm, tm), tm)
            residual_copy = pltpu.make_async_copy(x1_hbm.at[rows, :], o_ref, x1_sem)
            residual_copy.start()
        th = wd_ref.shape[0]
        gu = jnp.dot(hf_ref[...], wgu_ref[0], preferred_element_type=F32)
        gate, up = gu[:, :th], gu[:, th:]
        act = (gate * (1.0 / (1.0 + jnp.exp(-gate))) * up).astype(BF16)
        if first:
            residual_copy.wait()
        o_ref[...] += jnp.dot(act, wd_ref[...], preferred_element_type=F32)

    pl.when(pl.program_id(1) == 0)(functools.partial(chunk, True))
    pl.when(pl.program_id(1) != 0)(functools.partial(chunk, False))


def _ffn(hf, x1, w_gate_up, w_down):
    t = x1.shape[0]
    tm, th = TM_FFN, TH_FFN
    return pl.pallas_call(
        _ffn_kernel,
        grid=(t // tm, FFN_HIDDEN // th),
        in_specs=[
            pl.BlockSpec((tm, D_MODEL), lambda i, j: (i, 0)),
            pl.BlockSpec(memory_space=pl.ANY),
            pl.BlockSpec((1, D_MODEL, 2 * th), lambda i, j: (j, 0, 0)),
            pl.BlockSpec((th, D_MODEL), lambda i, j: (j, 0)),
        ],
        out_specs=pl.BlockSpec((tm, D_MODEL), lambda i, j: (i, 0)),
        out_shape=jax.ShapeDtypeStruct((t, D_MODEL), F32),
        scratch_shapes=[pltpu.SemaphoreType.DMA(())],
        compiler_params=pltpu.CompilerParams(
            dimension_semantics=("parallel", "arbitrary"),
            vmem_limit_bytes=V7X_VMEM_LIMIT_BYTES),
        name="ffn",
    )(hf, x1, w_gate_up, w_down)


def _rope_tables(positions):
    inv_freq = ROPE_THETA ** (-jnp.arange(0, ROT_DIM, 2, dtype=F32) / ROT_DIM)
    pad = jnp.zeros((DIFF_HALF - ROT_DIM,), F32)
    lane_freq = jnp.concatenate([inv_freq, inv_freq, pad, inv_freq, inv_freq, pad])
    ang = positions.astype(F32).reshape(-1, 1) * lane_freq[None, :]
    return jnp.cos(ang), jnp.sin(ang)


def kernel(x, mem, positions, g_mix, g_mem, w_in, conv_w, g_conv_out, g_dq, g_dk,
           lam_q1, lam_k1, lam_q2, lam_k2, g_sub, w_mem_kv, g_mq, g_mk, g_mem_out,
           w_o, g_ffn, w_gate, w_up, w_down):
    b, s, d = x.shape
    assert (s, d) == (SEQ, D_MODEL) and mem.shape == (b, MEM_LEN, D_MODEL)
    assert g_mix.shape[0] == 1
    t = b * s
    x2 = x.reshape(t, d)
    mem2 = mem.reshape(b * MEM_LEN, d)
    cos_t, sin_t = _rope_tables(positions)

    yc, qt, kn, vt, qm = _in_proj(x2, g_mix, w_in[0], conv_w[0], g_conv_out,
                                  cos_t, sin_t,
                                  jnp.tile(g_dq, (1, 2)), jnp.tile(g_dk, (1, 2)))
    km, vm = _mem_kv(mem2, g_mem, w_mem_kv[0], g_mk)

    score_bound = (DIFF_HALF ** 0.5 * LOG2_E) * jnp.max(jnp.abs(g_dq)) * jnp.max(jnp.abs(g_dk))
    bounded = (score_bound <= MAX_SHIFT_FREE_LOG2_SCORE).astype(jnp.int32).reshape(1)
    yd, wgu16, wd16 = _diff_attn(bounded, qt, kn.reshape(b, s, DIFF_WIDTH), vt,
                                 lam_q1, lam_k1, lam_q2, lam_k2, g_sub,
                                 w_gate[0], w_up[0], w_down[0])
    x1, hf = _out_proj(x2, yc, yd.reshape(t, DIFF_WIDTH), qm,
                       km.reshape(b, MEM_LEN, MEM_WIDTH), vm.reshape(b, MEM_LEN, MEM_WIDTH),
                       g_mq, g_mem_out, g_ffn, w_o[0])
    out = _ffn(hf, x1, wgu16, wd16)
    return out.reshape(b, s, d)
```

```python
import functools

import jax
import jax.numpy as jnp
import numpy as np
from jax import lax
from jax.experimental import pallas as pl
from jax.experimental.pallas import tpu as pltpu

F32 = jnp.float32
BF16 = jnp.bfloat16

D_MODEL = 2048
SEQ = 2048
MEM_LEN = 256
CONV_WIDTH = 512
CONV_K = 3
DIFF_WIDTH = 1024
DIFF_VDIM = 128
DIFF_HALF = 64
DIFF_HEADS = 8
MEM_WIDTH = 512
MEM_HEADS = 4
MEM_HEAD_DIM = 128
IN_WIDTH = 3 * CONV_WIDTH + 3 * DIFF_WIDTH + MEM_WIDTH
ROT_DIM = 16
ROPE_THETA = 500000.0
FFN_HIDDEN = 5632
EPS = 1e-6
LAMBDA_INIT = 0.8 - 0.6 * float(np.exp(-0.3 * 0))
LOG2_E = float(np.log2(np.e))
MAX_SHIFT_FREE_LOG2_SCORE = 60.0

V7X_LANES = 128
V7X_SUBLANES = 8
V7X_BF16_SUBLANES = 16
V7X_VMEM_LIMIT_BYTES = 56 * 1024 * 1024

TM_PROJ = 256
TM_MEMKV = 256
TM_OUT = 512
TM_FFN = 1024
TH_FFN = 256
WEIGHT_CHUNK_ROWS = 128
SCORE_LOOKAHEAD = 2

_OFF_U = 0
_OFF_C = CONV_WIDTH
_OFF_B = 2 * CONV_WIDTH
_OFF_Q = 3 * CONV_WIDTH
_OFF_K = _OFF_Q + DIFF_WIDTH
_OFF_V = _OFF_K + DIFF_WIDTH
_OFF_QM = _OFF_V + DIFF_WIDTH


def _rms_scale(t, width):
    return lax.rsqrt(jnp.sum(t * t, axis=-1, keepdims=True) * (1.0 / width) + EPS)


def _load_weight_as_bf16(w_hbm, w_scr, stage, sem):
    rows = stage.shape[1]
    n_chunks = w_hbm.shape[0] // rows

    def chunk_copy(c):
        return pltpu.make_async_copy(w_hbm.at[pl.ds(c * rows, rows), :], stage.at[c % 2],
                                     sem.at[c % 2])

    chunk_copy(0).start()
    for c in range(n_chunks):
        if c + 1 < n_chunks:
            chunk_copy(c + 1).start()
        chunk_copy(c).wait()
        w_scr[c * rows:(c + 1) * rows, :] = stage[c % 2].astype(BF16)


def _resident_weight_scratch(k, n):
    return [pltpu.VMEM((k, n), BF16), pltpu.VMEM((2, WEIGHT_CHUNK_ROWS, n), F32),
            pltpu.SemaphoreType.DMA((2,))]


def _qk_norm_rope(t, g2, cos_t, sin_a, sin_b):
    lane = lax.broadcasted_iota(jnp.int32, t.shape, 1)
    is_lo = lane < DIFF_HALF
    t2 = t * t
    s_lo = jnp.sum(jnp.where(is_lo, t2, 0.0), axis=-1, keepdims=True)
    s_hi = jnp.sum(jnp.where(is_lo, 0.0, t2), axis=-1, keepdims=True)
    r = jnp.where(is_lo,
                  lax.rsqrt(s_lo * (1.0 / DIFF_HALF) + EPS),
                  lax.rsqrt(s_hi * (1.0 / DIFF_HALF) + EPS))
    tn = t * r * g2
    half = ROT_DIM // 2
    return (tn * cos_t
            + pltpu.roll(tn, V7X_LANES - half, axis=1) * sin_a
            + pltpu.roll(tn, half, axis=1) * sin_b)


def _in_proj_kernel(x_ref, g_ref, w_hbm, convw_ref, gconv_ref, cos_ref, sin_ref,
                    gq_ref, gk_ref,
                    yconv_ref, qt_ref, kn_ref, vt_ref, qm_ref,
                    z_scr, w_ref, w_stage, w_sem, *, tiles_per_seq):
    i = pl.program_id(0)
    tm = x_ref.shape[0]
    halo = V7X_SUBLANES

    @pl.when(i == 0)
    def _():
        _load_weight_as_bf16(w_hbm, w_ref, w_stage, w_sem)

    @pl.when(i % tiles_per_seq == 0)
    def _():
        z_scr[0:halo, :] = jnp.zeros((halo, CONV_WIDTH), F32)

    x = x_ref[...]
    h = (x * _rms_scale(x, D_MODEL) * g_ref[...]).astype(BF16)

    def proj(off, width):
        return jnp.dot(h, w_ref[:, off:off + width], preferred_element_type=F32)

    z = proj(_OFF_C, CONV_WIDTH) * proj(_OFF_U, CONV_WIDTH)
    z_scr[halo:halo + tm, :] = z
    z1 = z_scr[halo - 1:halo - 1 + tm, :]
    z2 = z_scr[halo - 2:halo - 2 + tm, :]
    cw = convw_ref[...]
    conv = cw[0:1, :] * z2 + cw[1:2, :] * z1 + cw[2:3, :] * z
    y = proj(_OFF_B, CONV_WIDTH) * conv
    yconv_ref[...] = (y * _rms_scale(y, CONV_WIDTH) * gconv_ref[...]).astype(BF16)
    z_scr[0:halo, :] = z_scr[tm:tm + halo, :]

    qf = proj(_OFF_Q, DIFF_WIDTH)
    kf = proj(_OFF_K, DIFF_WIDTH)
    vf = proj(_OFF_V, DIFF_WIDTH)
    cos_t, sin_t = cos_ref[...], sin_ref[...]
    pos_in_half = lax.broadcasted_iota(jnp.int32, sin_t.shape, 1) % DIFF_HALF
    sin_a = jnp.where(pos_in_half < ROT_DIM // 2, -sin_t, 0.0)
    sin_b = jnp.where((pos_in_half >= ROT_DIM // 2) & (pos_in_half < ROT_DIM), sin_t, 0.0)
    gq, gk = gq_ref[...], gk_ref[...]
    for hd in range(DIFF_HEADS):
        cols = slice(hd * DIFF_VDIM, (hd + 1) * DIFF_VDIM)
        qn = _qk_norm_rope(qf[:, cols], gq, cos_t, sin_a, sin_b) * (DIFF_HALF ** -0.5 * LOG2_E)
        qt_ref[0, 0, cols, :] = qn.T.astype(BF16)
        kn_ref[:, cols] = _qk_norm_rope(kf[:, cols], gk, cos_t, sin_a, sin_b).astype(BF16)
        vt_ref[0, 0, cols, :] = vf[:, cols].T.astype(BF16)
    qm_ref[...] = proj(_OFF_QM, MEM_WIDTH)


def _in_proj(x2, g_mix, w_in, conv_w, g_conv, cos_t, sin_t, gq2, gk2):
    t = x2.shape[0]
    tm = TM_PROJ
    tps = SEQ // tm
    nb = t // SEQ
    grid = (t // tm,)
    row = lambda i: (i, 0)
    fixed = lambda i: (0, 0)
    tile_t = lambda i: (i // tps, i % tps, 0, 0)
    return pl.pallas_call(
        functools.partial(_in_proj_kernel, tiles_per_seq=tps),
        grid=grid,
        in_specs=[
            pl.BlockSpec((tm, D_MODEL), row),
            pl.BlockSpec((1, D_MODEL), fixed),
            pl.BlockSpec(memory_space=pl.ANY),
            pl.BlockSpec((CONV_K, CONV_WIDTH), fixed),
            pl.BlockSpec((1, CONV_WIDTH), fixed),
            pl.BlockSpec((tm, V7X_LANES), row),
            pl.BlockSpec((tm, V7X_LANES), row),
            pl.BlockSpec((1, DIFF_VDIM), fixed),
            pl.BlockSpec((1, DIFF_VDIM), fixed),
        ],
        out_specs=[
            pl.BlockSpec((tm, CONV_WIDTH), row),
            pl.BlockSpec((1, 1, DIFF_WIDTH, tm), tile_t),
            pl.BlockSpec((tm, DIFF_WIDTH), row),
            pl.BlockSpec((1, 1, DIFF_WIDTH, tm), tile_t),
            pl.BlockSpec((tm, MEM_WIDTH), row),
        ],
        out_shape=[
            jax.ShapeDtypeStruct((t, CONV_WIDTH), BF16),
            jax.ShapeDtypeStruct((nb, tps, DIFF_WIDTH, tm), BF16),
            jax.ShapeDtypeStruct((t, DIFF_WIDTH), BF16),
            jax.ShapeDtypeStruct((nb, tps, DIFF_WIDTH, tm), BF16),
            jax.ShapeDtypeStruct((t, MEM_WIDTH), F32),
        ],
        scratch_shapes=[pltpu.VMEM((tm + V7X_SUBLANES, CONV_WIDTH), F32)]
        + _resident_weight_scratch(D_MODEL, IN_WIDTH),
        compiler_params=pltpu.CompilerParams(
            dimension_semantics=("arbitrary",),
            vmem_limit_bytes=V7X_VMEM_LIMIT_BYTES),
        name="in_proj",
    )(x2, g_mix, w_in, conv_w, g_conv, cos_t, sin_t, gq2, gk2)


def _mem_kv_kernel(mem_ref, g_ref, w_hbm, gk_ref, k_ref, v_ref, w_ref, w_stage, w_sem):
    @pl.when(pl.program_id(0) == 0)
    def _():
        _load_weight_as_bf16(w_hbm, w_ref, w_stage, w_sem)

    x = mem_ref[...]
    h = (x * _rms_scale(x, D_MODEL) * g_ref[...]).astype(BF16)
    kv = jnp.dot(h, w_ref[...], preferred_element_type=F32)
    gk = gk_ref[...]
    for hd in range(MEM_HEADS):
        lo = hd * MEM_HEAD_DIM
        kh = kv[:, lo:lo + MEM_HEAD_DIM]
        k_ref[:, lo:lo + MEM_HEAD_DIM] = (kh * _rms_scale(kh, MEM_HEAD_DIM) * gk).astype(BF16)
    v_ref[...] = kv[:, MEM_WIDTH:].astype(BF16)


def _mem_kv(mem2, g_mem, w_kv, g_mk):
    t = mem2.shape[0]
    tm = TM_MEMKV
    row = lambda i: (i, 0)
    fixed = lambda i: (0, 0)
    return pl.pallas_call(
        _mem_kv_kernel,
        grid=(t // tm,),
        in_specs=[
            pl.BlockSpec((tm, D_MODEL), row),
            pl.BlockSpec((1, D_MODEL), fixed),
            pl.BlockSpec(memory_space=pl.ANY),
            pl.BlockSpec((1, MEM_HEAD_DIM), fixed),
        ],
        out_specs=[pl.BlockSpec((tm, MEM_WIDTH), row), pl.BlockSpec((tm, MEM_WIDTH), row)],
        out_shape=[jax.ShapeDtypeStruct((t, MEM_WIDTH), BF16),
                   jax.ShapeDtypeStruct((t, MEM_WIDTH), BF16)],
        scratch_shapes=_resident_weight_scratch(D_MODEL, 2 * MEM_WIDTH),
        compiler_params=pltpu.CompilerParams(
            dimension_semantics=("arbitrary",),
            vmem_limit_bytes=V7X_VMEM_LIMIT_BYTES),
        name="mem_kv",
    )(mem2, g_mem, w_kv, g_mk)


def _diff_attn_kernel(bounded_ref, qt_ref, kn_ref, vt_ref, lq1_ref, lk1_ref, lq2_ref, lk2_ref,
                      gsub_ref, wg_ref, wu_ref, wd_ref, o_ref, wgu_out, wd_out):
    nq, tq = qt_ref.shape[1], qt_ref.shape[3]
    tk = vt_ref.shape[3]

    lam = (jnp.exp(jnp.sum(lq1_ref[...] * lk1_ref[...], axis=-1, keepdims=True))
           - jnp.exp(jnp.sum(lq2_ref[...] * lk2_ref[...], axis=-1, keepdims=True))
           + LAMBDA_INIT)
    key = lax.broadcasted_iota(jnp.int32, (tk, tq), 0)
    qry = lax.broadcasted_iota(jnp.int32, (tk, tq), 1)
    causal = jnp.concatenate([key <= qry, key <= qry], axis=1)
    sum_rows = (lax.broadcasted_iota(jnp.int32, (V7X_BF16_SUBLANES, tk), 0) == 0).astype(BF16)

    def scores(c, j):
        qt = qt_ref[0, c]
        is_lo = lax.broadcasted_iota(jnp.int32, qt.shape, 0) < DIFF_HALF
        zero = jnp.zeros_like(qt)
        qq = jnp.concatenate([jnp.where(is_lo, qt, zero), jnp.where(is_lo, zero, qt)], axis=1)
        return jnp.dot(kn_ref[0, j * tk:(j + 1) * tk, :], qq, preferred_element_type=F32)

    steps = [(c, j) for c in range(nq) for j in range(c + 1)]

    def attend(shift_free):
        th = wgu_out.shape[2] // 2
        for ch in range(wgu_out.shape[0]):
            cols = slice(ch * th, (ch + 1) * th)
            wgu_out[ch, :, :th] = wg_ref[:, cols].astype(BF16)
            wgu_out[ch, :, th:] = wu_ref[:, cols].astype(BF16)
        wd_out[...] = wd_ref[...].astype(BF16)
        pending = [scores(*st) for st in steps[:SCORE_LOOKAHEAD]]
        m = acc = None
        for n, (c, j) in enumerate(steps):
            s = pending.pop(0)
            if n + SCORE_LOOKAHEAD < len(steps):
                pending.append(scores(*steps[n + SCORE_LOOKAHEAD]))
            if j == c:
                s = jnp.where(causal, s, -jnp.inf)
            vt_ext = jnp.concatenate([vt_ref[0, j], sum_rows], axis=0)
            if shift_free:
                pv = jnp.dot(vt_ext, jnp.exp2(s).astype(BF16), preferred_element_type=F32)
                acc = pv if j == 0 else acc + pv
            else:
                m_blk = jnp.max(s, axis=0, keepdims=True)
                m_new = m_blk if j == 0 else jnp.maximum(m, m_blk)
                p = jnp.exp2(s - m_new)
                pv = jnp.dot(vt_ext, p.astype(BF16), preferred_element_type=F32)
                acc = pv if j == 0 else jnp.exp2(m - m_new) * acc + pv
                m = m_new
            if j == c:
                o_all = acc[:DIFF_VDIM] / acc[DIFF_VDIM:DIFF_VDIM + 1]
                o = (o_all[:, :tq] - lam * o_all[:, tq:]).T
                y = o * _rms_scale(o, DIFF_VDIM) * gsub_ref[...] * (1.0 - LAMBDA_INIT)
                o_ref[0, c * tq:(c + 1) * tq, :] = y.astype(o_ref.dtype)

    pl.when(bounded_ref[0] != 0)(functools.partial(attend, True))
    pl.when(bounded_ref[0] == 0)(functools.partial(attend, False))


def _diff_attn(bounded, qt4, kn3, vt4, lq1, lk1, lq2, lk2, g_sub, w_gate, w_up, w_down):
    b, nq, _, tq = qt4.shape
    s = kn3.shape[1]
    nk, tk = vt4.shape[1], vt4.shape[3]
    assert tq == tk and nq == nk
    fixed = lambda bb, h: (0, 0)
    vec = lambda n: pl.BlockSpec((1, n), fixed)
    steps = b * DIFF_HEADS
    nh = FFN_HIDDEN // TH_FFN
    gu_rows = D_MODEL // steps
    down_rows, down_cols = FFN_HIDDEN // (steps // 2), D_MODEL // 2
    assert gu_rows % V7X_BF16_SUBLANES == 0 and down_rows % V7X_BF16_SUBLANES == 0
    step = lambda bb, h: bb * DIFF_HEADS + h
    gu_in = pl.BlockSpec((gu_rows, FFN_HIDDEN), lambda bb, h: (step(bb, h), 0))
    gu_out = pl.BlockSpec((nh, gu_rows, 2 * TH_FFN), lambda bb, h: (0, step(bb, h), 0))
    down_blk = pl.BlockSpec((down_rows, down_cols),
                            lambda bb, h: (step(bb, h) // 2, step(bb, h) % 2))
    return pl.pallas_call(
        _diff_attn_kernel,
        grid=(b, DIFF_HEADS),
        in_specs=[
            pl.BlockSpec(memory_space=pltpu.SMEM),
            pl.BlockSpec((1, nq, DIFF_VDIM, tq), lambda bb, h: (bb, 0, h, 0)),
            pl.BlockSpec((1, s, DIFF_VDIM), lambda bb, h: (bb, 0, h)),
            pl.BlockSpec((1, nk, DIFF_VDIM, tk), lambda bb, h: (bb, 0, h, 0)),
            vec(DIFF_HALF), vec(DIFF_HALF), vec(DIFF_HALF), vec(DIFF_HALF),
            vec(DIFF_VDIM),
            gu_in, gu_in, down_blk,
        ],
        out_specs=[pl.BlockSpec((1, s, DIFF_VDIM), lambda bb, h: (bb, 0, h)), gu_out, down_blk],
        out_shape=[jax.ShapeDtypeStruct((b, s, DIFF_WIDTH), BF16),
                   jax.ShapeDtypeStruct((nh, D_MODEL, 2 * TH_FFN), BF16),
                   jax.ShapeDtypeStruct((FFN_HIDDEN, D_MODEL), BF16)],
        compiler_params=pltpu.CompilerParams(
            dimension_semantics=("parallel", "parallel"),
            vmem_limit_bytes=V7X_VMEM_LIMIT_BYTES),
        name="diff_attn",
    )(bounded, qt4, kn3, vt4, lq1, lk1, lq2, lk2, g_sub, w_gate, w_up, w_down)


def _out_proj_kernel(x_ref, yc_ref, yd_ref, qm_ref, km_ref, vm_ref, gq_ref, gmo_ref, gffn_ref,
                     w_hbm, x1_ref, hf_ref, w_ref, w_stage, w_sem):
    @pl.when(pl.program_id(0) == 0)
    def _():
        _load_weight_as_bf16(w_hbm, w_ref, w_stage, w_sem)

    gq = gq_ref[...]
    probs, denoms = [], []
    for hd in range(MEM_HEADS):
        cols = slice(hd * MEM_HEAD_DIM, (hd + 1) * MEM_HEAD_DIM)
        qh = qm_ref[:, cols]
        qh = (qh * _rms_scale(qh, MEM_HEAD_DIM) * gq * (MEM_HEAD_DIM ** -0.5)).astype(BF16)
        s = lax.dot_general(qh, km_ref[0, :, cols], (((1,), (1,)), ((), ())),
                            preferred_element_type=F32)
        e = jnp.exp(s - jnp.max(s, axis=-1, keepdims=True))
        denoms.append(jnp.sum(e, axis=-1, keepdims=True))
        probs.append(e.astype(BF16))

    acc = jnp.dot(yc_ref[...], w_ref[0:CONV_WIDTH, :], preferred_element_type=F32)

    outs = []
    for hd in range(MEM_HEADS):
        cols = slice(hd * MEM_HEAD_DIM, (hd + 1) * MEM_HEAD_DIM)
        oh = jnp.dot(probs[hd], vm_ref[0, :, cols], preferred_element_type=F32)
        outs.append(oh / denoms[hd])
    o = jnp.concatenate(outs, axis=-1)
    ym = (o * _rms_scale(o, MEM_WIDTH) * gmo_ref[...]).astype(BF16)

    acc += jnp.dot(yd_ref[...], w_ref[CONV_WIDTH:CONV_WIDTH + DIFF_WIDTH, :],
                   preferred_element_type=F32)
    acc += jnp.dot(ym, w_ref[CONV_WIDTH + DIFF_WIDTH:, :], preferred_element_type=F32)
    x1 = x_ref[...] + acc
    x1_ref[...] = x1
    hf_ref[...] = (x1 * _rms_scale(x1, D_MODEL) * gffn_ref[...]).astype(BF16)


def _out_proj(x2, yc, yd, qm, km3, vm3, g_mq, g_mem_out, g_ffn, w_o):
    t = x2.shape[0]
    tm = TM_OUT
    tps = SEQ // tm
    row = lambda i: (i, 0)
    fixed = lambda i: (0, 0)
    mem_blk = lambda i: (i // tps, 0, 0)
    return pl.pallas_call(
        _out_proj_kernel,
        grid=(t // tm,),
        in_specs=[
            pl.BlockSpec((tm, D_MODEL), row),
            pl.BlockSpec((tm, CONV_WIDTH), row),
            pl.BlockSpec((tm, DIFF_WIDTH), row),
            pl.BlockSpec((tm, MEM_WIDTH), row),
            pl.BlockSpec((1, MEM_LEN, MEM_WIDTH), mem_blk),
            pl.BlockSpec((1, MEM_LEN, MEM_WIDTH), mem_blk),
            pl.BlockSpec((1, MEM_HEAD_DIM), fixed),
            pl.BlockSpec((1, MEM_WIDTH), fixed),
            pl.BlockSpec((1, D_MODEL), fixed),
            pl.BlockSpec(memory_space=pl.ANY),
        ],
        out_specs=[pl.BlockSpec((tm, D_MODEL), row), pl.BlockSpec((tm, D_MODEL), row)],
        out_shape=[jax.ShapeDtypeStruct((t, D_MODEL), F32),
                   jax.ShapeDtypeStruct((t, D_MODEL), BF16)],
        scratch_shapes=_resident_weight_scratch(D_MODEL, D_MODEL),
        compiler_params=pltpu.CompilerParams(
            dimension_semantics=("arbitrary",),
            vmem_limit_bytes=V7X_VMEM_LIMIT_BYTES),
        name="out_proj",
    )(x2, yc, yd, qm, km3, vm3, g_mq, g_mem_out, g_ffn, w_o)


def _ffn_kernel(hf_ref, x1_ref, wgu_ref, wd_ref, o_ref):
    @pl.when(pl.program_id(1) == 0)
    def _():
        o_ref[...] = x1_ref[...]

    th = wd_ref.shape[0]
    gu = jnp.dot(hf_ref[...], wgu_ref[0], preferred_element_type=F32)
    gate, up = gu[:, :th], gu[:, th:]
    act = (gate * (1.0 / (1.0 + jnp.exp(-gate))) * up).astype(BF16)
    o_ref[...] += jnp.dot(act, wd_ref[...], preferred_element_type=F32)


def _ffn(hf, x1, w_gate_up, w_down):
    t = x1.shape[0]
    tm, th = TM_FFN, TH_FFN
    return pl.pallas_call(
        _ffn_kernel,
        grid=(t // tm, FFN_HIDDEN // th),
        in_specs=[
            pl.BlockSpec((tm, D_MODEL), lambda i, j: (i, 0)),
            pl.BlockSpec((tm, D_MODEL), lambda i, j: (i, 0)),
            pl.BlockSpec((1, D_MODEL, 2 * th), lambda i, j: (j, 0, 0)),
            pl.BlockSpec((th, D_MODEL), lambda i, j: (j, 0)),
        ],
        out_specs=pl.BlockSpec((tm, D_MODEL), lambda i, j: (i, 0)),
        out_shape=jax.ShapeDtypeStruct((t, D_MODEL), F32),
        compiler_params=pltpu.CompilerParams(
            dimension_semantics=("parallel", "arbitrary"),
            vmem_limit_bytes=V7X_VMEM_LIMIT_BYTES),
        name="ffn",
    )(hf, x1, w_gate_up, w_down)


def _rope_tables(positions):
    inv_freq = ROPE_THETA ** (-jnp.arange(0, ROT_DIM, 2, dtype=F32) / ROT_DIM)
    pad = jnp.zeros((DIFF_HALF - ROT_DIM,), F32)
    lane_freq = jnp.concatenate([inv_freq, inv_freq, pad, inv_freq, inv_freq, pad])
    ang = positions.astype(F32).reshape(-1, 1) * lane_freq[None, :]
    return jnp.cos(ang), jnp.sin(ang)


def kernel(x, mem, positions, g_mix, g_mem, w_in, conv_w, g_conv_out, g_dq, g_dk,
           lam_q1, lam_k1, lam_q2, lam_k2, g_sub, w_mem_kv, g_mq, g_mk, g_mem_out,
           w_o, g_ffn, w_gate, w_up, w_down):
    b, s, d = x.shape
    assert (s, d) == (SEQ, D_MODEL) and mem.shape == (b, MEM_LEN, D_MODEL)
    assert g_mix.shape[0] == 1
    t = b * s
    x2 = x.reshape(t, d)
    mem2 = mem.reshape(b * MEM_LEN, d)
    cos_t, sin_t = _rope_tables(positions)

    yc, qt, kn, vt, qm = _in_proj(x2, g_mix, w_in[0], conv_w[0], g_conv_out,
                                  cos_t, sin_t,
                                  jnp.tile(g_dq, (1, 2)), jnp.tile(g_dk, (1, 2)))
    km, vm = _mem_kv(mem2, g_mem, w_mem_kv[0], g_mk)

    score_bound = (DIFF_HALF ** 0.5 * LOG2_E) * jnp.max(jnp.abs(g_dq)) * jnp.max(jnp.abs(g_dk))
    bounded = (score_bound <= MAX_SHIFT_FREE_LOG2_SCORE).astype(jnp.int32).reshape(1)
    yd, wgu16, wd16 = _diff_attn(bounded, qt, kn.reshape(b, s, DIFF_WIDTH), vt,
                                 lam_q1, lam_k1, lam_q2, lam_k2, g_sub,
                                 w_gate[0], w_up[0], w_down[0])
    x1, hf = _out_proj(x2, yc, yd.reshape(t, DIFF_WIDTH), qm,
                       km.reshape(b, MEM_LEN, MEM_WIDTH), vm.reshape(b, MEM_LEN, MEM_WIDTH),
                       g_mq, g_mem_out, g_ffn, w_o[0])
    out = _ffn(hf, x1, wgu16, wd16)
    return out.reshape(b, s, d)
```

```python
import functools

import jax
import jax.numpy as jnp
import numpy as np
from jax import lax
from jax.experimental import pallas as pl
from jax.experimental.pallas import tpu as pltpu

F32 = jnp.float32
BF16 = jnp.bfloat16

D_MODEL = 2048
SEQ = 2048
MEM_LEN = 256
CONV_WIDTH = 512
CONV_K = 3
DIFF_WIDTH = 1024
DIFF_VDIM = 128
DIFF_HALF = 64
DIFF_HEADS = 8
MEM_WIDTH = 512
MEM_HEADS = 4
MEM_HEAD_DIM = 128
IN_WIDTH = 3 * CONV_WIDTH + 3 * DIFF_WIDTH + MEM_WIDTH
ROT_DIM = 16
ROPE_THETA = 500000.0
FFN_HIDDEN = 5632
EPS = 1e-6
LAMBDA_INIT = 0.8 - 0.6 * float(np.exp(-0.3 * 0))
LOG2_E = float(np.log2(np.e))
MAX_SHIFT_FREE_LOG2_SCORE = 60.0

V7X_LANES = 128
V7X_SUBLANES = 8
V7X_BF16_SUBLANES = 16
V7X_VMEM_LIMIT_BYTES = 60 * 1024 * 1024

TM_PROJ = 256
TM_MEMKV = 256
TM_OUT = 512
TM_FFN = 1024
TH_FFN = 512
WEIGHT_CHUNK_ROWS = 128
SCORE_LOOKAHEAD = 2

_OFF_U = 0
_OFF_C = CONV_WIDTH
_OFF_B = 2 * CONV_WIDTH
_OFF_Q = 3 * CONV_WIDTH
_OFF_K = _OFF_Q + DIFF_WIDTH
_OFF_V = _OFF_K + DIFF_WIDTH
_OFF_QM = _OFF_V + DIFF_WIDTH


def _rms_scale(t, width):
    return lax.rsqrt(jnp.sum(t * t, axis=-1, keepdims=True) * (1.0 / width) + EPS)


def _load_weight_as_bf16(w_hbm, w_scr, stage, sem):
    rows = stage.shape[1]
    n_chunks = w_hbm.shape[0] // rows

    def chunk_copy(c):
        return pltpu.make_async_copy(w_hbm.at[pl.ds(c * rows, rows), :], stage.at[c % 2],
                                     sem.at[c % 2])

    chunk_copy(0).start()
    for c in range(n_chunks):
        if c + 1 < n_chunks:
            chunk_copy(c + 1).start()
        chunk_copy(c).wait()
        w_scr[c * rows:(c + 1) * rows, :] = stage[c % 2].astype(BF16)


def _resident_weight_scratch(k, n):
    return [pltpu.VMEM((k, n), BF16), pltpu.VMEM((2, WEIGHT_CHUNK_ROWS, n), F32),
            pltpu.SemaphoreType.DMA((2,))]


def _qk_norm_rope(t, g2, cos_t, sin_a, sin_b):
    lane = lax.broadcasted_iota(jnp.int32, t.shape, 1)
    is_lo = lane < DIFF_HALF
    t2 = t * t
    s_lo = jnp.sum(jnp.where(is_lo, t2, 0.0), axis=-1, keepdims=True)
    s_hi = jnp.sum(jnp.where(is_lo, 0.0, t2), axis=-1, keepdims=True)
    r = jnp.where(is_lo,
                  lax.rsqrt(s_lo * (1.0 / DIFF_HALF) + EPS),
                  lax.rsqrt(s_hi * (1.0 / DIFF_HALF) + EPS))
    tn = t * r * g2
    half = ROT_DIM // 2
    return (tn * cos_t
            + pltpu.roll(tn, V7X_LANES - half, axis=1) * sin_a
            + pltpu.roll(tn, half, axis=1) * sin_b)


def _in_proj_kernel(x_ref, g_ref, w_hbm, convw_ref, gconv_ref, cos_ref, sin_ref,
                    gq_ref, gk_ref,
                    yconv_ref, qt_ref, kn_ref, vt_ref, qm_ref,
                    z_scr, w_ref, w_stage, w_sem, *, tiles_per_seq):
    i = pl.program_id(0)
    tm = x_ref.shape[0]
    halo = V7X_SUBLANES

    @pl.when(i == 0)
    def _():
        _load_weight_as_bf16(w_hbm, w_ref, w_stage, w_sem)

    @pl.when(i % tiles_per_seq == 0)
    def _():
        z_scr[0:halo, :] = jnp.zeros((halo, CONV_WIDTH), F32)

    x = x_ref[...]
    h = (x * _rms_scale(x, D_MODEL) * g_ref[...]).astype(BF16)

    def proj(off, width):
        return jnp.dot(h, w_ref[:, off:off + width], preferred_element_type=F32)

    z = proj(_OFF_C, CONV_WIDTH) * proj(_OFF_U, CONV_WIDTH)
    z_scr[halo:halo + tm, :] = z
    z1 = z_scr[halo - 1:halo - 1 + tm, :]
    z2 = z_scr[halo - 2:halo - 2 + tm, :]
    cw = convw_ref[...]
    conv = cw[0:1, :] * z2 + cw[1:2, :] * z1 + cw[2:3, :] * z
    y = proj(_OFF_B, CONV_WIDTH) * conv
    yconv_ref[...] = (y * _rms_scale(y, CONV_WIDTH) * gconv_ref[...]).astype(BF16)
    z_scr[0:halo, :] = z_scr[tm:tm + halo, :]

    qf = proj(_OFF_Q, DIFF_WIDTH)
    kf = proj(_OFF_K, DIFF_WIDTH)
    vf = proj(_OFF_V, DIFF_WIDTH)
    cos_t, sin_t = cos_ref[...], sin_ref[...]
    pos_in_half = lax.broadcasted_iota(jnp.int32, sin_t.shape, 1) % DIFF_HALF
    sin_a = jnp.where(pos_in_half < ROT_DIM // 2, -sin_t, 0.0)
    sin_b = jnp.where((pos_in_half >= ROT_DIM // 2) & (pos_in_half < ROT_DIM), sin_t, 0.0)
    gq, gk = gq_ref[...], gk_ref[...]
    for hd in range(DIFF_HEADS):
        cols = slice(hd * DIFF_VDIM, (hd + 1) * DIFF_VDIM)
        qn = _qk_norm_rope(qf[:, cols], gq, cos_t, sin_a, sin_b) * (DIFF_HALF ** -0.5 * LOG2_E)
        qt_ref[0, 0, cols, :] = qn.T.astype(BF16)
        kn_ref[:, cols] = _qk_norm_rope(kf[:, cols], gk, cos_t, sin_a, sin_b).astype(BF16)
        vt_ref[0, 0, cols, :] = vf[:, cols].T.astype(BF16)
    qm_ref[...] = proj(_OFF_QM, MEM_WIDTH)


def _in_proj(x2, g_mix, w_in, conv_w, g_conv, cos_t, sin_t, gq2, gk2):
    t = x2.shape[0]
    tm = TM_PROJ
    tps = SEQ // tm
    nb = t // SEQ
    grid = (t // tm,)
    row = lambda i: (i, 0)
    fixed = lambda i: (0, 0)
    tile_t = lambda i: (i // tps, i % tps, 0, 0)
    return pl.pallas_call(
        functools.partial(_in_proj_kernel, tiles_per_seq=tps),
        grid=grid,
        in_specs=[
            pl.BlockSpec((tm, D_MODEL), row),
            pl.BlockSpec((1, D_MODEL), fixed),
            pl.BlockSpec(memory_space=pl.ANY),
            pl.BlockSpec((CONV_K, CONV_WIDTH), fixed),
            pl.BlockSpec((1, CONV_WIDTH), fixed),
            pl.BlockSpec((tm, V7X_LANES), row),
            pl.BlockSpec((tm, V7X_LANES), row),
            pl.BlockSpec((1, DIFF_VDIM), fixed),
            pl.BlockSpec((1, DIFF_VDIM), fixed),
        ],
        out_specs=[
            pl.BlockSpec((tm, CONV_WIDTH), row),
            pl.BlockSpec((1, 1, DIFF_WIDTH, tm), tile_t),
            pl.BlockSpec((tm, DIFF_WIDTH), row),
            pl.BlockSpec((1, 1, DIFF_WIDTH, tm), tile_t),
            pl.BlockSpec((tm, MEM_WIDTH), row),
        ],
        out_shape=[
            jax.ShapeDtypeStruct((t, CONV_WIDTH), BF16),
            jax.ShapeDtypeStruct((nb, tps, DIFF_WIDTH, tm), BF16),
            jax.ShapeDtypeStruct((t, DIFF_WIDTH), BF16),
            jax.ShapeDtypeStruct((nb, tps, DIFF_WIDTH, tm), BF16),
            jax.ShapeDtypeStruct((t, MEM_WIDTH), F32),
        ],
        scratch_shapes=[pltpu.VMEM((tm + V7X_SUBLANES, CONV_WIDTH), F32)]
        + _resident_weight_scratch(D_MODEL, IN_WIDTH),
        compiler_params=pltpu.CompilerParams(
            dimension_semantics=("arbitrary",),
            vmem_limit_bytes=V7X_VMEM_LIMIT_BYTES),
        name="in_proj",
    )(x2, g_mix, w_in, conv_w, g_conv, cos_t, sin_t, gq2, gk2)


def _mem_kv_kernel(mem_ref, g_ref, w_hbm, gk_ref, k_ref, v_ref, w_ref, w_stage, w_sem):
    @pl.when(pl.program_id(0) == 0)
    def _():
        _load_weight_as_bf16(w_hbm, w_ref, w_stage, w_sem)

    x = mem_ref[...]
    h = (x * _rms_scale(x, D_MODEL) * g_ref[...]).astype(BF16)
    kv = jnp.dot(h, w_ref[...], preferred_element_type=F32)
    gk = gk_ref[...]
    for hd in range(MEM_HEADS):
        lo = hd * MEM_HEAD_DIM
        kh = kv[:, lo:lo + MEM_HEAD_DIM]
        k_ref[:, lo:lo + MEM_HEAD_DIM] = (kh * _rms_scale(kh, MEM_HEAD_DIM) * gk).astype(BF16)
    v_ref[...] = kv[:, MEM_WIDTH:].astype(BF16)


def _mem_kv(mem2, g_mem, w_kv, g_mk):
    t = mem2.shape[0]
    tm = TM_MEMKV
    row = lambda i: (i, 0)
    fixed = lambda i: (0, 0)
    return pl.pallas_call(
        _mem_kv_kernel,
        grid=(t // tm,),
        in_specs=[
            pl.BlockSpec((tm, D_MODEL), row),
            pl.BlockSpec((1, D_MODEL), fixed),
            pl.BlockSpec(memory_space=pl.ANY),
            pl.BlockSpec((1, MEM_HEAD_DIM), fixed),
        ],
        out_specs=[pl.BlockSpec((tm, MEM_WIDTH), row), pl.BlockSpec((tm, MEM_WIDTH), row)],
        out_shape=[jax.ShapeDtypeStruct((t, MEM_WIDTH), BF16),
                   jax.ShapeDtypeStruct((t, MEM_WIDTH), BF16)],
        scratch_shapes=_resident_weight_scratch(D_MODEL, 2 * MEM_WIDTH),
        compiler_params=pltpu.CompilerParams(
            dimension_semantics=("arbitrary",),
            vmem_limit_bytes=V7X_VMEM_LIMIT_BYTES),
        name="mem_kv",
    )(mem2, g_mem, w_kv, g_mk)


def _diff_attn_kernel(bounded_ref, qt_ref, kn_ref, vt_ref, lq1_ref, lk1_ref, lq2_ref, lk2_ref,
                      gsub_ref, wg_ref, wu_ref, wd_ref, o_ref, wgu_out, wd_out):
    nq, tq = qt_ref.shape[1], qt_ref.shape[3]
    tk = vt_ref.shape[3]

    lam = (jnp.exp(jnp.sum(lq1_ref[...] * lk1_ref[...], axis=-1, keepdims=True))
           - jnp.exp(jnp.sum(lq2_ref[...] * lk2_ref[...], axis=-1, keepdims=True))
           + LAMBDA_INIT)
    key = lax.broadcasted_iota(jnp.int32, (tk, tq), 0)
    qry = lax.broadcasted_iota(jnp.int32, (tk, tq), 1)
    causal = jnp.concatenate([key <= qry, key <= qry], axis=1)
    sum_rows = (lax.broadcasted_iota(jnp.int32, (V7X_BF16_SUBLANES, tk), 0) == 0).astype(BF16)

    def scores(c, j):
        qt = qt_ref[0, c]
        is_lo = lax.broadcasted_iota(jnp.int32, qt.shape, 0) < DIFF_HALF
        zero = jnp.zeros_like(qt)
        qq = jnp.concatenate([jnp.where(is_lo, qt, zero), jnp.where(is_lo, zero, qt)], axis=1)
        return jnp.dot(kn_ref[0, j * tk:(j + 1) * tk, :], qq, preferred_element_type=F32)

    steps = [(c, j) for c in range(nq) for j in range(c + 1)]

    def attend(shift_free):
        th = wgu_out.shape[2] // 2
        for ch in range(wgu_out.shape[0]):
            cols = slice(ch * th, (ch + 1) * th)
            wgu_out[ch, :, :th] = wg_ref[:, cols].astype(BF16)
            wgu_out[ch, :, th:] = wu_ref[:, cols].astype(BF16)
        wd_out[...] = wd_ref[...].astype(BF16)
        pending = [scores(*st) for st in steps[:SCORE_LOOKAHEAD]]
        m = acc = None
        for n, (c, j) in enumerate(steps):
            s = pending.pop(0)
            if n + SCORE_LOOKAHEAD < len(steps):
                pending.append(scores(*steps[n + SCORE_LOOKAHEAD]))
            if j == c:
                s = jnp.where(causal, s, -jnp.inf)
            vt_ext = jnp.concatenate([vt_ref[0, j], sum_rows], axis=0)
            if shift_free:
                pv = jnp.dot(vt_ext, jnp.exp2(s).astype(BF16), preferred_element_type=F32)
                acc = pv if j == 0 else acc + pv
            else:
                m_blk = jnp.max(s, axis=0, keepdims=True)
                m_new = m_blk if j == 0 else jnp.maximum(m, m_blk)
                p = jnp.exp2(s - m_new)
                pv = jnp.dot(vt_ext, p.astype(BF16), preferred_element_type=F32)
                acc = pv if j == 0 else jnp.exp2(m - m_new) * acc + pv
                m = m_new
            if j == c:
                o_all = acc[:DIFF_VDIM] / acc[DIFF_VDIM:DIFF_VDIM + 1]
                o = (o_all[:, :tq] - lam * o_all[:, tq:]).T
                y = o * _rms_scale(o, DIFF_VDIM) * gsub_ref[...] * (1.0 - LAMBDA_INIT)
                o_ref[0, c * tq:(c + 1) * tq, :] = y.astype(o_ref.dtype)

    pl.when(bounded_ref[0] != 0)(functools.partial(attend, True))
    pl.when(bounded_ref[0] == 0)(functools.partial(attend, False))


def _diff_attn(bounded, qt4, kn3, vt4, lq1, lk1, lq2, lk2, g_sub, w_gate, w_up, w_down):
    b, nq, _, tq = qt4.shape
    s = kn3.shape[1]
    nk, tk = vt4.shape[1], vt4.shape[3]
    assert tq == tk and nq == nk
    fixed = lambda bb, h: (0, 0)
    vec = lambda n: pl.BlockSpec((1, n), fixed)
    steps = b * DIFF_HEADS
    nh = FFN_HIDDEN // TH_FFN
    gu_rows = D_MODEL // steps
    down_rows, down_cols = FFN_HIDDEN // (steps // 2), D_MODEL // 2
    assert gu_rows % V7X_BF16_SUBLANES == 0 and down_rows % V7X_BF16_SUBLANES == 0
    step = lambda bb, h: bb * DIFF_HEADS + h
    gu_in = pl.BlockSpec((gu_rows, FFN_HIDDEN), lambda bb, h: (step(bb, h), 0))
    gu_out = pl.BlockSpec((nh, gu_rows, 2 * TH_FFN), lambda bb, h: (0, step(bb, h), 0))
    down_blk = pl.BlockSpec((down_rows, down_cols),
                            lambda bb, h: (step(bb, h) // 2, step(bb, h) % 2))
    return pl.pallas_call(
        _diff_attn_kernel,
        grid=(b, DIFF_HEADS),
        in_specs=[
            pl.BlockSpec(memory_space=pltpu.SMEM),
            pl.BlockSpec((1, nq, DIFF_VDIM, tq), lambda bb, h: (bb, 0, h, 0)),
            pl.BlockSpec((1, s, DIFF_VDIM), lambda bb, h: (bb, 0, h)),
            pl.BlockSpec((1, nk, DIFF_VDIM, tk), lambda bb, h: (bb, 0, h, 0)),
            vec(DIFF_HALF), vec(DIFF_HALF), vec(DIFF_HALF), vec(DIFF_HALF),
            vec(DIFF_VDIM),
            gu_in, gu_in, down_blk,
        ],
        out_specs=[pl.BlockSpec((1, s, DIFF_VDIM), lambda bb, h: (bb, 0, h)), gu_out, down_blk],
        out_shape=[jax.ShapeDtypeStruct((b, s, DIFF_WIDTH), BF16),
                   jax.ShapeDtypeStruct((nh, D_MODEL, 2 * TH_FFN), BF16),
                   jax.ShapeDtypeStruct((FFN_HIDDEN, D_MODEL), BF16)],
        compiler_params=pltpu.CompilerParams(
            dimension_semantics=("parallel", "parallel"),
            vmem_limit_bytes=V7X_VMEM_LIMIT_BYTES),
        name="diff_attn",
    )(bounded, qt4, kn3, vt4, lq1, lk1, lq2, lk2, g_sub, w_gate, w_up, w_down)


def _out_proj_kernel(x_ref, yc_ref, yd_ref, qm_ref, km_ref, vm_ref, gq_ref, gmo_ref, gffn_ref,
                     w_hbm, x1_ref, hf_ref, w_ref, w_stage, w_sem):
    @pl.when(pl.program_id(0) == 0)
    def _():
        _load_weight_as_bf16(w_hbm, w_ref, w_stage, w_sem)

    gq = gq_ref[...]
    probs, denoms = [], []
    for hd in range(MEM_HEADS):
        cols = slice(hd * MEM_HEAD_DIM, (hd + 1) * MEM_HEAD_DIM)
        qh = qm_ref[:, cols]
        qh = (qh * _rms_scale(qh, MEM_HEAD_DIM) * gq * (MEM_HEAD_DIM ** -0.5)).astype(BF16)
        s = lax.dot_general(qh, km_ref[0, :, cols], (((1,), (1,)), ((), ())),
                            preferred_element_type=F32)
        e = jnp.exp(s - jnp.max(s, axis=-1, keepdims=True))
        denoms.append(jnp.sum(e, axis=-1, keepdims=True))
        probs.append(e.astype(BF16))

    acc = jnp.dot(yc_ref[...], w_ref[0:CONV_WIDTH, :], preferred_element_type=F32)

    outs = []
    for hd in range(MEM_HEADS):
        cols = slice(hd * MEM_HEAD_DIM, (hd + 1) * MEM_HEAD_DIM)
        oh = jnp.dot(probs[hd], vm_ref[0, :, cols], preferred_element_type=F32)
        outs.append(oh / denoms[hd])
    o = jnp.concatenate(outs, axis=-1)
    ym = (o * _rms_scale(o, MEM_WIDTH) * gmo_ref[...]).astype(BF16)

    acc += jnp.dot(yd_ref[...], w_ref[CONV_WIDTH:CONV_WIDTH + DIFF_WIDTH, :],
                   preferred_element_type=F32)
    acc += jnp.dot(ym, w_ref[CONV_WIDTH + DIFF_WIDTH:, :], preferred_element_type=F32)
    x1 = x_ref[...] + acc
    x1_ref[...] = x1
    hf_ref[...] = (x1 * _rms_scale(x1, D_MODEL) * gffn_ref[...]).astype(BF16)


def _out_proj(x2, yc, yd, qm, km3, vm3, g_mq, g_mem_out, g_ffn, w_o):
    t = x2.shape[0]
    tm = TM_OUT
    tps = SEQ // tm
    row = lambda i: (i, 0)
    fixed = lambda i: (0, 0)
    mem_blk = lambda i: (i // tps, 0, 0)
    return pl.pallas_call(
        _out_proj_kernel,
        grid=(t // tm,),
        in_specs=[
            pl.BlockSpec((tm, D_MODEL), row),
            pl.BlockSpec((tm, CONV_WIDTH), row),
            pl.BlockSpec((tm, DIFF_WIDTH), row),
            pl.BlockSpec((tm, MEM_WIDTH), row),
            pl.BlockSpec((1, MEM_LEN, MEM_WIDTH), mem_blk),
            pl.BlockSpec((1, MEM_LEN, MEM_WIDTH), mem_blk),
            pl.BlockSpec((1, MEM_HEAD_DIM), fixed),
            pl.BlockSpec((1, MEM_WIDTH), fixed),
            pl.BlockSpec((1, D_MODEL), fixed),
            pl.BlockSpec(memory_space=pl.ANY),
        ],
        out_specs=[pl.BlockSpec((tm, D_MODEL), row), pl.BlockSpec((tm, D_MODEL), row)],
        out_shape=[jax.ShapeDtypeStruct((t, D_MODEL), F32),
                   jax.ShapeDtypeStruct((t, D_MODEL), BF16)],
        scratch_shapes=_resident_weight_scratch(D_MODEL, D_MODEL),
        compiler_params=pltpu.CompilerParams(
            dimension_semantics=("arbitrary",),
            vmem_limit_bytes=V7X_VMEM_LIMIT_BYTES),
        name="out_proj",
    )(x2, yc, yd, qm, km3, vm3, g_mq, g_mem_out, g_ffn, w_o)


def _ffn_kernel(hf_ref, x1_ref, wgu_ref, wd_ref, o_ref):
    @pl.when(pl.program_id(1) == 0)
    def _():
        o_ref[...] = x1_ref[...]

    th = wd_ref.shape[0]
    gu = jnp.dot(hf_ref[...], wgu_ref[0], preferred_element_type=F32)
    gate, up = gu[:, :th], gu[:, th:]
    act = (gate * (1.0 / (1.0 + jnp.exp(-gate))) * up).astype(BF16)
    o_ref[...] += jnp.dot(act, wd_ref[...], preferred_element_type=F32)


def _ffn(hf, x1, w_gate_up, w_down):
    t = x1.shape[0]
    tm, th = TM_FFN, TH_FFN
    return pl.pallas_call(
        _ffn_kernel,
        grid=(t // tm, FFN_HIDDEN // th),
        in_specs=[
            pl.BlockSpec((tm, D_MODEL), lambda i, j: (i, 0)),
            pl.BlockSpec((tm, D_MODEL), lambda i, j: (i, 0)),
            pl.BlockSpec((1, D_MODEL, 2 * th), lambda i, j: (j, 0, 0)),
            pl.BlockSpec((th, D_MODEL), lambda i, j: (j, 0)),
        ],
        out_specs=pl.BlockSpec((tm, D_MODEL), lambda i, j: (i, 0)),
        out_shape=jax.ShapeDtypeStruct((t, D_MODEL), F32),
        compiler_params=pltpu.CompilerParams(
            dimension_semantics=("parallel", "arbitrary"),
            vmem_limit_bytes=V7X_VMEM_LIMIT_BYTES),
        name="ffn",
    )(hf, x1, w_gate_up, w_down)


def _rope_tables(positions):
    inv_freq = ROPE_THETA ** (-jnp.arange(0, ROT_DIM, 2, dtype=F32) / ROT_DIM)
    pad = jnp.zeros((DIFF_HALF - ROT_DIM,), F32)
    lane_freq = jnp.concatenate([inv_freq, inv_freq, pad, inv_freq, inv_freq, pad])
    ang = positions.astype(F32).reshape(-1, 1) * lane_freq[None, :]
    return jnp.cos(ang), jnp.sin(ang)


def kernel(x, mem, positions, g_mix, g_mem, w_in, conv_w, g_conv_out, g_dq, g_dk,
           lam_q1, lam_k1, lam_q2, lam_k2, g_sub, w_mem_kv, g_mq, g_mk, g_mem_out,
           w_o, g_ffn, w_gate, w_up, w_down):
    b, s, d = x.shape
    assert (s, d) == (SEQ, D_MODEL) and mem.shape == (b, MEM_LEN, D_MODEL)
    assert g_mix.shape[0] == 1
    t = b * s
    x2 = x.reshape(t, d)
    mem2 = mem.reshape(b * MEM_LEN, d)
    cos_t, sin_t = _rope_tables(positions)

    yc, qt, kn, vt, qm = _in_proj(x2, g_mix, w_in[0], conv_w[0], g_conv_out,
                                  cos_t, sin_t,
                                  jnp.tile(g_dq, (1, 2)), jnp.tile(g_dk, (1, 2)))
    km, vm = _mem_kv(mem2, g_mem, w_mem_kv[0], g_mk)

    score_bound = (DIFF_HALF ** 0.5 * LOG2_E) * jnp.max(jnp.abs(g_dq)) * jnp.max(jnp.abs(g_dk))
    bounded = (score_bound <= MAX_SHIFT_FREE_LOG2_SCORE).astype(jnp.int32).reshape(1)
    yd, wgu16, wd16 = _diff_attn(bounded, qt, kn.reshape(b, s, DIFF_WIDTH), vt,
                                 lam_q1, lam_k1, lam_q2, lam_k2, g_sub,
                                 w_gate[0], w_up[0], w_down[0])
    x1, hf = _out_proj(x2, yc, yd.reshape(t, DIFF_WIDTH), qm,
                       km.reshape(b, MEM_LEN, MEM_WIDTH), vm.reshape(b, MEM_LEN, MEM_WIDTH),
                       g_mq, g_mem_out, g_ffn, w_o[0])
    out = _ffn(hf, x1, wgu16, wd16)
    return out.reshape(b, s, d)
```

```python
import functools

import jax
import jax.numpy as jnp
import numpy as np
from jax import lax
from jax.experimental import pallas as pl
from jax.experimental.pallas import tpu as pltpu

F32 = jnp.float32
BF16 = jnp.bfloat16

D_MODEL = 2048
SEQ = 2048
MEM_LEN = 256
CONV_WIDTH = 512
CONV_K = 3
DIFF_WIDTH = 1024
DIFF_VDIM = 128
DIFF_HALF = 64
DIFF_HEADS = 8
MEM_WIDTH = 512
MEM_HEADS = 4
MEM_HEAD_DIM = 128
IN_WIDTH = 3 * CONV_WIDTH + 3 * DIFF_WIDTH + MEM_WIDTH
ROT_DIM = 16
ROPE_THETA = 500000.0
FFN_HIDDEN = 5632
EPS = 1e-6
LAMBDA_INIT = 0.8 - 0.6 * float(np.exp(-0.3 * 0))
LOG2_E = float(np.log2(np.e))
MAX_SHIFT_FREE_LOG2_SCORE = 60.0

V7X_LANES = 128
V7X_SUBLANES = 8
V7X_BF16_SUBLANES = 16
V7X_VMEM_LIMIT_BYTES = 60 * 1024 * 1024

TM_PROJ = 256
TM_MEMKV = 256
TM_OUT = 512
TM_FFN = 1024
TH_FFN = 512
WEIGHT_CHUNK_ROWS = 128
SCORE_LOOKAHEAD = 2
ATTN_HEADS_PER_STEP = 2

_OFF_U = 0
_OFF_C = CONV_WIDTH
_OFF_B = 2 * CONV_WIDTH
_OFF_Q = 3 * CONV_WIDTH
_OFF_K = _OFF_Q + DIFF_WIDTH
_OFF_V = _OFF_K + DIFF_WIDTH
_OFF_QM = _OFF_V + DIFF_WIDTH


def _rms_scale(t, width):
    return lax.rsqrt(jnp.sum(t * t, axis=-1, keepdims=True) * (1.0 / width) + EPS)


def _load_weight_as_bf16(w_hbm, w_scr, stage, sem):
    rows = stage.shape[1]
    n_chunks = w_hbm.shape[0] // rows

    def chunk_copy(c):
        return pltpu.make_async_copy(w_hbm.at[pl.ds(c * rows, rows), :], stage.at[c % 2],
                                     sem.at[c % 2])

    chunk_copy(0).start()
    for c in range(n_chunks):
        if c + 1 < n_chunks:
            chunk_copy(c + 1).start()
        chunk_copy(c).wait()
        w_scr[c * rows:(c + 1) * rows, :] = stage[c % 2].astype(BF16)


def _resident_weight_scratch(k, n):
    return [pltpu.VMEM((k, n), BF16), pltpu.VMEM((2, WEIGHT_CHUNK_ROWS, n), F32),
            pltpu.SemaphoreType.DMA((2,))]


def _qk_norm_rope(t, g2, cos_t, sin_a, sin_b):
    lane = lax.broadcasted_iota(jnp.int32, t.shape, 1)
    is_lo = lane < DIFF_HALF
    t2 = t * t
    s_lo = jnp.sum(jnp.where(is_lo, t2, 0.0), axis=-1, keepdims=True)
    s_hi = jnp.sum(jnp.where(is_lo, 0.0, t2), axis=-1, keepdims=True)
    r = jnp.where(is_lo,
                  lax.rsqrt(s_lo * (1.0 / DIFF_HALF) + EPS),
                  lax.rsqrt(s_hi * (1.0 / DIFF_HALF) + EPS))
    tn = t * r * g2
    half = ROT_DIM // 2
    return (tn * cos_t
            + pltpu.roll(tn, V7X_LANES - half, axis=1) * sin_a
            + pltpu.roll(tn, half, axis=1) * sin_b)


def _in_proj_kernel(x_ref, g_ref, w_hbm, convw_ref, gconv_ref, cos_ref, sin_ref,
                    gq_ref, gk_ref,
                    yconv_ref, qt_ref, kn_ref, vt_ref, qm_ref,
                    z_scr, w_ref, w_stage, w_sem, *, tiles_per_seq):
    i = pl.program_id(0)
    tm = x_ref.shape[0]
    halo = V7X_SUBLANES

    @pl.when(i == 0)
    def _():
        _load_weight_as_bf16(w_hbm, w_ref, w_stage, w_sem)

    @pl.when(i % tiles_per_seq == 0)
    def _():
        z_scr[0:halo, :] = jnp.zeros((halo, CONV_WIDTH), F32)

    x = x_ref[...]
    h = (x * _rms_scale(x, D_MODEL) * g_ref[...]).astype(BF16)

    def proj(off, width):
        return jnp.dot(h, w_ref[:, off:off + width], preferred_element_type=F32)

    z = proj(_OFF_C, CONV_WIDTH) * proj(_OFF_U, CONV_WIDTH)
    z_scr[halo:halo + tm, :] = z
    z1 = z_scr[halo - 1:halo - 1 + tm, :]
    z2 = z_scr[halo - 2:halo - 2 + tm, :]
    cw = convw_ref[...]
    conv = cw[0:1, :] * z2 + cw[1:2, :] * z1 + cw[2:3, :] * z
    y = proj(_OFF_B, CONV_WIDTH) * conv
    yconv_ref[...] = (y * _rms_scale(y, CONV_WIDTH) * gconv_ref[...]).astype(BF16)
    z_scr[0:halo, :] = z_scr[tm:tm + halo, :]

    qf = proj(_OFF_Q, DIFF_WIDTH)
    kf = proj(_OFF_K, DIFF_WIDTH)
    vf = proj(_OFF_V, DIFF_WIDTH)
    cos_t, sin_t = cos_ref[...], sin_ref[...]
    pos_in_half = lax.broadcasted_iota(jnp.int32, sin_t.shape, 1) % DIFF_HALF
    sin_a = jnp.where(pos_in_half < ROT_DIM // 2, -sin_t, 0.0)
    sin_b = jnp.where((pos_in_half >= ROT_DIM // 2) & (pos_in_half < ROT_DIM), sin_t, 0.0)
    gq, gk = gq_ref[...], gk_ref[...]
    for hd in range(DIFF_HEADS):
        cols = slice(hd * DIFF_VDIM, (hd + 1) * DIFF_VDIM)
        qn = _qk_norm_rope(qf[:, cols], gq, cos_t, sin_a, sin_b) * (DIFF_HALF ** -0.5 * LOG2_E)
        qt_ref[0, 0, cols, :] = qn.T.astype(BF16)
        kn_ref[:, cols] = _qk_norm_rope(kf[:, cols], gk, cos_t, sin_a, sin_b).astype(BF16)
        vt_ref[0, 0, cols, :] = vf[:, cols].T.astype(BF16)
    qm_ref[...] = proj(_OFF_QM, MEM_WIDTH)


def _in_proj(x2, g_mix, w_in, conv_w, g_conv, cos_t, sin_t, gq2, gk2):
    t = x2.shape[0]
    tm = TM_PROJ
    tps = SEQ // tm
    nb = t // SEQ
    grid = (t // tm,)
    row = lambda i: (i, 0)
    fixed = lambda i: (0, 0)
    tile_t = lambda i: (i // tps, i % tps, 0, 0)
    return pl.pallas_call(
        functools.partial(_in_proj_kernel, tiles_per_seq=tps),
        grid=grid,
        in_specs=[
            pl.BlockSpec((tm, D_MODEL), row),
            pl.BlockSpec((1, D_MODEL), fixed),
            pl.BlockSpec(memory_space=pl.ANY),
            pl.BlockSpec((CONV_K, CONV_WIDTH), fixed),
            pl.BlockSpec((1, CONV_WIDTH), fixed),
            pl.BlockSpec((tm, V7X_LANES), row),
            pl.BlockSpec((tm, V7X_LANES), row),
            pl.BlockSpec((1, DIFF_VDIM), fixed),
            pl.BlockSpec((1, DIFF_VDIM), fixed),
        ],
        out_specs=[
            pl.BlockSpec((tm, CONV_WIDTH), row),
            pl.BlockSpec((1, 1, DIFF_WIDTH, tm), tile_t),
            pl.BlockSpec((tm, DIFF_WIDTH), row),
            pl.BlockSpec((1, 1, DIFF_WIDTH, tm), tile_t),
            pl.BlockSpec((tm, MEM_WIDTH), row),
        ],
        out_shape=[
            jax.ShapeDtypeStruct((t, CONV_WIDTH), BF16),
            jax.ShapeDtypeStruct((nb, tps, DIFF_WIDTH, tm), BF16),
            jax.ShapeDtypeStruct((t, DIFF_WIDTH), BF16),
            jax.ShapeDtypeStruct((nb, tps, DIFF_WIDTH, tm), BF16),
            jax.ShapeDtypeStruct((t, MEM_WIDTH), F32),
        ],
        scratch_shapes=[pltpu.VMEM((tm + V7X_SUBLANES, CONV_WIDTH), F32)]
        + _resident_weight_scratch(D_MODEL, IN_WIDTH),
        compiler_params=pltpu.CompilerParams(
            dimension_semantics=("arbitrary",),
            vmem_limit_bytes=V7X_VMEM_LIMIT_BYTES),
        name="in_proj",
    )(x2, g_mix, w_in, conv_w, g_conv, cos_t, sin_t, gq2, gk2)


def _mem_kv_kernel(mem_ref, g_ref, w_hbm, gk_ref, k_ref, v_ref, w_ref, w_stage, w_sem):
    @pl.when(pl.program_id(0) == 0)
    def _():
        _load_weight_as_bf16(w_hbm, w_ref, w_stage, w_sem)

    x = mem_ref[...]
    h = (x * _rms_scale(x, D_MODEL) * g_ref[...]).astype(BF16)
    kv = jnp.dot(h, w_ref[...], preferred_element_type=F32)
    gk = gk_ref[...]
    for hd in range(MEM_HEADS):
        lo = hd * MEM_HEAD_DIM
        kh = kv[:, lo:lo + MEM_HEAD_DIM]
        k_ref[:, lo:lo + MEM_HEAD_DIM] = (kh * _rms_scale(kh, MEM_HEAD_DIM) * gk).astype(BF16)
    v_ref[...] = kv[:, MEM_WIDTH:].astype(BF16)


def _mem_kv(mem2, g_mem, w_kv, g_mk):
    t = mem2.shape[0]
    tm = TM_MEMKV
    row = lambda i: (i, 0)
    fixed = lambda i: (0, 0)
    return pl.pallas_call(
        _mem_kv_kernel,
        grid=(t // tm,),
        in_specs=[
            pl.BlockSpec((tm, D_MODEL), row),
            pl.BlockSpec((1, D_MODEL), fixed),
            pl.BlockSpec(memory_space=pl.ANY),
            pl.BlockSpec((1, MEM_HEAD_DIM), fixed),
        ],
        out_specs=[pl.BlockSpec((tm, MEM_WIDTH), row), pl.BlockSpec((tm, MEM_WIDTH), row)],
        out_shape=[jax.ShapeDtypeStruct((t, MEM_WIDTH), BF16),
                   jax.ShapeDtypeStruct((t, MEM_WIDTH), BF16)],
        scratch_shapes=_resident_weight_scratch(D_MODEL, 2 * MEM_WIDTH),
        compiler_params=pltpu.CompilerParams(
            dimension_semantics=("arbitrary",),
            vmem_limit_bytes=V7X_VMEM_LIMIT_BYTES),
        name="mem_kv",
    )(mem2, g_mem, w_kv, g_mk)


def _diff_attn_kernel(bounded_ref, qt_ref, kn_ref, vt_ref, lq1_ref, lk1_ref, lq2_ref, lk2_ref,
                      gsub_ref, wg_ref, wu_ref, wd_ref, o_ref, wgu_out, wd_out):
    nq, tq = qt_ref.shape[1], qt_ref.shape[3]
    tk = vt_ref.shape[3]

    lam = (jnp.exp(jnp.sum(lq1_ref[...] * lk1_ref[...], axis=-1, keepdims=True))
           - jnp.exp(jnp.sum(lq2_ref[...] * lk2_ref[...], axis=-1, keepdims=True))
           + LAMBDA_INIT)
    key = lax.broadcasted_iota(jnp.int32, (tk, tq), 0)
    qry = lax.broadcasted_iota(jnp.int32, (tk, tq), 1)
    causal = jnp.concatenate([key <= qry, key <= qry], axis=1)
    sum_rows = (lax.broadcasted_iota(jnp.int32, (V7X_BF16_SUBLANES, tk), 0) == 0).astype(BF16)

    def head(hd):
        return slice(hd * DIFF_VDIM, (hd + 1) * DIFF_VDIM)

    def scores(hd, c, j):
        qt = qt_ref[0, c, head(hd), :]
        is_lo = lax.broadcasted_iota(jnp.int32, qt.shape, 0) < DIFF_HALF
        zero = jnp.zeros_like(qt)
        qq = jnp.concatenate([jnp.where(is_lo, qt, zero), jnp.where(is_lo, zero, qt)], axis=1)
        return jnp.dot(kn_ref[0, j * tk:(j + 1) * tk, head(hd)], qq,
                       preferred_element_type=F32)

    steps = [(hd, c, j) for hd in range(ATTN_HEADS_PER_STEP)
             for c in range(nq) for j in range(c + 1)]

    def attend(shift_free):
        th = wgu_out.shape[2] // 2
        for ch in range(wgu_out.shape[0]):
            cols = slice(ch * th, (ch + 1) * th)
            wgu_out[ch, :, :th] = wg_ref[:, cols].astype(BF16)
            wgu_out[ch, :, th:] = wu_ref[:, cols].astype(BF16)
        wd_out[...] = wd_ref[...].astype(BF16)
        pending = [scores(*st) for st in steps[:SCORE_LOOKAHEAD]]
        m = acc = None
        for n, (hd, c, j) in enumerate(steps):
            s = pending.pop(0)
            if n + SCORE_LOOKAHEAD < len(steps):
                pending.append(scores(*steps[n + SCORE_LOOKAHEAD]))
            if j == c:
                s = jnp.where(causal, s, -jnp.inf)
            vt_ext = jnp.concatenate([vt_ref[0, j, head(hd), :], sum_rows], axis=0)
            if shift_free:
                pv = jnp.dot(vt_ext, jnp.exp2(s).astype(BF16), preferred_element_type=F32)
                acc = pv if j == 0 else acc + pv
            else:
                m_blk = jnp.max(s, axis=0, keepdims=True)
                m_new = m_blk if j == 0 else jnp.maximum(m, m_blk)
                p = jnp.exp2(s - m_new)
                pv = jnp.dot(vt_ext, p.astype(BF16), preferred_element_type=F32)
                acc = pv if j == 0 else jnp.exp2(m - m_new) * acc + pv
                m = m_new
            if j == c:
                o_all = acc[:DIFF_VDIM] / acc[DIFF_VDIM:DIFF_VDIM + 1]
                o = (o_all[:, :tq] - lam * o_all[:, tq:]).T
                y = o * _rms_scale(o, DIFF_VDIM) * gsub_ref[...] * (1.0 - LAMBDA_INIT)
                o_ref[0, c * tq:(c + 1) * tq, head(hd)] = y.astype(o_ref.dtype)

    pl.when(bounded_ref[0] != 0)(functools.partial(attend, True))
    pl.when(bounded_ref[0] == 0)(functools.partial(attend, False))


def _diff_attn(bounded, qt4, kn3, vt4, lq1, lk1, lq2, lk2, g_sub, w_gate, w_up, w_down):
    b, nq, _, tq = qt4.shape
    s = kn3.shape[1]
    nk, tk = vt4.shape[1], vt4.shape[3]
    assert tq == tk and nq == nk
    fixed = lambda bb, h: (0, 0)
    vec = lambda n: pl.BlockSpec((1, n), fixed)
    groups = DIFF_HEADS // ATTN_HEADS_PER_STEP
    width = ATTN_HEADS_PER_STEP * DIFF_VDIM
    steps = b * groups
    nh = FFN_HIDDEN // TH_FFN
    gu_rows, down_rows = D_MODEL // steps, FFN_HIDDEN // steps
    assert gu_rows % V7X_BF16_SUBLANES == 0 and down_rows % V7X_BF16_SUBLANES == 0
    step = lambda bb, g: bb * groups + g
    gu_in = pl.BlockSpec((gu_rows, FFN_HIDDEN), lambda bb, g: (step(bb, g), 0))
    gu_out = pl.BlockSpec((nh, gu_rows, 2 * TH_FFN), lambda bb, g: (0, step(bb, g), 0))
    down_blk = pl.BlockSpec((down_rows, D_MODEL), lambda bb, g: (step(bb, g), 0))
    return pl.pallas_call(
        _diff_attn_kernel,
        grid=(b, groups),
        in_specs=[
            pl.BlockSpec(memory_space=pltpu.SMEM),
            pl.BlockSpec((1, nq, width, tq), lambda bb, g: (bb, 0, g, 0)),
            pl.BlockSpec((1, s, width), lambda bb, g: (bb, 0, g)),
            pl.BlockSpec((1, nk, width, tk), lambda bb, g: (bb, 0, g, 0)),
            vec(DIFF_HALF), vec(DIFF_HALF), vec(DIFF_HALF), vec(DIFF_HALF),
            vec(DIFF_VDIM),
            gu_in, gu_in, down_blk,
        ],
        out_specs=[pl.BlockSpec((1, s, width), lambda bb, g: (bb, 0, g)), gu_out, down_blk],
        out_shape=[jax.ShapeDtypeStruct((b, s, DIFF_WIDTH), BF16),
                   jax.ShapeDtypeStruct((nh, D_MODEL, 2 * TH_FFN), BF16),
                   jax.ShapeDtypeStruct((FFN_HIDDEN, D_MODEL), BF16)],
        compiler_params=pltpu.CompilerParams(
            dimension_semantics=("parallel", "parallel"),
            vmem_limit_bytes=V7X_VMEM_LIMIT_BYTES),
        name="diff_attn",
    )(bounded, qt4, kn3, vt4, lq1, lk1, lq2, lk2, g_sub, w_gate, w_up, w_down)


def _out_proj_kernel(x_ref, yc_ref, yd_ref, qm_ref, km_ref, vm_ref, gq_ref, gmo_ref, gffn_ref,
                     w_hbm, x1_ref, hf_ref, w_ref, w_stage, w_sem):
    @pl.when(pl.program_id(0) == 0)
    def _():
        _load_weight_as_bf16(w_hbm, w_ref, w_stage, w_sem)

    gq = gq_ref[...]
    probs, denoms = [], []
    for hd in range(MEM_HEADS):
        cols = slice(hd * MEM_HEAD_DIM, (hd + 1) * MEM_HEAD_DIM)
        qh = qm_ref[:, cols]
        qh = (qh * _rms_scale(qh, MEM_HEAD_DIM) * gq * (MEM_HEAD_DIM ** -0.5)).astype(BF16)
        s = lax.dot_general(qh, km_ref[0, :, cols], (((1,), (1,)), ((), ())),
                            preferred_element_type=F32)
        e = jnp.exp(s - jnp.max(s, axis=-1, keepdims=True))
        denoms.append(jnp.sum(e, axis=-1, keepdims=True))
        probs.append(e.astype(BF16))

    acc = jnp.dot(yc_ref[...], w_ref[0:CONV_WIDTH, :], preferred_element_type=F32)

    outs = []
    for hd in range(MEM_HEADS):
        cols = slice(hd * MEM_HEAD_DIM, (hd + 1) * MEM_HEAD_DIM)
        oh = jnp.dot(probs[hd], vm_ref[0, :, cols], preferred_element_type=F32)
        outs.append(oh / denoms[hd])
    o = jnp.concatenate(outs, axis=-1)
    ym = (o * _rms_scale(o, MEM_WIDTH) * gmo_ref[...]).astype(BF16)

    acc += jnp.dot(yd_ref[...], w_ref[CONV_WIDTH:CONV_WIDTH + DIFF_WIDTH, :],
                   preferred_element_type=F32)
    acc += jnp.dot(ym, w_ref[CONV_WIDTH + DIFF_WIDTH:, :], preferred_element_type=F32)
    x1 = x_ref[...] + acc
    x1_ref[...] = x1
    hf_ref[...] = (x1 * _rms_scale(x1, D_MODEL) * gffn_ref[...]).astype(BF16)


def _out_proj(x2, yc, yd, qm, km3, vm3, g_mq, g_mem_out, g_ffn, w_o):
    t = x2.shape[0]
    tm = TM_OUT
    tps = SEQ // tm
    row = lambda i: (i, 0)
    fixed = lambda i: (0, 0)
    mem_blk = lambda i: (i // tps, 0, 0)
    return pl.pallas_call(
        _out_proj_kernel,
        grid=(t // tm,),
        in_specs=[
            pl.BlockSpec((tm, D_MODEL), row),
            pl.BlockSpec((tm, CONV_WIDTH), row),
            pl.BlockSpec((tm, DIFF_WIDTH), row),
            pl.BlockSpec((tm, MEM_WIDTH), row),
            pl.BlockSpec((1, MEM_LEN, MEM_WIDTH), mem_blk),
            pl.BlockSpec((1, MEM_LEN, MEM_WIDTH), mem_blk),
            pl.BlockSpec((1, MEM_HEAD_DIM), fixed),
            pl.BlockSpec((1, MEM_WIDTH), fixed),
            pl.BlockSpec((1, D_MODEL), fixed),
            pl.BlockSpec(memory_space=pl.ANY),
        ],
        out_specs=[pl.BlockSpec((tm, D_MODEL), row), pl.BlockSpec((tm, D_MODEL), row)],
        out_shape=[jax.ShapeDtypeStruct((t, D_MODEL), F32),
                   jax.ShapeDtypeStruct((t, D_MODEL), BF16)],
        scratch_shapes=_resident_weight_scratch(D_MODEL, D_MODEL),
        compiler_params=pltpu.CompilerParams(
            dimension_semantics=("arbitrary",),
            vmem_limit_bytes=V7X_VMEM_LIMIT_BYTES),
        name="out_proj",
    )(x2, yc, yd, qm, km3, vm3, g_mq, g_mem_out, g_ffn, w_o)


def _ffn_kernel(hf_ref, x1_ref, wgu_ref, wd_ref, o_ref):
    @pl.when(pl.program_id(1) == 0)
    def _():
        o_ref[...] = x1_ref[...]

    th = wd_ref.shape[0]
    gu = jnp.dot(hf_ref[...], wgu_ref[0], preferred_element_type=F32)
    gate, up = gu[:, :th], gu[:, th:]
    act = (gate * (1.0 / (1.0 + jnp.exp(-gate))) * up).astype(BF16)
    o_ref[...] += jnp.dot(act, wd_ref[...], preferred_element_type=F32)


def _ffn(hf, x1, w_gate_up, w_down):
    t = x1.shape[0]
    tm, th = TM_FFN, TH_FFN
    return pl.pallas_call(
        _ffn_kernel,
        grid=(t // tm, FFN_HIDDEN // th),
        in_specs=[
            pl.BlockSpec((tm, D_MODEL), lambda i, j: (i, 0)),
            pl.BlockSpec((tm, D_MODEL), lambda i, j: (i, 0)),
            pl.BlockSpec((1, D_MODEL, 2 * th), lambda i, j: (j, 0, 0)),
            pl.BlockSpec((th, D_MODEL), lambda i, j: (j, 0)),
        ],
        out_specs=pl.BlockSpec((tm, D_MODEL), lambda i, j: (i, 0)),
        out_shape=jax.ShapeDtypeStruct((t, D_MODEL), F32),
        compiler_params=pltpu.CompilerParams(
            dimension_semantics=("parallel", "arbitrary"),
            vmem_limit_bytes=V7X_VMEM_LIMIT_BYTES),
        name="ffn",
    )(hf, x1, w_gate_up, w_down)


def _rope_tables(positions):
    inv_freq = ROPE_THETA ** (-jnp.arange(0, ROT_DIM, 2, dtype=F32) / ROT_DIM)
    pad = jnp.zeros((DIFF_HALF - ROT_DIM,), F32)
    lane_freq = jnp.concatenate([inv_freq, inv_freq, pad, inv_freq, inv_freq, pad])
    ang = positions.astype(F32).reshape(-1, 1) * lane_freq[None, :]
    return jnp.cos(ang), jnp.sin(ang)


def kernel(x, mem, positions, g_mix, g_mem, w_in, conv_w, g_conv_out, g_dq, g_dk,
           lam_q1, lam_k1, lam_q2, lam_k2, g_sub, w_mem_kv, g_mq, g_mk, g_mem_out,
           w_o, g_ffn, w_gate, w_up, w_down):
    b, s, d = x.shape
    assert (s, d) == (SEQ, D_MODEL) and mem.shape == (b, MEM_LEN, D_MODEL)
    assert g_mix.shape[0] == 1
    t = b * s
    x2 = x.reshape(t, d)
    mem2 = mem.reshape(b * MEM_LEN, d)
    cos_t, sin_t = _rope_tables(positions)

    yc, qt, kn, vt, qm = _in_proj(x2, g_mix, w_in[0], conv_w[0], g_conv_out,
                                  cos_t, sin_t,
                                  jnp.tile(g_dq, (1, 2)), jnp.tile(g_dk, (1, 2)))
    km, vm = _mem_kv(mem2, g_mem, w_mem_kv[0], g_mk)

    score_bound = (DIFF_HALF ** 0.5 * LOG2_E) * jnp.max(jnp.abs(g_dq)) * jnp.max(jnp.abs(g_dk))
    bounded = (score_bound <= MAX_SHIFT_FREE_LOG2_SCORE).astype(jnp.int32).reshape(1)
    yd, wgu16, wd16 = _diff_attn(bounded, qt, kn.reshape(b, s, DIFF_WIDTH), vt,
                                 lam_q1, lam_k1, lam_q2, lam_k2, g_sub,
                                 w_gate[0], w_up[0], w_down[0])
    x1, hf = _out_proj(x2, yc, yd.reshape(t, DIFF_WIDTH), qm,
                       km.reshape(b, MEM_LEN, MEM_WIDTH), vm.reshape(b, MEM_LEN, MEM_WIDTH),
                       g_mq, g_mem_out, g_ffn, w_o[0])
    out = _ffn(hf, x1, wgu16, wd16)
    return out.reshape(b, s, d)
```

```python
import functools

import jax
import jax.numpy as jnp
import numpy as np
from jax import lax
from jax.experimental import pallas as pl
from jax.experimental.pallas import tpu as pltpu

F32 = jnp.float32
BF16 = jnp.bfloat16

D_MODEL = 2048
SEQ = 2048
MEM_LEN = 256
CONV_WIDTH = 512
CONV_K = 3
DIFF_WIDTH = 1024
DIFF_VDIM = 128
DIFF_HALF = 64
DIFF_HEADS = 8
MEM_WIDTH = 512
MEM_HEADS = 4
MEM_HEAD_DIM = 128
IN_WIDTH = 3 * CONV_WIDTH + 3 * DIFF_WIDTH + MEM_WIDTH
ROT_DIM = 16
ROPE_THETA = 500000.0
FFN_HIDDEN = 5632
EPS = 1e-6
LAMBDA_INIT = 0.8 - 0.6 * float(np.exp(-0.3 * 0))
LOG2_E = float(np.log2(np.e))
MAX_SHIFT_FREE_LOG2_SCORE = 60.0

V7X_LANES = 128
V7X_SUBLANES = 8
V7X_BF16_SUBLANES = 16
V7X_VMEM_LIMIT_BYTES = 60 * 1024 * 1024

TM_PROJ = 256
TM_MEMKV = 256
TM_OUT = 512
TM_FFN = 1024
TH_FFN = 512
WEIGHT_CHUNK_ROWS = 128
SCORE_LOOKAHEAD = 2
ATTN_HEADS_PER_STEP = 2

_OFF_U = 0
_OFF_C = CONV_WIDTH
_OFF_B = 2 * CONV_WIDTH
_OFF_Q = 3 * CONV_WIDTH
_OFF_K = _OFF_Q + DIFF_WIDTH
_OFF_V = _OFF_K + DIFF_WIDTH
_OFF_QM = _OFF_V + DIFF_WIDTH


def _rms_scale(t, width):
    return lax.rsqrt(jnp.sum(t * t, axis=-1, keepdims=True) * (1.0 / width) + EPS)


def _load_weight_as_bf16(w_hbm, w_scr, stage, sem):
    rows = stage.shape[1]
    n_chunks = w_hbm.shape[0] // rows

    def chunk_copy(c):
        return pltpu.make_async_copy(w_hbm.at[pl.ds(c * rows, rows), :], stage.at[c % 2],
                                     sem.at[c % 2])

    chunk_copy(0).start()
    for c in range(n_chunks):
        if c + 1 < n_chunks:
            chunk_copy(c + 1).start()
        chunk_copy(c).wait()
        w_scr[c * rows:(c + 1) * rows, :] = stage[c % 2].astype(BF16)


def _resident_weight_scratch(k, n):
    return [pltpu.VMEM((k, n), BF16), pltpu.VMEM((2, WEIGHT_CHUNK_ROWS, n), F32),
            pltpu.SemaphoreType.DMA((2,))]


def _qk_norm_rope(t, g2, cos_t, sin_a, sin_b):
    lane = lax.broadcasted_iota(jnp.int32, t.shape, 1)
    is_lo = lane < DIFF_HALF
    t2 = t * t
    s_lo = jnp.sum(jnp.where(is_lo, t2, 0.0), axis=-1, keepdims=True)
    s_hi = jnp.sum(jnp.where(is_lo, 0.0, t2), axis=-1, keepdims=True)
    r = jnp.where(is_lo,
                  lax.rsqrt(s_lo * (1.0 / DIFF_HALF) + EPS),
                  lax.rsqrt(s_hi * (1.0 / DIFF_HALF) + EPS))
    tn = t * r * g2
    half = ROT_DIM // 2
    return (tn * cos_t
            + pltpu.roll(tn, V7X_LANES - half, axis=1) * sin_a
            + pltpu.roll(tn, half, axis=1) * sin_b)


def _in_proj_kernel(x_ref, g_ref, w_hbm, convw_ref, gconv_ref, pos_ref, freq_ref,
                    gq_ref, gk_ref,
                    yconv_ref, qt_ref, kn_ref, vt_ref, qm_ref,
                    z_scr, w_ref, w_stage, w_sem, *, tiles_per_seq):
    i = pl.program_id(0)
    tm = x_ref.shape[0]
    halo = V7X_SUBLANES

    @pl.when(i == 0)
    def _():
        _load_weight_as_bf16(w_hbm, w_ref, w_stage, w_sem)

    @pl.when(i % tiles_per_seq == 0)
    def _():
        z_scr[0:halo, :] = jnp.zeros((halo, CONV_WIDTH), F32)

    x = x_ref[...]
    h = (x * _rms_scale(x, D_MODEL) * g_ref[...]).astype(BF16)

    def proj(off, width):
        return jnp.dot(h, w_ref[:, off:off + width], preferred_element_type=F32)

    z = proj(_OFF_C, CONV_WIDTH) * proj(_OFF_U, CONV_WIDTH)
    z_scr[halo:halo + tm, :] = z
    z1 = z_scr[halo - 1:halo - 1 + tm, :]
    z2 = z_scr[halo - 2:halo - 2 + tm, :]
    cw = convw_ref[...]
    conv = cw[0:1, :] * z2 + cw[1:2, :] * z1 + cw[2:3, :] * z
    y = proj(_OFF_B, CONV_WIDTH) * conv
    yconv_ref[...] = (y * _rms_scale(y, CONV_WIDTH) * gconv_ref[...]).astype(BF16)
    z_scr[0:halo, :] = z_scr[tm:tm + halo, :]

    qf = proj(_OFF_Q, DIFF_WIDTH)
    kf = proj(_OFF_K, DIFF_WIDTH)
    vf = proj(_OFF_V, DIFF_WIDTH)
    ang = pos_ref[...].astype(F32) * freq_ref[...]
    cos_t, sin_t = jnp.cos(ang), jnp.sin(ang)
    pos_in_half = lax.broadcasted_iota(jnp.int32, sin_t.shape, 1) % DIFF_HALF
    sin_a = jnp.where(pos_in_half < ROT_DIM // 2, -sin_t, 0.0)
    sin_b = jnp.where((pos_in_half >= ROT_DIM // 2) & (pos_in_half < ROT_DIM), sin_t, 0.0)
    gq, gk = gq_ref[...], gk_ref[...]
    for hd in range(DIFF_HEADS):
        cols = slice(hd * DIFF_VDIM, (hd + 1) * DIFF_VDIM)
        qn = _qk_norm_rope(qf[:, cols], gq, cos_t, sin_a, sin_b) * (DIFF_HALF ** -0.5 * LOG2_E)
        qt_ref[0, 0, cols, :] = qn.T.astype(BF16)
        kn_ref[:, cols] = _qk_norm_rope(kf[:, cols], gk, cos_t, sin_a, sin_b).astype(BF16)
        vt_ref[0, 0, cols, :] = vf[:, cols].T.astype(BF16)
    qm_ref[...] = proj(_OFF_QM, MEM_WIDTH)


def _in_proj(x2, g_mix, w_in, conv_w, g_conv, pos, lane_freq, gq2, gk2):
    t = x2.shape[0]
    tm = TM_PROJ
    tps = SEQ // tm
    nb = t // SEQ
    grid = (t // tm,)
    row = lambda i: (i, 0)
    fixed = lambda i: (0, 0)
    tile_t = lambda i: (i // tps, i % tps, 0, 0)
    return pl.pallas_call(
        functools.partial(_in_proj_kernel, tiles_per_seq=tps),
        grid=grid,
        in_specs=[
            pl.BlockSpec((tm, D_MODEL), row),
            pl.BlockSpec((1, D_MODEL), fixed),
            pl.BlockSpec(memory_space=pl.ANY),
            pl.BlockSpec((CONV_K, CONV_WIDTH), fixed),
            pl.BlockSpec((1, CONV_WIDTH), fixed),
            pl.BlockSpec((tm, 1), row),
            pl.BlockSpec((1, V7X_LANES), fixed),
            pl.BlockSpec((1, DIFF_VDIM), fixed),
            pl.BlockSpec((1, DIFF_VDIM), fixed),
        ],
        out_specs=[
            pl.BlockSpec((tm, CONV_WIDTH), row),
            pl.BlockSpec((1, 1, DIFF_WIDTH, tm), tile_t),
            pl.BlockSpec((tm, DIFF_WIDTH), row),
            pl.BlockSpec((1, 1, DIFF_WIDTH, tm), tile_t),
            pl.BlockSpec((tm, MEM_WIDTH), row),
        ],
        out_shape=[
            jax.ShapeDtypeStruct((t, CONV_WIDTH), BF16),
            jax.ShapeDtypeStruct((nb, tps, DIFF_WIDTH, tm), BF16),
            jax.ShapeDtypeStruct((t, DIFF_WIDTH), BF16),
            jax.ShapeDtypeStruct((nb, tps, DIFF_WIDTH, tm), BF16),
            jax.ShapeDtypeStruct((t, MEM_WIDTH), F32),
        ],
        scratch_shapes=[pltpu.VMEM((tm + V7X_SUBLANES, CONV_WIDTH), F32)]
        + _resident_weight_scratch(D_MODEL, IN_WIDTH),
        compiler_params=pltpu.CompilerParams(
            dimension_semantics=("arbitrary",),
            vmem_limit_bytes=V7X_VMEM_LIMIT_BYTES),
        name="in_proj",
    )(x2, g_mix, w_in, conv_w, g_conv, pos, lane_freq, gq2, gk2)


def _mem_kv_kernel(mem_ref, g_ref, w_hbm, gk_ref, k_ref, v_ref, w_ref, w_stage, w_sem):
    @pl.when(pl.program_id(0) == 0)
    def _():
        _load_weight_as_bf16(w_hbm, w_ref, w_stage, w_sem)

    x = mem_ref[...]
    h = (x * _rms_scale(x, D_MODEL) * g_ref[...]).astype(BF16)
    kv = jnp.dot(h, w_ref[...], preferred_element_type=F32)
    gk = gk_ref[...]
    for hd in range(MEM_HEADS):
        lo = hd * MEM_HEAD_DIM
        kh = kv[:, lo:lo + MEM_HEAD_DIM]
        k_ref[:, lo:lo + MEM_HEAD_DIM] = (kh * _rms_scale(kh, MEM_HEAD_DIM) * gk).astype(BF16)
    v_ref[...] = kv[:, MEM_WIDTH:].astype(BF16)


def _mem_kv(mem2, g_mem, w_kv, g_mk):
    t = mem2.shape[0]
    tm = TM_MEMKV
    row = lambda i: (i, 0)
    fixed = lambda i: (0, 0)
    return pl.pallas_call(
        _mem_kv_kernel,
        grid=(t // tm,),
        in_specs=[
            pl.BlockSpec((tm, D_MODEL), row),
            pl.BlockSpec((1, D_MODEL), fixed),
            pl.BlockSpec(memory_space=pl.ANY),
            pl.BlockSpec((1, MEM_HEAD_DIM), fixed),
        ],
        out_specs=[pl.BlockSpec((tm, MEM_WIDTH), row), pl.BlockSpec((tm, MEM_WIDTH), row)],
        out_shape=[jax.ShapeDtypeStruct((t, MEM_WIDTH), BF16),
                   jax.ShapeDtypeStruct((t, MEM_WIDTH), BF16)],
        scratch_shapes=_resident_weight_scratch(D_MODEL, 2 * MEM_WIDTH),
        compiler_params=pltpu.CompilerParams(
            dimension_semantics=("arbitrary",),
            vmem_limit_bytes=V7X_VMEM_LIMIT_BYTES),
        name="mem_kv",
    )(mem2, g_mem, w_kv, g_mk)


def _diff_attn_kernel(bounded_ref, qt_ref, kn_ref, vt_ref, lq1_ref, lk1_ref, lq2_ref, lk2_ref,
                      gsub_ref, wg_ref, wu_ref, wd_ref, o_ref, wgu_out, wd_out):
    nq, tq = qt_ref.shape[1], qt_ref.shape[3]
    tk = vt_ref.shape[3]

    lam = (jnp.exp(jnp.sum(lq1_ref[...] * lk1_ref[...], axis=-1, keepdims=True))
           - jnp.exp(jnp.sum(lq2_ref[...] * lk2_ref[...], axis=-1, keepdims=True))
           + LAMBDA_INIT)
    key = lax.broadcasted_iota(jnp.int32, (tk, tq), 0)
    qry = lax.broadcasted_iota(jnp.int32, (tk, tq), 1)
    causal = jnp.concatenate([key <= qry, key <= qry], axis=1)
    sum_rows = (lax.broadcasted_iota(jnp.int32, (V7X_BF16_SUBLANES, tk), 0) == 0).astype(BF16)

    def head(hd):
        return slice(hd * DIFF_VDIM, (hd + 1) * DIFF_VDIM)

    def scores(hd, c, j):
        qt = qt_ref[0, c, head(hd), :]
        is_lo = lax.broadcasted_iota(jnp.int32, qt.shape, 0) < DIFF_HALF
        zero = jnp.zeros_like(qt)
        qq = jnp.concatenate([jnp.where(is_lo, qt, zero), jnp.where(is_lo, zero, qt)], axis=1)
        return jnp.dot(kn_ref[0, j * tk:(j + 1) * tk, head(hd)], qq,
                       preferred_element_type=F32)

    steps = [(hd, c, j) for hd in range(ATTN_HEADS_PER_STEP)
             for c in range(nq) for j in range(c + 1)]

    def attend(shift_free):
        th = wgu_out.shape[2] // 2
        for ch in range(wgu_out.shape[0]):
            cols = slice(ch * th, (ch + 1) * th)
            wgu_out[ch, :, :th] = wg_ref[:, cols].astype(BF16)
            wgu_out[ch, :, th:] = wu_ref[:, cols].astype(BF16)
        wd_out[...] = wd_ref[...].astype(BF16)
        pending = [scores(*st) for st in steps[:SCORE_LOOKAHEAD]]
        m = acc = None
        for n, (hd, c, j) in enumerate(steps):
            s = pending.pop(0)
            if n + SCORE_LOOKAHEAD < len(steps):
                pending.append(scores(*steps[n + SCORE_LOOKAHEAD]))
            if j == c:
                s = jnp.where(causal, s, -jnp.inf)
            vt_ext = jnp.concatenate([vt_ref[0, j, head(hd), :], sum_rows], axis=0)
            if shift_free:
                pv = jnp.dot(vt_ext, jnp.exp2(s).astype(BF16), preferred_element_type=F32)
                acc = pv if j == 0 else acc + pv
            else:
                m_blk = jnp.max(s, axis=0, keepdims=True)
                m_new = m_blk if j == 0 else jnp.maximum(m, m_blk)
                p = jnp.exp2(s - m_new)
                pv = jnp.dot(vt_ext, p.astype(BF16), preferred_element_type=F32)
                acc = pv if j == 0 else jnp.exp2(m - m_new) * acc + pv
                m = m_new
            if j == c:
                o_all = acc[:DIFF_VDIM] / acc[DIFF_VDIM:DIFF_VDIM + 1]
                o = (o_all[:, :tq] - lam * o_all[:, tq:]).T
                y = o * _rms_scale(o, DIFF_VDIM) * gsub_ref[...] * (1.0 - LAMBDA_INIT)
                o_ref[0, c * tq:(c + 1) * tq, head(hd)] = y.astype(o_ref.dtype)

    pl.when(bounded_ref[0] != 0)(functools.partial(attend, True))
    pl.when(bounded_ref[0] == 0)(functools.partial(attend, False))


def _diff_attn(bounded, qt4, kn3, vt4, lq1, lk1, lq2, lk2, g_sub, w_gate, w_up, w_down):
    b, nq, _, tq = qt4.shape
    s = kn3.shape[1]
    nk, tk = vt4.shape[1], vt4.shape[3]
    assert tq == tk and nq == nk
    fixed = lambda bb, h: (0, 0)
    vec = lambda n: pl.BlockSpec((1, n), fixed)
    groups = DIFF_HEADS // ATTN_HEADS_PER_STEP
    width = ATTN_HEADS_PER_STEP * DIFF_VDIM
    steps = b * groups
    nh = FFN_HIDDEN // TH_FFN
    gu_rows, down_rows = D_MODEL // steps, FFN_HIDDEN // steps
    assert gu_rows % V7X_BF16_SUBLANES == 0 and down_rows % V7X_BF16_SUBLANES == 0
    step = lambda bb, g: bb * groups + g
    gu_in = pl.BlockSpec((gu_rows, FFN_HIDDEN), lambda bb, g: (step(bb, g), 0))
    gu_out = pl.BlockSpec((nh, gu_rows, 2 * TH_FFN), lambda bb, g: (0, step(bb, g), 0))
    down_blk = pl.BlockSpec((down_rows, D_MODEL), lambda bb, g: (step(bb, g), 0))
    return pl.pallas_call(
        _diff_attn_kernel,
        grid=(b, groups),
        in_specs=[
            pl.BlockSpec(memory_space=pltpu.SMEM),
            pl.BlockSpec((1, nq, width, tq), lambda bb, g: (bb, 0, g, 0)),
            pl.BlockSpec((1, s, width), lambda bb, g: (bb, 0, g)),
            pl.BlockSpec((1, nk, width, tk), lambda bb, g: (bb, 0, g, 0)),
            vec(DIFF_HALF), vec(DIFF_HALF), vec(DIFF_HALF), vec(DIFF_HALF),
            vec(DIFF_VDIM),
            gu_in, gu_in, down_blk,
        ],
        out_specs=[pl.BlockSpec((1, s, width), lambda bb, g: (bb, 0, g)), gu_out, down_blk],
        out_shape=[jax.ShapeDtypeStruct((b, s, DIFF_WIDTH), BF16),
                   jax.ShapeDtypeStruct((nh, D_MODEL, 2 * TH_FFN), BF16),
                   jax.ShapeDtypeStruct((FFN_HIDDEN, D_MODEL), BF16)],
        compiler_params=pltpu.CompilerParams(
            dimension_semantics=("parallel", "parallel"),
            vmem_limit_bytes=V7X_VMEM_LIMIT_BYTES),
        name="diff_attn",
    )(bounded, qt4, kn3, vt4, lq1, lk1, lq2, lk2, g_sub, w_gate, w_up, w_down)


def _out_proj_kernel(x_ref, yc_ref, yd_ref, qm_ref, km_ref, vm_ref, gq_ref, gmo_ref, gffn_ref,
                     w_hbm, x1_ref, hf_ref, w_ref, w_stage, w_sem):
    @pl.when(pl.program_id(0) == 0)
    def _():
        _load_weight_as_bf16(w_hbm, w_ref, w_stage, w_sem)

    gq = gq_ref[...]
    probs, denoms = [], []
    for hd in range(MEM_HEADS):
        cols = slice(hd * MEM_HEAD_DIM, (hd + 1) * MEM_HEAD_DIM)
        qh = qm_ref[:, cols]
        qh = (qh * _rms_scale(qh, MEM_HEAD_DIM) * gq * (MEM_HEAD_DIM ** -0.5)).astype(BF16)
        s = lax.dot_general(qh, km_ref[0, :, cols], (((1,), (1,)), ((), ())),
                            preferred_element_type=F32)
        e = jnp.exp(s - jnp.max(s, axis=-1, keepdims=True))
        denoms.append(jnp.sum(e, axis=-1, keepdims=True))
        probs.append(e.astype(BF16))

    acc = jnp.dot(yc_ref[...], w_ref[0:CONV_WIDTH, :], preferred_element_type=F32)

    outs = []
    for hd in range(MEM_HEADS):
        cols = slice(hd * MEM_HEAD_DIM, (hd + 1) * MEM_HEAD_DIM)
        oh = jnp.dot(probs[hd], vm_ref[0, :, cols], preferred_element_type=F32)
        outs.append(oh / denoms[hd])
    o = jnp.concatenate(outs, axis=-1)
    ym = (o * _rms_scale(o, MEM_WIDTH) * gmo_ref[...]).astype(BF16)

    acc += jnp.dot(yd_ref[...], w_ref[CONV_WIDTH:CONV_WIDTH + DIFF_WIDTH, :],
                   preferred_element_type=F32)
    acc += jnp.dot(ym, w_ref[CONV_WIDTH + DIFF_WIDTH:, :], preferred_element_type=F32)
    x1 = x_ref[...] + acc
    x1_ref[...] = x1
    hf_ref[...] = (x1 * _rms_scale(x1, D_MODEL) * gffn_ref[...]).astype(BF16)


def _out_proj(x2, yc, yd, qm, km3, vm3, g_mq, g_mem_out, g_ffn, w_o):
    t = x2.shape[0]
    tm = TM_OUT
    tps = SEQ // tm
    row = lambda i: (i, 0)
    fixed = lambda i: (0, 0)
    mem_blk = lambda i: (i // tps, 0, 0)
    return pl.pallas_call(
        _out_proj_kernel,
        grid=(t // tm,),
        in_specs=[
            pl.BlockSpec((tm, D_MODEL), row),
            pl.BlockSpec((tm, CONV_WIDTH), row),
            pl.BlockSpec((tm, DIFF_WIDTH), row),
            pl.BlockSpec((tm, MEM_WIDTH), row),
            pl.BlockSpec((1, MEM_LEN, MEM_WIDTH), mem_blk),
            pl.BlockSpec((1, MEM_LEN, MEM_WIDTH), mem_blk),
            pl.BlockSpec((1, MEM_HEAD_DIM), fixed),
            pl.BlockSpec((1, MEM_WIDTH), fixed),
            pl.BlockSpec((1, D_MODEL), fixed),
            pl.BlockSpec(memory_space=pl.ANY),
        ],
        out_specs=[pl.BlockSpec((tm, D_MODEL), row), pl.BlockSpec((tm, D_MODEL), row)],
        out_shape=[jax.ShapeDtypeStruct((t, D_MODEL), F32),
                   jax.ShapeDtypeStruct((t, D_MODEL), BF16)],
        scratch_shapes=_resident_weight_scratch(D_MODEL, D_MODEL),
        compiler_params=pltpu.CompilerParams(
            dimension_semantics=("arbitrary",),
            vmem_limit_bytes=V7X_VMEM_LIMIT_BYTES),
        name="out_proj",
    )(x2, yc, yd, qm, km3, vm3, g_mq, g_mem_out, g_ffn, w_o)


def _ffn_kernel(hf_ref, x1_ref, wgu_ref, wd_ref, o_ref):
    @pl.when(pl.program_id(1) == 0)
    def _():
        o_ref[...] = x1_ref[...]

    th = wd_ref.shape[0]
    gu = jnp.dot(hf_ref[...], wgu_ref[0], preferred_element_type=F32)
    gate, up = gu[:, :th], gu[:, th:]
    act = (gate * (1.0 / (1.0 + jnp.exp(-gate))) * up).astype(BF16)
    o_ref[...] += jnp.dot(act, wd_ref[...], preferred_element_type=F32)


def _ffn(hf, x1, w_gate_up, w_down):
    t = x1.shape[0]
    tm, th = TM_FFN, TH_FFN
    return pl.pallas_call(
        _ffn_kernel,
        grid=(t // tm, FFN_HIDDEN // th),
        in_specs=[
            pl.BlockSpec((tm, D_MODEL), lambda i, j: (i, 0)),
            pl.BlockSpec((tm, D_MODEL), lambda i, j: (i, 0)),
            pl.BlockSpec((1, D_MODEL, 2 * th), lambda i, j: (j, 0, 0)),
            pl.BlockSpec((th, D_MODEL), lambda i, j: (j, 0)),
        ],
        out_specs=pl.BlockSpec((tm, D_MODEL), lambda i, j: (i, 0)),
        out_shape=jax.ShapeDtypeStruct((t, D_MODEL), F32),
        compiler_params=pltpu.CompilerParams(
            dimension_semantics=("parallel", "arbitrary"),
            vmem_limit_bytes=V7X_VMEM_LIMIT_BYTES),
        name="ffn",
    )(hf, x1, w_gate_up, w_down)


def _rope_lane_freq():
    inv_freq = ROPE_THETA ** (-jnp.arange(0, ROT_DIM, 2, dtype=F32) / ROT_DIM)
    pad = jnp.zeros((DIFF_HALF - ROT_DIM,), F32)
    return jnp.concatenate([inv_freq, inv_freq, pad, inv_freq, inv_freq, pad]).reshape(1, -1)


def kernel(x, mem, positions, g_mix, g_mem, w_in, conv_w, g_conv_out, g_dq, g_dk,
           lam_q1, lam_k1, lam_q2, lam_k2, g_sub, w_mem_kv, g_mq, g_mk, g_mem_out,
           w_o, g_ffn, w_gate, w_up, w_down):
    b, s, d = x.shape
    assert (s, d) == (SEQ, D_MODEL) and mem.shape == (b, MEM_LEN, D_MODEL)
    assert g_mix.shape[0] == 1
    t = b * s
    x2 = x.reshape(t, d)
    mem2 = mem.reshape(b * MEM_LEN, d)
    yc, qt, kn, vt, qm = _in_proj(x2, g_mix, w_in[0], conv_w[0], g_conv_out,
                                  positions.reshape(t, 1), _rope_lane_freq(),
                                  jnp.tile(g_dq, (1, 2)), jnp.tile(g_dk, (1, 2)))
    km, vm = _mem_kv(mem2, g_mem, w_mem_kv[0], g_mk)

    score_bound = (DIFF_HALF ** 0.5 * LOG2_E) * jnp.max(jnp.abs(g_dq)) * jnp.max(jnp.abs(g_dk))
    bounded = (score_bound <= MAX_SHIFT_FREE_LOG2_SCORE).astype(jnp.int32).reshape(1)
    yd, wgu16, wd16 = _diff_attn(bounded, qt, kn.reshape(b, s, DIFF_WIDTH), vt,
                                 lam_q1, lam_k1, lam_q2, lam_k2, g_sub,
                                 w_gate[0], w_up[0], w_down[0])
    x1, hf = _out_proj(x2, yc, yd.reshape(t, DIFF_WIDTH), qm,
                       km.reshape(b, MEM_LEN, MEM_WIDTH), vm.reshape(b, MEM_LEN, MEM_WIDTH),
                       g_mq, g_mem_out, g_ffn, w_o[0])
    out = _ffn(hf, x1, wgu16, wd16)
    return out.reshape(b, s, d)
```

```python
import functools

import jax
import jax.numpy as jnp
import numpy as np
from jax import lax
from jax.experimental import pallas as pl
from jax.experimental.pallas import tpu as pltpu

F32 = jnp.float32
BF16 = jnp.bfloat16

D_MODEL = 2048
SEQ = 2048
MEM_LEN = 256
CONV_WIDTH = 512
CONV_K = 3
DIFF_WIDTH = 1024
DIFF_VDIM = 128
DIFF_HALF = 64
DIFF_HEADS = 8
MEM_WIDTH = 512
MEM_HEADS = 4
MEM_HEAD_DIM = 128
IN_WIDTH = 3 * CONV_WIDTH + 3 * DIFF_WIDTH + MEM_WIDTH
ROT_DIM = 16
ROPE_THETA = 500000.0
FFN_HIDDEN = 5632
EPS = 1e-6
LAMBDA_INIT = 0.8 - 0.6 * float(np.exp(-0.3 * 0))
LOG2_E = float(np.log2(np.e))
MAX_SHIFT_FREE_LOG2_SCORE = 60.0

V7X_LANES = 128
V7X_SUBLANES = 8
V7X_BF16_SUBLANES = 16
V7X_VMEM_LIMIT_BYTES = 60 * 1024 * 1024

TM_PROJ = 256
TM_MEMKV = 256
TM_OUT = 512
TM_FFN = 1024
TH_FFN = 512
WEIGHT_CHUNK_ROWS = 128
SCORE_LOOKAHEAD = 2
ATTN_HEADS_PER_STEP = 2

_OFF_U = 0
_OFF_C = CONV_WIDTH
_OFF_B = 2 * CONV_WIDTH
_OFF_Q = 3 * CONV_WIDTH
_OFF_K = _OFF_Q + DIFF_WIDTH
_OFF_V = _OFF_K + DIFF_WIDTH
_OFF_QM = _OFF_V + DIFF_WIDTH


def _rms_scale(t, width):
    return lax.rsqrt(jnp.sum(t * t, axis=-1, keepdims=True) * (1.0 / width) + EPS)


def _load_weight_as_bf16(w_hbm, w_scr, stage, sem):
    rows = stage.shape[1]
    n_chunks = w_hbm.shape[0] // rows

    def chunk_copy(c):
        return pltpu.make_async_copy(w_hbm.at[pl.ds(c * rows, rows), :], stage.at[c % 2],
                                     sem.at[c % 2])

    chunk_copy(0).start()
    for c in range(n_chunks):
        if c + 1 < n_chunks:
            chunk_copy(c + 1).start()
        chunk_copy(c).wait()
        w_scr[c * rows:(c + 1) * rows, :] = stage[c % 2].astype(BF16)


def _resident_weight_scratch(k, n):
    return [pltpu.VMEM((k, n), BF16), pltpu.VMEM((2, WEIGHT_CHUNK_ROWS, n), F32),
            pltpu.SemaphoreType.DMA((2,))]


def _qk_norm_rope(t, g2, cos_t, sin_a, sin_b):
    lane = lax.broadcasted_iota(jnp.int32, t.shape, 1)
    is_lo = lane < DIFF_HALF
    t2 = t * t
    s_lo = jnp.sum(jnp.where(is_lo, t2, 0.0), axis=-1, keepdims=True)
    s_hi = jnp.sum(jnp.where(is_lo, 0.0, t2), axis=-1, keepdims=True)
    r = jnp.where(is_lo,
                  lax.rsqrt(s_lo * (1.0 / DIFF_HALF) + EPS),
                  lax.rsqrt(s_hi * (1.0 / DIFF_HALF) + EPS))
    tn = t * r * g2
    half = ROT_DIM // 2
    return (tn * cos_t
            + pltpu.roll(tn, V7X_LANES - half, axis=1) * sin_a
            + pltpu.roll(tn, half, axis=1) * sin_b)


def _in_proj_kernel(x_ref, g_ref, w_hbm, convw_ref, gconv_ref, pos_ref, freq_ref,
                    gq_ref, gk_ref, wkv_ref,
                    yconv_ref, qt_ref, kn_ref, vt_ref, qm_ref, wkv_out,
                    z_scr, w_ref, w_stage, w_sem, *, tiles_per_seq):
    i = pl.program_id(0)
    tm = x_ref.shape[0]
    halo = V7X_SUBLANES

    @pl.when(i == 0)
    def _():
        _load_weight_as_bf16(w_hbm, w_ref, w_stage, w_sem)

    @pl.when(i % tiles_per_seq == 0)
    def _():
        z_scr[0:halo, :] = jnp.zeros((halo, CONV_WIDTH), F32)

    wkv_out[...] = wkv_ref[...].astype(BF16)

    x = x_ref[...]
    h = (x * _rms_scale(x, D_MODEL) * g_ref[...]).astype(BF16)

    def proj(off, width):
        return jnp.dot(h, w_ref[:, off:off + width], preferred_element_type=F32)

    z = proj(_OFF_C, CONV_WIDTH) * proj(_OFF_U, CONV_WIDTH)
    z_scr[halo:halo + tm, :] = z
    z1 = z_scr[halo - 1:halo - 1 + tm, :]
    z2 = z_scr[halo - 2:halo - 2 + tm, :]
    cw = convw_ref[...]
    conv = cw[0:1, :] * z2 + cw[1:2, :] * z1 + cw[2:3, :] * z
    y = proj(_OFF_B, CONV_WIDTH) * conv
    yconv_ref[...] = (y * _rms_scale(y, CONV_WIDTH) * gconv_ref[...]).astype(BF16)
    z_scr[0:halo, :] = z_scr[tm:tm + halo, :]

    qf = proj(_OFF_Q, DIFF_WIDTH)
    kf = proj(_OFF_K, DIFF_WIDTH)
    vf = proj(_OFF_V, DIFF_WIDTH)
    ang = pos_ref[...].astype(F32) * freq_ref[...]
    cos_t, sin_t = jnp.cos(ang), jnp.sin(ang)
    pos_in_half = lax.broadcasted_iota(jnp.int32, sin_t.shape, 1) % DIFF_HALF
    sin_a = jnp.where(pos_in_half < ROT_DIM // 2, -sin_t, 0.0)
    sin_b = jnp.where((pos_in_half >= ROT_DIM // 2) & (pos_in_half < ROT_DIM), sin_t, 0.0)
    gq, gk = gq_ref[...], gk_ref[...]
    for hd in range(DIFF_HEADS):
        cols = slice(hd * DIFF_VDIM, (hd + 1) * DIFF_VDIM)
        qn = _qk_norm_rope(qf[:, cols], gq, cos_t, sin_a, sin_b) * (DIFF_HALF ** -0.5 * LOG2_E)
        qt_ref[0, 0, cols, :] = qn.T.astype(BF16)
        kn_ref[:, cols] = _qk_norm_rope(kf[:, cols], gk, cos_t, sin_a, sin_b).astype(BF16)
        vt_ref[0, 0, cols, :] = vf[:, cols].T.astype(BF16)
    qm_ref[...] = proj(_OFF_QM, MEM_WIDTH)


def _in_proj(x2, g_mix, w_in, conv_w, g_conv, pos, lane_freq, gq2, gk2, w_kv):
    t = x2.shape[0]
    tm = TM_PROJ
    tps = SEQ // tm
    nb = t // SEQ
    grid = (t // tm,)
    row = lambda i: (i, 0)
    fixed = lambda i: (0, 0)
    tile_t = lambda i: (i // tps, i % tps, 0, 0)
    kv_rows = w_kv.shape[0] // grid[0]
    assert kv_rows % V7X_BF16_SUBLANES == 0
    kv_blk = pl.BlockSpec((kv_rows, w_kv.shape[1]), row)
    return pl.pallas_call(
        functools.partial(_in_proj_kernel, tiles_per_seq=tps),
        grid=grid,
        in_specs=[
            pl.BlockSpec((tm, D_MODEL), row),
            pl.BlockSpec((1, D_MODEL), fixed),
            pl.BlockSpec(memory_space=pl.ANY),
            pl.BlockSpec((CONV_K, CONV_WIDTH), fixed),
            pl.BlockSpec((1, CONV_WIDTH), fixed),
            pl.BlockSpec((tm, 1), row),
            pl.BlockSpec((1, V7X_LANES), fixed),
            pl.BlockSpec((1, DIFF_VDIM), fixed),
            pl.BlockSpec((1, DIFF_VDIM), fixed),
            kv_blk,
        ],
        out_specs=[
            pl.BlockSpec((tm, CONV_WIDTH), row),
            pl.BlockSpec((1, 1, DIFF_WIDTH, tm), tile_t),
            pl.BlockSpec((tm, DIFF_WIDTH), row),
            pl.BlockSpec((1, 1, DIFF_WIDTH, tm), tile_t),
            pl.BlockSpec((tm, MEM_WIDTH), row),
            kv_blk,
        ],
        out_shape=[
            jax.ShapeDtypeStruct((t, CONV_WIDTH), BF16),
            jax.ShapeDtypeStruct((nb, tps, DIFF_WIDTH, tm), BF16),
            jax.ShapeDtypeStruct((t, DIFF_WIDTH), BF16),
            jax.ShapeDtypeStruct((nb, tps, DIFF_WIDTH, tm), BF16),
            jax.ShapeDtypeStruct((t, MEM_WIDTH), F32),
            jax.ShapeDtypeStruct(w_kv.shape, BF16),
        ],
        scratch_shapes=[pltpu.VMEM((tm + V7X_SUBLANES, CONV_WIDTH), F32)]
        + _resident_weight_scratch(D_MODEL, IN_WIDTH),
        compiler_params=pltpu.CompilerParams(
            dimension_semantics=("arbitrary",),
            vmem_limit_bytes=V7X_VMEM_LIMIT_BYTES),
        name="in_proj",
    )(x2, g_mix, w_in, conv_w, g_conv, pos, lane_freq, gq2, gk2, w_kv)


def _mem_kv_kernel(mem_ref, g_ref, w_ref, gk_ref, k_ref, v_ref):
    x = mem_ref[...]
    h = (x * _rms_scale(x, D_MODEL) * g_ref[...]).astype(BF16)
    kv = jnp.dot(h, w_ref[...], preferred_element_type=F32)
    gk = gk_ref[...]
    for hd in range(MEM_HEADS):
        lo = hd * MEM_HEAD_DIM
        kh = kv[:, lo:lo + MEM_HEAD_DIM]
        k_ref[:, lo:lo + MEM_HEAD_DIM] = (kh * _rms_scale(kh, MEM_HEAD_DIM) * gk).astype(BF16)
    v_ref[...] = kv[:, MEM_WIDTH:].astype(BF16)


def _mem_kv(mem2, g_mem, w_kv, g_mk):
    t = mem2.shape[0]
    tm = TM_MEMKV
    row = lambda i: (i, 0)
    fixed = lambda i: (0, 0)
    return pl.pallas_call(
        _mem_kv_kernel,
        grid=(t // tm,),
        in_specs=[
            pl.BlockSpec((tm, D_MODEL), row),
            pl.BlockSpec((1, D_MODEL), fixed),
            pl.BlockSpec((D_MODEL, 2 * MEM_WIDTH), fixed),
            pl.BlockSpec((1, MEM_HEAD_DIM), fixed),
        ],
        out_specs=[pl.BlockSpec((tm, MEM_WIDTH), row), pl.BlockSpec((tm, MEM_WIDTH), row)],
        out_shape=[jax.ShapeDtypeStruct((t, MEM_WIDTH), BF16),
                   jax.ShapeDtypeStruct((t, MEM_WIDTH), BF16)],
        compiler_params=pltpu.CompilerParams(
            dimension_semantics=("parallel",),
            vmem_limit_bytes=V7X_VMEM_LIMIT_BYTES),
        name="mem_kv",
    )(mem2, g_mem, w_kv, g_mk)


def _diff_attn_kernel(bounded_ref, qt_ref, kn_ref, vt_ref, lq1_ref, lk1_ref, lq2_ref, lk2_ref,
                      gsub_ref, wg_ref, wu_ref, wd_ref, wo_ref, o_ref, wgu_out, wd_out, wo_out):
    nq, tq = qt_ref.shape[1], qt_ref.shape[3]
    tk = vt_ref.shape[3]

    lam = (jnp.exp(jnp.sum(lq1_ref[...] * lk1_ref[...], axis=-1, keepdims=True))
           - jnp.exp(jnp.sum(lq2_ref[...] * lk2_ref[...], axis=-1, keepdims=True))
           + LAMBDA_INIT)
    key = lax.broadcasted_iota(jnp.int32, (tk, tq), 0)
    qry = lax.broadcasted_iota(jnp.int32, (tk, tq), 1)
    causal = jnp.concatenate([key <= qry, key <= qry], axis=1)
    sum_rows = (lax.broadcasted_iota(jnp.int32, (V7X_BF16_SUBLANES, tk), 0) == 0).astype(BF16)

    def head(hd):
        return slice(hd * DIFF_VDIM, (hd + 1) * DIFF_VDIM)

    def scores(hd, c, j):
        qt = qt_ref[0, c, head(hd), :]
        is_lo = lax.broadcasted_iota(jnp.int32, qt.shape, 0) < DIFF_HALF
        zero = jnp.zeros_like(qt)
        qq = jnp.concatenate([jnp.where(is_lo, qt, zero), jnp.where(is_lo, zero, qt)], axis=1)
        return jnp.dot(kn_ref[0, j * tk:(j + 1) * tk, head(hd)], qq,
                       preferred_element_type=F32)

    steps = [(hd, c, j) for hd in range(ATTN_HEADS_PER_STEP)
             for c in range(nq) for j in range(c + 1)]

    def attend(shift_free):
        th = wgu_out.shape[2] // 2
        for ch in range(wgu_out.shape[0]):
            cols = slice(ch * th, (ch + 1) * th)
            wgu_out[ch, :, :th] = wg_ref[:, cols].astype(BF16)
            wgu_out[ch, :, th:] = wu_ref[:, cols].astype(BF16)
        wd_out[...] = wd_ref[...].astype(BF16)
        wo_out[...] = wo_ref[...].astype(BF16)
        pending = [scores(*st) for st in steps[:SCORE_LOOKAHEAD]]
        m = acc = None
        for n, (hd, c, j) in enumerate(steps):
            s = pending.pop(0)
            if n + SCORE_LOOKAHEAD < len(steps):
                pending.append(scores(*steps[n + SCORE_LOOKAHEAD]))
            if j == c:
                s = jnp.where(causal, s, -jnp.inf)
            vt_ext = jnp.concatenate([vt_ref[0, j, head(hd), :], sum_rows], axis=0)
            if shift_free:
                pv = jnp.dot(vt_ext, jnp.exp2(s).astype(BF16), preferred_element_type=F32)
                acc = pv if j == 0 else acc + pv
            else:
                m_blk = jnp.max(s, axis=0, keepdims=True)
                m_new = m_blk if j == 0 else jnp.maximum(m, m_blk)
                p = jnp.exp2(s - m_new)
                pv = jnp.dot(vt_ext, p.astype(BF16), preferred_element_type=F32)
                acc = pv if j == 0 else jnp.exp2(m - m_new) * acc + pv
                m = m_new
            if j == c:
                o_all = acc[:DIFF_VDIM] / acc[DIFF_VDIM:DIFF_VDIM + 1]
                o = (o_all[:, :tq] - lam * o_all[:, tq:]).T
                y = o * _rms_scale(o, DIFF_VDIM) * gsub_ref[...] * (1.0 - LAMBDA_INIT)
                o_ref[0, c * tq:(c + 1) * tq, head(hd)] = y.astype(o_ref.dtype)

    pl.when(bounded_ref[0] != 0)(functools.partial(attend, True))
    pl.when(bounded_ref[0] == 0)(functools.partial(attend, False))


def _diff_attn(bounded, qt4, kn3, vt4, lq1, lk1, lq2, lk2, g_sub, w_gate, w_up, w_down, w_o):
    b, nq, _, tq = qt4.shape
    s = kn3.shape[1]
    nk, tk = vt4.shape[1], vt4.shape[3]
    assert tq == tk and nq == nk
    fixed = lambda bb, h: (0, 0)
    vec = lambda n: pl.BlockSpec((1, n), fixed)
    groups = DIFF_HEADS // ATTN_HEADS_PER_STEP
    width = ATTN_HEADS_PER_STEP * DIFF_VDIM
    steps = b * groups
    nh = FFN_HIDDEN // TH_FFN
    gu_rows, down_rows, wo_rows = D_MODEL // steps, FFN_HIDDEN // steps, w_o.shape[0] // steps
    assert all(r % V7X_BF16_SUBLANES == 0 for r in (gu_rows, down_rows, wo_rows))
    step = lambda bb, g: bb * groups + g
    gu_in = pl.BlockSpec((gu_rows, FFN_HIDDEN), lambda bb, g: (step(bb, g), 0))
    gu_out = pl.BlockSpec((nh, gu_rows, 2 * TH_FFN), lambda bb, g: (0, step(bb, g), 0))
    down_blk = pl.BlockSpec((down_rows, D_MODEL), lambda bb, g: (step(bb, g), 0))
    wo_blk = pl.BlockSpec((wo_rows, w_o.shape[1]), lambda bb, g: (step(bb, g), 0))
    return pl.pallas_call(
        _diff_attn_kernel,
        grid=(b, groups),
        in_specs=[
            pl.BlockSpec(memory_space=pltpu.SMEM),
            pl.BlockSpec((1, nq, width, tq), lambda bb, g: (bb, 0, g, 0)),
            pl.BlockSpec((1, s, width), lambda bb, g: (bb, 0, g)),
            pl.BlockSpec((1, nk, width, tk), lambda bb, g: (bb, 0, g, 0)),
            vec(DIFF_HALF), vec(DIFF_HALF), vec(DIFF_HALF), vec(DIFF_HALF),
            vec(DIFF_VDIM),
            gu_in, gu_in, down_blk, wo_blk,
        ],
        out_specs=[pl.BlockSpec((1, s, width), lambda bb, g: (bb, 0, g)), gu_out, down_blk,
                   wo_blk],
        out_shape=[jax.ShapeDtypeStruct((b, s, DIFF_WIDTH), BF16),
                   jax.ShapeDtypeStruct((nh, D_MODEL, 2 * TH_FFN), BF16),
                   jax.ShapeDtypeStruct((FFN_HIDDEN, D_MODEL), BF16),
                   jax.ShapeDtypeStruct(w_o.shape, BF16)],
        compiler_params=pltpu.CompilerParams(
            dimension_semantics=("parallel", "parallel"),
            vmem_limit_bytes=V7X_VMEM_LIMIT_BYTES),
        name="diff_attn",
    )(bounded, qt4, kn3, vt4, lq1, lk1, lq2, lk2, g_sub, w_gate, w_up, w_down, w_o)


def _out_proj_kernel(x_ref, yc_ref, yd_ref, qm_ref, km_ref, vm_ref, gq_ref, gmo_ref, gffn_ref,
                     w_ref, x1_ref, hf_ref):
    gq = gq_ref[...]
    probs, denoms = [], []
    for hd in range(MEM_HEADS):
        cols = slice(hd * MEM_HEAD_DIM, (hd + 1) * MEM_HEAD_DIM)
        qh = qm_ref[:, cols]
        qh = (qh * _rms_scale(qh, MEM_HEAD_DIM) * gq * (MEM_HEAD_DIM ** -0.5)).astype(BF16)
        s = lax.dot_general(qh, km_ref[0, :, cols], (((1,), (1,)), ((), ())),
                            preferred_element_type=F32)
        e = jnp.exp(s - jnp.max(s, axis=-1, keepdims=True))
        denoms.append(jnp.sum(e, axis=-1, keepdims=True))
        probs.append(e.astype(BF16))

    acc = jnp.dot(yc_ref[...], w_ref[0:CONV_WIDTH, :], preferred_element_type=F32)

    outs = []
    for hd in range(MEM_HEADS):
        cols = slice(hd * MEM_HEAD_DIM, (hd + 1) * MEM_HEAD_DIM)
        oh = jnp.dot(probs[hd], vm_ref[0, :, cols], preferred_element_type=F32)
        outs.append(oh / denoms[hd])
    o = jnp.concatenate(outs, axis=-1)
    ym = (o * _rms_scale(o, MEM_WIDTH) * gmo_ref[...]).astype(BF16)

    acc += jnp.dot(yd_ref[...], w_ref[CONV_WIDTH:CONV_WIDTH + DIFF_WIDTH, :],
                   preferred_element_type=F32)
    acc += jnp.dot(ym, w_ref[CONV_WIDTH + DIFF_WIDTH:, :], preferred_element_type=F32)
    x1 = x_ref[...] + acc
    x1_ref[...] = x1
    hf_ref[...] = (x1 * _rms_scale(x1, D_MODEL) * gffn_ref[...]).astype(BF16)


def _out_proj(x2, yc, yd, qm, km3, vm3, g_mq, g_mem_out, g_ffn, w_o):
    t = x2.shape[0]
    tm = TM_OUT
    tps = SEQ // tm
    row = lambda i: (i, 0)
    fixed = lambda i: (0, 0)
    mem_blk = lambda i: (i // tps, 0, 0)
    return pl.pallas_call(
        _out_proj_kernel,
        grid=(t // tm,),
        in_specs=[
            pl.BlockSpec((tm, D_MODEL), row),
            pl.BlockSpec((tm, CONV_WIDTH), row),
            pl.BlockSpec((tm, DIFF_WIDTH), row),
            pl.BlockSpec((tm, MEM_WIDTH), row),
            pl.BlockSpec((1, MEM_LEN, MEM_WIDTH), mem_blk),
            pl.BlockSpec((1, MEM_LEN, MEM_WIDTH), mem_blk),
            pl.BlockSpec((1, MEM_HEAD_DIM), fixed),
            pl.BlockSpec((1, MEM_WIDTH), fixed),
            pl.BlockSpec((1, D_MODEL), fixed),
            pl.BlockSpec((D_MODEL, D_MODEL), fixed, pipeline_mode=pl.Buffered(1)),
        ],
        out_specs=[pl.BlockSpec((tm, D_MODEL), row), pl.BlockSpec((tm, D_MODEL), row)],
        out_shape=[jax.ShapeDtypeStruct((t, D_MODEL), F32),
                   jax.ShapeDtypeStruct((t, D_MODEL), BF16)],
        compiler_params=pltpu.CompilerParams(
            dimension_semantics=("parallel",),
            vmem_limit_bytes=V7X_VMEM_LIMIT_BYTES),
        name="out_proj",
    )(x2, yc, yd, qm, km3, vm3, g_mq, g_mem_out, g_ffn, w_o)


def _ffn_kernel(hf_ref, x1_ref, wgu_ref, wd_ref, o_ref):
    @pl.when(pl.program_id(1) == 0)
    def _():
        o_ref[...] = x1_ref[...]

    th = wd_ref.shape[0]
    gu = jnp.dot(hf_ref[...], wgu_ref[0], preferred_element_type=F32)
    gate, up = gu[:, :th], gu[:, th:]
    act = (gate * (1.0 / (1.0 + jnp.exp(-gate))) * up).astype(BF16)
    o_ref[...] += jnp.dot(act, wd_ref[...], preferred_element_type=F32)


def _ffn(hf, x1, w_gate_up, w_down):
    t = x1.shape[0]
    tm, th = TM_FFN, TH_FFN
    return pl.pallas_call(
        _ffn_kernel,
        grid=(t // tm, FFN_HIDDEN // th),
        in_specs=[
            pl.BlockSpec((tm, D_MODEL), lambda i, j: (i, 0)),
            pl.BlockSpec((tm, D_MODEL), lambda i, j: (i, 0)),
            pl.BlockSpec((1, D_MODEL, 2 * th), lambda i, j: (j, 0, 0)),
            pl.BlockSpec((th, D_MODEL), lambda i, j: (j, 0)),
        ],
        out_specs=pl.BlockSpec((tm, D_MODEL), lambda i, j: (i, 0)),
        out_shape=jax.ShapeDtypeStruct((t, D_MODEL), F32),
        compiler_params=pltpu.CompilerParams(
            dimension_semantics=("parallel", "arbitrary"),
            vmem_limit_bytes=V7X_VMEM_LIMIT_BYTES),
        name="ffn",
    )(hf, x1, w_gate_up, w_down)


def _rope_lane_freq():
    inv_freq = ROPE_THETA ** (-jnp.arange(0, ROT_DIM, 2, dtype=F32) / ROT_DIM)
    pad = jnp.zeros((DIFF_HALF - ROT_DIM,), F32)
    return jnp.concatenate([inv_freq, inv_freq, pad, inv_freq, inv_freq, pad]).reshape(1, -1)


def kernel(x, mem, positions, g_mix, g_mem, w_in, conv_w, g_conv_out, g_dq, g_dk,
           lam_q1, lam_k1, lam_q2, lam_k2, g_sub, w_mem_kv, g_mq, g_mk, g_mem_out,
           w_o, g_ffn, w_gate, w_up, w_down):
    b, s, d = x.shape
    assert (s, d) == (SEQ, D_MODEL) and mem.shape == (b, MEM_LEN, D_MODEL)
    assert g_mix.shape[0] == 1
    t = b * s
    x2 = x.reshape(t, d)
    mem2 = mem.reshape(b * MEM_LEN, d)
    yc, qt, kn, vt, qm, wkv16 = _in_proj(x2, g_mix, w_in[0], conv_w[0], g_conv_out,
                                  positions.reshape(t, 1), _rope_lane_freq(),
                                  jnp.tile(g_dq, (1, 2)), jnp.tile(g_dk, (1, 2)), w_mem_kv[0])
    km, vm = _mem_kv(mem2, g_mem, wkv16, g_mk)

    score_bound = (DIFF_HALF ** 0.5 * LOG2_E) * jnp.max(jnp.abs(g_dq)) * jnp.max(jnp.abs(g_dk))
    bounded = (score_bound <= MAX_SHIFT_FREE_LOG2_SCORE).astype(jnp.int32).reshape(1)
    yd, wgu16, wd16, wo16 = _diff_attn(bounded, qt, kn.reshape(b, s, DIFF_WIDTH), vt,
                                       lam_q1, lam_k1, lam_q2, lam_k2, g_sub,
                                       w_gate[0], w_up[0], w_down[0], w_o[0])
    x1, hf = _out_proj(x2, yc, yd.reshape(t, DIFF_WIDTH), qm,
                       km.reshape(b, MEM_LEN, MEM_WIDTH), vm.reshape(b, MEM_LEN, MEM_WIDTH),
                       g_mq, g_mem_out, g_ffn, wo16)
    out = _ffn(hf, x1, wgu16, wd16)
    return out.reshape(b, s, d)
```

```python
import functools

import jax
import jax.numpy as jnp
import numpy as np
from jax import lax
from jax.experimental import pallas as pl
from jax.experimental.pallas import tpu as pltpu

F32 = jnp.float32
BF16 = jnp.bfloat16

D_MODEL = 2048
SEQ = 2048
MEM_LEN = 256
CONV_WIDTH = 512
CONV_K = 3
DIFF_WIDTH = 1024
DIFF_VDIM = 128
DIFF_HALF = 64
DIFF_HEADS = 8
MEM_WIDTH = 512
MEM_HEADS = 4
MEM_HEAD_DIM = 128
IN_WIDTH = 3 * CONV_WIDTH + 3 * DIFF_WIDTH + MEM_WIDTH
ROT_DIM = 16
ROPE_THETA = 500000.0
FFN_HIDDEN = 5632
EPS = 1e-6
LAMBDA_INIT = 0.8 - 0.6 * float(np.exp(-0.3 * 0))
LOG2_E = float(np.log2(np.e))
MAX_SHIFT_FREE_LOG2_SCORE = 60.0

V7X_LANES = 128
V7X_SUBLANES = 8
V7X_BF16_SUBLANES = 16
V7X_VMEM_LIMIT_BYTES = 60 * 1024 * 1024

TM_PROJ = 256
TM_MEMKV = 256
TM_OUT = 512
TM_FFN = 1024
TH_FFN = 512
WEIGHT_CHUNK_ROWS = 128
SCORE_LOOKAHEAD = 3
ATTN_HEADS_PER_STEP = 2

_OFF_U = 0
_OFF_C = CONV_WIDTH
_OFF_B = 2 * CONV_WIDTH
_OFF_Q = 3 * CONV_WIDTH
_OFF_K = _OFF_Q + DIFF_WIDTH
_OFF_V = _OFF_K + DIFF_WIDTH
_OFF_QM = _OFF_V + DIFF_WIDTH


def _rms_scale(t, width):
    return lax.rsqrt(jnp.sum(t * t, axis=-1, keepdims=True) * (1.0 / width) + EPS)


def _load_weight_as_bf16(w_hbm, w_scr, stage, sem):
    rows = stage.shape[1]
    n_chunks = w_hbm.shape[0] // rows

    def chunk_copy(c):
        return pltpu.make_async_copy(w_hbm.at[pl.ds(c * rows, rows), :], stage.at[c % 2],
                                     sem.at[c % 2])

    chunk_copy(0).start()
    for c in range(n_chunks):
        if c + 1 < n_chunks:
            chunk_copy(c + 1).start()
        chunk_copy(c).wait()
        w_scr[c * rows:(c + 1) * rows, :] = stage[c % 2].astype(BF16)


def _resident_weight_scratch(k, n):
    return [pltpu.VMEM((k, n), BF16), pltpu.VMEM((2, WEIGHT_CHUNK_ROWS, n), F32),
            pltpu.SemaphoreType.DMA((2,))]


def _qk_norm_rope(t, g2, cos_t, sin_a, sin_b):
    lane = lax.broadcasted_iota(jnp.int32, t.shape, 1)
    is_lo = lane < DIFF_HALF
    t2 = t * t
    s_lo = jnp.sum(jnp.where(is_lo, t2, 0.0), axis=-1, keepdims=True)
    s_hi = jnp.sum(jnp.where(is_lo, 0.0, t2), axis=-1, keepdims=True)
    r = jnp.where(is_lo,
                  lax.rsqrt(s_lo * (1.0 / DIFF_HALF) + EPS),
                  lax.rsqrt(s_hi * (1.0 / DIFF_HALF) + EPS))
    tn = t * r * g2
    half = ROT_DIM // 2
    return (tn * cos_t
            + pltpu.roll(tn, V7X_LANES - half, axis=1) * sin_a
            + pltpu.roll(tn, half, axis=1) * sin_b)


def _in_proj_kernel(x_ref, g_ref, w_hbm, convw_ref, gconv_ref, pos_ref, freq_ref,
                    gq_ref, gk_ref, wkv_ref,
                    yconv_ref, qt_ref, kn_ref, vt_ref, qm_ref, wkv_out,
                    z_scr, w_ref, w_stage, w_sem, *, tiles_per_seq):
    i = pl.program_id(0)
    tm = x_ref.shape[0]
    halo = V7X_SUBLANES

    @pl.when(i == 0)
    def _():
        _load_weight_as_bf16(w_hbm, w_ref, w_stage, w_sem)

    @pl.when(i % tiles_per_seq == 0)
    def _():
        z_scr[0:halo, :] = jnp.zeros((halo, CONV_WIDTH), F32)

    wkv_out[...] = wkv_ref[...].astype(BF16)

    x = x_ref[...]
    h = (x * _rms_scale(x, D_MODEL) * g_ref[...]).astype(BF16)

    def proj(off, width):
        return jnp.dot(h, w_ref[:, off:off + width], preferred_element_type=F32)

    z = proj(_OFF_C, CONV_WIDTH) * proj(_OFF_U, CONV_WIDTH)
    z_scr[halo:halo + tm, :] = z
    z1 = z_scr[halo - 1:halo - 1 + tm, :]
    z2 = z_scr[halo - 2:halo - 2 + tm, :]
    cw = convw_ref[...]
    conv = cw[0:1, :] * z2 + cw[1:2, :] * z1 + cw[2:3, :] * z
    y = proj(_OFF_B, CONV_WIDTH) * conv
    yconv_ref[...] = (y * _rms_scale(y, CONV_WIDTH) * gconv_ref[...]).astype(BF16)
    z_scr[0:halo, :] = z_scr[tm:tm + halo, :]

    qf = proj(_OFF_Q, DIFF_WIDTH)
    kf = proj(_OFF_K, DIFF_WIDTH)
    vf = proj(_OFF_V, DIFF_WIDTH)
    ang = pos_ref[...].astype(F32) * freq_ref[...]
    cos_t, sin_t = jnp.cos(ang), jnp.sin(ang)
    pos_in_half = lax.broadcasted_iota(jnp.int32, sin_t.shape, 1) % DIFF_HALF
    sin_a = jnp.where(pos_in_half < ROT_DIM // 2, -sin_t, 0.0)
    sin_b = jnp.where((pos_in_half >= ROT_DIM // 2) & (pos_in_half < ROT_DIM), sin_t, 0.0)
    gq, gk = gq_ref[...], gk_ref[...]
    for hd in range(DIFF_HEADS):
        cols = slice(hd * DIFF_VDIM, (hd + 1) * DIFF_VDIM)
        qn = _qk_norm_rope(qf[:, cols], gq, cos_t, sin_a, sin_b) * (DIFF_HALF ** -0.5 * LOG2_E)
        qt_ref[0, 0, cols, :] = qn.T.astype(BF16)
        kn_ref[:, cols] = _qk_norm_rope(kf[:, cols], gk, cos_t, sin_a, sin_b).astype(BF16)
        vt_ref[0, 0, cols, :] = vf[:, cols].T.astype(BF16)
    qm_ref[...] = proj(_OFF_QM, MEM_WIDTH)


def _in_proj(x2, g_mix, w_in, conv_w, g_conv, pos, lane_freq, gq2, gk2, w_kv):
    t = x2.shape[0]
    tm = TM_PROJ
    tps = SEQ // tm
    nb = t // SEQ
    grid = (t // tm,)
    row = lambda i: (i, 0)
    fixed = lambda i: (0, 0)
    tile_t = lambda i: (i // tps, i % tps, 0, 0)
    kv_rows = w_kv.shape[0] // grid[0]
    assert kv_rows % V7X_BF16_SUBLANES == 0
    kv_blk = pl.BlockSpec((kv_rows, w_kv.shape[1]), row)
    return pl.pallas_call(
        functools.partial(_in_proj_kernel, tiles_per_seq=tps),
        grid=grid,
        in_specs=[
            pl.BlockSpec((tm, D_MODEL), row),
            pl.BlockSpec((1, D_MODEL), fixed),
            pl.BlockSpec(memory_space=pl.ANY),
            pl.BlockSpec((CONV_K, CONV_WIDTH), fixed),
            pl.BlockSpec((1, CONV_WIDTH), fixed),
            pl.BlockSpec((tm, 1), row),
            pl.BlockSpec((1, V7X_LANES), fixed),
            pl.BlockSpec((1, DIFF_VDIM), fixed),
            pl.BlockSpec((1, DIFF_VDIM), fixed),
            kv_blk,
        ],
        out_specs=[
            pl.BlockSpec((tm, CONV_WIDTH), row),
            pl.BlockSpec((1, 1, DIFF_WIDTH, tm), tile_t),
            pl.BlockSpec((tm, DIFF_WIDTH), row),
            pl.BlockSpec((1, 1, DIFF_WIDTH, tm), tile_t),
            pl.BlockSpec((tm, MEM_WIDTH), row),
            kv_blk,
        ],
        out_shape=[
            jax.ShapeDtypeStruct((t, CONV_WIDTH), BF16),
            jax.ShapeDtypeStruct((nb, tps, DIFF_WIDTH, tm), BF16),
            jax.ShapeDtypeStruct((t, DIFF_WIDTH), BF16),
            jax.ShapeDtypeStruct((nb, tps, DIFF_WIDTH, tm), BF16),
            jax.ShapeDtypeStruct((t, MEM_WIDTH), F32),
            jax.ShapeDtypeStruct(w_kv.shape, BF16),
        ],
        scratch_shapes=[pltpu.VMEM((tm + V7X_SUBLANES, CONV_WIDTH), F32)]
        + _resident_weight_scratch(D_MODEL, IN_WIDTH),
        compiler_params=pltpu.CompilerParams(
            dimension_semantics=("arbitrary",),
            vmem_limit_bytes=V7X_VMEM_LIMIT_BYTES),
        name="in_proj",
    )(x2, g_mix, w_in, conv_w, g_conv, pos, lane_freq, gq2, gk2, w_kv)


def _mem_kv_kernel(mem_ref, g_ref, w_ref, gk_ref, k_ref, v_ref):
    x = mem_ref[...]
    h = (x * _rms_scale(x, D_MODEL) * g_ref[...]).astype(BF16)
    kv = jnp.dot(h, w_ref[...], preferred_element_type=F32)
    gk = gk_ref[...]
    for hd in range(MEM_HEADS):
        lo = hd * MEM_HEAD_DIM
        kh = kv[:, lo:lo + MEM_HEAD_DIM]
        k_ref[:, lo:lo + MEM_HEAD_DIM] = (kh * _rms_scale(kh, MEM_HEAD_DIM) * gk).astype(BF16)
    v_ref[...] = kv[:, MEM_WIDTH:].astype(BF16)


def _mem_kv(mem2, g_mem, w_kv, g_mk):
    t = mem2.shape[0]
    tm = TM_MEMKV
    row = lambda i: (i, 0)
    fixed = lambda i: (0, 0)
    return pl.pallas_call(
        _mem_kv_kernel,
        grid=(t // tm,),
        in_specs=[
            pl.BlockSpec((tm, D_MODEL), row),
            pl.BlockSpec((1, D_MODEL), fixed),
            pl.BlockSpec((D_MODEL, 2 * MEM_WIDTH), fixed),
            pl.BlockSpec((1, MEM_HEAD_DIM), fixed),
        ],
        out_specs=[pl.BlockSpec((tm, MEM_WIDTH), row), pl.BlockSpec((tm, MEM_WIDTH), row)],
        out_shape=[jax.ShapeDtypeStruct((t, MEM_WIDTH), BF16),
                   jax.ShapeDtypeStruct((t, MEM_WIDTH), BF16)],
        compiler_params=pltpu.CompilerParams(
            dimension_semantics=("parallel",),
            vmem_limit_bytes=V7X_VMEM_LIMIT_BYTES),
        name="mem_kv",
    )(mem2, g_mem, w_kv, g_mk)


def _diff_attn_kernel(bounded_ref, qt_ref, kn_ref, vt_ref, lq1_ref, lk1_ref, lq2_ref, lk2_ref,
                      gsub_ref, wg_ref, wu_ref, wd_ref, wo_ref, o_ref, wgu_out, wd_out, wo_out):
    nq, tq = qt_ref.shape[1], qt_ref.shape[3]
    tk = vt_ref.shape[3]

    lam = (jnp.exp(jnp.sum(lq1_ref[...] * lk1_ref[...], axis=-1, keepdims=True))
           - jnp.exp(jnp.sum(lq2_ref[...] * lk2_ref[...], axis=-1, keepdims=True))
           + LAMBDA_INIT)
    key = lax.broadcasted_iota(jnp.int32, (tk, tq), 0)
    qry = lax.broadcasted_iota(jnp.int32, (tk, tq), 1)
    causal = jnp.concatenate([key <= qry, key <= qry], axis=1)
    sum_rows = (lax.broadcasted_iota(jnp.int32, (V7X_BF16_SUBLANES, tk), 0) == 0).astype(BF16)

    def head(hd):
        return slice(hd * DIFF_VDIM, (hd + 1) * DIFF_VDIM)

    def scores(hd, c, j):
        qt = qt_ref[0, c, head(hd), :]
        is_lo = lax.broadcasted_iota(jnp.int32, qt.shape, 0) < DIFF_HALF
        zero = jnp.zeros_like(qt)
        qq = jnp.concatenate([jnp.where(is_lo, qt, zero), jnp.where(is_lo, zero, qt)], axis=1)
        return jnp.dot(kn_ref[0, j * tk:(j + 1) * tk, head(hd)], qq,
                       preferred_element_type=F32)

    steps = [(hd, c, j) for hd in range(ATTN_HEADS_PER_STEP)
             for c in range(nq) for j in range(c + 1)]

    def attend(shift_free):
        th = wgu_out.shape[2] // 2
        for ch in range(wgu_out.shape[0]):
            cols = slice(ch * th, (ch + 1) * th)
            wgu_out[ch, :, :th] = wg_ref[:, cols].astype(BF16)
            wgu_out[ch, :, th:] = wu_ref[:, cols].astype(BF16)
        wd_out[...] = wd_ref[...].astype(BF16)
        wo_out[...] = wo_ref[...].astype(BF16)
        pending = [scores(*st) for st in steps[:SCORE_LOOKAHEAD]]
        m = acc = None
        for n, (hd, c, j) in enumerate(steps):
            s = pending.pop(0)
            if n + SCORE_LOOKAHEAD < len(steps):
                pending.append(scores(*steps[n + SCORE_LOOKAHEAD]))
            if j == c:
                s = jnp.where(causal, s, -jnp.inf)
            vt_ext = jnp.concatenate([vt_ref[0, j, head(hd), :], sum_rows], axis=0)
            if shift_free:
                pv = jnp.dot(vt_ext, jnp.exp2(s).astype(BF16), preferred_element_type=F32)
                acc = pv if j == 0 else acc + pv
            else:
                m_blk = jnp.max(s, axis=0, keepdims=True)
                m_new = m_blk if j == 0 else jnp.maximum(m, m_blk)
                p = jnp.exp2(s - m_new)
                pv = jnp.dot(vt_ext, p.astype(BF16), preferred_element_type=F32)
                acc = pv if j == 0 else jnp.exp2(m - m_new) * acc + pv
                m = m_new
            if j == c:
                o_all = acc[:DIFF_VDIM] / acc[DIFF_VDIM:DIFF_VDIM + 1]
                o = (o_all[:, :tq] - lam * o_all[:, tq:]).T
                y = o * _rms_scale(o, DIFF_VDIM) * gsub_ref[...] * (1.0 - LAMBDA_INIT)
                o_ref[0, c * tq:(c + 1) * tq, head(hd)] = y.astype(o_ref.dtype)

    pl.when(bounded_ref[0] != 0)(functools.partial(attend, True))
    pl.when(bounded_ref[0] == 0)(functools.partial(attend, False))


def _diff_attn(bounded, qt4, kn3, vt4, lq1, lk1, lq2, lk2, g_sub, w_gate, w_up, w_down, w_o):
    b, nq, _, tq = qt4.shape
    s = kn3.shape[1]
    nk, tk = vt4.shape[1], vt4.shape[3]
    assert tq == tk and nq == nk
    fixed = lambda bb, h: (0, 0)
    vec = lambda n: pl.BlockSpec((1, n), fixed)
    groups = DIFF_HEADS // ATTN_HEADS_PER_STEP
    width = ATTN_HEADS_PER_STEP * DIFF_VDIM
    steps = b * groups
    nh = FFN_HIDDEN // TH_FFN
    gu_rows, down_rows, wo_rows = D_MODEL // steps, FFN_HIDDEN // steps, w_o.shape[0] // steps
    assert all(r % V7X_BF16_SUBLANES == 0 for r in (gu_rows, down_rows, wo_rows))
    step = lambda bb, g: bb * groups + g
    gu_in = pl.BlockSpec((gu_rows, FFN_HIDDEN), lambda bb, g: (step(bb, g), 0))
    gu_out = pl.BlockSpec((nh, gu_rows, 2 * TH_FFN), lambda bb, g: (0, step(bb, g), 0))
    down_blk = pl.BlockSpec((down_rows, D_MODEL), lambda bb, g: (step(bb, g), 0))
    wo_blk = pl.BlockSpec((wo_rows, w_o.shape[1]), lambda bb, g: (step(bb, g), 0))
    return pl.pallas_call(
        _diff_attn_kernel,
        grid=(b, groups),
        in_specs=[
            pl.BlockSpec(memory_space=pltpu.SMEM),
            pl.BlockSpec((1, nq, width, tq), lambda bb, g: (bb, 0, g, 0)),
            pl.BlockSpec((1, s, width), lambda bb, g: (bb, 0, g)),
            pl.BlockSpec((1, nk, width, tk), lambda bb, g: (bb, 0, g, 0)),
            vec(DIFF_HALF), vec(DIFF_HALF), vec(DIFF_HALF), vec(DIFF_HALF),
            vec(DIFF_VDIM),
            gu_in, gu_in, down_blk, wo_blk,
        ],
        out_specs=[pl.BlockSpec((1, s, width), lambda bb, g: (bb, 0, g)), gu_out, down_blk,
                   wo_blk],
        out_shape=[jax.ShapeDtypeStruct((b, s, DIFF_WIDTH), BF16),
                   jax.ShapeDtypeStruct((nh, D_MODEL, 2 * TH_FFN), BF16),
                   jax.ShapeDtypeStruct((FFN_HIDDEN, D_MODEL), BF16),
                   jax.ShapeDtypeStruct(w_o.shape, BF16)],
        compiler_params=pltpu.CompilerParams(
            dimension_semantics=("parallel", "parallel"),
            vmem_limit_bytes=V7X_VMEM_LIMIT_BYTES),
        name="diff_attn",
    )(bounded, qt4, kn3, vt4, lq1, lk1, lq2, lk2, g_sub, w_gate, w_up, w_down, w_o)


def _out_proj_kernel(x_ref, yc_ref, yd_ref, qm_ref, km_ref, vm_ref, gq_ref, gmo_ref, gffn_ref,
                     w_ref, x1_ref, hf_ref):
    gq = gq_ref[...]
    probs, denoms = [], []
    for hd in range(MEM_HEADS):
        cols = slice(hd * MEM_HEAD_DIM, (hd + 1) * MEM_HEAD_DIM)
        qh = qm_ref[:, cols]
        qh = (qh * _rms_scale(qh, MEM_HEAD_DIM) * gq * (MEM_HEAD_DIM ** -0.5)).astype(BF16)
        s = lax.dot_general(qh, km_ref[0, :, cols], (((1,), (1,)), ((), ())),
                            preferred_element_type=F32)
        e = jnp.exp(s - jnp.max(s, axis=-1, keepdims=True))
        denoms.append(jnp.sum(e, axis=-1, keepdims=True))
        probs.append(e.astype(BF16))

    acc = jnp.dot(yc_ref[...], w_ref[0:CONV_WIDTH, :], preferred_element_type=F32)

    outs = []
    for hd in range(MEM_HEADS):
        cols = slice(hd * MEM_HEAD_DIM, (hd + 1) * MEM_HEAD_DIM)
        oh = jnp.dot(probs[hd], vm_ref[0, :, cols], preferred_element_type=F32)
        outs.append(oh / denoms[hd])
    o = jnp.concatenate(outs, axis=-1)
    ym = (o * _rms_scale(o, MEM_WIDTH) * gmo_ref[...]).astype(BF16)

    acc += jnp.dot(yd_ref[...], w_ref[CONV_WIDTH:CONV_WIDTH + DIFF_WIDTH, :],
                   preferred_element_type=F32)
    acc += jnp.dot(ym, w_ref[CONV_WIDTH + DIFF_WIDTH:, :], preferred_element_type=F32)
    x1 = x_ref[...] + acc
    x1_ref[...] = x1
    hf_ref[...] = (x1 * _rms_scale(x1, D_MODEL) * gffn_ref[...]).astype(BF16)


def _out_proj(x2, yc, yd, qm, km3, vm3, g_mq, g_mem_out, g_ffn, w_o):
    t = x2.shape[0]
    tm = TM_OUT
    tps = SEQ // tm
    row = lambda i: (i, 0)
    fixed = lambda i: (0, 0)
    mem_blk = lambda i: (i // tps, 0, 0)
    return pl.pallas_call(
        _out_proj_kernel,
        grid=(t // tm,),
        in_specs=[
            pl.BlockSpec((tm, D_MODEL), row),
            pl.BlockSpec((tm, CONV_WIDTH), row),
            pl.BlockSpec((tm, DIFF_WIDTH), row),
            pl.BlockSpec((tm, MEM_WIDTH), row),
            pl.BlockSpec((1, MEM_LEN, MEM_WIDTH), mem_blk),
            pl.BlockSpec((1, MEM_LEN, MEM_WIDTH), mem_blk),
            pl.BlockSpec((1, MEM_HEAD_DIM), fixed),
            pl.BlockSpec((1, MEM_WIDTH), fixed),
            pl.BlockSpec((1, D_MODEL), fixed),
            pl.BlockSpec((D_MODEL, D_MODEL), fixed, pipeline_mode=pl.Buffered(1)),
        ],
        out_specs=[pl.BlockSpec((tm, D_MODEL), row), pl.BlockSpec((tm, D_MODEL), row)],
        out_shape=[jax.ShapeDtypeStruct((t, D_MODEL), F32),
                   jax.ShapeDtypeStruct((t, D_MODEL), BF16)],
        compiler_params=pltpu.CompilerParams(
            dimension_semantics=("parallel",),
            vmem_limit_bytes=V7X_VMEM_LIMIT_BYTES),
        name="out_proj",
    )(x2, yc, yd, qm, km3, vm3, g_mq, g_mem_out, g_ffn, w_o)


def _ffn_kernel(hf_ref, x1_ref, wgu_ref, wd_ref, o_ref):
    @pl.when(pl.program_id(1) == 0)
    def _():
        o_ref[...] = x1_ref[...]

    th = wd_ref.shape[0]
    gu = jnp.dot(hf_ref[...], wgu_ref[0], preferred_element_type=F32)
    gate, up = gu[:, :th], gu[:, th:]
    act = (gate * (1.0 / (1.0 + jnp.exp(-gate))) * up).astype(BF16)
    o_ref[...] += jnp.dot(act, wd_ref[...], preferred_element_type=F32)


def _ffn(hf, x1, w_gate_up, w_down):
    t = x1.shape[0]
    tm, th = TM_FFN, TH_FFN
    return pl.pallas_call(
        _ffn_kernel,
        grid=(t // tm, FFN_HIDDEN // th),
        in_specs=[
            pl.BlockSpec((tm, D_MODEL), lambda i, j: (i, 0)),
            pl.BlockSpec((tm, D_MODEL), lambda i, j: (i, 0)),
            pl.BlockSpec((1, D_MODEL, 2 * th), lambda i, j: (j, 0, 0)),
            pl.BlockSpec((th, D_MODEL), lambda i, j: (j, 0)),
        ],
        out_specs=pl.BlockSpec((tm, D_MODEL), lambda i, j: (i, 0)),
        out_shape=jax.ShapeDtypeStruct((t, D_MODEL), F32),
        compiler_params=pltpu.CompilerParams(
            dimension_semantics=("parallel", "arbitrary"),
            vmem_limit_bytes=V7X_VMEM_LIMIT_BYTES),
        name="ffn",
    )(hf, x1, w_gate_up, w_down)


def _rope_lane_freq():
    inv_freq = ROPE_THETA ** (-jnp.arange(0, ROT_DIM, 2, dtype=F32) / ROT_DIM)
    pad = jnp.zeros((DIFF_HALF - ROT_DIM,), F32)
    return jnp.concatenate([inv_freq, inv_freq, pad, inv_freq, inv_freq, pad]).reshape(1, -1)


def kernel(x, mem, positions, g_mix, g_mem, w_in, conv_w, g_conv_out, g_dq, g_dk,
           lam_q1, lam_k1, lam_q2, lam_k2, g_sub, w_mem_kv, g_mq, g_mk, g_mem_out,
           w_o, g_ffn, w_gate, w_up, w_down):
    b, s, d = x.shape
    assert (s, d) == (SEQ, D_MODEL) and mem.shape == (b, MEM_LEN, D_MODEL)
    assert g_mix.shape[0] == 1
    t = b * s
    x2 = x.reshape(t, d)
    mem2 = mem.reshape(b * MEM_LEN, d)
    yc, qt, kn, vt, qm, wkv16 = _in_proj(x2, g_mix, w_in[0], conv_w[0], g_conv_out,
                                  positions.reshape(t, 1), _rope_lane_freq(),
                                  jnp.tile(g_dq, (1, 2)), jnp.tile(g_dk, (1, 2)), w_mem_kv[0])
    km, vm = _mem_kv(mem2, g_mem, wkv16, g_mk)

    score_bound = (DIFF_HALF ** 0.5 * LOG2_E) * jnp.max(jnp.abs(g_dq)) * jnp.max(jnp.abs(g_dk))
    bounded = (score_bound <= MAX_SHIFT_FREE_LOG2_SCORE).astype(jnp.int32).reshape(1)
    yd, wgu16, wd16, wo16 = _diff_attn(bounded, qt, kn.reshape(b, s, DIFF_WIDTH), vt,
                                       lam_q1, lam_k1, lam_q2, lam_k2, g_sub,
                                       w_gate[0], w_up[0], w_down[0], w_o[0])
    x1, hf = _out_proj(x2, yc, yd.reshape(t, DIFF_WIDTH), qm,
                       km.reshape(b, MEM_LEN, MEM_WIDTH), vm.reshape(b, MEM_LEN, MEM_WIDTH),
                       g_mq, g_mem_out, g_ffn, wo16)
    out = _ffn(hf, x1, wgu16, wd16)
    return out.reshape(b, s, d)
```

```python
import functools

import jax
import jax.numpy as jnp
import numpy as np
from jax import lax
from jax.experimental import pallas as pl
from jax.experimental.pallas import tpu as pltpu

F32 = jnp.float32
BF16 = jnp.bfloat16

D_MODEL = 2048
SEQ = 2048
MEM_LEN = 256
CONV_WIDTH = 512
CONV_K = 3
DIFF_WIDTH = 1024
DIFF_VDIM = 128
DIFF_HALF = 64
DIFF_HEADS = 8
MEM_WIDTH = 512
MEM_HEADS = 4
MEM_HEAD_DIM = 128
IN_WIDTH = 3 * CONV_WIDTH + 3 * DIFF_WIDTH + MEM_WIDTH
ROT_DIM = 16
ROPE_THETA = 500000.0
FFN_HIDDEN = 5632
EPS = 1e-6
LAMBDA_INIT = 0.8 - 0.6 * float(np.exp(-0.3 * 0))
LOG2_E = float(np.log2(np.e))
MAX_SHIFT_FREE_LOG2_SCORE = 60.0

V7X_LANES = 128
V7X_SUBLANES = 8
V7X_BF16_SUBLANES = 16
V7X_VMEM_LIMIT_BYTES = 60 * 1024 * 1024

TM_PROJ = 256
TM_MEMKV = 256
TM_OUT = 512
TM_FFN = 1024
TH_FFN = 512
WEIGHT_CHUNK_ROWS = 128
SCORE_LOOKAHEAD = 2
ATTN_HEADS_PER_STEP = 2

_OFF_U = 0
_OFF_C = CONV_WIDTH
_OFF_B = 2 * CONV_WIDTH
_OFF_Q = 3 * CONV_WIDTH
_OFF_K = _OFF_Q + DIFF_WIDTH
_OFF_V = _OFF_K + DIFF_WIDTH
_OFF_QM = _OFF_V + DIFF_WIDTH


def _rms_scale(t, width):
    return lax.rsqrt(jnp.sum(t * t, axis=-1, keepdims=True) * (1.0 / width) + EPS)


def _load_weight_as_bf16(w_hbm, w_scr, stage, sem):
    rows = stage.shape[1]
    n_chunks = w_hbm.shape[0] // rows

    def chunk_copy(c):
        return pltpu.make_async_copy(w_hbm.at[pl.ds(c * rows, rows), :], stage.at[c % 2],
                                     sem.at[c % 2])

    chunk_copy(0).start()
    for c in range(n_chunks):
        if c + 1 < n_chunks:
            chunk_copy(c + 1).start()
        chunk_copy(c).wait()
        w_scr[c * rows:(c + 1) * rows, :] = stage[c % 2].astype(BF16)


def _resident_weight_scratch(k, n):
    return [pltpu.VMEM((k, n), BF16), pltpu.VMEM((2, WEIGHT_CHUNK_ROWS, n), F32),
            pltpu.SemaphoreType.DMA((2,))]


def _qk_norm_rope(t, g2, cos_t, sin_a, sin_b):
    lane = lax.broadcasted_iota(jnp.int32, t.shape, 1)
    is_lo = lane < DIFF_HALF
    t2 = t * t
    s_lo = jnp.sum(jnp.where(is_lo, t2, 0.0), axis=-1, keepdims=True)
    s_hi = jnp.sum(jnp.where(is_lo, 0.0, t2), axis=-1, keepdims=True)
    r = jnp.where(is_lo,
                  lax.rsqrt(s_lo * (1.0 / DIFF_HALF) + EPS),
                  lax.rsqrt(s_hi * (1.0 / DIFF_HALF) + EPS))
    tn = t * r * g2
    half = ROT_DIM // 2
    return (tn * cos_t
            + pltpu.roll(tn, V7X_LANES - half, axis=1) * sin_a
            + pltpu.roll(tn, half, axis=1) * sin_b)


def _in_proj_kernel(x_ref, g_ref, w_hbm, convw_ref, gconv_ref, pos_ref, freq_ref,
                    gq_ref, gk_ref, wkv_ref,
                    yconv_ref, qt_ref, kn_ref, vt_ref, qm_ref, wkv_out,
                    z_scr, w_ref, w_stage, w_sem, *, tiles_per_seq):
    i = pl.program_id(0)
    tm = x_ref.shape[0]
    halo = V7X_SUBLANES

    @pl.when(i == 0)
    def _():
        _load_weight_as_bf16(w_hbm, w_ref, w_stage, w_sem)

    @pl.when(i % tiles_per_seq == 0)
    def _():
        z_scr[0:halo, :] = jnp.zeros((halo, CONV_WIDTH), F32)

    wkv_out[...] = wkv_ref[...].astype(BF16)

    x = x_ref[...]
    h = (x * _rms_scale(x, D_MODEL) * g_ref[...]).astype(BF16)

    def proj(off, width):
        return jnp.dot(h, w_ref[:, off:off + width], preferred_element_type=F32)

    z = proj(_OFF_C, CONV_WIDTH) * proj(_OFF_U, CONV_WIDTH)
    z_scr[halo:halo + tm, :] = z
    z1 = z_scr[halo - 1:halo - 1 + tm, :]
    z2 = z_scr[halo - 2:halo - 2 + tm, :]
    cw = convw_ref[...]
    conv = cw[0:1, :] * z2 + cw[1:2, :] * z1 + cw[2:3, :] * z
    y = proj(_OFF_B, CONV_WIDTH) * conv
    yconv_ref[...] = (y * _rms_scale(y, CONV_WIDTH) * gconv_ref[...]).astype(BF16)
    z_scr[0:halo, :] = z_scr[tm:tm + halo, :]

    qf = proj(_OFF_Q, DIFF_WIDTH)
    kf = proj(_OFF_K, DIFF_WIDTH)
    vf = proj(_OFF_V, DIFF_WIDTH)
    pos = jnp.broadcast_to(pos_ref[0].astype(F32), (V7X_LANES, tm)).T
    ang = pos * freq_ref[...]
    cos_t, sin_t = jnp.cos(ang), jnp.sin(ang)
    pos_in_half = lax.broadcasted_iota(jnp.int32, sin_t.shape, 1) % DIFF_HALF
    sin_a = jnp.where(pos_in_half < ROT_DIM // 2, -sin_t, 0.0)
    sin_b = jnp.where((pos_in_half >= ROT_DIM // 2) & (pos_in_half < ROT_DIM), sin_t, 0.0)
    gq, gk = gq_ref[...], gk_ref[...]
    for hd in range(DIFF_HEADS):
        cols = slice(hd * DIFF_VDIM, (hd + 1) * DIFF_VDIM)
        qn = _qk_norm_rope(qf[:, cols], gq, cos_t, sin_a, sin_b) * (DIFF_HALF ** -0.5 * LOG2_E)
        qt_ref[0, 0, cols, :] = qn.T.astype(BF16)
        kn_ref[:, cols] = _qk_norm_rope(kf[:, cols], gk, cos_t, sin_a, sin_b).astype(BF16)
        vt_ref[0, 0, cols, :] = vf[:, cols].T.astype(BF16)
    qm_ref[...] = proj(_OFF_QM, MEM_WIDTH)


def _in_proj(x2, g_mix, w_in, conv_w, g_conv, pos, lane_freq, gq2, gk2, w_kv):
    t = x2.shape[0]
    tm = TM_PROJ
    tps = SEQ // tm
    nb = t // SEQ
    grid = (t // tm,)
    row = lambda i: (i, 0)
    fixed = lambda i: (0, 0)
    tile_t = lambda i: (i // tps, i % tps, 0, 0)
    kv_rows = w_kv.shape[0] // grid[0]
    assert kv_rows % V7X_BF16_SUBLANES == 0
    kv_blk = pl.BlockSpec((kv_rows, w_kv.shape[1]), row)
    return pl.pallas_call(
        functools.partial(_in_proj_kernel, tiles_per_seq=tps),
        grid=grid,
        in_specs=[
            pl.BlockSpec((tm, D_MODEL), row),
            pl.BlockSpec((1, D_MODEL), fixed),
            pl.BlockSpec(memory_space=pl.ANY),
            pl.BlockSpec((CONV_K, CONV_WIDTH), fixed),
            pl.BlockSpec((1, CONV_WIDTH), fixed),
            pl.BlockSpec((1, 1, tm), lambda i: (i, 0, 0)),
            pl.BlockSpec((1, V7X_LANES), fixed),
            pl.BlockSpec((1, DIFF_VDIM), fixed),
            pl.BlockSpec((1, DIFF_VDIM), fixed),
            kv_blk,
        ],
        out_specs=[
            pl.BlockSpec((tm, CONV_WIDTH), row),
            pl.BlockSpec((1, 1, DIFF_WIDTH, tm), tile_t),
            pl.BlockSpec((tm, DIFF_WIDTH), row),
            pl.BlockSpec((1, 1, DIFF_WIDTH, tm), tile_t),
            pl.BlockSpec((tm, MEM_WIDTH), row),
            kv_blk,
        ],
        out_shape=[
            jax.ShapeDtypeStruct((t, CONV_WIDTH), BF16),
            jax.ShapeDtypeStruct((nb, tps, DIFF_WIDTH, tm), BF16),
            jax.ShapeDtypeStruct((t, DIFF_WIDTH), BF16),
            jax.ShapeDtypeStruct((nb, tps, DIFF_WIDTH, tm), BF16),
            jax.ShapeDtypeStruct((t, MEM_WIDTH), F32),
            jax.ShapeDtypeStruct(w_kv.shape, BF16),
        ],
        scratch_shapes=[pltpu.VMEM((tm + V7X_SUBLANES, CONV_WIDTH), F32)]
        + _resident_weight_scratch(D_MODEL, IN_WIDTH),
        compiler_params=pltpu.CompilerParams(
            dimension_semantics=("arbitrary",),
            vmem_limit_bytes=V7X_VMEM_LIMIT_BYTES),
        name="in_proj",
    )(x2, g_mix, w_in, conv_w, g_conv, pos, lane_freq, gq2, gk2, w_kv)


def _mem_kv_kernel(mem_ref, g_ref, w_ref, gk_ref, k_ref, v_ref):
    x = mem_ref[...]
    h = (x * _rms_scale(x, D_MODEL) * g_ref[...]).astype(BF16)
    kv = jnp.dot(h, w_ref[...], preferred_element_type=F32)
    gk = gk_ref[...]
    for hd in range(MEM_HEADS):
        lo = hd * MEM_HEAD_DIM
        kh = kv[:, lo:lo + MEM_HEAD_DIM]
        k_ref[:, lo:lo + MEM_HEAD_DIM] = (kh * _rms_scale(kh, MEM_HEAD_DIM) * gk).astype(BF16)
    v_ref[...] = kv[:, MEM_WIDTH:].astype(BF16)


def _mem_kv(mem2, g_mem, w_kv, g_mk):
    t = mem2.shape[0]
    tm = TM_MEMKV
    row = lambda i: (i, 0)
    fixed = lambda i: (0, 0)
    return pl.pallas_call(
        _mem_kv_kernel,
        grid=(t // tm,),
        in_specs=[
            pl.BlockSpec((tm, D_MODEL), row),
            pl.BlockSpec((1, D_MODEL), fixed),
            pl.BlockSpec((D_MODEL, 2 * MEM_WIDTH), fixed),
            pl.BlockSpec((1, MEM_HEAD_DIM), fixed),
        ],
        out_specs=[pl.BlockSpec((tm, MEM_WIDTH), row), pl.BlockSpec((tm, MEM_WIDTH), row)],
        out_shape=[jax.ShapeDtypeStruct((t, MEM_WIDTH), BF16),
                   jax.ShapeDtypeStruct((t, MEM_WIDTH), BF16)],
        compiler_params=pltpu.CompilerParams(
            dimension_semantics=("parallel",),
            vmem_limit_bytes=V7X_VMEM_LIMIT_BYTES),
        name="mem_kv",
    )(mem2, g_mem, w_kv, g_mk)


def _diff_attn_kernel(bounded_ref, qt_ref, kn_ref, vt_ref, lq1_ref, lk1_ref, lq2_ref, lk2_ref,
                      gsub_ref, wg_ref, wu_ref, wd_ref, wo_ref, o_ref, wgu_out, wd_out, wo_out):
    nq, tq = qt_ref.shape[1], qt_ref.shape[3]
    tk = vt_ref.shape[3]

    lam = (jnp.exp(jnp.sum(lq1_ref[...] * lk1_ref[...], axis=-1, keepdims=True))
           - jnp.exp(jnp.sum(lq2_ref[...] * lk2_ref[...], axis=-1, keepdims=True))
           + LAMBDA_INIT)
    key = lax.broadcasted_iota(jnp.int32, (tk, tq), 0)
    qry = lax.broadcasted_iota(jnp.int32, (tk, tq), 1)
    causal = jnp.concatenate([key <= qry, key <= qry], axis=1)
    sum_rows = (lax.broadcasted_iota(jnp.int32, (V7X_BF16_SUBLANES, tk), 0) == 0).astype(BF16)

    def head(hd):
        return slice(hd * DIFF_VDIM, (hd + 1) * DIFF_VDIM)

    def scores(hd, c, j):
        qt = qt_ref[0, c, head(hd), :]
        is_lo = lax.broadcasted_iota(jnp.int32, qt.shape, 0) < DIFF_HALF
        zero = jnp.zeros_like(qt)
        qq = jnp.concatenate([jnp.where(is_lo, qt, zero), jnp.where(is_lo, zero, qt)], axis=1)
        return jnp.dot(kn_ref[0, j * tk:(j + 1) * tk, head(hd)], qq,
                       preferred_element_type=F32)

    steps = [(hd, c, j) for hd in range(ATTN_HEADS_PER_STEP)
             for c in range(nq) for j in range(c + 1)]

    def attend(shift_free):
        th = wgu_out.shape[2] // 2
        for ch in range(wgu_out.shape[0]):
            cols = slice(ch * th, (ch + 1) * th)
            wgu_out[ch, :, :th] = wg_ref[:, cols].astype(BF16)
            wgu_out[ch, :, th:] = wu_ref[:, cols].astype(BF16)
        wd_out[...] = wd_ref[...].astype(BF16)
        wo_out[...] = wo_ref[...].astype(BF16)
        pending = [scores(*st) for st in steps[:SCORE_LOOKAHEAD]]
        m = acc = None
        for n, (hd, c, j) in enumerate(steps):
            s = pending.pop(0)
            if n + SCORE_LOOKAHEAD < len(steps):
                pending.append(scores(*steps[n + SCORE_LOOKAHEAD]))
            if j == c:
                s = jnp.where(causal, s, -jnp.inf)
            vt_ext = jnp.concatenate([vt_ref[0, j, head(hd), :], sum_rows], axis=0)
            if shift_free:
                pv = jnp.dot(vt_ext, jnp.exp2(s).astype(BF16), preferred_element_type=F32)
                acc = pv if j == 0 else acc + pv
            else:
                m_blk = jnp.max(s, axis=0, keepdims=True)
                m_new = m_blk if j == 0 else jnp.maximum(m, m_blk)
                p = jnp.exp2(s - m_new)
                pv = jnp.dot(vt_ext, p.astype(BF16), preferred_element_type=F32)
                acc = pv if j == 0 else jnp.exp2(m - m_new) * acc + pv
                m = m_new
            if j == c:
                o_all = acc[:DIFF_VDIM] / acc[DIFF_VDIM:DIFF_VDIM + 1]
                o = (o_all[:, :tq] - lam * o_all[:, tq:]).T
                y = o * _rms_scale(o, DIFF_VDIM) * gsub_ref[...] * (1.0 - LAMBDA_INIT)
                o_ref[0, c * tq:(c + 1) * tq, head(hd)] = y.astype(o_ref.dtype)

    pl.when(bounded_ref[0] != 0)(functools.partial(attend, True))
    pl.when(bounded_ref[0] == 0)(functools.partial(attend, False))


def _diff_attn(bounded, qt4, kn3, vt4, lq1, lk1, lq2, lk2, g_sub, w_gate, w_up, w_down, w_o):
    b, nq, _, tq = qt4.shape
    s = kn3.shape[1]
    nk, tk = vt4.shape[1], vt4.shape[3]
    assert tq == tk and nq == nk
    fixed = lambda bb, h: (0, 0)
    vec = lambda n: pl.BlockSpec((1, n), fixed)
    groups = DIFF_HEADS // ATTN_HEADS_PER_STEP
    width = ATTN_HEADS_PER_STEP * DIFF_VDIM
    steps = b * groups
    nh = FFN_HIDDEN // TH_FFN
    gu_rows, down_rows, wo_rows = D_MODEL // steps, FFN_HIDDEN // steps, w_o.shape[0] // steps
    assert all(r % V7X_BF16_SUBLANES == 0 for r in (gu_rows, down_rows, wo_rows))
    step = lambda bb, g: bb * groups + g
    gu_in = pl.BlockSpec((gu_rows, FFN_HIDDEN), lambda bb, g: (step(bb, g), 0))
    gu_out = pl.BlockSpec((nh, gu_rows, 2 * TH_FFN), lambda bb, g: (0, step(bb, g), 0))
    down_blk = pl.BlockSpec((down_rows, D_MODEL), lambda bb, g: (step(bb, g), 0))
    wo_blk = pl.BlockSpec((wo_rows, w_o.shape[1]), lambda bb, g: (step(bb, g), 0))
    return pl.pallas_call(
        _diff_attn_kernel,
        grid=(b, groups),
        in_specs=[
            pl.BlockSpec(memory_space=pltpu.SMEM),
            pl.BlockSpec((1, nq, width, tq), lambda bb, g: (bb, 0, g, 0)),
            pl.BlockSpec((1, s, width), lambda bb, g: (bb, 0, g)),
            pl.BlockSpec((1, nk, width, tk), lambda bb, g: (bb, 0, g, 0)),
            vec(DIFF_HALF), vec(DIFF_HALF), vec(DIFF_HALF), vec(DIFF_HALF),
            vec(DIFF_VDIM),
            gu_in, gu_in, down_blk, wo_blk,
        ],
        out_specs=[pl.BlockSpec((1, s, width), lambda bb, g: (bb, 0, g)), gu_out, down_blk,
                   wo_blk],
        out_shape=[jax.ShapeDtypeStruct((b, s, DIFF_WIDTH), BF16),
                   jax.ShapeDtypeStruct((nh, D_MODEL, 2 * TH_FFN), BF16),
                   jax.ShapeDtypeStruct((FFN_HIDDEN, D_MODEL), BF16),
                   jax.ShapeDtypeStruct(w_o.shape, BF16)],
        compiler_params=pltpu.CompilerParams(
            dimension_semantics=("parallel", "parallel"),
            vmem_limit_bytes=V7X_VMEM_LIMIT_BYTES),
        name="diff_attn",
    )(bounded, qt4, kn3, vt4, lq1, lk1, lq2, lk2, g_sub, w_gate, w_up, w_down, w_o)


def _out_proj_kernel(x_ref, yc_ref, yd_ref, qm_ref, km_ref, vm_ref, gq_ref, gmo_ref, gffn_ref,
                     w_ref, x1_ref, hf_ref):
    gq = gq_ref[...]
    probs, denoms = [], []
    for hd in range(MEM_HEADS):
        cols = slice(hd * MEM_HEAD_DIM, (hd + 1) * MEM_HEAD_DIM)
        qh = qm_ref[:, cols]
        qh = (qh * _rms_scale(qh, MEM_HEAD_DIM) * gq * (MEM_HEAD_DIM ** -0.5)).astype(BF16)
        s = lax.dot_general(qh, km_ref[0, :, cols], (((1,), (1,)), ((), ())),
                            preferred_element_type=F32)
        e = jnp.exp(s - jnp.max(s, axis=-1, keepdims=True))
        denoms.append(jnp.sum(e, axis=-1, keepdims=True))
        probs.append(e.astype(BF16))

    acc = jnp.dot(yc_ref[...], w_ref[0:CONV_WIDTH, :], preferred_element_type=F32)

    outs = []
    for hd in range(MEM_HEADS):
        cols = slice(hd * MEM_HEAD_DIM, (hd + 1) * MEM_HEAD_DIM)
        oh = jnp.dot(probs[hd], vm_ref[0, :, cols], preferred_element_type=F32)
        outs.append(oh / denoms[hd])
    o = jnp.concatenate(outs, axis=-1)
    ym = (o * _rms_scale(o, MEM_WIDTH) * gmo_ref[...]).astype(BF16)

    acc += jnp.dot(yd_ref[...], w_ref[CONV_WIDTH:CONV_WIDTH + DIFF_WIDTH, :],
                   preferred_element_type=F32)
    acc += jnp.dot(ym, w_ref[CONV_WIDTH + DIFF_WIDTH:, :], preferred_element_type=F32)
    x1 = x_ref[...] + acc
    x1_ref[...] = x1
    hf_ref[...] = (x1 * _rms_scale(x1, D_MODEL) * gffn_ref[...]).astype(BF16)


def _out_proj(x2, yc, yd, qm, km3, vm3, g_mq, g_mem_out, g_ffn, w_o):
    t = x2.shape[0]
    tm = TM_OUT
    tps = SEQ // tm
    row = lambda i: (i, 0)
    fixed = lambda i: (0, 0)
    mem_blk = lambda i: (i // tps, 0, 0)
    return pl.pallas_call(
        _out_proj_kernel,
        grid=(t // tm,),
        in_specs=[
            pl.BlockSpec((tm, D_MODEL), row),
            pl.BlockSpec((tm, CONV_WIDTH), row),
            pl.BlockSpec((tm, DIFF_WIDTH), row),
            pl.BlockSpec((tm, MEM_WIDTH), row),
            pl.BlockSpec((1, MEM_LEN, MEM_WIDTH), mem_blk),
            pl.BlockSpec((1, MEM_LEN, MEM_WIDTH), mem_blk),
            pl.BlockSpec((1, MEM_HEAD_DIM), fixed),
            pl.BlockSpec((1, MEM_WIDTH), fixed),
            pl.BlockSpec((1, D_MODEL), fixed),
            pl.BlockSpec((D_MODEL, D_MODEL), fixed, pipeline_mode=pl.Buffered(1)),
        ],
        out_specs=[pl.BlockSpec((tm, D_MODEL), row), pl.BlockSpec((tm, D_MODEL), row)],
        out_shape=[jax.ShapeDtypeStruct((t, D_MODEL), F32),
                   jax.ShapeDtypeStruct((t, D_MODEL), BF16)],
        compiler_params=pltpu.CompilerParams(
            dimension_semantics=("parallel",),
            vmem_limit_bytes=V7X_VMEM_LIMIT_BYTES),
        name="out_proj",
    )(x2, yc, yd, qm, km3, vm3, g_mq, g_mem_out, g_ffn, w_o)


def _ffn_kernel(hf_ref, x1_ref, wgu_ref, wd_ref, o_ref):
    @pl.when(pl.program_id(1) == 0)
    def _():
        o_ref[...] = x1_ref[...]

    th = wd_ref.shape[0]
    gu = jnp.dot(hf_ref[...], wgu_ref[0], preferred_element_type=F32)
    gate, up = gu[:, :th], gu[:, th:]
    act = (gate * (1.0 / (1.0 + jnp.exp(-gate))) * up).astype(BF16)
    o_ref[...] += jnp.dot(act, wd_ref[...], preferred_element_type=F32)


def _ffn(hf, x1, w_gate_up, w_down):
    t = x1.shape[0]
    tm, th = TM_FFN, TH_FFN
    return pl.pallas_call(
        _ffn_kernel,
        grid=(t // tm, FFN_HIDDEN // th),
        in_specs=[
            pl.BlockSpec((tm, D_MODEL), lambda i, j: (i, 0)),
            pl.BlockSpec((tm, D_MODEL), lambda i, j: (i, 0)),
            pl.BlockSpec((1, D_MODEL, 2 * th), lambda i, j: (j, 0, 0)),
            pl.BlockSpec((th, D_MODEL), lambda i, j: (j, 0)),
        ],
        out_specs=pl.BlockSpec((tm, D_MODEL), lambda i, j: (i, 0)),
        out_shape=jax.ShapeDtypeStruct((t, D_MODEL), F32),
        compiler_params=pltpu.CompilerParams(
            dimension_semantics=("parallel", "arbitrary"),
            vmem_limit_bytes=V7X_VMEM_LIMIT_BYTES),
        name="ffn",
    )(hf, x1, w_gate_up, w_down)


def _rope_lane_freq():
    inv_freq = ROPE_THETA ** (-jnp.arange(0, ROT_DIM, 2, dtype=F32) / ROT_DIM)
    pad = jnp.zeros((DIFF_HALF - ROT_DIM,), F32)
    return jnp.concatenate([inv_freq, inv_freq, pad, inv_freq, inv_freq, pad]).reshape(1, -1)


def kernel(x, mem, positions, g_mix, g_mem, w_in, conv_w, g_conv_out, g_dq, g_dk,
           lam_q1, lam_k1, lam_q2, lam_k2, g_sub, w_mem_kv, g_mq, g_mk, g_mem_out,
           w_o, g_ffn, w_gate, w_up, w_down):
    b, s, d = x.shape
    assert (s, d) == (SEQ, D_MODEL) and mem.shape == (b, MEM_LEN, D_MODEL)
    assert g_mix.shape[0] == 1
    t = b * s
    x2 = x.reshape(t, d)
    mem2 = mem.reshape(b * MEM_LEN, d)
    yc, qt, kn, vt, qm, wkv16 = _in_proj(x2, g_mix, w_in[0], conv_w[0], g_conv_out,
                                  positions.reshape(t // TM_PROJ, 1, TM_PROJ), _rope_lane_freq(),
                                  jnp.tile(g_dq, (1, 2)), jnp.tile(g_dk, (1, 2)), w_mem_kv[0])
    km, vm = _mem_kv(mem2, g_mem, wkv16, g_mk)

    score_bound = (DIFF_HALF ** 0.5 * LOG2_E) * jnp.max(jnp.abs(g_dq)) * jnp.max(jnp.abs(g_dk))
    bounded = (score_bound <= MAX_SHIFT_FREE_LOG2_SCORE).astype(jnp.int32).reshape(1)
    yd, wgu16, wd16, wo16 = _diff_attn(bounded, qt, kn.reshape(b, s, DIFF_WIDTH), vt,
                                       lam_q1, lam_k1, lam_q2, lam_k2, g_sub,
                                       w_gate[0], w_up[0], w_down[0], w_o[0])
    x1, hf = _out_proj(x2, yc, yd.reshape(t, DIFF_WIDTH), qm,
                       km.reshape(b, MEM_LEN, MEM_WIDTH), vm.reshape(b, MEM_LEN, MEM_WIDTH),
                       g_mq, g_mem_out, g_ffn, wo16)
    out = _ffn(hf, x1, wgu16, wd16)
    return out.reshape(b, s, d)
```

```python
import functools

import jax
import jax.numpy as jnp
import numpy as np
from jax import lax
from jax.experimental import pallas as pl
from jax.experimental.pallas import tpu as pltpu

F32 = jnp.float32
BF16 = jnp.bfloat16

D_MODEL = 2048
SEQ = 2048
MEM_LEN = 256
CONV_WIDTH = 512
CONV_K = 3
DIFF_WIDTH = 1024
DIFF_VDIM = 128
DIFF_HALF = 64
DIFF_HEADS = 8
MEM_WIDTH = 512
MEM_HEADS = 4
MEM_HEAD_DIM = 128
IN_WIDTH = 3 * CONV_WIDTH + 3 * DIFF_WIDTH + MEM_WIDTH
ROT_DIM = 16
ROPE_THETA = 500000.0
FFN_HIDDEN = 5632
EPS = 1e-6
LAMBDA_INIT = 0.8 - 0.6 * float(np.exp(-0.3 * 0))
LOG2_E = float(np.log2(np.e))
MAX_SHIFT_FREE_LOG2_SCORE = 60.0

V7X_LANES = 128
V7X_SUBLANES = 8
V7X_BF16_SUBLANES = 16
V7X_VMEM_LIMIT_BYTES = 60 * 1024 * 1024

TM_PROJ = 256
TM_MEMKV = 256
TM_OUT = 512
TM_FFN = 1024
TH_FFN = 512
WEIGHT_CHUNK_ROWS = 128
SCORE_LOOKAHEAD = 2
ATTN_HEADS_PER_STEP = 2

_OFF_U = 0
_OFF_C = CONV_WIDTH
_OFF_B = 2 * CONV_WIDTH
_OFF_Q = 3 * CONV_WIDTH
_OFF_K = _OFF_Q + DIFF_WIDTH
_OFF_V = _OFF_K + DIFF_WIDTH
_OFF_QM = _OFF_V + DIFF_WIDTH


def _rms_scale(t, width):
    return lax.rsqrt(jnp.sum(t * t, axis=-1, keepdims=True) * (1.0 / width) + EPS)


def _load_weight_as_bf16(w_hbm, w_scr, stage, sem):
    rows = stage.shape[1]
    n_chunks = w_hbm.shape[0] // rows

    def chunk_copy(c):
        return pltpu.make_async_copy(w_hbm.at[pl.ds(c * rows, rows), :], stage.at[c % 2],
                                     sem.at[c % 2])

    chunk_copy(0).start()
    for c in range(n_chunks):
        if c + 1 < n_chunks:
            chunk_copy(c + 1).start()
        chunk_copy(c).wait()
        w_scr[c * rows:(c + 1) * rows, :] = stage[c % 2].astype(BF16)


def _resident_weight_scratch(k, n):
    return [pltpu.VMEM((k, n), BF16), pltpu.VMEM((2, WEIGHT_CHUNK_ROWS, n), F32),
            pltpu.SemaphoreType.DMA((2,))]


def _qk_norm_rope(t, g2, cos_t, sin_a, sin_b):
    lane = lax.broadcasted_iota(jnp.int32, t.shape, 1)
    is_lo = lane < DIFF_HALF
    t2 = t * t
    s_lo = jnp.sum(jnp.where(is_lo, t2, 0.0), axis=-1, keepdims=True)
    s_hi = jnp.sum(jnp.where(is_lo, 0.0, t2), axis=-1, keepdims=True)
    r = jnp.where(is_lo,
                  lax.rsqrt(s_lo * (1.0 / DIFF_HALF) + EPS),
                  lax.rsqrt(s_hi * (1.0 / DIFF_HALF) + EPS))
    tn = t * r * g2
    half = ROT_DIM // 2
    return (tn * cos_t
            + pltpu.roll(tn, V7X_LANES - half, axis=1) * sin_a
            + pltpu.roll(tn, half, axis=1) * sin_b)


def _in_proj_kernel(x_ref, g_ref, w_hbm, convw_ref, gconv_ref, pos_ref, freq_ref,
                    gq_ref, gk_ref, wkv_ref,
                    yconv_ref, qt_ref, kn_ref, vt_ref, qm_ref, wkv_out,
                    z_scr, w_ref, w_stage, w_sem, *, tiles_per_seq):
    i = pl.program_id(0)
    tm = x_ref.shape[0]
    halo = V7X_SUBLANES

    @pl.when(i == 0)
    def _():
        _load_weight_as_bf16(w_hbm, w_ref, w_stage, w_sem)

    @pl.when(i % tiles_per_seq == 0)
    def _():
        z_scr[0:halo, :] = jnp.zeros((halo, CONV_WIDTH), F32)

    wkv_out[...] = wkv_ref[...].astype(BF16)

    x = x_ref[...]
    h = (x * g_ref[...]).astype(BF16)
    r = _rms_scale(x, D_MODEL)

    def proj(off, width):
        return jnp.dot(h, w_ref[:, off:off + width], preferred_element_type=F32) * r

    z = proj(_OFF_C, CONV_WIDTH) * proj(_OFF_U, CONV_WIDTH)
    z_scr[halo:halo + tm, :] = z
    z1 = z_scr[halo - 1:halo - 1 + tm, :]
    z2 = z_scr[halo - 2:halo - 2 + tm, :]
    cw = convw_ref[...]
    conv = cw[0:1, :] * z2 + cw[1:2, :] * z1 + cw[2:3, :] * z
    y = proj(_OFF_B, CONV_WIDTH) * conv
    yconv_ref[...] = (y * _rms_scale(y, CONV_WIDTH) * gconv_ref[...]).astype(BF16)
    z_scr[0:halo, :] = z_scr[tm:tm + halo, :]

    qf = proj(_OFF_Q, DIFF_WIDTH)
    kf = proj(_OFF_K, DIFF_WIDTH)
    vf = proj(_OFF_V, DIFF_WIDTH)
    pos = jnp.broadcast_to(pos_ref[0].astype(F32), (V7X_LANES, tm)).T
    ang = pos * freq_ref[...]
    cos_t, sin_t = jnp.cos(ang), jnp.sin(ang)
    pos_in_half = lax.broadcasted_iota(jnp.int32, sin_t.shape, 1) % DIFF_HALF
    sin_a = jnp.where(pos_in_half < ROT_DIM // 2, -sin_t, 0.0)
    sin_b = jnp.where((pos_in_half >= ROT_DIM // 2) & (pos_in_half < ROT_DIM), sin_t, 0.0)
    gq, gk = gq_ref[...], gk_ref[...]
    for hd in range(DIFF_HEADS):
        cols = slice(hd * DIFF_VDIM, (hd + 1) * DIFF_VDIM)
        qn = _qk_norm_rope(qf[:, cols], gq, cos_t, sin_a, sin_b) * (DIFF_HALF ** -0.5 * LOG2_E)
        qt_ref[0, 0, cols, :] = qn.T.astype(BF16)
        kn_ref[:, cols] = _qk_norm_rope(kf[:, cols], gk, cos_t, sin_a, sin_b).astype(BF16)
        vt_ref[0, 0, cols, :] = vf[:, cols].T.astype(BF16)
    qm_ref[...] = proj(_OFF_QM, MEM_WIDTH)


def _in_proj(x2, g_mix, w_in, conv_w, g_conv, pos, lane_freq, gq2, gk2, w_kv):
    t = x2.shape[0]
    tm = TM_PROJ
    tps = SEQ // tm
    nb = t // SEQ
    grid = (t // tm,)
    row = lambda i: (i, 0)
    fixed = lambda i: (0, 0)
    tile_t = lambda i: (i // tps, i % tps, 0, 0)
    kv_rows = w_kv.shape[0] // grid[0]
    assert kv_rows % V7X_BF16_SUBLANES == 0
    kv_blk = pl.BlockSpec((kv_rows, w_kv.shape[1]), row)
    return pl.pallas_call(
        functools.partial(_in_proj_kernel, tiles_per_seq=tps),
        grid=grid,
        in_specs=[
            pl.BlockSpec((tm, D_MODEL), row),
            pl.BlockSpec((1, D_MODEL), fixed),
            pl.BlockSpec(memory_space=pl.ANY),
            pl.BlockSpec((CONV_K, CONV_WIDTH), fixed),
            pl.BlockSpec((1, CONV_WIDTH), fixed),
            pl.BlockSpec((1, 1, tm), lambda i: (i, 0, 0)),
            pl.BlockSpec((1, V7X_LANES), fixed),
            pl.BlockSpec((1, DIFF_VDIM), fixed),
            pl.BlockSpec((1, DIFF_VDIM), fixed),
            kv_blk,
        ],
        out_specs=[
            pl.BlockSpec((tm, CONV_WIDTH), row),
            pl.BlockSpec((1, 1, DIFF_WIDTH, tm), tile_t),
            pl.BlockSpec((tm, DIFF_WIDTH), row),
            pl.BlockSpec((1, 1, DIFF_WIDTH, tm), tile_t),
            pl.BlockSpec((tm, MEM_WIDTH), row),
            kv_blk,
        ],
        out_shape=[
            jax.ShapeDtypeStruct((t, CONV_WIDTH), BF16),
            jax.ShapeDtypeStruct((nb, tps, DIFF_WIDTH, tm), BF16),
            jax.ShapeDtypeStruct((t, DIFF_WIDTH), BF16),
            jax.ShapeDtypeStruct((nb, tps, DIFF_WIDTH, tm), BF16),
            jax.ShapeDtypeStruct((t, MEM_WIDTH), F32),
            jax.ShapeDtypeStruct(w_kv.shape, BF16),
        ],
        scratch_shapes=[pltpu.VMEM((tm + V7X_SUBLANES, CONV_WIDTH), F32)]
        + _resident_weight_scratch(D_MODEL, IN_WIDTH),
        compiler_params=pltpu.CompilerParams(
            dimension_semantics=("arbitrary",),
            vmem_limit_bytes=V7X_VMEM_LIMIT_BYTES),
        name="in_proj",
    )(x2, g_mix, w_in, conv_w, g_conv, pos, lane_freq, gq2, gk2, w_kv)


def _mem_kv_kernel(mem_ref, g_ref, w_ref, gk_ref, k_ref, v_ref):
    x = mem_ref[...]
    h = (x * _rms_scale(x, D_MODEL) * g_ref[...]).astype(BF16)
    kv = jnp.dot(h, w_ref[...], preferred_element_type=F32)
    gk = gk_ref[...]
    for hd in range(MEM_HEADS):
        lo = hd * MEM_HEAD_DIM
        kh = kv[:, lo:lo + MEM_HEAD_DIM]
        k_ref[:, lo:lo + MEM_HEAD_DIM] = (kh * _rms_scale(kh, MEM_HEAD_DIM) * gk).astype(BF16)
    v_ref[...] = kv[:, MEM_WIDTH:].astype(BF16)


def _mem_kv(mem2, g_mem, w_kv, g_mk):
    t = mem2.shape[0]
    tm = TM_MEMKV
    row = lambda i: (i, 0)
    fixed = lambda i: (0, 0)
    return pl.pallas_call(
        _mem_kv_kernel,
        grid=(t // tm,),
        in_specs=[
            pl.BlockSpec((tm, D_MODEL), row),
            pl.BlockSpec((1, D_MODEL), fixed),
            pl.BlockSpec((D_MODEL, 2 * MEM_WIDTH), fixed),
            pl.BlockSpec((1, MEM_HEAD_DIM), fixed),
        ],
        out_specs=[pl.BlockSpec((tm, MEM_WIDTH), row), pl.BlockSpec((tm, MEM_WIDTH), row)],
        out_shape=[jax.ShapeDtypeStruct((t, MEM_WIDTH), BF16),
                   jax.ShapeDtypeStruct((t, MEM_WIDTH), BF16)],
        compiler_params=pltpu.CompilerParams(
            dimension_semantics=("parallel",),
            vmem_limit_bytes=V7X_VMEM_LIMIT_BYTES),
        name="mem_kv",
    )(mem2, g_mem, w_kv, g_mk)


def _diff_attn_kernel(bounded_ref, qt_ref, kn_ref, vt_ref, lq1_ref, lk1_ref, lq2_ref, lk2_ref,
                      gsub_ref, wg_ref, wu_ref, wd_ref, wo_ref, o_ref, wgu_out, wd_out, wo_out):
    nq, tq = qt_ref.shape[1], qt_ref.shape[3]
    tk = vt_ref.shape[3]

    lam = (jnp.exp(jnp.sum(lq1_ref[...] * lk1_ref[...], axis=-1, keepdims=True))
           - jnp.exp(jnp.sum(lq2_ref[...] * lk2_ref[...], axis=-1, keepdims=True))
           + LAMBDA_INIT)
    key = lax.broadcasted_iota(jnp.int32, (tk, tq), 0)
    qry = lax.broadcasted_iota(jnp.int32, (tk, tq), 1)
    causal = jnp.concatenate([key <= qry, key <= qry], axis=1)
    sum_rows = (lax.broadcasted_iota(jnp.int32, (V7X_BF16_SUBLANES, tk), 0) == 0).astype(BF16)

    def head(hd):
        return slice(hd * DIFF_VDIM, (hd + 1) * DIFF_VDIM)

    def scores(hd, c, j):
        qt = qt_ref[0, c, head(hd), :]
        is_lo = lax.broadcasted_iota(jnp.int32, qt.shape, 0) < DIFF_HALF
        zero = jnp.zeros_like(qt)
        qq = jnp.concatenate([jnp.where(is_lo, qt, zero), jnp.where(is_lo, zero, qt)], axis=1)
        return jnp.dot(kn_ref[0, j * tk:(j + 1) * tk, head(hd)], qq,
                       preferred_element_type=F32)

    steps = [(hd, c, j) for hd in range(ATTN_HEADS_PER_STEP)
             for c in range(nq) for j in range(c + 1)]

    def attend(shift_free):
        th = wgu_out.shape[2] // 2
        for ch in range(wgu_out.shape[0]):
            cols = slice(ch * th, (ch + 1) * th)
            wgu_out[ch, :, :th] = wg_ref[:, cols].astype(BF16)
            wgu_out[ch, :, th:] = wu_ref[:, cols].astype(BF16)
        wd_out[...] = wd_ref[...].astype(BF16)
        wo_out[...] = wo_ref[...].astype(BF16)
        pending = [scores(*st) for st in steps[:SCORE_LOOKAHEAD]]
        m = acc = None
        for n, (hd, c, j) in enumerate(steps):
            s = pending.pop(0)
            if n + SCORE_LOOKAHEAD < len(steps):
                pending.append(scores(*steps[n + SCORE_LOOKAHEAD]))
            if j == c:
                s = jnp.where(causal, s, -jnp.inf)
            vt_ext = jnp.concatenate([vt_ref[0, j, head(hd), :], sum_rows], axis=0)
            if shift_free:
                pv = jnp.dot(vt_ext, jnp.exp2(s).astype(BF16), preferred_element_type=F32)
                acc = pv if j == 0 else acc + pv
            else:
                m_blk = jnp.max(s, axis=0, keepdims=True)
                m_new = m_blk if j == 0 else jnp.maximum(m, m_blk)
                p = jnp.exp2(s - m_new)
                pv = jnp.dot(vt_ext, p.astype(BF16), preferred_element_type=F32)
                acc = pv if j == 0 else jnp.exp2(m - m_new) * acc + pv
                m = m_new
            if j == c:
                o_all = acc[:DIFF_VDIM] / acc[DIFF_VDIM:DIFF_VDIM + 1]
                o = (o_all[:, :tq] - lam * o_all[:, tq:]).T
                y = o * _rms_scale(o, DIFF_VDIM) * gsub_ref[...] * (1.0 - LAMBDA_INIT)
                o_ref[0, c * tq:(c + 1) * tq, head(hd)] = y.astype(o_ref.dtype)

    pl.when(bounded_ref[0] != 0)(functools.partial(attend, True))
    pl.when(bounded_ref[0] == 0)(functools.partial(attend, False))


def _diff_attn(bounded, qt4, kn3, vt4, lq1, lk1, lq2, lk2, g_sub, w_gate, w_up, w_down, w_o):
    b, nq, _, tq = qt4.shape
    s = kn3.shape[1]
    nk, tk = vt4.shape[1], vt4.shape[3]
    assert tq == tk and nq == nk
    fixed = lambda bb, h: (0, 0)
    vec = lambda n: pl.BlockSpec((1, n), fixed)
    groups = DIFF_HEADS // ATTN_HEADS_PER_STEP
    width = ATTN_HEADS_PER_STEP * DIFF_VDIM
    steps = b * groups
    nh = FFN_HIDDEN // TH_FFN
    gu_rows, down_rows, wo_rows = D_MODEL // steps, FFN_HIDDEN // steps, w_o.shape[0] // steps
    assert all(r % V7X_BF16_SUBLANES == 0 for r in (gu_rows, down_rows, wo_rows))
    step = lambda bb, g: bb * groups + g
    gu_in = pl.BlockSpec((gu_rows, FFN_HIDDEN), lambda bb, g: (step(bb, g), 0))
    gu_out = pl.BlockSpec((nh, gu_rows, 2 * TH_FFN), lambda bb, g: (0, step(bb, g), 0))
    down_blk = pl.BlockSpec((down_rows, D_MODEL), lambda bb, g: (step(bb, g), 0))
    wo_blk = pl.BlockSpec((wo_rows, w_o.shape[1]), lambda bb, g: (step(bb, g), 0))
    return pl.pallas_call(
        _diff_attn_kernel,
        grid=(b, groups),
        in_specs=[
            pl.BlockSpec(memory_space=pltpu.SMEM),
            pl.BlockSpec((1, nq, width, tq), lambda bb, g: (bb, 0, g, 0)),
            pl.BlockSpec((1, s, width), lambda bb, g: (bb, 0, g)),
            pl.BlockSpec((1, nk, width, tk), lambda bb, g: (bb, 0, g, 0)),
            vec(DIFF_HALF), vec(DIFF_HALF), vec(DIFF_HALF), vec(DIFF_HALF),
            vec(DIFF_VDIM),
            gu_in, gu_in, down_blk, wo_blk,
        ],
        out_specs=[pl.BlockSpec((1, s, width), lambda bb, g: (bb, 0, g)), gu_out, down_blk,
                   wo_blk],
        out_shape=[jax.ShapeDtypeStruct((b, s, DIFF_WIDTH), BF16),
                   jax.ShapeDtypeStruct((nh, D_MODEL, 2 * TH_FFN), BF16),
                   jax.ShapeDtypeStruct((FFN_HIDDEN, D_MODEL), BF16),
                   jax.ShapeDtypeStruct(w_o.shape, BF16)],
        compiler_params=pltpu.CompilerParams(
            dimension_semantics=("parallel", "parallel"),
            vmem_limit_bytes=V7X_VMEM_LIMIT_BYTES),
        name="diff_attn",
    )(bounded, qt4, kn3, vt4, lq1, lk1, lq2, lk2, g_sub, w_gate, w_up, w_down, w_o)


def _out_proj_kernel(x_ref, yc_ref, yd_ref, qm_ref, km_ref, vm_ref, gq_ref, gmo_ref, gffn_ref,
                     w_ref, x1_ref, hf_ref):
    gq = gq_ref[...]
    probs, denoms = [], []
    for hd in range(MEM_HEADS):
        cols = slice(hd * MEM_HEAD_DIM, (hd + 1) * MEM_HEAD_DIM)
        qh = qm_ref[:, cols]
        qh = (qh * _rms_scale(qh, MEM_HEAD_DIM) * gq * (MEM_HEAD_DIM ** -0.5)).astype(BF16)
        s = lax.dot_general(qh, km_ref[0, :, cols], (((1,), (1,)), ((), ())),
                            preferred_element_type=F32)
        e = jnp.exp(s - jnp.max(s, axis=-1, keepdims=True))
        denoms.append(jnp.sum(e, axis=-1, keepdims=True))
        probs.append(e.astype(BF16))

    acc = jnp.dot(yc_ref[...], w_ref[0:CONV_WIDTH, :], preferred_element_type=F32)

    outs = []
    for hd in range(MEM_HEADS):
        cols = slice(hd * MEM_HEAD_DIM, (hd + 1) * MEM_HEAD_DIM)
        oh = jnp.dot(probs[hd], vm_ref[0, :, cols], preferred_element_type=F32)
        outs.append(oh / denoms[hd])
    o = jnp.concatenate(outs, axis=-1)
    ym = (o * _rms_scale(o, MEM_WIDTH) * gmo_ref[...]).astype(BF16)

    acc += jnp.dot(yd_ref[...], w_ref[CONV_WIDTH:CONV_WIDTH + DIFF_WIDTH, :],
                   preferred_element_type=F32)
    acc += jnp.dot(ym, w_ref[CONV_WIDTH + DIFF_WIDTH:, :], preferred_element_type=F32)
    x1 = x_ref[...] + acc
    x1_ref[...] = x1
    hf_ref[...] = (x1 * _rms_scale(x1, D_MODEL) * gffn_ref[...]).astype(BF16)


def _out_proj(x2, yc, yd, qm, km3, vm3, g_mq, g_mem_out, g_ffn, w_o):
    t = x2.shape[0]
    tm = TM_OUT
    tps = SEQ // tm
    row = lambda i: (i, 0)
    fixed = lambda i: (0, 0)
    mem_blk = lambda i: (i // tps, 0, 0)
    return pl.pallas_call(
        _out_proj_kernel,
        grid=(t // tm,),
        in_specs=[
            pl.BlockSpec((tm, D_MODEL), row),
            pl.BlockSpec((tm, CONV_WIDTH), row),
            pl.BlockSpec((tm, DIFF_WIDTH), row),
            pl.BlockSpec((tm, MEM_WIDTH), row),
            pl.BlockSpec((1, MEM_LEN, MEM_WIDTH), mem_blk),
            pl.BlockSpec((1, MEM_LEN, MEM_WIDTH), mem_blk),
            pl.BlockSpec((1, MEM_HEAD_DIM), fixed),
            pl.BlockSpec((1, MEM_WIDTH), fixed),
            pl.BlockSpec((1, D_MODEL), fixed),
            pl.BlockSpec((D_MODEL, D_MODEL), fixed, pipeline_mode=pl.Buffered(1)),
        ],
        out_specs=[pl.BlockSpec((tm, D_MODEL), row), pl.BlockSpec((tm, D_MODEL), row)],
        out_shape=[jax.ShapeDtypeStruct((t, D_MODEL), F32),
                   jax.ShapeDtypeStruct((t, D_MODEL), BF16)],
        compiler_params=pltpu.CompilerParams(
            dimension_semantics=("parallel",),
            vmem_limit_bytes=V7X_VMEM_LIMIT_BYTES),
        name="out_proj",
    )(x2, yc, yd, qm, km3, vm3, g_mq, g_mem_out, g_ffn, w_o)


def _ffn_kernel(hf_ref, x1_ref, wgu_ref, wd_ref, o_ref):
    @pl.when(pl.program_id(1) == 0)
    def _():
        o_ref[...] = x1_ref[...]

    th = wd_ref.shape[0]
    gu = jnp.dot(hf_ref[...], wgu_ref[0], preferred_element_type=F32)
    gate, up = gu[:, :th], gu[:, th:]
    act = (gate * (1.0 / (1.0 + jnp.exp(-gate))) * up).astype(BF16)
    o_ref[...] += jnp.dot(act, wd_ref[...], preferred_element_type=F32)


def _ffn(hf, x1, w_gate_up, w_down):
    t = x1.shape[0]
    tm, th = TM_FFN, TH_FFN
    return pl.pallas_call(
        _ffn_kernel,
        grid=(t // tm, FFN_HIDDEN // th),
        in_specs=[
            pl.BlockSpec((tm, D_MODEL), lambda i, j: (i, 0)),
            pl.BlockSpec((tm, D_MODEL), lambda i, j: (i, 0)),
            pl.BlockSpec((1, D_MODEL, 2 * th), lambda i, j: (j, 0, 0)),
            pl.BlockSpec((th, D_MODEL), lambda i, j: (j, 0)),
        ],
        out_specs=pl.BlockSpec((tm, D_MODEL), lambda i, j: (i, 0)),
        out_shape=jax.ShapeDtypeStruct((t, D_MODEL), F32),
        compiler_params=pltpu.CompilerParams(
            dimension_semantics=("parallel", "arbitrary"),
            vmem_limit_bytes=V7X_VMEM_LIMIT_BYTES),
        name="ffn",
    )(hf, x1, w_gate_up, w_down)


def _rope_lane_freq():
    inv_freq = ROPE_THETA ** (-jnp.arange(0, ROT_DIM, 2, dtype=F32) / ROT_DIM)
    pad = jnp.zeros((DIFF_HALF - ROT_DIM,), F32)
    return jnp.concatenate([inv_freq, inv_freq, pad, inv_freq, inv_freq, pad]).reshape(1, -1)


def kernel(x, mem, positions, g_mix, g_mem, w_in, conv_w, g_conv_out, g_dq, g_dk,
           lam_q1, lam_k1, lam_q2, lam_k2, g_sub, w_mem_kv, g_mq, g_mk, g_mem_out,
           w_o, g_ffn, w_gate, w_up, w_down):
    b, s, d = x.shape
    assert (s, d) == (SEQ, D_MODEL) and mem.shape == (b, MEM_LEN, D_MODEL)
    assert g_mix.shape[0] == 1
    t = b * s
    x2 = x.reshape(t, d)
    mem2 = mem.reshape(b * MEM_LEN, d)
    yc, qt, kn, vt, qm, wkv16 = _in_proj(x2, g_mix, w_in[0], conv_w[0], g_conv_out,
                                  positions.reshape(t // TM_PROJ, 1, TM_PROJ), _rope_lane_freq(),
                                  jnp.tile(g_dq, (1, 2)), jnp.tile(g_dk, (1, 2)), w_mem_kv[0])
    km, vm = _mem_kv(mem2, g_mem, wkv16, g_mk)

    score_bound = (DIFF_HALF ** 0.5 * LOG2_E) * jnp.max(jnp.abs(g_dq)) * jnp.max(jnp.abs(g_dk))
    bounded = (score_bound <= MAX_SHIFT_FREE_LOG2_SCORE).astype(jnp.int32).reshape(1)
    yd, wgu16, wd16, wo16 = _diff_attn(bounded, qt, kn.reshape(b, s, DIFF_WIDTH), vt,
                                       lam_q1, lam_k1, lam_q2, lam_k2, g_sub,
                                       w_gate[0], w_up[0], w_down[0], w_o[0])
    x1, hf = _out_proj(x2, yc, yd.reshape(t, DIFF_WIDTH), qm,
                       km.reshape(b, MEM_LEN, MEM_WIDTH), vm.reshape(b, MEM_LEN, MEM_WIDTH),
                       g_mq, g_mem_out, g_ffn, wo16)
    out = _ffn(hf, x1, wgu16, wd16)
    return out.reshape(b, s, d)
```

```python
import functools

import jax
import jax.numpy as jnp
import numpy as np
from jax import lax
from jax.experimental import pallas as pl
from jax.experimental.pallas import tpu as pltpu

F32 = jnp.float32
BF16 = jnp.bfloat16

D_MODEL = 2048
SEQ = 2048
MEM_LEN = 256
CONV_WIDTH = 512
CONV_K = 3
DIFF_WIDTH = 1024
DIFF_VDIM = 128
DIFF_HALF = 64
DIFF_HEADS = 8
MEM_WIDTH = 512
MEM_HEADS = 4
MEM_HEAD_DIM = 128
IN_WIDTH = 3 * CONV_WIDTH + 3 * DIFF_WIDTH + MEM_WIDTH
ROT_DIM = 16
ROPE_THETA = 500000.0
FFN_HIDDEN = 5632
EPS = 1e-6
LAMBDA_INIT = 0.8 - 0.6 * float(np.exp(-0.3 * 0))
LOG2_E = float(np.log2(np.e))
MAX_SHIFT_FREE_LOG2_SCORE = 60.0

V7X_LANES = 128
V7X_SUBLANES = 8
V7X_BF16_SUBLANES = 16
V7X_VMEM_LIMIT_BYTES = 60 * 1024 * 1024

TM_PROJ = 512
T_ATTN = 256
TM_MEMKV = 256
TM_OUT = 512
TM_FFN = 1024
TH_FFN = 512
WEIGHT_CHUNK_ROWS = 128
SCORE_LOOKAHEAD = 2
ATTN_HEADS_PER_STEP = 2

_OFF_U = 0
_OFF_C = CONV_WIDTH
_OFF_B = 2 * CONV_WIDTH
_OFF_Q = 3 * CONV_WIDTH
_OFF_K = _OFF_Q + DIFF_WIDTH
_OFF_V = _OFF_K + DIFF_WIDTH
_OFF_QM = _OFF_V + DIFF_WIDTH


def _rms_scale(t, width):
    return lax.rsqrt(jnp.sum(t * t, axis=-1, keepdims=True) * (1.0 / width) + EPS)


def _load_weight_as_bf16(w_hbm, w_scr, stage, sem):
    rows = stage.shape[1]
    n_chunks = w_hbm.shape[0] // rows

    def chunk_copy(c):
        return pltpu.make_async_copy(w_hbm.at[pl.ds(c * rows, rows), :], stage.at[c % 2],
                                     sem.at[c % 2])

    chunk_copy(0).start()
    for c in range(n_chunks):
        if c + 1 < n_chunks:
            chunk_copy(c + 1).start()
        chunk_copy(c).wait()
        w_scr[c * rows:(c + 1) * rows, :] = stage[c % 2].astype(BF16)


def _resident_weight_scratch(k, n):
    return [pltpu.VMEM((k, n), BF16), pltpu.VMEM((2, WEIGHT_CHUNK_ROWS, n), F32),
            pltpu.SemaphoreType.DMA((2,))]


def _qk_norm_rope(t, g2, cos_t, sin_a, sin_b):
    lane = lax.broadcasted_iota(jnp.int32, t.shape, 1)
    is_lo = lane < DIFF_HALF
    t2 = t * t
    s_lo = jnp.sum(jnp.where(is_lo, t2, 0.0), axis=-1, keepdims=True)
    s_hi = jnp.sum(jnp.where(is_lo, 0.0, t2), axis=-1, keepdims=True)
    r = jnp.where(is_lo,
                  lax.rsqrt(s_lo * (1.0 / DIFF_HALF) + EPS),
                  lax.rsqrt(s_hi * (1.0 / DIFF_HALF) + EPS))
    tn = t * r * g2
    half = ROT_DIM // 2
    return (tn * cos_t
            + pltpu.roll(tn, V7X_LANES - half, axis=1) * sin_a
            + pltpu.roll(tn, half, axis=1) * sin_b)


def _in_proj_kernel(x_ref, g_ref, w_hbm, convw_ref, gconv_ref, pos_ref, freq_ref,
                    gq_ref, gk_ref, wkv_ref,
                    yconv_ref, qt_ref, kn_ref, vt_ref, qm_ref, wkv_out,
                    z_scr, w_ref, w_stage, w_sem, *, tiles_per_seq):
    i = pl.program_id(0)
    halo = V7X_SUBLANES

    @pl.when(i == 0)
    def _():
        _load_weight_as_bf16(w_hbm, w_ref, w_stage, w_sem)

    @pl.when(i % tiles_per_seq == 0)
    def _():
        z_scr[0:halo, :] = jnp.zeros((halo, CONV_WIDTH), F32)

    wkv_out[...] = wkv_ref[...].astype(BF16)

    def proj(h, off, width):
        return jnp.dot(h, w_ref[:, off:off + width], preferred_element_type=F32)

    ts = qt_ref.shape[3]
    for sub in range(qt_ref.shape[1]):
        rows = slice(sub * ts, (sub + 1) * ts)
        x = x_ref[rows, :]
        h = (x * _rms_scale(x, D_MODEL) * g_ref[...]).astype(BF16)

        z = proj(h, _OFF_C, CONV_WIDTH) * proj(h, _OFF_U, CONV_WIDTH)
        z_scr[halo:halo + ts, :] = z
        z1 = z_scr[halo - 1:halo - 1 + ts, :]
        z2 = z_scr[halo - 2:halo - 2 + ts, :]
        cw = convw_ref[...]
        conv = cw[0:1, :] * z2 + cw[1:2, :] * z1 + cw[2:3, :] * z
        y = proj(h, _OFF_B, CONV_WIDTH) * conv
        yconv_ref[rows, :] = (y * _rms_scale(y, CONV_WIDTH) * gconv_ref[...]).astype(BF16)
        z_scr[0:halo, :] = z_scr[ts:ts + halo, :]

        qf = proj(h, _OFF_Q, DIFF_WIDTH)
        kf = proj(h, _OFF_K, DIFF_WIDTH)
        vf = proj(h, _OFF_V, DIFF_WIDTH)
        pos = jnp.broadcast_to(pos_ref[0][:, rows].astype(F32), (V7X_LANES, ts)).T
        ang = pos * freq_ref[...]
        cos_t, sin_t = jnp.cos(ang), jnp.sin(ang)
        pos_in_half = lax.broadcasted_iota(jnp.int32, sin_t.shape, 1) % DIFF_HALF
        sin_a = jnp.where(pos_in_half < ROT_DIM // 2, -sin_t, 0.0)
        sin_b = jnp.where((pos_in_half >= ROT_DIM // 2) & (pos_in_half < ROT_DIM), sin_t, 0.0)
        gq, gk = gq_ref[...], gk_ref[...]
        for hd in range(DIFF_HEADS):
            cols = slice(hd * DIFF_VDIM, (hd + 1) * DIFF_VDIM)
            qn = (_qk_norm_rope(qf[:, cols], gq, cos_t, sin_a, sin_b)
                  * (DIFF_HALF ** -0.5 * LOG2_E))
            qt_ref[0, sub, cols, :] = qn.T.astype(BF16)
            kn_ref[rows, cols] = _qk_norm_rope(kf[:, cols], gk, cos_t, sin_a, sin_b).astype(BF16)
            vt_ref[0, sub, cols, :] = vf[:, cols].T.astype(BF16)
        qm_ref[rows, :] = proj(h, _OFF_QM, MEM_WIDTH)


def _in_proj(x2, g_mix, w_in, conv_w, g_conv, pos, lane_freq, gq2, gk2, w_kv):
    t = x2.shape[0]
    tm, ta = TM_PROJ, T_ATTN
    tps = SEQ // tm
    sub = tm // ta
    nb = t // SEQ
    grid = (t // tm,)
    row = lambda i: (i, 0)
    fixed = lambda i: (0, 0)
    tile_t = lambda i: (i // tps, i % tps, 0, 0)
    kv_rows = w_kv.shape[0] // grid[0]
    assert kv_rows % V7X_BF16_SUBLANES == 0
    kv_blk = pl.BlockSpec((kv_rows, w_kv.shape[1]), row)
    return pl.pallas_call(
        functools.partial(_in_proj_kernel, tiles_per_seq=tps),
        grid=grid,
        in_specs=[
            pl.BlockSpec((tm, D_MODEL), row),
            pl.BlockSpec((1, D_MODEL), fixed),
            pl.BlockSpec(memory_space=pl.ANY),
            pl.BlockSpec((CONV_K, CONV_WIDTH), fixed),
            pl.BlockSpec((1, CONV_WIDTH), fixed),
            pl.BlockSpec((1, 1, tm), lambda i: (i, 0, 0)),
            pl.BlockSpec((1, V7X_LANES), fixed),
            pl.BlockSpec((1, DIFF_VDIM), fixed),
            pl.BlockSpec((1, DIFF_VDIM), fixed),
            kv_blk,
        ],
        out_specs=[
            pl.BlockSpec((tm, CONV_WIDTH), row),
            pl.BlockSpec((1, sub, DIFF_WIDTH, ta), tile_t),
            pl.BlockSpec((tm, DIFF_WIDTH), row),
            pl.BlockSpec((1, sub, DIFF_WIDTH, ta), tile_t),
            pl.BlockSpec((tm, MEM_WIDTH), row),
            kv_blk,
        ],
        out_shape=[
            jax.ShapeDtypeStruct((t, CONV_WIDTH), BF16),
            jax.ShapeDtypeStruct((nb, SEQ // ta, DIFF_WIDTH, ta), BF16),
            jax.ShapeDtypeStruct((t, DIFF_WIDTH), BF16),
            jax.ShapeDtypeStruct((nb, SEQ // ta, DIFF_WIDTH, ta), BF16),
            jax.ShapeDtypeStruct((t, MEM_WIDTH), F32),
            jax.ShapeDtypeStruct(w_kv.shape, BF16),
        ],
        scratch_shapes=[pltpu.VMEM((ta + V7X_SUBLANES, CONV_WIDTH), F32)]
        + _resident_weight_scratch(D_MODEL, IN_WIDTH),
        compiler_params=pltpu.CompilerParams(
            dimension_semantics=("arbitrary",),
            vmem_limit_bytes=V7X_VMEM_LIMIT_BYTES),
        name="in_proj",
    )(x2, g_mix, w_in, conv_w, g_conv, pos, lane_freq, gq2, gk2, w_kv)


def _mem_kv_kernel(mem_ref, g_ref, w_ref, gk_ref, k_ref, v_ref):
    x = mem_ref[...]
    h = (x * _rms_scale(x, D_MODEL) * g_ref[...]).astype(BF16)
    kv = jnp.dot(h, w_ref[...], preferred_element_type=F32)
    gk = gk_ref[...]
    for hd in range(MEM_HEADS):
        lo = hd * MEM_HEAD_DIM
        kh = kv[:, lo:lo + MEM_HEAD_DIM]
        k_ref[:, lo:lo + MEM_HEAD_DIM] = (kh * _rms_scale(kh, MEM_HEAD_DIM) * gk).astype(BF16)
    v_ref[...] = kv[:, MEM_WIDTH:].astype(BF16)


def _mem_kv(mem2, g_mem, w_kv, g_mk):
    t = mem2.shape[0]
    tm = TM_MEMKV
    row = lambda i: (i, 0)
    fixed = lambda i: (0, 0)
    return pl.pallas_call(
        _mem_kv_kernel,
        grid=(t // tm,),
        in_specs=[
            pl.BlockSpec((tm, D_MODEL), row),
            pl.BlockSpec((1, D_MODEL), fixed),
            pl.BlockSpec((D_MODEL, 2 * MEM_WIDTH), fixed),
            pl.BlockSpec((1, MEM_HEAD_DIM), fixed),
        ],
        out_specs=[pl.BlockSpec((tm, MEM_WIDTH), row), pl.BlockSpec((tm, MEM_WIDTH), row)],
        out_shape=[jax.ShapeDtypeStruct((t, MEM_WIDTH), BF16),
                   jax.ShapeDtypeStruct((t, MEM_WIDTH), BF16)],
        compiler_params=pltpu.CompilerParams(
            dimension_semantics=("parallel",),
            vmem_limit_bytes=V7X_VMEM_LIMIT_BYTES),
        name="mem_kv",
    )(mem2, g_mem, w_kv, g_mk)


def _diff_attn_kernel(bounded_ref, qt_ref, kn_ref, vt_ref, lq1_ref, lk1_ref, lq2_ref, lk2_ref,
                      gsub_ref, wg_ref, wu_ref, wd_ref, wo_ref, o_ref, wgu_out, wd_out, wo_out):
    nq, tq = qt_ref.shape[1], qt_ref.shape[3]
    tk = vt_ref.shape[3]

    lam = (jnp.exp(jnp.sum(lq1_ref[...] * lk1_ref[...], axis=-1, keepdims=True))
           - jnp.exp(jnp.sum(lq2_ref[...] * lk2_ref[...], axis=-1, keepdims=True))
           + LAMBDA_INIT)
    key = lax.broadcasted_iota(jnp.int32, (tk, tq), 0)
    qry = lax.broadcasted_iota(jnp.int32, (tk, tq), 1)
    causal = jnp.concatenate([key <= qry, key <= qry], axis=1)
    sum_rows = (lax.broadcasted_iota(jnp.int32, (V7X_BF16_SUBLANES, tk), 0) == 0).astype(BF16)

    def head(hd):
        return slice(hd * DIFF_VDIM, (hd + 1) * DIFF_VDIM)

    def scores(hd, c, j):
        qt = qt_ref[0, c, head(hd), :]
        is_lo = lax.broadcasted_iota(jnp.int32, qt.shape, 0) < DIFF_HALF
        zero = jnp.zeros_like(qt)
        qq = jnp.concatenate([jnp.where(is_lo, qt, zero), jnp.where(is_lo, zero, qt)], axis=1)
        return jnp.dot(kn_ref[0, j * tk:(j + 1) * tk, head(hd)], qq,
                       preferred_element_type=F32)

    steps = [(hd, c, j) for hd in range(ATTN_HEADS_PER_STEP)
             for c in range(nq) for j in range(c + 1)]

    def attend(shift_free):
        th = wgu_out.shape[2] // 2
        for ch in range(wgu_out.shape[0]):
            cols = slice(ch * th, (ch + 1) * th)
            wgu_out[ch, :, :th] = wg_ref[:, cols].astype(BF16)
            wgu_out[ch, :, th:] = wu_ref[:, cols].astype(BF16)
        wd_out[...] = wd_ref[...].astype(BF16)
        wo_out[...] = wo_ref[...].astype(BF16)
        pending = [scores(*st) for st in steps[:SCORE_LOOKAHEAD]]
        m = acc = None
        for n, (hd, c, j) in enumerate(steps):
            s = pending.pop(0)
            if n + SCORE_LOOKAHEAD < len(steps):
                pending.append(scores(*steps[n + SCORE_LOOKAHEAD]))
            if j == c:
                s = jnp.where(causal, s, -jnp.inf)
            vt_ext = jnp.concatenate([vt_ref[0, j, head(hd), :], sum_rows], axis=0)
            if shift_free:
                pv = jnp.dot(vt_ext, jnp.exp2(s).astype(BF16), preferred_element_type=F32)
                acc = pv if j == 0 else acc + pv
            else:
                m_blk = jnp.max(s, axis=0, keepdims=True)
                m_new = m_blk if j == 0 else jnp.maximum(m, m_blk)
                p = jnp.exp2(s - m_new)
                pv = jnp.dot(vt_ext, p.astype(BF16), preferred_element_type=F32)
                acc = pv if j == 0 else jnp.exp2(m - m_new) * acc + pv
                m = m_new
            if j == c:
                o_all = acc[:DIFF_VDIM] / acc[DIFF_VDIM:DIFF_VDIM + 1]
                o = (o_all[:, :tq] - lam * o_all[:, tq:]).T
                y = o * _rms_scale(o, DIFF_VDIM) * gsub_ref[...] * (1.0 - LAMBDA_INIT)
                o_ref[0, c * tq:(c + 1) * tq, head(hd)] = y.astype(o_ref.dtype)

    pl.when(bounded_ref[0] != 0)(functools.partial(attend, True))
    pl.when(bounded_ref[0] == 0)(functools.partial(attend, False))


def _diff_attn(bounded, qt4, kn3, vt4, lq1, lk1, lq2, lk2, g_sub, w_gate, w_up, w_down, w_o):
    b, nq, _, tq = qt4.shape
    s = kn3.shape[1]
    nk, tk = vt4.shape[1], vt4.shape[3]
    assert tq == tk and nq == nk
    fixed = lambda bb, h: (0, 0)
    vec = lambda n: pl.BlockSpec((1, n), fixed)
    groups = DIFF_HEADS // ATTN_HEADS_PER_STEP
    width = ATTN_HEADS_PER_STEP * DIFF_VDIM
    steps = b * groups
    nh = FFN_HIDDEN // TH_FFN
    gu_rows, down_rows, wo_rows = D_MODEL // steps, FFN_HIDDEN // steps, w_o.shape[0] // steps
    assert all(r % V7X_BF16_SUBLANES == 0 for r in (gu_rows, down_rows, wo_rows))
    step = lambda bb, g: bb * groups + g
    gu_in = pl.BlockSpec((gu_rows, FFN_HIDDEN), lambda bb, g: (step(bb, g), 0))
    gu_out = pl.BlockSpec((nh, gu_rows, 2 * TH_FFN), lambda bb, g: (0, step(bb, g), 0))
    down_blk = pl.BlockSpec((down_rows, D_MODEL), lambda bb, g: (step(bb, g), 0))
    wo_blk = pl.BlockSpec((wo_rows, w_o.shape[1]), lambda bb, g: (step(bb, g), 0))
    return pl.pallas_call(
        _diff_attn_kernel,
        grid=(b, groups),
        in_specs=[
            pl.BlockSpec(memory_space=pltpu.SMEM),
            pl.BlockSpec((1, nq, width, tq), lambda bb, g: (bb, 0, g, 0)),
            pl.BlockSpec((1, s, width), lambda bb, g: (bb, 0, g)),
            pl.BlockSpec((1, nk, width, tk), lambda bb, g: (bb, 0, g, 0)),
            vec(DIFF_HALF), vec(DIFF_HALF), vec(DIFF_HALF), vec(DIFF_HALF),
            vec(DIFF_VDIM),
            gu_in, gu_in, down_blk, wo_blk,
        ],
        out_specs=[pl.BlockSpec((1, s, width), lambda bb, g: (bb, 0, g)), gu_out, down_blk,
                   wo_blk],
        out_shape=[jax.ShapeDtypeStruct((b, s, DIFF_WIDTH), BF16),
                   jax.ShapeDtypeStruct((nh, D_MODEL, 2 * TH_FFN), BF16),
                   jax.ShapeDtypeStruct((FFN_HIDDEN, D_MODEL), BF16),
                   jax.ShapeDtypeStruct(w_o.shape, BF16)],
        compiler_params=pltpu.CompilerParams(
            dimension_semantics=("parallel", "parallel"),
            vmem_limit_bytes=V7X_VMEM_LIMIT_BYTES),
        name="diff_attn",
    )(bounded, qt4, kn3, vt4, lq1, lk1, lq2, lk2, g_sub, w_gate, w_up, w_down, w_o)


def _out_proj_kernel(x_ref, yc_ref, yd_ref, qm_ref, km_ref, vm_ref, gq_ref, gmo_ref, gffn_ref,
                     w_ref, x1_ref, hf_ref):
    gq = gq_ref[...]
    probs, denoms = [], []
    for hd in range(MEM_HEADS):
        cols = slice(hd * MEM_HEAD_DIM, (hd + 1) * MEM_HEAD_DIM)
        qh = qm_ref[:, cols]
        qh = (qh * _rms_scale(qh, MEM_HEAD_DIM) * gq * (MEM_HEAD_DIM ** -0.5)).astype(BF16)
        s = lax.dot_general(qh, km_ref[0, :, cols], (((1,), (1,)), ((), ())),
                            preferred_element_type=F32)
        e = jnp.exp(s - jnp.max(s, axis=-1, keepdims=True))
        denoms.append(jnp.sum(e, axis=-1, keepdims=True))
        probs.append(e.astype(BF16))

    acc = jnp.dot(yc_ref[...], w_ref[0:CONV_WIDTH, :], preferred_element_type=F32)

    outs = []
    for hd in range(MEM_HEADS):
        cols = slice(hd * MEM_HEAD_DIM, (hd + 1) * MEM_HEAD_DIM)
        oh = jnp.dot(probs[hd], vm_ref[0, :, cols], preferred_element_type=F32)
        outs.append(oh / denoms[hd])
    o = jnp.concatenate(outs, axis=-1)
    ym = (o * _rms_scale(o, MEM_WIDTH) * gmo_ref[...]).astype(BF16)

    acc += jnp.dot(yd_ref[...], w_ref[CONV_WIDTH:CONV_WIDTH + DIFF_WIDTH, :],
                   preferred_element_type=F32)
    acc += jnp.dot(ym, w_ref[CONV_WIDTH + DIFF_WIDTH:, :], preferred_element_type=F32)
    x1 = x_ref[...] + acc
    x1_ref[...] = x1
    hf_ref[...] = (x1 * _rms_scale(x1, D_MODEL) * gffn_ref[...]).astype(BF16)


def _out_proj(x2, yc, yd, qm, km3, vm3, g_mq, g_mem_out, g_ffn, w_o):
    t = x2.shape[0]
    tm = TM_OUT
    tps = SEQ // tm
    row = lambda i: (i, 0)
    fixed = lambda i: (0, 0)
    mem_blk = lambda i: (i // tps, 0, 0)
    return pl.pallas_call(
        _out_proj_kernel,
        grid=(t // tm,),
        in_specs=[
            pl.BlockSpec((tm, D_MODEL), row),
            pl.BlockSpec((tm, CONV_WIDTH), row),
            pl.BlockSpec((tm, DIFF_WIDTH), row),
            pl.BlockSpec((tm, MEM_WIDTH), row),
            pl.BlockSpec((1, MEM_LEN, MEM_WIDTH), mem_blk),
            pl.BlockSpec((1, MEM_LEN, MEM_WIDTH), mem_blk),
            pl.BlockSpec((1, MEM_HEAD_DIM), fixed),
            pl.BlockSpec((1, MEM_WIDTH), fixed),
            pl.BlockSpec((1, D_MODEL), fixed),
            pl.BlockSpec((D_MODEL, D_MODEL), fixed, pipeline_mode=pl.Buffered(1)),
        ],
        out_specs=[pl.BlockSpec((tm, D_MODEL), row), pl.BlockSpec((tm, D_MODEL), row)],
        out_shape=[jax.ShapeDtypeStruct((t, D_MODEL), F32),
                   jax.ShapeDtypeStruct((t, D_MODEL), BF16)],
        compiler_params=pltpu.CompilerParams(
            dimension_semantics=("parallel",),
            vmem_limit_bytes=V7X_VMEM_LIMIT_BYTES),
        name="out_proj",
    )(x2, yc, yd, qm, km3, vm3, g_mq, g_mem_out, g_ffn, w_o)


def _ffn_kernel(hf_ref, x1_ref, wgu_ref, wd_ref, o_ref):
    @pl.when(pl.program_id(1) == 0)
    def _():
        o_ref[...] = x1_ref[...]

    th = wd_ref.shape[0]
    gu = jnp.dot(hf_ref[...], wgu_ref[0], preferred_element_type=F32)
    gate, up = gu[:, :th], gu[:, th:]
    act = (gate * (1.0 / (1.0 + jnp.exp(-gate))) * up).astype(BF16)
    o_ref[...] += jnp.dot(act, wd_ref[...], preferred_element_type=F32)


def _ffn(hf, x1, w_gate_up, w_down):
    t = x1.shape[0]
    tm, th = TM_FFN, TH_FFN
    return pl.pallas_call(
        _ffn_kernel,
        grid=(t // tm, FFN_HIDDEN // th),
        in_specs=[
            pl.BlockSpec((tm, D_MODEL), lambda i, j: (i, 0)),
            pl.BlockSpec((tm, D_MODEL), lambda i, j: (i, 0)),
            pl.BlockSpec((1, D_MODEL, 2 * th), lambda i, j: (j, 0, 0)),
            pl.BlockSpec((th, D_MODEL), lambda i, j: (j, 0)),
        ],
        out_specs=pl.BlockSpec((tm, D_MODEL), lambda i, j: (i, 0)),
        out_shape=jax.ShapeDtypeStruct((t, D_MODEL), F32),
        compiler_params=pltpu.CompilerParams(
            dimension_semantics=("parallel", "arbitrary"),
            vmem_limit_bytes=V7X_VMEM_LIMIT_BYTES),
        name="ffn",
    )(hf, x1, w_gate_up, w_down)


def _rope_lane_freq():
    inv_freq = ROPE_THETA ** (-jnp.arange(0, ROT_DIM, 2, dtype=F32) / ROT_DIM)
    pad = jnp.zeros((DIFF_HALF - ROT_DIM,), F32)
    return jnp.concatenate([inv_freq, inv_freq, pad, inv_freq, inv_freq, pad]).reshape(1, -1)


def kernel(x, mem, positions, g_mix, g_mem, w_in, conv_w, g_conv_out, g_dq, g_dk,
           lam_q1, lam_k1, lam_q2, lam_k2, g_sub, w_mem_kv, g_mq, g_mk, g_mem_out,
           w_o, g_ffn, w_gate, w_up, w_down):
    b, s, d = x.shape
    assert (s, d) == (SEQ, D_MODEL) and mem.shape == (b, MEM_LEN, D_MODEL)
    assert g_mix.shape[0] == 1
    t = b * s
    x2 = x.reshape(t, d)
    mem2 = mem.reshape(b * MEM_LEN, d)
    yc, qt, kn, vt, qm, wkv16 = _in_proj(x2, g_mix, w_in[0], conv_w[0], g_conv_out,
                                  positions.reshape(t // TM_PROJ, 1, TM_PROJ), _rope_lane_freq(),
                                  jnp.tile(g_dq, (1, 2)), jnp.tile(g_dk, (1, 2)), w_mem_kv[0])
    km, vm = _mem_kv(mem2, g_mem, wkv16, g_mk)

    score_bound = (DIFF_HALF ** 0.5 * LOG2_E) * jnp.max(jnp.abs(g_dq)) * jnp.max(jnp.abs(g_dk))
    bounded = (score_bound <= MAX_SHIFT_FREE_LOG2_SCORE).astype(jnp.int32).reshape(1)
    yd, wgu16, wd16, wo16 = _diff_attn(bounded, qt, kn.reshape(b, s, DIFF_WIDTH), vt,
                                       lam_q1, lam_k1, lam_q2, lam_k2, g_sub,
                                       w_gate[0], w_up[0], w_down[0], w_o[0])
    x1, hf = _out_proj(x2, yc, yd.reshape(t, DIFF_WIDTH), qm,
                       km.reshape(b, MEM_LEN, MEM_WIDTH), vm.reshape(b, MEM_LEN, MEM_WIDTH),
                       g_mq, g_mem_out, g_ffn, wo16)
    out = _ffn(hf, x1, wgu16, wd16)
    return out.reshape(b, s, d)
```

```python
import functools

import jax
import jax.numpy as jnp
import numpy as np
from jax import lax
from jax.experimental import pallas as pl
from jax.experimental.pallas import tpu as pltpu

F32 = jnp.float32
BF16 = jnp.bfloat16

D_MODEL = 2048
SEQ = 2048
MEM_LEN = 256
CONV_WIDTH = 512
CONV_K = 3
DIFF_WIDTH = 1024
DIFF_VDIM = 128
DIFF_HALF = 64
DIFF_HEADS = 8
MEM_WIDTH = 512
MEM_HEADS = 4
MEM_HEAD_DIM = 128
IN_WIDTH = 3 * CONV_WIDTH + 3 * DIFF_WIDTH + MEM_WIDTH
ROT_DIM = 16
ROPE_THETA = 500000.0
FFN_HIDDEN = 5632
EPS = 1e-6
LAMBDA_INIT = 0.8 - 0.6 * float(np.exp(-0.3 * 0))
LOG2_E = float(np.log2(np.e))
MAX_SHIFT_FREE_LOG2_SCORE = 60.0

V7X_LANES = 128
V7X_SUBLANES = 8
V7X_BF16_SUBLANES = 16
V7X_VMEM_LIMIT_BYTES = 60 * 1024 * 1024

TM_PROJ = 256
TM_MEMKV = 256
TM_OUT = 512
TM_FFN = 1024
TH_FFN = 512
WEIGHT_CHUNK_ROWS = 128
SCORE_LOOKAHEAD = 2
ATTN_HEADS_PER_STEP = 2

_OFF_U = 0
_OFF_C = CONV_WIDTH
_OFF_B = 2 * CONV_WIDTH
_OFF_Q = 3 * CONV_WIDTH
_OFF_K = _OFF_Q + DIFF_WIDTH
_OFF_V = _OFF_K + DIFF_WIDTH
_OFF_QM = _OFF_V + DIFF_WIDTH


def _rms_scale(t, width):
    return lax.rsqrt(jnp.sum(t * t, axis=-1, keepdims=True) * (1.0 / width) + EPS)


def _load_weight_as_bf16(w_hbm, w_scr, stage, sem):
    rows = stage.shape[1]
    n_chunks = w_hbm.shape[0] // rows

    def chunk_copy(c):
        return pltpu.make_async_copy(w_hbm.at[pl.ds(c * rows, rows), :], stage.at[c % 2],
                                     sem.at[c % 2])

    chunk_copy(0).start()
    for c in range(n_chunks):
        if c + 1 < n_chunks:
            chunk_copy(c + 1).start()
        chunk_copy(c).wait()
        w_scr[c * rows:(c + 1) * rows, :] = stage[c % 2].astype(BF16)


def _resident_weight_scratch(k, n):
    return [pltpu.VMEM((k, n), BF16), pltpu.VMEM((2, WEIGHT_CHUNK_ROWS, n), F32),
            pltpu.SemaphoreType.DMA((2,))]


def _qk_norm_rope(t, g2, cos_t, sin_a, sin_b):
    lane = lax.broadcasted_iota(jnp.int32, t.shape, 1)
    is_lo = lane < DIFF_HALF
    t2 = t * t
    s_lo = jnp.sum(jnp.where(is_lo, t2, 0.0), axis=-1, keepdims=True)
    s_hi = jnp.sum(jnp.where(is_lo, 0.0, t2), axis=-1, keepdims=True)
    r = jnp.where(is_lo,
                  lax.rsqrt(s_lo * (1.0 / DIFF_HALF) + EPS),
                  lax.rsqrt(s_hi * (1.0 / DIFF_HALF) + EPS))
    tn = t * r * g2
    half = ROT_DIM // 2
    return (tn * cos_t
            + pltpu.roll(tn, V7X_LANES - half, axis=1) * sin_a
            + pltpu.roll(tn, half, axis=1) * sin_b)


def _in_proj_kernel(x_ref, g_ref, w_hbm, convw_ref, gconv_ref, pos_ref, freq_ref,
                    gq_ref, gk_ref, wkv_ref,
                    yconv_ref, qt_ref, kn_ref, vt_ref, qm_ref, wkv_out,
                    z_scr, w_ref, w_stage, w_sem, *, tiles_per_seq):
    i = pl.program_id(0)
    tm = x_ref.shape[0]
    halo = V7X_SUBLANES

    @pl.when(i == 0)
    def _():
        _load_weight_as_bf16(w_hbm, w_ref, w_stage, w_sem)

    @pl.when(i % tiles_per_seq == 0)
    def _():
        z_scr[0:halo, :] = jnp.zeros((halo, CONV_WIDTH), F32)

    wkv_out[...] = wkv_ref[...].astype(BF16)

    x = x_ref[...]
    h = (x * _rms_scale(x, D_MODEL) * g_ref[...]).astype(BF16)

    def proj(off, width):
        return jnp.dot(h, w_ref[:, off:off + width], preferred_element_type=F32)

    z = proj(_OFF_C, CONV_WIDTH) * proj(_OFF_U, CONV_WIDTH)
    z_scr[halo:halo + tm, :] = z
    z1 = z_scr[halo - 1:halo - 1 + tm, :]
    z2 = z_scr[halo - 2:halo - 2 + tm, :]
    cw = convw_ref[...]
    conv = cw[0:1, :] * z2 + cw[1:2, :] * z1 + cw[2:3, :] * z
    y = proj(_OFF_B, CONV_WIDTH) * conv
    yconv_ref[...] = (y * _rms_scale(y, CONV_WIDTH) * gconv_ref[...]).astype(BF16)
    z_scr[0:halo, :] = z_scr[tm:tm + halo, :]

    qf = proj(_OFF_Q, DIFF_WIDTH)
    kf = proj(_OFF_K, DIFF_WIDTH)
    vf = proj(_OFF_V, DIFF_WIDTH)
    pos = jnp.broadcast_to(pos_ref[0].astype(F32), (V7X_LANES, tm)).T
    ang = pos * freq_ref[...]
    cos_t, sin_t = jnp.cos(ang), jnp.sin(ang)
    pos_in_half = lax.broadcasted_iota(jnp.int32, sin_t.shape, 1) % DIFF_HALF
    sin_a = jnp.where(pos_in_half < ROT_DIM // 2, -sin_t, 0.0)
    sin_b = jnp.where((pos_in_half >= ROT_DIM // 2) & (pos_in_half < ROT_DIM), sin_t, 0.0)
    gq, gk = gq_ref[...], gk_ref[...]
    for hd in range(DIFF_HEADS):
        cols = slice(hd * DIFF_VDIM, (hd + 1) * DIFF_VDIM)
        qn = _qk_norm_rope(qf[:, cols], gq, cos_t, sin_a, sin_b) * (DIFF_HALF ** -0.5 * LOG2_E)
        qt_ref[0, 0, cols, :] = qn.T.astype(BF16)
        kn_ref[:, cols] = _qk_norm_rope(kf[:, cols], gk, cos_t, sin_a, sin_b).astype(BF16)
        vt_ref[0, 0, cols, :] = vf[:, cols].T.astype(BF16)
    qm_ref[...] = proj(_OFF_QM, MEM_WIDTH)


def _in_proj(x2, g_mix, w_in, conv_w, g_conv, pos, lane_freq, gq2, gk2, w_kv):
    t = x2.shape[0]
    tm = TM_PROJ
    tps = SEQ // tm
    nb = t // SEQ
    grid = (t // tm,)
    row = lambda i: (i, 0)
    fixed = lambda i: (0, 0)
    tile_t = lambda i: (i // tps, i % tps, 0, 0)
    kv_rows = w_kv.shape[0] // grid[0]
    assert kv_rows % V7X_BF16_SUBLANES == 0
    kv_blk = pl.BlockSpec((kv_rows, w_kv.shape[1]), row)
    return pl.pallas_call(
        functools.partial(_in_proj_kernel, tiles_per_seq=tps),
        grid=grid,
        in_specs=[
            pl.BlockSpec((tm, D_MODEL), row),
            pl.BlockSpec((1, D_MODEL), fixed),
            pl.BlockSpec(memory_space=pl.ANY),
            pl.BlockSpec((CONV_K, CONV_WIDTH), fixed),
            pl.BlockSpec((1, CONV_WIDTH), fixed),
            pl.BlockSpec((1, 1, tm), lambda i: (i, 0, 0)),
            pl.BlockSpec((1, V7X_LANES), fixed),
            pl.BlockSpec((1, DIFF_VDIM), fixed),
            pl.BlockSpec((1, DIFF_VDIM), fixed),
            kv_blk,
        ],
        out_specs=[
            pl.BlockSpec((tm, CONV_WIDTH), row),
            pl.BlockSpec((1, 1, DIFF_WIDTH, tm), tile_t),
            pl.BlockSpec((tm, DIFF_WIDTH), row),
            pl.BlockSpec((1, 1, DIFF_WIDTH, tm), tile_t),
            pl.BlockSpec((tm, MEM_WIDTH), row),
            kv_blk,
        ],
        out_shape=[
            jax.ShapeDtypeStruct((t, CONV_WIDTH), BF16),
            jax.ShapeDtypeStruct((nb, tps, DIFF_WIDTH, tm), BF16),
            jax.ShapeDtypeStruct((t, DIFF_WIDTH), BF16),
            jax.ShapeDtypeStruct((nb, tps, DIFF_WIDTH, tm), BF16),
            jax.ShapeDtypeStruct((t, MEM_WIDTH), F32),
            jax.ShapeDtypeStruct(w_kv.shape, BF16),
        ],
        scratch_shapes=[pltpu.VMEM((tm + V7X_SUBLANES, CONV_WIDTH), F32)]
        + _resident_weight_scratch(D_MODEL, IN_WIDTH),
        compiler_params=pltpu.CompilerParams(
            dimension_semantics=("arbitrary",),
            vmem_limit_bytes=V7X_VMEM_LIMIT_BYTES),
        name="in_proj",
    )(x2, g_mix, w_in, conv_w, g_conv, pos, lane_freq, gq2, gk2, w_kv)


def _mem_kv_kernel(mem_ref, g_ref, w_ref, gk_ref, k_ref, v_ref):
    x = mem_ref[...]
    h = (x * _rms_scale(x, D_MODEL) * g_ref[...]).astype(BF16)
    kv = jnp.dot(h, w_ref[...], preferred_element_type=F32)
    gk = gk_ref[...]
    for hd in range(MEM_HEADS):
        lo = hd * MEM_HEAD_DIM
        kh = kv[:, lo:lo + MEM_HEAD_DIM]
        k_ref[:, lo:lo + MEM_HEAD_DIM] = (kh * _rms_scale(kh, MEM_HEAD_DIM) * gk).astype(BF16)
    v_ref[...] = kv[:, MEM_WIDTH:].astype(BF16)


def _mem_kv(mem2, g_mem, w_kv, g_mk):
    t = mem2.shape[0]
    tm = TM_MEMKV
    row = lambda i: (i, 0)
    fixed = lambda i: (0, 0)
    return pl.pallas_call(
        _mem_kv_kernel,
        grid=(t // tm,),
        in_specs=[
            pl.BlockSpec((tm, D_MODEL), row),
            pl.BlockSpec((1, D_MODEL), fixed),
            pl.BlockSpec((D_MODEL, 2 * MEM_WIDTH), fixed),
            pl.BlockSpec((1, MEM_HEAD_DIM), fixed),
        ],
        out_specs=[pl.BlockSpec((tm, MEM_WIDTH), row), pl.BlockSpec((tm, MEM_WIDTH), row)],
        out_shape=[jax.ShapeDtypeStruct((t, MEM_WIDTH), BF16),
                   jax.ShapeDtypeStruct((t, MEM_WIDTH), BF16)],
        compiler_params=pltpu.CompilerParams(
            dimension_semantics=("parallel",),
            vmem_limit_bytes=V7X_VMEM_LIMIT_BYTES),
        name="mem_kv",
    )(mem2, g_mem, w_kv, g_mk)


def _diff_attn_kernel(bounded_ref, qt_ref, kn_ref, vt_ref, lq1_ref, lk1_ref, lq2_ref, lk2_ref,
                      gsub_ref, wg_ref, wu_ref, wd_ref, wo_ref, o_ref, wgu_out, wd_out, wo_out):
    nq, tq = qt_ref.shape[1], qt_ref.shape[3]
    tk = vt_ref.shape[3]

    lam = (jnp.exp(jnp.sum(lq1_ref[...] * lk1_ref[...], axis=-1, keepdims=True))
           - jnp.exp(jnp.sum(lq2_ref[...] * lk2_ref[...], axis=-1, keepdims=True))
           + LAMBDA_INIT)
    key = lax.broadcasted_iota(jnp.int32, (tk, tq), 0)
    qry = lax.broadcasted_iota(jnp.int32, (tk, tq), 1)
    causal = jnp.concatenate([key <= qry, key <= qry], axis=1)
    sum_rows = (lax.broadcasted_iota(jnp.int32, (V7X_BF16_SUBLANES, tk), 0) == 0).astype(BF16)

    def head(hd):
        return slice(hd * DIFF_VDIM, (hd + 1) * DIFF_VDIM)

    def scores(hd, c, j):
        qt = qt_ref[0, c, head(hd), :]
        is_lo = lax.broadcasted_iota(jnp.int32, qt.shape, 0) < DIFF_HALF
        zero = jnp.zeros_like(qt)
        qq = jnp.concatenate([jnp.where(is_lo, qt, zero), jnp.where(is_lo, zero, qt)], axis=1)
        return jnp.dot(kn_ref[0, j * tk:(j + 1) * tk, head(hd)], qq,
                       preferred_element_type=F32)

    steps = [(hd, c, j) for hd in range(ATTN_HEADS_PER_STEP)
             for c in range(nq) for j in range(c + 1)]

    def attend(shift_free):
        th = wgu_out.shape[2] // 2
        for ch in range(wgu_out.shape[0]):
            cols = slice(ch * th, (ch + 1) * th)
            wgu_out[ch, :, :th] = wg_ref[:, cols].astype(BF16)
            wgu_out[ch, :, th:] = wu_ref[:, cols].astype(BF16)
        wd_out[...] = wd_ref[...].astype(BF16)
        wo_out[...] = wo_ref[...].astype(BF16)
        pending = [scores(*st) for st in steps[:SCORE_LOOKAHEAD]]
        m = acc = None
        for n, (hd, c, j) in enumerate(steps):
            s = pending.pop(0)
            if n + SCORE_LOOKAHEAD < len(steps):
                pending.append(scores(*steps[n + SCORE_LOOKAHEAD]))
            if j == c:
                s = jnp.where(causal, s, -jnp.inf)
            vt_ext = jnp.concatenate([vt_ref[0, j, head(hd), :], sum_rows], axis=0)
            if shift_free:
                pv = jnp.dot(vt_ext, jnp.exp2(s).astype(BF16), preferred_element_type=F32)
                acc = pv if j == 0 else acc + pv
            else:
                m_blk = jnp.max(s, axis=0, keepdims=True)
                m_new = m_blk if j == 0 else jnp.maximum(m, m_blk)
                p = jnp.exp2(s - m_new)
                pv = jnp.dot(vt_ext, p.astype(BF16), preferred_element_type=F32)
                acc = pv if j == 0 else jnp.exp2(m - m_new) * acc + pv
                m = m_new
            if j == c:
                o_all = acc[:DIFF_VDIM] / acc[DIFF_VDIM:DIFF_VDIM + 1]
                o = (o_all[:, :tq] - lam * o_all[:, tq:]).T
                y = o * _rms_scale(o, DIFF_VDIM) * gsub_ref[...] * (1.0 - LAMBDA_INIT)
                o_ref[0, c * tq:(c + 1) * tq, head(hd)] = y.astype(o_ref.dtype)

    pl.when(bounded_ref[0] != 0)(functools.partial(attend, True))
    pl.when(bounded_ref[0] == 0)(functools.partial(attend, False))


def _diff_attn(bounded, qt4, kn3, vt4, lq1, lk1, lq2, lk2, g_sub, w_gate, w_up, w_down, w_o):
    b, nq, _, tq = qt4.shape
    s = kn3.shape[1]
    nk, tk = vt4.shape[1], vt4.shape[3]
    assert tq == tk and nq == nk
    fixed = lambda bb, h: (0, 0)
    vec = lambda n: pl.BlockSpec((1, n), fixed)
    groups = DIFF_HEADS // ATTN_HEADS_PER_STEP
    width = ATTN_HEADS_PER_STEP * DIFF_VDIM
    steps = b * groups
    nh = FFN_HIDDEN // TH_FFN
    gu_rows, down_rows, wo_rows = D_MODEL // steps, FFN_HIDDEN // steps, w_o.shape[0] // steps
    assert all(r % V7X_BF16_SUBLANES == 0 for r in (gu_rows, down_rows, wo_rows))
    step = lambda bb, g: bb * groups + g
    gu_in = pl.BlockSpec((gu_rows, FFN_HIDDEN), lambda bb, g: (step(bb, g), 0))
    gu_out = pl.BlockSpec((nh, gu_rows, 2 * TH_FFN), lambda bb, g: (0, step(bb, g), 0))
    down_blk = pl.BlockSpec((down_rows, D_MODEL), lambda bb, g: (step(bb, g), 0))
    wo_blk = pl.BlockSpec((wo_rows, w_o.shape[1]), lambda bb, g: (step(bb, g), 0))
    return pl.pallas_call(
        _diff_attn_kernel,
        grid=(b, groups),
        in_specs=[
            pl.BlockSpec(memory_space=pltpu.SMEM),
            pl.BlockSpec((1, nq, width, tq), lambda bb, g: (bb, 0, g, 0)),
            pl.BlockSpec((1, s, width), lambda bb, g: (bb, 0, g)),
            pl.BlockSpec((1, nk, width, tk), lambda bb, g: (bb, 0, g, 0)),
            vec(DIFF_HALF), vec(DIFF_HALF), vec(DIFF_HALF), vec(DIFF_HALF),
            vec(DIFF_VDIM),
            gu_in, gu_in, down_blk, wo_blk,
        ],
        out_specs=[pl.BlockSpec((1, s, width), lambda bb, g: (bb, 0, g)), gu_out, down_blk,
                   wo_blk],
        out_shape=[jax.ShapeDtypeStruct((b, s, DIFF_WIDTH), BF16),
                   jax.ShapeDtypeStruct((nh, D_MODEL, 2 * TH_FFN), BF16),
                   jax.ShapeDtypeStruct((FFN_HIDDEN, D_MODEL), BF16),
                   jax.ShapeDtypeStruct(w_o.shape, BF16)],
        compiler_params=pltpu.CompilerParams(
            dimension_semantics=("parallel", "parallel"),
            vmem_limit_bytes=V7X_VMEM_LIMIT_BYTES),
        name="diff_attn",
    )(bounded, qt4, kn3, vt4, lq1, lk1, lq2, lk2, g_sub, w_gate, w_up, w_down, w_o)


def _out_proj_kernel(x_ref, yc_ref, yd_ref, qm_ref, km_ref, vm_ref, gq_ref, gmo_ref, gffn_ref,
                     w_ref, x1_ref, hf_ref):
    gq = gq_ref[...]
    probs, denoms = [], []
    for hd in range(MEM_HEADS):
        cols = slice(hd * MEM_HEAD_DIM, (hd + 1) * MEM_HEAD_DIM)
        qh = qm_ref[:, cols]
        qh = (qh * _rms_scale(qh, MEM_HEAD_DIM) * gq * (MEM_HEAD_DIM ** -0.5)).astype(BF16)
        s = lax.dot_general(qh, km_ref[0, :, cols], (((1,), (1,)), ((), ())),
                            preferred_element_type=F32)
        e = jnp.exp(s - jnp.max(s, axis=-1, keepdims=True))
        denoms.append(jnp.sum(e, axis=-1, keepdims=True))
        probs.append(e.astype(BF16))

    acc = jnp.dot(yc_ref[...], w_ref[0:CONV_WIDTH, :], preferred_element_type=F32)

    outs = []
    for hd in range(MEM_HEADS):
        cols = slice(hd * MEM_HEAD_DIM, (hd + 1) * MEM_HEAD_DIM)
        oh = jnp.dot(probs[hd], vm_ref[0, :, cols], preferred_element_type=F32)
        outs.append(oh / denoms[hd])
    o = jnp.concatenate(outs, axis=-1)
    ym = (o * _rms_scale(o, MEM_WIDTH) * gmo_ref[...]).astype(BF16)

    acc += jnp.dot(yd_ref[...], w_ref[CONV_WIDTH:CONV_WIDTH + DIFF_WIDTH, :],
                   preferred_element_type=F32)
    acc += jnp.dot(ym, w_ref[CONV_WIDTH + DIFF_WIDTH:, :], preferred_element_type=F32)
    x1 = x_ref[...] + acc
    x1_ref[...] = x1
    hf_ref[...] = (x1 * _rms_scale(x1, D_MODEL) * gffn_ref[...]).astype(BF16)


def _out_proj(x2, yc, yd, qm, km3, vm3, g_mq, g_mem_out, g_ffn, w_o):
    t = x2.shape[0]
    tm = TM_OUT
    tps = SEQ // tm
    row = lambda i: (i, 0)
    fixed = lambda i: (0, 0)
    mem_blk = lambda i: (i // tps, 0, 0)
    return pl.pallas_call(
        _out_proj_kernel,
        grid=(t // tm,),
        in_specs=[
            pl.BlockSpec((tm, D_MODEL), row),
            pl.BlockSpec((tm, CONV_WIDTH), row),
            pl.BlockSpec((tm, DIFF_WIDTH), row),
            pl.BlockSpec((tm, MEM_WIDTH), row),
            pl.BlockSpec((1, MEM_LEN, MEM_WIDTH), mem_blk),
            pl.BlockSpec((1, MEM_LEN, MEM_WIDTH), mem_blk),
            pl.BlockSpec((1, MEM_HEAD_DIM), fixed),
            pl.BlockSpec((1, MEM_WIDTH), fixed),
            pl.BlockSpec((1, D_MODEL), fixed),
            pl.BlockSpec((D_MODEL, D_MODEL), fixed, pipeline_mode=pl.Buffered(1)),
        ],
        out_specs=[pl.BlockSpec((tm, D_MODEL), row), pl.BlockSpec((tm, D_MODEL), row)],
        out_shape=[jax.ShapeDtypeStruct((t, D_MODEL), F32),
                   jax.ShapeDtypeStruct((t, D_MODEL), BF16)],
        compiler_params=pltpu.CompilerParams(
            dimension_semantics=("parallel",),
            vmem_limit_bytes=V7X_VMEM_LIMIT_BYTES),
        name="out_proj",
    )(x2, yc, yd, qm, km3, vm3, g_mq, g_mem_out, g_ffn, w_o)


def _ffn_kernel(hf_ref, x1_ref, wgu_ref, wd_ref, o_ref):
    @pl.when(pl.program_id(1) == 0)
    def _():
        o_ref[...] = x1_ref[...]

    th = wd_ref.shape[0]
    gu = jnp.dot(hf_ref[...], wgu_ref[0], preferred_element_type=F32)
    gate, up = gu[:, :th], gu[:, th:]
    act = (gate * (1.0 / (1.0 + jnp.exp(-gate))) * up).astype(BF16)
    o_ref[...] += jnp.dot(act, wd_ref[...], preferred_element_type=F32)


def _ffn(hf, x1, w_gate_up, w_down):
    t = x1.shape[0]
    tm, th = TM_FFN, TH_FFN
    return pl.pallas_call(
        _ffn_kernel,
        grid=(t // tm, FFN_HIDDEN // th),
        in_specs=[
            pl.BlockSpec((tm, D_MODEL), lambda i, j: (i, 0)),
            pl.BlockSpec((tm, D_MODEL), lambda i, j: (i, 0)),
            pl.BlockSpec((1, D_MODEL, 2 * th), lambda i, j: (j, 0, 0)),
            pl.BlockSpec((th, D_MODEL), lambda i, j: (j, 0)),
        ],
        out_specs=pl.BlockSpec((tm, D_MODEL), lambda i, j: (i, 0)),
        out_shape=jax.ShapeDtypeStruct((t, D_MODEL), F32),
        compiler_params=pltpu.CompilerParams(
            dimension_semantics=("parallel", "arbitrary"),
            vmem_limit_bytes=V7X_VMEM_LIMIT_BYTES),
        name="ffn",
    )(hf, x1, w_gate_up, w_down)


def _rope_lane_freq():
    inv_freq = ROPE_THETA ** (-jnp.arange(0, ROT_DIM, 2, dtype=F32) / ROT_DIM)
    pad = jnp.zeros((DIFF_HALF - ROT_DIM,), F32)
    return jnp.concatenate([inv_freq, inv_freq, pad, inv_freq, inv_freq, pad]).reshape(1, -1)


def kernel(x, mem, positions, g_mix, g_mem, w_in, conv_w, g_conv_out, g_dq, g_dk,
           lam_q1, lam_k1, lam_q2, lam_k2, g_sub, w_mem_kv, g_mq, g_mk, g_mem_out,
           w_o, g_ffn, w_gate, w_up, w_down):
    b, s, d = x.shape
    assert (s, d) == (SEQ, D_MODEL) and mem.shape == (b, MEM_LEN, D_MODEL)
    assert g_mix.shape[0] == 1
    t = b * s
    x2 = x.reshape(t, d)
    mem2 = mem.reshape(b * MEM_LEN, d)
    yc, qt, kn, vt, qm, wkv16 = _in_proj(x2, g_mix, w_in[0], conv_w[0], g_conv_out,
                                  positions.reshape(t // TM_PROJ, 1, TM_PROJ), _rope_lane_freq(),
                                  jnp.tile(g_dq, (1, 2)), jnp.tile(g_dk, (1, 2)), w_mem_kv[0])
    km, vm = _mem_kv(mem2, g_mem, wkv16, g_mk)

    score_bound = (DIFF_HALF ** 0.5 * LOG2_E) * jnp.max(jnp.abs(g_dq)) * jnp.max(jnp.abs(g_dk))
    bounded = (score_bound <= MAX_SHIFT_FREE_LOG2_SCORE).astype(jnp.int32).reshape(1)
    yd, wgu16, wd16, wo16 = _diff_attn(bounded, qt, kn.reshape(b, s, DIFF_WIDTH), vt,
                                       lam_q1, lam_k1, lam_q2, lam_k2, g_sub,
                                       w_gate[0], w_up[0], w_down[0], w_o[0])
    x1, hf = _out_proj(x2, yc, yd.reshape(t, DIFF_WIDTH), qm,
                       km.reshape(b, MEM_LEN, MEM_WIDTH), vm.reshape(b, MEM_LEN, MEM_WIDTH),
                       g_mq, g_mem_out, g_ffn, wo16)
    out = _ffn(hf, x1, wgu16, wd16)
    return out.reshape(b, s, d)
```

```python
import functools

import jax
import jax.numpy as jnp
import numpy as np
from jax import lax
from jax.experimental import pallas as pl
from jax.experimental.pallas import tpu as pltpu

F32 = jnp.float32
BF16 = jnp.bfloat16

D_MODEL = 2048
SEQ = 2048
MEM_LEN = 256
CONV_WIDTH = 512
CONV_K = 3
DIFF_WIDTH = 1024
DIFF_VDIM = 128
DIFF_HALF = 64
DIFF_HEADS = 8
MEM_WIDTH = 512
MEM_HEADS = 4
MEM_HEAD_DIM = 128
IN_WIDTH = 3 * CONV_WIDTH + 3 * DIFF_WIDTH + MEM_WIDTH
ROT_DIM = 16
ROPE_THETA = 500000.0
FFN_HIDDEN = 5632
EPS = 1e-6
LAMBDA_INIT = 0.8 - 0.6 * float(np.exp(-0.3 * 0))
LOG2_E = float(np.log2(np.e))
MAX_SHIFT_FREE_LOG2_SCORE = 60.0

V7X_LANES = 128
V7X_SUBLANES = 8
V7X_BF16_SUBLANES = 16
V7X_VMEM_LIMIT_BYTES = 60 * 1024 * 1024

TM_PROJ = 512
T_ATTN = 256
TM_MEMKV = 256
TM_OUT = 512
TM_FFN = 1024
TH_FFN = 512
WEIGHT_CHUNK_ROWS = 128
SCORE_LOOKAHEAD = 2
ATTN_HEADS_PER_STEP = 2

_OFF_U = 0
_OFF_C = CONV_WIDTH
_OFF_B = 2 * CONV_WIDTH
_OFF_Q = 3 * CONV_WIDTH
_OFF_K = _OFF_Q + DIFF_WIDTH
_OFF_V = _OFF_K + DIFF_WIDTH
_OFF_QM = _OFF_V + DIFF_WIDTH


def _rms_scale(t, width):
    return lax.rsqrt(jnp.sum(t * t, axis=-1, keepdims=True) * (1.0 / width) + EPS)


def _load_weight_as_bf16(w_hbm, w_scr, stage, sem):
    rows = stage.shape[1]
    n_chunks = w_hbm.shape[0] // rows

    def chunk_copy(c):
        return pltpu.make_async_copy(w_hbm.at[pl.ds(c * rows, rows), :], stage.at[c % 2],
                                     sem.at[c % 2])

    chunk_copy(0).start()
    for c in range(n_chunks):
        if c + 1 < n_chunks:
            chunk_copy(c + 1).start()
        chunk_copy(c).wait()
        w_scr[c * rows:(c + 1) * rows, :] = stage[c % 2].astype(BF16)


def _resident_weight_scratch(k, n):
    return [pltpu.VMEM((k, n), BF16), pltpu.VMEM((2, WEIGHT_CHUNK_ROWS, n), F32),
            pltpu.SemaphoreType.DMA((2,))]


def _qk_norm_rope(t, g2, cos_t, sin_a, sin_b):
    lane = lax.broadcasted_iota(jnp.int32, t.shape, 1)
    is_lo = lane < DIFF_HALF
    t2 = t * t
    s_lo = jnp.sum(jnp.where(is_lo, t2, 0.0), axis=-1, keepdims=True)
    s_hi = jnp.sum(jnp.where(is_lo, 0.0, t2), axis=-1, keepdims=True)
    r = jnp.where(is_lo,
                  lax.rsqrt(s_lo * (1.0 / DIFF_HALF) + EPS),
                  lax.rsqrt(s_hi * (1.0 / DIFF_HALF) + EPS))
    tn = t * r * g2
    half = ROT_DIM // 2
    return (tn * cos_t
            + pltpu.roll(tn, V7X_LANES - half, axis=1) * sin_a
            + pltpu.roll(tn, half, axis=1) * sin_b)


def _in_proj_kernel(x_ref, g_ref, w_hbm, convw_ref, gconv_ref, pos_ref, freq_ref,
                    gq_ref, gk_ref, wkv_ref,
                    yconv_ref, qt_ref, kn_ref, vt_ref, qm_ref, wkv_out,
                    z_scr, w_ref, w_stage, w_sem, *, tiles_per_seq):
    i = pl.program_id(0)
    halo = V7X_SUBLANES

    @pl.when(i == 0)
    def _():
        _load_weight_as_bf16(w_hbm, w_ref, w_stage, w_sem)

    @pl.when(i % tiles_per_seq == 0)
    def _():
        z_scr[0:halo, :] = jnp.zeros((halo, CONV_WIDTH), F32)

    wkv_out[...] = wkv_ref[...].astype(BF16)

    x = x_ref[...]
    h = (x * _rms_scale(x, D_MODEL) * g_ref[...]).astype(BF16)

    def proj(off, width):
        return jnp.dot(h, w_ref[:, off:off + width], preferred_element_type=F32)

    ts = qt_ref.shape[3]
    subs = [slice(sub * ts, (sub + 1) * ts) for sub in range(qt_ref.shape[1])]

    z_all = proj(_OFF_C, CONV_WIDTH) * proj(_OFF_U, CONV_WIDTH)
    b_all = proj(_OFF_B, CONV_WIDTH)
    cw = convw_ref[...]
    for rows in subs:
        z = z_all[rows]
        z_scr[halo:halo + ts, :] = z
        z1 = z_scr[halo - 1:halo - 1 + ts, :]
        z2 = z_scr[halo - 2:halo - 2 + ts, :]
        conv = cw[0:1, :] * z2 + cw[1:2, :] * z1 + cw[2:3, :] * z
        y = b_all[rows] * conv
        yconv_ref[rows, :] = (y * _rms_scale(y, CONV_WIDTH) * gconv_ref[...]).astype(BF16)
        z_scr[0:halo, :] = z_scr[ts:ts + halo, :]

    qf = proj(_OFF_Q, DIFF_WIDTH)
    kf = proj(_OFF_K, DIFF_WIDTH)
    vf = proj(_OFF_V, DIFF_WIDTH)
    gq, gk = gq_ref[...], gk_ref[...]
    for sub, rows in enumerate(subs):
        pos = jnp.broadcast_to(pos_ref[0][:, rows].astype(F32), (V7X_LANES, ts)).T
        ang = pos * freq_ref[...]
        cos_t, sin_t = jnp.cos(ang), jnp.sin(ang)
        pos_in_half = lax.broadcasted_iota(jnp.int32, sin_t.shape, 1) % DIFF_HALF
        sin_a = jnp.where(pos_in_half < ROT_DIM // 2, -sin_t, 0.0)
        sin_b = jnp.where((pos_in_half >= ROT_DIM // 2) & (pos_in_half < ROT_DIM), sin_t, 0.0)
        for hd in range(DIFF_HEADS):
            cols = slice(hd * DIFF_VDIM, (hd + 1) * DIFF_VDIM)
            qn = (_qk_norm_rope(qf[rows, cols], gq, cos_t, sin_a, sin_b)
                  * (DIFF_HALF ** -0.5 * LOG2_E))
            qt_ref[0, sub, cols, :] = qn.T.astype(BF16)
            kn_ref[rows, cols] = _qk_norm_rope(kf[rows, cols], gk, cos_t, sin_a,
                                               sin_b).astype(BF16)
            vt_ref[0, sub, cols, :] = vf[rows, cols].T.astype(BF16)
    qm_ref[...] = proj(_OFF_QM, MEM_WIDTH)


def _in_proj(x2, g_mix, w_in, conv_w, g_conv, pos, lane_freq, gq2, gk2, w_kv):
    t = x2.shape[0]
    tm, ta = TM_PROJ, T_ATTN
    tps = SEQ // tm
    sub = tm // ta
    nb = t // SEQ
    grid = (t // tm,)
    row = lambda i: (i, 0)
    fixed = lambda i: (0, 0)
    tile_t = lambda i: (i // tps, i % tps, 0, 0)
    kv_rows = w_kv.shape[0] // grid[0]
    assert kv_rows % V7X_BF16_SUBLANES == 0
    kv_blk = pl.BlockSpec((kv_rows, w_kv.shape[1]), row)
    return pl.pallas_call(
        functools.partial(_in_proj_kernel, tiles_per_seq=tps),
        grid=grid,
        in_specs=[
            pl.BlockSpec((tm, D_MODEL), row),
            pl.BlockSpec((1, D_MODEL), fixed),
            pl.BlockSpec(memory_space=pl.ANY),
            pl.BlockSpec((CONV_K, CONV_WIDTH), fixed),
            pl.BlockSpec((1, CONV_WIDTH), fixed),
            pl.BlockSpec((1, 1, tm), lambda i: (i, 0, 0)),
            pl.BlockSpec((1, V7X_LANES), fixed),
            pl.BlockSpec((1, DIFF_VDIM), fixed),
            pl.BlockSpec((1, DIFF_VDIM), fixed),
            kv_blk,
        ],
        out_specs=[
            pl.BlockSpec((tm, CONV_WIDTH), row),
            pl.BlockSpec((1, sub, DIFF_WIDTH, ta), tile_t),
            pl.BlockSpec((tm, DIFF_WIDTH), row),
            pl.BlockSpec((1, sub, DIFF_WIDTH, ta), tile_t),
            pl.BlockSpec((tm, MEM_WIDTH), row),
            kv_blk,
        ],
        out_shape=[
            jax.ShapeDtypeStruct((t, CONV_WIDTH), BF16),
            jax.ShapeDtypeStruct((nb, SEQ // ta, DIFF_WIDTH, ta), BF16),
            jax.ShapeDtypeStruct((t, DIFF_WIDTH), BF16),
            jax.ShapeDtypeStruct((nb, SEQ // ta, DIFF_WIDTH, ta), BF16),
            jax.ShapeDtypeStruct((t, MEM_WIDTH), F32),
            jax.ShapeDtypeStruct(w_kv.shape, BF16),
        ],
        scratch_shapes=[pltpu.VMEM((ta + V7X_SUBLANES, CONV_WIDTH), F32)]
        + _resident_weight_scratch(D_MODEL, IN_WIDTH),
        compiler_params=pltpu.CompilerParams(
            dimension_semantics=("arbitrary",),
            vmem_limit_bytes=V7X_VMEM_LIMIT_BYTES),
        name="in_proj",
    )(x2, g_mix, w_in, conv_w, g_conv, pos, lane_freq, gq2, gk2, w_kv)


def _mem_kv_kernel(mem_ref, g_ref, w_ref, gk_ref, k_ref, v_ref):
    x = mem_ref[...]
    h = (x * _rms_scale(x, D_MODEL) * g_ref[...]).astype(BF16)
    kv = jnp.dot(h, w_ref[...], preferred_element_type=F32)
    gk = gk_ref[...]
    for hd in range(MEM_HEADS):
        lo = hd * MEM_HEAD_DIM
        kh = kv[:, lo:lo + MEM_HEAD_DIM]
        k_ref[:, lo:lo + MEM_HEAD_DIM] = (kh * _rms_scale(kh, MEM_HEAD_DIM) * gk).astype(BF16)
    v_ref[...] = kv[:, MEM_WIDTH:].astype(BF16)


def _mem_kv(mem2, g_mem, w_kv, g_mk):
    t = mem2.shape[0]
    tm = TM_MEMKV
    row = lambda i: (i, 0)
    fixed = lambda i: (0, 0)
    return pl.pallas_call(
        _mem_kv_kernel,
        grid=(t // tm,),
        in_specs=[
            pl.BlockSpec((tm, D_MODEL), row),
            pl.BlockSpec((1, D_MODEL), fixed),
            pl.BlockSpec((D_MODEL, 2 * MEM_WIDTH), fixed),
            pl.BlockSpec((1, MEM_HEAD_DIM), fixed),
        ],
        out_specs=[pl.BlockSpec((tm, MEM_WIDTH), row), pl.BlockSpec((tm, MEM_WIDTH), row)],
        out_shape=[jax.ShapeDtypeStruct((t, MEM_WIDTH), BF16),
                   jax.ShapeDtypeStruct((t, MEM_WIDTH), BF16)],
        compiler_params=pltpu.CompilerParams(
            dimension_semantics=("parallel",),
            vmem_limit_bytes=V7X_VMEM_LIMIT_BYTES),
        name="mem_kv",
    )(mem2, g_mem, w_kv, g_mk)


def _diff_attn_kernel(bounded_ref, qt_ref, kn_ref, vt_ref, lq1_ref, lk1_ref, lq2_ref, lk2_ref,
                      gsub_ref, wg_ref, wu_ref, wd_ref, wo_ref, o_ref, wgu_out, wd_out, wo_out):
    nq, tq = qt_ref.shape[1], qt_ref.shape[3]
    tk = vt_ref.shape[3]

    lam = (jnp.exp(jnp.sum(lq1_ref[...] * lk1_ref[...], axis=-1, keepdims=True))
           - jnp.exp(jnp.sum(lq2_ref[...] * lk2_ref[...], axis=-1, keepdims=True))
           + LAMBDA_INIT)
    key = lax.broadcasted_iota(jnp.int32, (tk, tq), 0)
    qry = lax.broadcasted_iota(jnp.int32, (tk, tq), 1)
    causal = jnp.concatenate([key <= qry, key <= qry], axis=1)
    sum_rows = (lax.broadcasted_iota(jnp.int32, (V7X_BF16_SUBLANES, tk), 0) == 0).astype(BF16)

    def head(hd):
        return slice(hd * DIFF_VDIM, (hd + 1) * DIFF_VDIM)

    def scores(hd, c, j):
        qt = qt_ref[0, c, head(hd), :]
        is_lo = lax.broadcasted_iota(jnp.int32, qt.shape, 0) < DIFF_HALF
        zero = jnp.zeros_like(qt)
        qq = jnp.concatenate([jnp.where(is_lo, qt, zero), jnp.where(is_lo, zero, qt)], axis=1)
        return jnp.dot(kn_ref[0, j * tk:(j + 1) * tk, head(hd)], qq,
                       preferred_element_type=F32)

    steps = [(hd, c, j) for hd in range(ATTN_HEADS_PER_STEP)
             for c in range(nq) for j in range(c + 1)]

    def attend(shift_free):
        th = wgu_out.shape[2] // 2
        for ch in range(wgu_out.shape[0]):
            cols = slice(ch * th, (ch + 1) * th)
            wgu_out[ch, :, :th] = wg_ref[:, cols].astype(BF16)
            wgu_out[ch, :, th:] = wu_ref[:, cols].astype(BF16)
        wd_out[...] = wd_ref[...].astype(BF16)
        wo_out[...] = wo_ref[...].astype(BF16)
        pending = [scores(*st) for st in steps[:SCORE_LOOKAHEAD]]
        m = acc = None
        for n, (hd, c, j) in enumerate(steps):
            s = pending.pop(0)
            if n + SCORE_LOOKAHEAD < len(steps):
                pending.append(scores(*steps[n + SCORE_LOOKAHEAD]))
            if j == c:
                s = jnp.where(causal, s, -jnp.inf)
            vt_ext = jnp.concatenate([vt_ref[0, j, head(hd), :], sum_rows], axis=0)
            if shift_free:
                pv = jnp.dot(vt_ext, jnp.exp2(s).astype(BF16), preferred_element_type=F32)
                acc = pv if j == 0 else acc + pv
            else:
                m_blk = jnp.max(s, axis=0, keepdims=True)
                m_new = m_blk if j == 0 else jnp.maximum(m, m_blk)
                p = jnp.exp2(s - m_new)
                pv = jnp.dot(vt_ext, p.astype(BF16), preferred_element_type=F32)
                acc = pv if j == 0 else jnp.exp2(m - m_new) * acc + pv
                m = m_new
            if j == c:
                o_all = acc[:DIFF_VDIM] / acc[DIFF_VDIM:DIFF_VDIM + 1]
                o = (o_all[:, :tq] - lam * o_all[:, tq:]).T
                y = o * _rms_scale(o, DIFF_VDIM) * gsub_ref[...] * (1.0 - LAMBDA_INIT)
                o_ref[0, c * tq:(c + 1) * tq, head(hd)] = y.astype(o_ref.dtype)

    pl.when(bounded_ref[0] != 0)(functools.partial(attend, True))
    pl.when(bounded_ref[0] == 0)(functools.partial(attend, False))


def _diff_attn(bounded, qt4, kn3, vt4, lq1, lk1, lq2, lk2, g_sub, w_gate, w_up, w_down, w_o):
    b, nq, _, tq = qt4.shape
    s = kn3.shape[1]
    nk, tk = vt4.shape[1], vt4.shape[3]
    assert tq == tk and nq == nk
    fixed = lambda bb, h: (0, 0)
    vec = lambda n: pl.BlockSpec((1, n), fixed)
    groups = DIFF_HEADS // ATTN_HEADS_PER_STEP
    width = ATTN_HEADS_PER_STEP * DIFF_VDIM
    steps = b * groups
    nh = FFN_HIDDEN // TH_FFN
    gu_rows, down_rows, wo_rows = D_MODEL // steps, FFN_HIDDEN // steps, w_o.shape[0] // steps
    assert all(r % V7X_BF16_SUBLANES == 0 for r in (gu_rows, down_rows, wo_rows))
    step = lambda bb, g: bb * groups + g
    gu_in = pl.BlockSpec((gu_rows, FFN_HIDDEN), lambda bb, g: (step(bb, g), 0))
    gu_out = pl.BlockSpec((nh, gu_rows, 2 * TH_FFN), lambda bb, g: (0, step(bb, g), 0))
    down_blk = pl.BlockSpec((down_rows, D_MODEL), lambda bb, g: (step(bb, g), 0))
    wo_blk = pl.BlockSpec((wo_rows, w_o.shape[1]), lambda bb, g: (step(bb, g), 0))
    return pl.pallas_call(
        _diff_attn_kernel,
        grid=(b, groups),
        in_specs=[
            pl.BlockSpec(memory_space=pltpu.SMEM),
            pl.BlockSpec((1, nq, width, tq), lambda bb, g: (bb, 0, g, 0)),
            pl.BlockSpec((1, s, width), lambda bb, g: (bb, 0, g)),
            pl.BlockSpec((1, nk, width, tk), lambda bb, g: (bb, 0, g, 0)),
            vec(DIFF_HALF), vec(DIFF_HALF), vec(DIFF_HALF), vec(DIFF_HALF),
            vec(DIFF_VDIM),
            gu_in, gu_in, down_blk, wo_blk,
        ],
        out_specs=[pl.BlockSpec((1, s, width), lambda bb, g: (bb, 0, g)), gu_out, down_blk,
                   wo_blk],
        out_shape=[jax.ShapeDtypeStruct((b, s, DIFF_WIDTH), BF16),
                   jax.ShapeDtypeStruct((nh, D_MODEL, 2 * TH_FFN), BF16),
                   jax.ShapeDtypeStruct((FFN_HIDDEN, D_MODEL), BF16),
                   jax.ShapeDtypeStruct(w_o.shape, BF16)],
        compiler_params=pltpu.CompilerParams(
            dimension_semantics=("parallel", "parallel"),
            vmem_limit_bytes=V7X_VMEM_LIMIT_BYTES),
        name="diff_attn",
    )(bounded, qt4, kn3, vt4, lq1, lk1, lq2, lk2, g_sub, w_gate, w_up, w_down, w_o)


def _out_proj_kernel(x_ref, yc_ref, yd_ref, qm_ref, km_ref, vm_ref, gq_ref, gmo_ref, gffn_ref,
                     w_ref, x1_ref, hf_ref):
    gq = gq_ref[...]
    probs, denoms = [], []
    for hd in range(MEM_HEADS):
        cols = slice(hd * MEM_HEAD_DIM, (hd + 1) * MEM_HEAD_DIM)
        qh = qm_ref[:, cols]
        qh = (qh * _rms_scale(qh, MEM_HEAD_DIM) * gq * (MEM_HEAD_DIM ** -0.5)).astype(BF16)
        s = lax.dot_general(qh, km_ref[0, :, cols], (((1,), (1,)), ((), ())),
                            preferred_element_type=F32)
        e = jnp.exp(s - jnp.max(s, axis=-1, keepdims=True))
        denoms.append(jnp.sum(e, axis=-1, keepdims=True))
        probs.append(e.astype(BF16))

    acc = jnp.dot(yc_ref[...], w_ref[0:CONV_WIDTH, :], preferred_element_type=F32)

    outs = []
    for hd in range(MEM_HEADS):
        cols = slice(hd * MEM_HEAD_DIM, (hd + 1) * MEM_HEAD_DIM)
        oh = jnp.dot(probs[hd], vm_ref[0, :, cols], preferred_element_type=F32)
        outs.append(oh / denoms[hd])
    o = jnp.concatenate(outs, axis=-1)
    ym = (o * _rms_scale(o, MEM_WIDTH) * gmo_ref[...]).astype(BF16)

    acc += jnp.dot(yd_ref[...], w_ref[CONV_WIDTH:CONV_WIDTH + DIFF_WIDTH, :],
                   preferred_element_type=F32)
    acc += jnp.dot(ym, w_ref[CONV_WIDTH + DIFF_WIDTH:, :], preferred_element_type=F32)
    x1 = x_ref[...] + acc
    x1_ref[...] = x1
    hf_ref[...] = (x1 * _rms_scale(x1, D_MODEL) * gffn_ref[...]).astype(BF16)


def _out_proj(x2, yc, yd, qm, km3, vm3, g_mq, g_mem_out, g_ffn, w_o):
    t = x2.shape[0]
    tm = TM_OUT
    tps = SEQ // tm
    row = lambda i: (i, 0)
    fixed = lambda i: (0, 0)
    mem_blk = lambda i: (i // tps, 0, 0)
    return pl.pallas_call(
        _out_proj_kernel,
        grid=(t // tm,),
        in_specs=[
            pl.BlockSpec((tm, D_MODEL), row),
            pl.BlockSpec((tm, CONV_WIDTH), row),
            pl.BlockSpec((tm, DIFF_WIDTH), row),
            pl.BlockSpec((tm, MEM_WIDTH), row),
            pl.BlockSpec((1, MEM_LEN, MEM_WIDTH), mem_blk),
            pl.BlockSpec((1, MEM_LEN, MEM_WIDTH), mem_blk),
            pl.BlockSpec((1, MEM_HEAD_DIM), fixed),
            pl.BlockSpec((1, MEM_WIDTH), fixed),
            pl.BlockSpec((1, D_MODEL), fixed),
            pl.BlockSpec((D_MODEL, D_MODEL), fixed, pipeline_mode=pl.Buffered(1)),
        ],
        out_specs=[pl.BlockSpec((tm, D_MODEL), row), pl.BlockSpec((tm, D_MODEL), row)],
        out_shape=[jax.ShapeDtypeStruct((t, D_MODEL), F32),
                   jax.ShapeDtypeStruct((t, D_MODEL), BF16)],
        compiler_params=pltpu.CompilerParams(
            dimension_semantics=("parallel",),
            vmem_limit_bytes=V7X_VMEM_LIMIT_BYTES),
        name="out_proj",
    )(x2, yc, yd, qm, km3, vm3, g_mq, g_mem_out, g_ffn, w_o)


def _ffn_kernel(hf_ref, x1_ref, wgu_ref, wd_ref, o_ref):
    @pl.when(pl.program_id(1) == 0)
    def _():
        o_ref[...] = x1_ref[...]

    th = wd_ref.shape[0]
    gu = jnp.dot(hf_ref[...], wgu_ref[0], preferred_element_type=F32)
    gate, up = gu[:, :th], gu[:, th:]
    act = (gate * (1.0 / (1.0 + jnp.exp(-gate))) * up).astype(BF16)
    o_ref[...] += jnp.dot(act, wd_ref[...], preferred_element_type=F32)


def _ffn(hf, x1, w_gate_up, w_down):
    t = x1.shape[0]
    tm, th = TM_FFN, TH_FFN
    return pl.pallas_call(
        _ffn_kernel,
        grid=(t // tm, FFN_HIDDEN // th),
        in_specs=[
            pl.BlockSpec((tm, D_MODEL), lambda i, j: (i, 0)),
            pl.BlockSpec((tm, D_MODEL), lambda i, j: (i, 0)),
            pl.BlockSpec((1, D_MODEL, 2 * th), lambda i, j: (j, 0, 0)),
            pl.BlockSpec((th, D_MODEL), lambda i, j: (j, 0)),
        ],
        out_specs=pl.BlockSpec((tm, D_MODEL), lambda i, j: (i, 0)),
        out_shape=jax.ShapeDtypeStruct((t, D_MODEL), F32),
        compiler_params=pltpu.CompilerParams(
            dimension_semantics=("parallel", "arbitrary"),
            vmem_limit_bytes=V7X_VMEM_LIMIT_BYTES),
        name="ffn",
    )(hf, x1, w_gate_up, w_down)


def _rope_lane_freq():
    inv_freq = ROPE_THETA ** (-jnp.arange(0, ROT_DIM, 2, dtype=F32) / ROT_DIM)
    pad = jnp.zeros((DIFF_HALF - ROT_DIM,), F32)
    return jnp.concatenate([inv_freq, inv_freq, pad, inv_freq, inv_freq, pad]).reshape(1, -1)


def kernel(x, mem, positions, g_mix, g_mem, w_in, conv_w, g_conv_out, g_dq, g_dk,
           lam_q1, lam_k1, lam_q2, lam_k2, g_sub, w_mem_kv, g_mq, g_mk, g_mem_out,
           w_o, g_ffn, w_gate, w_up, w_down):
    b, s, d = x.shape
    assert (s, d) == (SEQ, D_MODEL) and mem.shape == (b, MEM_LEN, D_MODEL)
    assert g_mix.shape[0] == 1
    t = b * s
    x2 = x.reshape(t, d)
    mem2 = mem.reshape(b * MEM_LEN, d)
    yc, qt, kn, vt, qm, wkv16 = _in_proj(x2, g_mix, w_in[0], conv_w[0], g_conv_out,
                                  positions.reshape(t // TM_PROJ, 1, TM_PROJ), _rope_lane_freq(),
                                  jnp.tile(g_dq, (1, 2)), jnp.tile(g_dk, (1, 2)), w_mem_kv[0])
    km, vm = _mem_kv(mem2, g_mem, wkv16, g_mk)

    score_bound = (DIFF_HALF ** 0.5 * LOG2_E) * jnp.max(jnp.abs(g_dq)) * jnp.max(jnp.abs(g_dk))
    bounded = (score_bound <= MAX_SHIFT_FREE_LOG2_SCORE).astype(jnp.int32).reshape(1)
    yd, wgu16, wd16, wo16 = _diff_attn(bounded, qt, kn.reshape(b, s, DIFF_WIDTH), vt,
                                       lam_q1, lam_k1, lam_q2, lam_k2, g_sub,
                                       w_gate[0], w_up[0], w_down[0], w_o[0])
    x1, hf = _out_proj(x2, yc, yd.reshape(t, DIFF_WIDTH), qm,
                       km.reshape(b, MEM_LEN, MEM_WIDTH), vm.reshape(b, MEM_LEN, MEM_WIDTH),
                       g_mq, g_mem_out, g_ffn, wo16)
    out = _ffn(hf, x1, wgu16, wd16)
    return out.reshape(b, s, d)
```

```python
import functools

import jax
import jax.numpy as jnp
import numpy as np
from jax import lax
from jax.experimental import pallas as pl
from jax.experimental.pallas import tpu as pltpu

F32 = jnp.float32
BF16 = jnp.bfloat16

D_MODEL = 2048
SEQ = 2048
MEM_LEN = 256
CONV_WIDTH = 512
CONV_K = 3
DIFF_WIDTH = 1024
DIFF_VDIM = 128
DIFF_HALF = 64
DIFF_HEADS = 8
MEM_WIDTH = 512
MEM_HEADS = 4
MEM_HEAD_DIM = 128
IN_WIDTH = 3 * CONV_WIDTH + 3 * DIFF_WIDTH + MEM_WIDTH
ROT_DIM = 16
ROPE_THETA = 500000.0
FFN_HIDDEN = 5632
EPS = 1e-6
LAMBDA_INIT = 0.8 - 0.6 * float(np.exp(-0.3 * 0))
LOG2_E = float(np.log2(np.e))
MAX_SHIFT_FREE_LOG2_SCORE = 60.0

V7X_LANES = 128
V7X_SUBLANES = 8
V7X_BF16_SUBLANES = 16
V7X_VMEM_LIMIT_BYTES = 60 * 1024 * 1024

TM_PROJ = 512
T_ATTN = 256
TM_MEMKV = 1024
TM_OUT = 512
TM_FFN = 1024
TH_FFN = 512
WEIGHT_CHUNK_ROWS = 128
SCORE_LOOKAHEAD = 2
ATTN_HEADS_PER_STEP = 2

_OFF_U = 0
_OFF_C = CONV_WIDTH
_OFF_B = 2 * CONV_WIDTH
_OFF_Q = 3 * CONV_WIDTH
_OFF_K = _OFF_Q + DIFF_WIDTH
_OFF_V = _OFF_K + DIFF_WIDTH
_OFF_QM = _OFF_V + DIFF_WIDTH


def _rms_scale(t, width):
    return lax.rsqrt(jnp.sum(t * t, axis=-1, keepdims=True) * (1.0 / width) + EPS)


def _load_weight_as_bf16(w_hbm, w_scr, stage, sem):
    rows = stage.shape[1]
    n_chunks = w_hbm.shape[0] // rows

    def chunk_copy(c):
        return pltpu.make_async_copy(w_hbm.at[pl.ds(c * rows, rows), :], stage.at[c % 2],
                                     sem.at[c % 2])

    chunk_copy(0).start()
    for c in range(n_chunks):
        if c + 1 < n_chunks:
            chunk_copy(c + 1).start()
        chunk_copy(c).wait()
        w_scr[c * rows:(c + 1) * rows, :] = stage[c % 2].astype(BF16)


def _resident_weight_scratch(k, n):
    return [pltpu.VMEM((k, n), BF16), pltpu.VMEM((2, WEIGHT_CHUNK_ROWS, n), F32),
            pltpu.SemaphoreType.DMA((2,))]


def _qk_norm_rope(t, g2, cos_t, sin_a, sin_b):
    lane = lax.broadcasted_iota(jnp.int32, t.shape, 1)
    is_lo = lane < DIFF_HALF
    t2 = t * t
    s_lo = jnp.sum(jnp.where(is_lo, t2, 0.0), axis=-1, keepdims=True)
    s_hi = jnp.sum(jnp.where(is_lo, 0.0, t2), axis=-1, keepdims=True)
    r = jnp.where(is_lo,
                  lax.rsqrt(s_lo * (1.0 / DIFF_HALF) + EPS),
                  lax.rsqrt(s_hi * (1.0 / DIFF_HALF) + EPS))
    tn = t * r * g2
    half = ROT_DIM // 2
    return (tn * cos_t
            + pltpu.roll(tn, V7X_LANES - half, axis=1) * sin_a
            + pltpu.roll(tn, half, axis=1) * sin_b)


def _in_proj_kernel(x_ref, g_ref, w_hbm, convw_ref, gconv_ref, pos_ref, freq_ref,
                    gq_ref, gk_ref, wkv_ref,
                    yconv_ref, qt_ref, kn_ref, vt_ref, qm_ref, wkv_out,
                    z_scr, w_ref, w_stage, w_sem, *, tiles_per_seq):
    i = pl.program_id(0)
    halo = V7X_SUBLANES

    @pl.when(i == 0)
    def _():
        _load_weight_as_bf16(w_hbm, w_ref, w_stage, w_sem)

    @pl.when(i % tiles_per_seq == 0)
    def _():
        z_scr[0:halo, :] = jnp.zeros((halo, CONV_WIDTH), F32)

    wkv_out[...] = wkv_ref[...].astype(BF16)

    x = x_ref[...]
    h = (x * _rms_scale(x, D_MODEL) * g_ref[...]).astype(BF16)

    def proj(off, width):
        return jnp.dot(h, w_ref[:, off:off + width], preferred_element_type=F32)

    ts = qt_ref.shape[3]
    subs = [slice(sub * ts, (sub + 1) * ts) for sub in range(qt_ref.shape[1])]

    z_all = proj(_OFF_C, CONV_WIDTH) * proj(_OFF_U, CONV_WIDTH)
    b_all = proj(_OFF_B, CONV_WIDTH)
    cw = convw_ref[...]
    for rows in subs:
        z = z_all[rows]
        z_scr[halo:halo + ts, :] = z
        z1 = z_scr[halo - 1:halo - 1 + ts, :]
        z2 = z_scr[halo - 2:halo - 2 + ts, :]
        conv = cw[0:1, :] * z2 + cw[1:2, :] * z1 + cw[2:3, :] * z
        y = b_all[rows] * conv
        yconv_ref[rows, :] = (y * _rms_scale(y, CONV_WIDTH) * gconv_ref[...]).astype(BF16)
        z_scr[0:halo, :] = z_scr[ts:ts + halo, :]

    qf = proj(_OFF_Q, DIFF_WIDTH)
    kf = proj(_OFF_K, DIFF_WIDTH)
    vf = proj(_OFF_V, DIFF_WIDTH)
    gq, gk = gq_ref[...], gk_ref[...]
    for sub, rows in enumerate(subs):
        pos = jnp.broadcast_to(pos_ref[0][:, rows].astype(F32), (V7X_LANES, ts)).T
        ang = pos * freq_ref[...]
        cos_t, sin_t = jnp.cos(ang), jnp.sin(ang)
        pos_in_half = lax.broadcasted_iota(jnp.int32, sin_t.shape, 1) % DIFF_HALF
        sin_a = jnp.where(pos_in_half < ROT_DIM // 2, -sin_t, 0.0)
        sin_b = jnp.where((pos_in_half >= ROT_DIM // 2) & (pos_in_half < ROT_DIM), sin_t, 0.0)
        for hd in range(DIFF_HEADS):
            cols = slice(hd * DIFF_VDIM, (hd + 1) * DIFF_VDIM)
            qn = (_qk_norm_rope(qf[rows, cols], gq, cos_t, sin_a, sin_b)
                  * (DIFF_HALF ** -0.5 * LOG2_E))
            qt_ref[0, sub, cols, :] = qn.T.astype(BF16)
            kn_ref[rows, cols] = _qk_norm_rope(kf[rows, cols], gk, cos_t, sin_a,
                                               sin_b).astype(BF16)
            vt_ref[0, sub, cols, :] = vf[rows, cols].T.astype(BF16)
    qm_ref[...] = proj(_OFF_QM, MEM_WIDTH)


def _in_proj(x2, g_mix, w_in, conv_w, g_conv, pos, lane_freq, gq2, gk2, w_kv):
    t = x2.shape[0]
    tm, ta = TM_PROJ, T_ATTN
    tps = SEQ // tm
    sub = tm // ta
    nb = t // SEQ
    grid = (t // tm,)
    row = lambda i: (i, 0)
    fixed = lambda i: (0, 0)
    tile_t = lambda i: (i // tps, i % tps, 0, 0)
    kv_rows = w_kv.shape[0] // grid[0]
    assert kv_rows % V7X_BF16_SUBLANES == 0
    kv_blk = pl.BlockSpec((kv_rows, w_kv.shape[1]), row)
    return pl.pallas_call(
        functools.partial(_in_proj_kernel, tiles_per_seq=tps),
        grid=grid,
        in_specs=[
            pl.BlockSpec((tm, D_MODEL), row),
            pl.BlockSpec((1, D_MODEL), fixed),
            pl.BlockSpec(memory_space=pl.ANY),
            pl.BlockSpec((CONV_K, CONV_WIDTH), fixed),
            pl.BlockSpec((1, CONV_WIDTH), fixed),
            pl.BlockSpec((1, 1, tm), lambda i: (i, 0, 0)),
            pl.BlockSpec((1, V7X_LANES), fixed),
            pl.BlockSpec((1, DIFF_VDIM), fixed),
            pl.BlockSpec((1, DIFF_VDIM), fixed),
            kv_blk,
        ],
        out_specs=[
            pl.BlockSpec((tm, CONV_WIDTH), row),
            pl.BlockSpec((1, sub, DIFF_WIDTH, ta), tile_t),
            pl.BlockSpec((tm, DIFF_WIDTH), row),
            pl.BlockSpec((1, sub, DIFF_WIDTH, ta), tile_t),
            pl.BlockSpec((tm, MEM_WIDTH), row),
            kv_blk,
        ],
        out_shape=[
            jax.ShapeDtypeStruct((t, CONV_WIDTH), BF16),
            jax.ShapeDtypeStruct((nb, SEQ // ta, DIFF_WIDTH, ta), BF16),
            jax.ShapeDtypeStruct((t, DIFF_WIDTH), BF16),
            jax.ShapeDtypeStruct((nb, SEQ // ta, DIFF_WIDTH, ta), BF16),
            jax.ShapeDtypeStruct((t, MEM_WIDTH), F32),
            jax.ShapeDtypeStruct(w_kv.shape, BF16),
        ],
        scratch_shapes=[pltpu.VMEM((ta + V7X_SUBLANES, CONV_WIDTH), F32)]
        + _resident_weight_scratch(D_MODEL, IN_WIDTH),
        compiler_params=pltpu.CompilerParams(
            dimension_semantics=("arbitrary",),
            vmem_limit_bytes=V7X_VMEM_LIMIT_BYTES),
        name="in_proj",
    )(x2, g_mix, w_in, conv_w, g_conv, pos, lane_freq, gq2, gk2, w_kv)


def _mem_kv_kernel(mem_ref, g_ref, w_ref, gk_ref, k_ref, v_ref):
    x = mem_ref[...]
    h = (x * _rms_scale(x, D_MODEL) * g_ref[...]).astype(BF16)
    kv = jnp.dot(h, w_ref[...], preferred_element_type=F32)
    gk = gk_ref[...]
    for hd in range(MEM_HEADS):
        lo = hd * MEM_HEAD_DIM
        kh = kv[:, lo:lo + MEM_HEAD_DIM]
        k_ref[:, lo:lo + MEM_HEAD_DIM] = (kh * _rms_scale(kh, MEM_HEAD_DIM) * gk).astype(BF16)
    v_ref[...] = kv[:, MEM_WIDTH:].astype(BF16)


def _mem_kv(mem2, g_mem, w_kv, g_mk):
    t = mem2.shape[0]
    tm = TM_MEMKV
    row = lambda i: (i, 0)
    fixed = lambda i: (0, 0)
    return pl.pallas_call(
        _mem_kv_kernel,
        grid=(t // tm,),
        in_specs=[
            pl.BlockSpec((tm, D_MODEL), row),
            pl.BlockSpec((1, D_MODEL), fixed),
            pl.BlockSpec((D_MODEL, 2 * MEM_WIDTH), fixed),
            pl.BlockSpec((1, MEM_HEAD_DIM), fixed),
        ],
        out_specs=[pl.BlockSpec((tm, MEM_WIDTH), row), pl.BlockSpec((tm, MEM_WIDTH), row)],
        out_shape=[jax.ShapeDtypeStruct((t, MEM_WIDTH), BF16),
                   jax.ShapeDtypeStruct((t, MEM_WIDTH), BF16)],
        compiler_params=pltpu.CompilerParams(
            dimension_semantics=("parallel",),
            vmem_limit_bytes=V7X_VMEM_LIMIT_BYTES),
        name="mem_kv",
    )(mem2, g_mem, w_kv, g_mk)


def _diff_attn_kernel(bounded_ref, qt_ref, kn_ref, vt_ref, lq1_ref, lk1_ref, lq2_ref, lk2_ref,
                      gsub_ref, wg_ref, wu_ref, wd_ref, wo_ref, o_ref, wgu_out, wd_out, wo_out):
    nq, tq = qt_ref.shape[1], qt_ref.shape[3]
    tk = vt_ref.shape[3]

    lam = (jnp.exp(jnp.sum(lq1_ref[...] * lk1_ref[...], axis=-1, keepdims=True))
           - jnp.exp(jnp.sum(lq2_ref[...] * lk2_ref[...], axis=-1, keepdims=True))
           + LAMBDA_INIT)
    key = lax.broadcasted_iota(jnp.int32, (tk, tq), 0)
    qry = lax.broadcasted_iota(jnp.int32, (tk, tq), 1)
    causal = jnp.concatenate([key <= qry, key <= qry], axis=1)
    sum_rows = (lax.broadcasted_iota(jnp.int32, (V7X_BF16_SUBLANES, tk), 0) == 0).astype(BF16)

    def head(hd):
        return slice(hd * DIFF_VDIM, (hd + 1) * DIFF_VDIM)

    def scores(hd, c, j):
        qt = qt_ref[0, c, head(hd), :]
        is_lo = lax.broadcasted_iota(jnp.int32, qt.shape, 0) < DIFF_HALF
        zero = jnp.zeros_like(qt)
        qq = jnp.concatenate([jnp.where(is_lo, qt, zero), jnp.where(is_lo, zero, qt)], axis=1)
        return jnp.dot(kn_ref[0, j * tk:(j + 1) * tk, head(hd)], qq,
                       preferred_element_type=F32)

    steps = [(hd, c, j) for hd in range(ATTN_HEADS_PER_STEP)
             for c in range(nq) for j in range(c + 1)]

    def attend(shift_free):
        th = wgu_out.shape[2] // 2
        for ch in range(wgu_out.shape[0]):
            cols = slice(ch * th, (ch + 1) * th)
            wgu_out[ch, :, :th] = wg_ref[:, cols].astype(BF16)
            wgu_out[ch, :, th:] = wu_ref[:, cols].astype(BF16)
        wd_out[...] = wd_ref[...].astype(BF16)
        wo_out[...] = wo_ref[...].astype(BF16)
        pending = [scores(*st) for st in steps[:SCORE_LOOKAHEAD]]
        m = acc = None
        for n, (hd, c, j) in enumerate(steps):
            s = pending.pop(0)
            if n + SCORE_LOOKAHEAD < len(steps):
                pending.append(scores(*steps[n + SCORE_LOOKAHEAD]))
            if j == c:
                s = jnp.where(causal, s, -jnp.inf)
            vt_ext = jnp.concatenate([vt_ref[0, j, head(hd), :], sum_rows], axis=0)
            if shift_free:
                pv = jnp.dot(vt_ext, jnp.exp2(s).astype(BF16), preferred_element_type=F32)
                acc = pv if j == 0 else acc + pv
            else:
                m_blk = jnp.max(s, axis=0, keepdims=True)
                m_new = m_blk if j == 0 else jnp.maximum(m, m_blk)
                p = jnp.exp2(s - m_new)
                pv = jnp.dot(vt_ext, p.astype(BF16), preferred_element_type=F32)
                acc = pv if j == 0 else jnp.exp2(m - m_new) * acc + pv
                m = m_new
            if j == c:
                o_all = acc[:DIFF_VDIM] / acc[DIFF_VDIM:DIFF_VDIM + 1]
                o = (o_all[:, :tq] - lam * o_all[:, tq:]).T
                y = o * _rms_scale(o, DIFF_VDIM) * gsub_ref[...] * (1.0 - LAMBDA_INIT)
                o_ref[0, c * tq:(c + 1) * tq, head(hd)] = y.astype(o_ref.dtype)

    pl.when(bounded_ref[0] != 0)(functools.partial(attend, True))
    pl.when(bounded_ref[0] == 0)(functools.partial(attend, False))


def _diff_attn(bounded, qt4, kn3, vt4, lq1, lk1, lq2, lk2, g_sub, w_gate, w_up, w_down, w_o):
    b, nq, _, tq = qt4.shape
    s = kn3.shape[1]
    nk, tk = vt4.shape[1], vt4.shape[3]
    assert tq == tk and nq == nk
    fixed = lambda bb, h: (0, 0)
    vec = lambda n: pl.BlockSpec((1, n), fixed)
    groups = DIFF_HEADS // ATTN_HEADS_PER_STEP
    width = ATTN_HEADS_PER_STEP * DIFF_VDIM
    steps = b * groups
    nh = FFN_HIDDEN // TH_FFN
    gu_rows, down_rows, wo_rows = D_MODEL // steps, FFN_HIDDEN // steps, w_o.shape[0] // steps
    assert all(r % V7X_BF16_SUBLANES == 0 for r in (gu_rows, down_rows, wo_rows))
    step = lambda bb, g: bb * groups + g
    gu_in = pl.BlockSpec((gu_rows, FFN_HIDDEN), lambda bb, g: (step(bb, g), 0))
    gu_out = pl.BlockSpec((nh, gu_rows, 2 * TH_FFN), lambda bb, g: (0, step(bb, g), 0))
    down_blk = pl.BlockSpec((down_rows, D_MODEL), lambda bb, g: (step(bb, g), 0))
    wo_blk = pl.BlockSpec((wo_rows, w_o.shape[1]), lambda bb, g: (step(bb, g), 0))
    return pl.pallas_call(
        _diff_attn_kernel,
        grid=(b, groups),
        in_specs=[
            pl.BlockSpec(memory_space=pltpu.SMEM),
            pl.BlockSpec((1, nq, width, tq), lambda bb, g: (bb, 0, g, 0)),
            pl.BlockSpec((1, s, width), lambda bb, g: (bb, 0, g)),
            pl.BlockSpec((1, nk, width, tk), lambda bb, g: (bb, 0, g, 0)),
            vec(DIFF_HALF), vec(DIFF_HALF), vec(DIFF_HALF), vec(DIFF_HALF),
            vec(DIFF_VDIM),
            gu_in, gu_in, down_blk, wo_blk,
        ],
        out_specs=[pl.BlockSpec((1, s, width), lambda bb, g: (bb, 0, g)), gu_out, down_blk,
                   wo_blk],
        out_shape=[jax.ShapeDtypeStruct((b, s, DIFF_WIDTH), BF16),
                   jax.ShapeDtypeStruct((nh, D_MODEL, 2 * TH_FFN), BF16),
                   jax.ShapeDtypeStruct((FFN_HIDDEN, D_MODEL), BF16),
                   jax.ShapeDtypeStruct(w_o.shape, BF16)],
        compiler_params=pltpu.CompilerParams(
            dimension_semantics=("parallel", "parallel"),
            vmem_limit_bytes=V7X_VMEM_LIMIT_BYTES),
        name="diff_attn",
    )(bounded, qt4, kn3, vt4, lq1, lk1, lq2, lk2, g_sub, w_gate, w_up, w_down, w_o)


def _out_proj_kernel(x_ref, yc_ref, yd_ref, qm_ref, km_ref, vm_ref, gq_ref, gmo_ref, gffn_ref,
                     w_ref, x1_ref, hf_ref):
    gq = gq_ref[...]
    probs, denoms = [], []
    for hd in range(MEM_HEADS):
        cols = slice(hd * MEM_HEAD_DIM, (hd + 1) * MEM_HEAD_DIM)
        qh = qm_ref[:, cols]
        qh = (qh * _rms_scale(qh, MEM_HEAD_DIM) * gq * (MEM_HEAD_DIM ** -0.5)).astype(BF16)
        s = lax.dot_general(qh, km_ref[0, :, cols], (((1,), (1,)), ((), ())),
                            preferred_element_type=F32)
        e = jnp.exp(s - jnp.max(s, axis=-1, keepdims=True))
        denoms.append(jnp.sum(e, axis=-1, keepdims=True))
        probs.append(e.astype(BF16))

    acc = jnp.dot(yc_ref[...], w_ref[0:CONV_WIDTH, :], preferred_element_type=F32)

    outs = []
    for hd in range(MEM_HEADS):
        cols = slice(hd * MEM_HEAD_DIM, (hd + 1) * MEM_HEAD_DIM)
        oh = jnp.dot(probs[hd], vm_ref[0, :, cols], preferred_element_type=F32)
        outs.append(oh / denoms[hd])
    o = jnp.concatenate(outs, axis=-1)
    ym = (o * _rms_scale(o, MEM_WIDTH) * gmo_ref[...]).astype(BF16)

    acc += jnp.dot(yd_ref[...], w_ref[CONV_WIDTH:CONV_WIDTH + DIFF_WIDTH, :],
                   preferred_element_type=F32)
    acc += jnp.dot(ym, w_ref[CONV_WIDTH + DIFF_WIDTH:, :], preferred_element_type=F32)
    x1 = x_ref[...] + acc
    x1_ref[...] = x1
    hf_ref[...] = (x1 * _rms_scale(x1, D_MODEL) * gffn_ref[...]).astype(BF16)


def _out_proj(x2, yc, yd, qm, km3, vm3, g_mq, g_mem_out, g_ffn, w_o):
    t = x2.shape[0]
    tm = TM_OUT
    tps = SEQ // tm
    row = lambda i: (i, 0)
    fixed = lambda i: (0, 0)
    mem_blk = lambda i: (i // tps, 0, 0)
    return pl.pallas_call(
        _out_proj_kernel,
        grid=(t // tm,),
        in_specs=[
            pl.BlockSpec((tm, D_MODEL), row),
            pl.BlockSpec((tm, CONV_WIDTH), row),
            pl.BlockSpec((tm, DIFF_WIDTH), row),
            pl.BlockSpec((tm, MEM_WIDTH), row),
            pl.BlockSpec((1, MEM_LEN, MEM_WIDTH), mem_blk),
            pl.BlockSpec((1, MEM_LEN, MEM_WIDTH), mem_blk),
            pl.BlockSpec((1, MEM_HEAD_DIM), fixed),
            pl.BlockSpec((1, MEM_WIDTH), fixed),
            pl.BlockSpec((1, D_MODEL), fixed),
            pl.BlockSpec((D_MODEL, D_MODEL), fixed, pipeline_mode=pl.Buffered(1)),
        ],
        out_specs=[pl.BlockSpec((tm, D_MODEL), row), pl.BlockSpec((tm, D_MODEL), row)],
        out_shape=[jax.ShapeDtypeStruct((t, D_MODEL), F32),
                   jax.ShapeDtypeStruct((t, D_MODEL), BF16)],
        compiler_params=pltpu.CompilerParams(
            dimension_semantics=("parallel",),
            vmem_limit_bytes=V7X_VMEM_LIMIT_BYTES),
        name="out_proj",
    )(x2, yc, yd, qm, km3, vm3, g_mq, g_mem_out, g_ffn, w_o)


def _ffn_kernel(hf_ref, x1_ref, wgu_ref, wd_ref, o_ref):
    @pl.when(pl.program_id(1) == 0)
    def _():
        o_ref[...] = x1_ref[...]

    th = wd_ref.shape[0]
    gu = jnp.dot(hf_ref[...], wgu_ref[0], preferred_element_type=F32)
    gate, up = gu[:, :th], gu[:, th:]
    act = (gate * (1.0 / (1.0 + jnp.exp(-gate))) * up).astype(BF16)
    o_ref[...] += jnp.dot(act, wd_ref[...], preferred_element_type=F32)


def _ffn(hf, x1, w_gate_up, w_down):
    t = x1.shape[0]
    tm, th = TM_FFN, TH_FFN
    return pl.pallas_call(
        _ffn_kernel,
        grid=(t // tm, FFN_HIDDEN // th),
        in_specs=[
            pl.BlockSpec((tm, D_MODEL), lambda i, j: (i, 0)),
            pl.BlockSpec((tm, D_MODEL), lambda i, j: (i, 0)),
            pl.BlockSpec((1, D_MODEL, 2 * th), lambda i, j: (j, 0, 0)),
            pl.BlockSpec((th, D_MODEL), lambda i, j: (j, 0)),
        ],
        out_specs=pl.BlockSpec((tm, D_MODEL), lambda i, j: (i, 0)),
        out_shape=jax.ShapeDtypeStruct((t, D_MODEL), F32),
        compiler_params=pltpu.CompilerParams(
            dimension_semantics=("parallel", "arbitrary"),
            vmem_limit_bytes=V7X_VMEM_LIMIT_BYTES),
        name="ffn",
    )(hf, x1, w_gate_up, w_down)


def _rope_lane_freq():
    inv_freq = ROPE_THETA ** (-jnp.arange(0, ROT_DIM, 2, dtype=F32) / ROT_DIM)
    pad = jnp.zeros((DIFF_HALF - ROT_DIM,), F32)
    return jnp.concatenate([inv_freq, inv_freq, pad, inv_freq, inv_freq, pad]).reshape(1, -1)


def kernel(x, mem, positions, g_mix, g_mem, w_in, conv_w, g_conv_out, g_dq, g_dk,
           lam_q1, lam_k1, lam_q2, lam_k2, g_sub, w_mem_kv, g_mq, g_mk, g_mem_out,
           w_o, g_ffn, w_gate, w_up, w_down):
    b, s, d = x.shape
    assert (s, d) == (SEQ, D_MODEL) and mem.shape == (b, MEM_LEN, D_MODEL)
    assert g_mix.shape[0] == 1
    t = b * s
    x2 = x.reshape(t, d)
    mem2 = mem.reshape(b * MEM_LEN, d)
    yc, qt, kn, vt, qm, wkv16 = _in_proj(x2, g_mix, w_in[0], conv_w[0], g_conv_out,
                                  positions.reshape(t // TM_PROJ, 1, TM_PROJ), _rope_lane_freq(),
                                  jnp.tile(g_dq, (1, 2)), jnp.tile(g_dk, (1, 2)), w_mem_kv[0])
    km, vm = _mem_kv(mem2, g_mem, wkv16, g_mk)

    score_bound = (DIFF_HALF ** 0.5 * LOG2_E) * jnp.max(jnp.abs(g_dq)) * jnp.max(jnp.abs(g_dk))
    bounded = (score_bound <= MAX_SHIFT_FREE_LOG2_SCORE).astype(jnp.int32).reshape(1)
    yd, wgu16, wd16, wo16 = _diff_attn(bounded, qt, kn.reshape(b, s, DIFF_WIDTH), vt,
                                       lam_q1, lam_k1, lam_q2, lam_k2, g_sub,
                                       w_gate[0], w_up[0], w_down[0], w_o[0])
    x1, hf = _out_proj(x2, yc, yd.reshape(t, DIFF_WIDTH), qm,
                       km.reshape(b, MEM_LEN, MEM_WIDTH), vm.reshape(b, MEM_LEN, MEM_WIDTH),
                       g_mq, g_mem_out, g_ffn, wo16)
    out = _ffn(hf, x1, wgu16, wd16)
    return out.reshape(b, s, d)
```
